```python
import math
import jax, jax.numpy as jnp
from jax import lax
import numpy as np

D_MODEL = 2048
BATCH = 2
SEQ = 8192
DEPTH = 2

HEAD_DIM = 128
H_DIFF = 6
DK_DIFF = HEAD_DIM // 2
DV_DIFF = HEAD_DIM
H_FOX = 6
H_DSA = 4
H_IDX = 8
D_IDX = 64
TOPK_MAX = 256
D_FF = 5632
BLOCK = 128
ROPE_THETA = 10000.0
NORM_EPS = 1e-6
N_BRANCH = 3
N_MOD = 9
NEG_INF = -1e30
FORGET_BIAS_MEAN = 2.0
IDX_W_SCALE = (H_IDX ** -0.5) * (D_IDX ** -0.5)

W_DIFF = H_DIFF * HEAD_DIM
W_FOX = H_FOX * HEAD_DIM
W_DSA = H_DSA * HEAD_DIM
IN_SIZES = (W_DIFF, W_DIFF, W_DIFF, W_FOX, W_FOX, W_FOX, H_FOX, W_DSA, W_DSA, W_DSA, H_IDX * D_IDX, D_IDX, H_IDX)
D_IN = sum(IN_SIZES)

kernel_name = "hybrid_diff_fox_dsa_macaron_adaln"


def rms_norm(x, gain):
    xf = x.astype(jnp.float32)
    y = xf * lax.rsqrt(jnp.mean(xf * xf, axis=-1, keepdims=True) + NORM_EPS)
    return (y * gain.astype(jnp.float32)).astype(x.dtype)


def rope_tables(seq, dim):
    inv_freq = 1.0 / (ROPE_THETA ** (jnp.arange(0, dim, 2, dtype=jnp.float32) / dim))
    ang = jnp.arange(seq, dtype=jnp.float32)[:, None] * inv_freq[None, :]
    return jnp.cos(ang), jnp.sin(ang)


def apply_rope(t, cos, sin):
    half = t.shape[-1] // 2
    shape = (1, cos.shape[0]) + (1,) * (t.ndim - 3) + (half,)
    cs = cos.reshape(shape).astype(t.dtype)
    sn = sin.reshape(shape).astype(t.dtype)
    t1, t2 = t[..., :half], t[..., half:]
    return jnp.concatenate([t1 * cs - t2 * sn, t1 * sn + t2 * cs], axis=-1)


def to_blocks(t):
    b, s = t.shape[:2]
    return jnp.moveaxis(t.reshape((b, s // BLOCK, BLOCK) + t.shape[2:]), 1, 0)


def from_blocks(t):
    nb, b, blk = t.shape[:3]
    return jnp.moveaxis(t, 0, 1).reshape((b, nb * blk) + t.shape[3:])


def causal_mask(blk, n_keys):
    q_pos = blk * BLOCK + jnp.arange(BLOCK)
    return q_pos, jnp.arange(n_keys)[None, :] <= q_pos[:, None]


def swiglu(h, w1, w3, w2):
    return (jax.nn.silu(h @ w1) * (h @ w3)) @ w2


def diff_attention(q, k, v, lam):
    n_keys = k.shape[1]
    k1, k2 = k[..., 0, :], k[..., 1, :]
    scale = DK_DIFF ** -0.5

    def block(args):
        q1b, q2b, blk = args
        _, causal = causal_mask(blk, n_keys)

        def probs(qb, kk):
            s = jnp.einsum('bqhd,bkhd->bhqk', qb, kk).astype(jnp.float32) * scale
            return jax.nn.softmax(jnp.where(causal[None, None], s, NEG_INF), axis=-1)

        p = probs(q1b, k1) - lam * probs(q2b, k2)
        return jnp.einsum('bhqk,bkhd->bqhd', p.astype(v.dtype), v)

    o = lax.map(block, (to_blocks(q[..., 0, :]), to_blocks(q[..., 1, :]), jnp.arange(n_keys // BLOCK)))
    return from_blocks(o)


def forgetting_attention(q, k, v, log_f):
    n_keys = k.shape[1]
    cum = jnp.cumsum(log_f, axis=1)
    cum_k = jnp.swapaxes(cum, 1, 2)[:, :, None, :]
    scale = HEAD_DIM ** -0.5

    def block(args):
        qb, cum_q, blk = args
        _, causal = causal_mask(blk, n_keys)
        decay = jnp.swapaxes(cum_q, 1, 2)[..., None] - cum_k
        s = jnp.einsum('bqhd,bkhd->bhqk', qb, k).astype(jnp.float32) * scale + decay
        p = jax.nn.softmax(jnp.where(causal[None, None], s, NEG_INF), axis=-1)
        return jnp.einsum('bhqk,bkhd->bqhd', p.astype(v.dtype), v)

    o = lax.map(block, (to_blocks(q), to_blocks(cum), jnp.arange(n_keys // BLOCK)))
    return from_blocks(o)


def dsa_attention(q, k, v, q_idx, k_idx, w_idx):
    n_keys = k.shape[1]
    k_sel = min(TOPK_MAX, n_keys // 4)
    scale = HEAD_DIM ** -0.5

    def block(args):
        qb, qib, wb, blk = args
        q_pos, causal = causal_mask(blk, n_keys)
        rel = jax.nn.relu(jnp.einsum('bqhd,bsd->bqhs', qib, k_idx).astype(jnp.float32))
        score = jnp.einsum('bqh,bqhs->bqs', wb.astype(jnp.float32), rel)
        score = jnp.where(causal[None], score, NEG_INF)
        _, idx = lax.top_k(score, k_sel)
        valid = idx <= q_pos[None, :, None]
        kg = jax.vmap(lambda kb, ib: kb[ib])(k, idx)
        vg = jax.vmap(lambda vb, ib: vb[ib])(v, idx)
        s = jnp.einsum('bqhd,bqkhd->bhqk', qb, kg).astype(jnp.float32) * scale
        p = jax.nn.softmax(jnp.where(valid[:, None], s, NEG_INF), axis=-1)
        return jnp.einsum('bhqk,bqkhd->bqhd', p.astype(v.dtype), vg)

    o = lax.map(block, (to_blocks(q), to_blocks(q_idx), to_blocks(w_idx), jnp.arange(n_keys // BLOCK)))
    return from_blocks(o)


def hybrid_mixer(h, w_in, b_forget, lam_q1, lam_k1, lam_q2, lam_k2, subln_gain,
                 merge_w, merge_b, w_branch_a, w_branch_b, w_branch_c, w_out, lam_init):
    b, s, _ = h.shape
    offsets = []
    acc = 0
    for n in IN_SIZES[:-1]:
        acc += n
        offsets.append(acc)
    z = h @ w_in
    qa, ka, va, qb, kb, vb, fb, qc, kc, vc, qi, ki, wi = jnp.split(z, offsets, axis=-1)
    cos_a, sin_a = rope_tables(s, DK_DIFF)
    cos_c, sin_c = rope_tables(s, HEAD_DIM)
    cos_i, sin_i = rope_tables(s, D_IDX)

    f32 = jnp.float32
    lam = (jnp.exp(jnp.sum(lam_q1.astype(f32) * lam_k1.astype(f32)))
           - jnp.exp(jnp.sum(lam_q2.astype(f32) * lam_k2.astype(f32))) + lam_init)
    qa = apply_rope(qa.reshape(b, s, H_DIFF, 2, DK_DIFF), cos_a, sin_a)
    ka = apply_rope(ka.reshape(b, s, H_DIFF, 2, DK_DIFF), cos_a, sin_a)
    oa = diff_attention(qa, ka, va.reshape(b, s, H_DIFF, DV_DIFF), lam)
    oa = rms_norm(oa, subln_gain) * (1.0 - lam_init)

    log_f = jax.nn.log_sigmoid((fb + b_forget).astype(f32))
    ob = forgetting_attention(qb.reshape(b, s, H_FOX, HEAD_DIM), kb.reshape(b, s, H_FOX, HEAD_DIM),
                              vb.reshape(b, s, H_FOX, HEAD_DIM), log_f)

    oc = dsa_attention(apply_rope(qc.reshape(b, s, H_DSA, HEAD_DIM), cos_c, sin_c),
                       apply_rope(kc.reshape(b, s, H_DSA, HEAD_DIM), cos_c, sin_c),
                       vc.reshape(b, s, H_DSA, HEAD_DIM),
                       apply_rope(qi.reshape(b, s, H_IDX, D_IDX), cos_i, sin_i),
                       apply_rope(ki, cos_i, sin_i),
                       wi * IDX_W_SCALE)

    ya = oa.reshape(b, s, W_DIFF) @ w_branch_a
    yb = ob.reshape(b, s, W_FOX) @ w_branch_b
    yc = oc.reshape(b, s, W_DSA) @ w_branch_c
    gates = jax.nn.sigmoid((h @ merge_w + merge_b).astype(f32)).astype(h.dtype)
    gates = gates.reshape(b, s, N_BRANCH, D_MODEL)
    merged = gates[:, :, 0] * ya + gates[:, :, 1] * yb + gates[:, :, 2] * yc
    return merged @ w_out


def setup_inputs(seed: int = 0) -> dict:
    key = jax.random.key(seed)
    ks = iter(jax.random.split(key, 40))

    def nrm(shape, std):
        return jax.random.normal(next(ks), shape, jnp.float32) * std

    L, D = DEPTH, D_MODEL
    return {
        'x': nrm((BATCH, SEQ, D), 1.0),
        'c': nrm((BATCH, D), 1.0),
        'w_ada': nrm((L, D, N_MOD * D), 0.5 * D ** -0.5),
        'b_ada': nrm((L, N_MOD * D), 0.01),
        'norm_ffn1': 1.0 + nrm((L, D), 0.02),
        'ffn1_w1': nrm((L, D, D_FF), D ** -0.5),
        'ffn1_w3': nrm((L, D, D_FF), D ** -0.5),
        'ffn1_w2': nrm((L, D_FF, D), D_FF ** -0.5),
        'norm_mix': 1.0 + nrm((L, D), 0.02),
        'w_in': nrm((L, D, D_IN), D ** -0.5),
        'b_forget': FORGET_BIAS_MEAN + nrm((L, H_FOX), 0.5),
        'lam_q1': nrm((L, DK_DIFF), 0.1),
        'lam_k1': nrm((L, DK_DIFF), 0.1),
        'lam_q2': nrm((L, DK_DIFF), 0.1),
        'lam_k2': nrm((L, DK_DIFF), 0.1),
        'subln_gain': 1.0 + nrm((L, DV_DIFF), 0.02),
        'merge_w': nrm((L, D, N_BRANCH * D), D ** -0.5),
        'merge_b': nrm((L, N_BRANCH * D), 0.01),
        'w_branch_a': nrm((L, W_DIFF, D), W_DIFF ** -0.5),
        'w_branch_b': nrm((L, W_FOX, D), W_FOX ** -0.5),
        'w_branch_c': nrm((L, W_DSA, D), W_DSA ** -0.5),
        'w_out': nrm((L, D, D), D ** -0.5),
        'norm_ffn2': 1.0 + nrm((L, D), 0.02),
        'ffn2_w1': nrm((L, D, D_FF), D ** -0.5),
        'ffn2_w3': nrm((L, D, D_FF), D ** -0.5),
        'ffn2_w2': nrm((L, D_FF, D), D_FF ** -0.5),
        'norm_final': 1.0 + nrm((D,), 0.02),
    }


def reference(x, c, w_ada, b_ada, norm_ffn1, ffn1_w1, ffn1_w3, ffn1_w2, norm_mix, w_in, b_forget,
              lam_q1, lam_k1, lam_q2, lam_k2, subln_gain, merge_w, merge_b, w_branch_a, w_branch_b,
              w_branch_c, w_out, norm_ffn2, ffn2_w1, ffn2_w3, ffn2_w2, norm_final):
    c_act = jax.nn.silu(c)
    for l in range(DEPTH):
        lam_init = 0.8 - 0.6 * math.exp(-0.3 * l)
        mod = (c_act @ w_ada[l] + b_ada[l]).reshape(c.shape[0], N_MOD, D_MODEL)[:, :, None, :]
        sh1, sc1, g1, sh2, sc2, g2, sh3, sc3, g3 = [mod[:, i] for i in range(N_MOD)]
        h = rms_norm(x, norm_ffn1[l]) * (1.0 + sc1) + sh1
        x = x + 0.5 * g1 * swiglu(h, ffn1_w1[l], ffn1_w3[l], ffn1_w2[l])
        h = rms_norm(x, norm_mix[l]) * (1.0 + sc2) + sh2
        x = x + g2 * hybrid_mixer(h, w_in[l], b_forget[l], lam_q1[l], lam_k1[l], lam_q2[l], lam_k2[l],
                                  subln_gain[l], merge_w[l], merge_b[l], w_branch_a[l], w_branch_b[l],
                                  w_branch_c[l], w_out[l], lam_init)
        h = rms_norm(x, norm_ffn2[l]) * (1.0 + sc3) + sh3
        x = x + 0.5 * g3 * swiglu(h, ffn2_w1[l], ffn2_w3[l], ffn2_w2[l])
    return rms_norm(x, norm_final)
```

```python
import functools
import math

import numpy as np
import jax
import jax.numpy as jnp
from jax import lax
from jax.experimental import pallas as pl
from jax.experimental.pallas import tpu as pltpu

HEAD_DIM = 128
H_DIFF = 6
DK_DIFF = HEAD_DIM // 2
H_FOX = 6
H_DSA = 4
H_IDX = 8
D_IDX = 64
TOPK_MAX = 256
ROPE_THETA = 10000.0
NORM_EPS = 1e-6
N_BRANCH = 3
N_MOD = 9
NEG_INF = -1e30
IDX_W_SCALE = (H_IDX ** -0.5) * (D_IDX ** -0.5)

W_DIFF = H_DIFF * HEAD_DIM
W_FOX = H_FOX * HEAD_DIM
W_DSA = H_DSA * HEAD_DIM
W_IDX = H_IDX * D_IDX

LANES = 128
VMEM_LIMIT = 56 * 1024 * 1024
INT32_MIN = -(2 ** 31)

BF16 = jnp.bfloat16
F32 = jnp.float32


def _order_key_of(value):
    bits = int(np.array(value, np.float32).view(np.int32))
    return bits ^ ((bits >> 31) & 0x7FFFFFFF)


KEY_NEG_INF = _order_key_of(NEG_INF)


def _params(*semantics):
    return pltpu.CompilerParams(dimension_semantics=semantics, vmem_limit_bytes=VMEM_LIMIT)


def _nt_dot(a, b):
    return lax.dot_general(a, b, (((1,), (1,)), ((), ())), preferred_element_type=F32)


def _dot(a, b):
    return jnp.dot(a, b, preferred_element_type=F32)


def _adaln_kernel(c_ref, w_ref, b_ref, o_ref):
    c = c_ref[...]
    c_act = (c * jax.nn.sigmoid(c)).astype(BF16)
    o_ref[...] = _dot(c_act, w_ref[...].astype(BF16)) + b_ref[...]


def adaln_mod(c, w_ada, b_ada):
    depth, d, nd = w_ada.shape
    b = c.shape[0]
    tn = min(1024, d)
    return pl.pallas_call(
        _adaln_kernel,
        grid=(depth, nd // tn),
        in_specs=[
            pl.BlockSpec((b, d), lambda l, j: (0, 0)),
            pl.BlockSpec((None, d, tn), lambda l, j: (l, 0, j)),
            pl.BlockSpec((None, 1, tn), lambda l, j: (l, 0, j)),
        ],
        out_specs=pl.BlockSpec((None, b, tn), lambda l, j: (l, 0, j)),
        out_shape=jax.ShapeDtypeStruct((depth, b, nd), F32),
        compiler_params=_params("arbitrary", "arbitrary"),
        name="adaln_mod",
    )(c, w_ada, b_ada.reshape(depth, 1, nd))


def _norm_mod_kernel(x_ref, gain_ref, sc_ref, sh_ref, o_ref):
    x = x_ref[...]
    y = x * lax.rsqrt(jnp.mean(x * x, axis=-1, keepdims=True) + NORM_EPS)
    o_ref[...] = ((y * gain_ref[...]) * (1.0 + sc_ref[...]) + sh_ref[...]).astype(o_ref.dtype)


def norm_mod(x, gain, modl, i_shift, seq):
    n, d = x.shape
    tm = min(512, seq)
    return pl.pallas_call(
        _norm_mod_kernel,
        grid=(n // tm,),
        in_specs=[
            pl.BlockSpec((tm, d), lambda i: (i, 0)),
            pl.BlockSpec((1, d), lambda i: (0, 0)),
            pl.BlockSpec((None, 1, d), lambda i: ((i * tm) // seq * N_MOD + i_shift + 1, 0, 0)),
            pl.BlockSpec((None, 1, d), lambda i: ((i * tm) // seq * N_MOD + i_shift, 0, 0)),
        ],
        out_specs=pl.BlockSpec((tm, d), lambda i: (i, 0)),
        out_shape=jax.ShapeDtypeStruct((n, d), BF16),
        compiler_params=_params("arbitrary"),
        name="norm_mod",
    )(x, gain.reshape(1, d), modl, modl)


def _final_norm_kernel(x_ref, gain_ref, o_ref):
    x = x_ref[...]
    y = x * lax.rsqrt(jnp.mean(x * x, axis=-1, keepdims=True) + NORM_EPS)
    o_ref[...] = y * gain_ref[...]


def final_norm(x, gain):
    n, d = x.shape
    tm = min(512, n)
    return pl.pallas_call(
        _final_norm_kernel,
        grid=(n // tm,),
        in_specs=[pl.BlockSpec((tm, d), lambda i: (i, 0)), pl.BlockSpec((1, d), lambda i: (0, 0))],
        out_specs=pl.BlockSpec((tm, d), lambda i: (i, 0)),
        out_shape=jax.ShapeDtypeStruct((n, d), F32),
        compiler_params=_params("arbitrary"),
        name="final_norm",
    )(x, gain.reshape(1, d))


def _ffn_up_kernel(h_ref, w1_ref, w3_ref, o_ref):
    h = h_ref[...]
    a = _dot(h, w1_ref[...])
    b = _dot(h, w3_ref[...])
    o_ref[...] = ((a * jax.nn.sigmoid(a)) * b).astype(o_ref.dtype)


def ffn_up(h, w1, w3):
    n, d = h.shape
    f = w1.shape[1]
    tm = min(1024, n)
    tn = 512 if f % 512 == 0 else f
    return pl.pallas_call(
        _ffn_up_kernel,
        grid=(n // tm, f // tn),
        in_specs=[
            pl.BlockSpec((tm, d), lambda i, j: (i, 0)),
            pl.BlockSpec((d, tn), lambda i, j: (0, j)),
            pl.BlockSpec((d, tn), lambda i, j: (0, j)),
        ],
        out_specs=pl.BlockSpec((tm, tn), lambda i, j: (i, j)),
        out_shape=jax.ShapeDtypeStruct((n, f), BF16),
        compiler_params=_params("arbitrary", "arbitrary"),
        name="ffn_up",
    )(h, w1, w3)


def _resid_mm_kernel(a_ref, w_ref, x_ref, g_ref, o_ref, *, gscale):
    y = _dot(a_ref[...], w_ref[...])
    o_ref[...] = x_ref[...] + (gscale * g_ref[...]) * y


def resid_mm(a, w, x, modl, i_gate, gscale, seq):
    n, k = a.shape
    d = w.shape[1]
    tm = min(512, seq)
    tn = min(512, d)
    return pl.pallas_call(
        functools.partial(_resid_mm_kernel, gscale=gscale),
        grid=(n // tm, d // tn),
        in_specs=[
            pl.BlockSpec((tm, k), lambda i, j: (i, 0)),
            pl.BlockSpec((k, tn), lambda i, j: (0, j)),
            pl.BlockSpec((tm, tn), lambda i, j: (i, j)),
            pl.BlockSpec((None, 1, tn), lambda i, j: ((i * tm) // seq * N_MOD + i_gate, 0, j)),
        ],
        out_specs=pl.BlockSpec((tm, tn), lambda i, j: (i, j)),
        out_shape=jax.ShapeDtypeStruct((n, d), F32),
        compiler_params=_params("arbitrary", "arbitrary"),
        name="resid_mm",
    )(a, w, x, modl)


def _swap_halves(z, group):
    if group == LANES:
        return pltpu.roll(z, LANES // 2, axis=1)
    half = group // 2
    lane = lax.broadcasted_iota(jnp.int32, z.shape, 1)
    from_above = pltpu.roll(z, LANES - half, axis=1)
    from_below = pltpu.roll(z, half, axis=1)
    return jnp.where((lane & (group - 1)) < half, from_above, from_below)


def _proj_kernel(h_ref, w_ref, cos_ref, sin_ref, o_ref, *, group):
    z = _dot(h_ref[...], w_ref[...])
    if group == 0:
        o_ref[...] = z.astype(o_ref.dtype)
        return
    cos = cos_ref[...]
    sin = sin_ref[...]
    for c in range(z.shape[1] // LANES):
        zc = z[:, c * LANES:(c + 1) * LANES]
        o_ref[:, c * LANES:(c + 1) * LANES] = (zc * cos + _swap_halves(zc, group) * sin).astype(o_ref.dtype)


def proj(h, w, cos, sin, group, seq):
    n, d = h.shape
    cols = w.shape[1]
    tm = min(512, seq)
    n_chunks = cols // LANES
    tn = max(t * LANES for t in range(1, n_chunks + 1) if n_chunks % t == 0 and d * t * LANES * 2 <= 9 * 2 ** 20)
    s_blocks = seq // tm
    return pl.pallas_call(
        functools.partial(_proj_kernel, group=group),
        grid=(cols // tn, n // tm),
        in_specs=[
            pl.BlockSpec((tm, d), lambda j, i: (i, 0)),
            pl.BlockSpec((d, tn), lambda j, i: (0, j)),
            pl.BlockSpec((tm, LANES), lambda j, i: (i % s_blocks, 0)),
            pl.BlockSpec((tm, LANES), lambda j, i: (i % s_blocks, 0)),
        ],
        out_specs=pl.BlockSpec((tm, tn), lambda j, i: (i, j)),
        out_shape=jax.ShapeDtypeStruct((n, cols), BF16),
        compiler_params=_params("arbitrary", "arbitrary"),
        name=f"proj_rope{group}",
    )(h, w, cos, sin)


def _small_proj_kernel(h_ref, w_ref, wt_ref, bf_ref, logf_ref, wi_ref):
    h = h_ref[...]
    zt = _nt_dot(wt_ref[...], h)
    logf_ref[...] = jax.nn.log_sigmoid(zt[0:8, :] + bf_ref[...])
    wi_ref[...] = _dot(h, w_ref[...]) * IDX_W_SCALE


def small_proj(h, w_small, w_small_t, b_forget8, seq):
    n, d = h.shape
    tm = min(512, seq)
    return pl.pallas_call(
        _small_proj_kernel,
        grid=(n // tm,),
        in_specs=[
            pl.BlockSpec((tm, d), lambda i: (i, 0)),
            pl.BlockSpec((d, LANES), lambda i: (0, 0)),
            pl.BlockSpec((16, d), lambda i: (0, 0)),
            pl.BlockSpec((8, 1), lambda i: (0, 0)),
        ],
        out_specs=[
            pl.BlockSpec((8, tm), lambda i: (0, i)),
            pl.BlockSpec((tm, LANES), lambda i: (i, 0)),
        ],
        out_shape=[jax.ShapeDtypeStruct((8, n), F32), jax.ShapeDtypeStruct((n, LANES), F32)],
        compiler_params=_params("arbitrary"),
        name="small_proj",
    )(h, w_small, w_small_t, b_forget8)


def _split3(x):
    x1 = x.astype(BF16)
    r1 = x - x1.astype(F32)
    x2 = r1.astype(BF16)
    x3 = (r1 - x2.astype(F32)).astype(BF16)
    return x1, x2, x3


def _neg_cumsum_kernel(x_ref, o_ref):
    x = x_ref[...]
    chunks = x.shape[0]
    r = lax.broadcasted_iota(jnp.int32, (LANES, LANES), 0)
    c = lax.broadcasted_iota(jnp.int32, (LANES, LANES), 1)
    upper = (r <= c).astype(BF16)
    within = sum(_dot(p, upper) for p in _split3(x))
    totals = jnp.broadcast_to(within[:, LANES - 1:LANES], (chunks, LANES))
    rr = lax.broadcasted_iota(jnp.int32, (chunks, chunks), 0)
    cc = lax.broadcasted_iota(jnp.int32, (chunks, chunks), 1)
    strict_lower = (cc < rr).astype(BF16)
    offset = sum(_dot(strict_lower, p) for p in _split3(totals))
    o_ref[...] = -(within + offset)


def neg_cumsum(logf_t, batch, seq):
    rows = logf_t.shape[0] * batch
    chunks = seq // LANES
    x = logf_t.reshape(rows, chunks, LANES)
    out = pl.pallas_call(
        _neg_cumsum_kernel,
        grid=(rows,),
        in_specs=[pl.BlockSpec((None, chunks, LANES), lambda i: (i, 0, 0))],
        out_specs=pl.BlockSpec((None, chunks, LANES), lambda i: (i, 0, 0)),
        out_shape=jax.ShapeDtypeStruct((rows, chunks, LANES), F32),
        compiler_params=_params("arbitrary"),
        name="neg_cumsum",
    )(x)
    return out.reshape(logf_t.shape[0], batch, seq)


def _online_softmax_step(s, v, m_ref, l_ref, acc_ref):
    m_prev = m_ref[...]
    m_new = jnp.maximum(m_prev, jnp.max(s, axis=1, keepdims=True))
    alpha = jnp.exp(m_prev - m_new)
    p = jnp.exp(s - m_new)
    l_ref[...] = alpha * l_ref[...] + jnp.sum(p, axis=1, keepdims=True)
    acc_ref[...] = alpha * acc_ref[...] + _dot(p.astype(v.dtype), v)
    m_ref[...] = m_new


def _init_softmax_state(m_ref, l_ref, acc_ref):
    m_ref[...] = jnp.full(m_ref.shape, -jnp.inf, F32)
    l_ref[...] = jnp.zeros(l_ref.shape, F32)
    acc_ref[...] = jnp.zeros(acc_ref.shape, F32)


def _causal_tile_mask(q_block, k_tile, tq, tk):
    row = q_block * tq + lax.broadcasted_iota(jnp.int32, (tq, tk), 0)
    col = k_tile * tk + lax.broadcasted_iota(jnp.int32, (tq, tk), 1)
    return col <= row


def _sweep_key_tiles(q_block, tq, tk, tile_fn):
    n_full = (q_block * tq) // tk
    n_tiles = ((q_block + 1) * tq + tk - 1) // tk

    def run(masked):
        def body(kj, carry):
            tile_fn(kj, masked)
            return carry
        return body

    lax.fori_loop(0, n_full, run(False), 0)
    lax.fori_loop(n_full, n_tiles, run(True), 0)


def _fox_kernel(q_ref, k_ref, v_ref, nc_ref, o_ref, m_ref, l_ref, acc_ref, *, tq, tk):
    qb = pl.program_id(2)
    q = q_ref[...]
    scale = HEAD_DIM ** -0.5
    _init_softmax_state(m_ref, l_ref, acc_ref)

    def tile(kj, masked):
        off = pl.multiple_of(kj * tk, tk)
        k = k_ref[pl.ds(off, tk), :]
        v = v_ref[pl.ds(off, tk), :]
        s = _nt_dot(q, k) * scale + nc_ref[kj]
        if masked:
            s = jnp.where(_causal_tile_mask(qb, kj, tq, tk), s, NEG_INF)
        _online_softmax_step(s, v, m_ref, l_ref, acc_ref)

    _sweep_key_tiles(qb, tq, tk, tile)
    o_ref[...] = (acc_ref[...] / l_ref[...]).astype(o_ref.dtype)


def fox_attention(plain, neg_cum, seq, tq, tk):
    b = plain.shape[0]
    nc = neg_cum.reshape(8, b, seq // tk, 1, tk)
    head = lambda base: pl.BlockSpec((None, seq, HEAD_DIM), lambda bi, h, i: (bi, 0, base + h))
    return pl.pallas_call(
        functools.partial(_fox_kernel, tq=tq, tk=tk),
        grid=(b, H_FOX, seq // tq),
        in_specs=[
            pl.BlockSpec((None, tq, HEAD_DIM), lambda bi, h, i: (bi, i, H_DIFF + h)),
            head(H_DIFF + H_FOX),
            head(H_DIFF + 2 * H_FOX),
            pl.BlockSpec((None, None, seq // tk, 1, tk), lambda bi, h, i: (h, bi, 0, 0, 0)),
        ],
        out_specs=pl.BlockSpec((None, tq, HEAD_DIM), lambda bi, h, i: (bi, i, h)),
        out_shape=jax.ShapeDtypeStruct((b, seq, W_FOX), BF16),
        scratch_shapes=[pltpu.VMEM((tq, 1), F32), pltpu.VMEM((tq, 1), F32), pltpu.VMEM((tq, HEAD_DIM), F32)],
        compiler_params=_params("arbitrary", "arbitrary", "arbitrary"),
        name="fox_attention",
    )(plain, plain, plain, nc)


def _diff_kernel(lam_ref, gain_ref, q_ref, k_ref, v_ref, o_ref,
                 m1_ref, l1_ref, acc1_ref, m2_ref, l2_ref, acc2_ref, *, tq, tk, lam_init):
    qb = pl.program_id(2)
    q = q_ref[...]
    lane = lax.broadcasted_iota(jnp.int32, q.shape, 1)
    zero = jnp.zeros_like(q)
    q1 = jnp.where(lane < DK_DIFF, q, zero)
    q2 = jnp.where(lane >= DK_DIFF, q, zero)
    scale = DK_DIFF ** -0.5
    _init_softmax_state(m1_ref, l1_ref, acc1_ref)
    _init_softmax_state(m2_ref, l2_ref, acc2_ref)

    def tile(kj, masked):
        off = pl.multiple_of(kj * tk, tk)
        k = k_ref[pl.ds(off, tk), :]
        v = v_ref[pl.ds(off, tk), :]
        mask = _causal_tile_mask(qb, kj, tq, tk) if masked else None
        for qq, m_ref, l_ref, acc_ref in ((q1, m1_ref, l1_ref, acc1_ref), (q2, m2_ref, l2_ref, acc2_ref)):
            s = _nt_dot(qq, k) * scale
            if masked:
                s = jnp.where(mask, s, NEG_INF)
            _online_softmax_step(s, v, m_ref, l_ref, acc_ref)

    _sweep_key_tiles(qb, tq, tk, tile)

    lam_vecs = lam_ref[...]
    dot1 = jnp.sum(lam_vecs[0:1] * lam_vecs[1:2], axis=1, keepdims=True)
    dot2 = jnp.sum(lam_vecs[2:3] * lam_vecs[3:4], axis=1, keepdims=True)
    lam = jnp.exp(dot1) - jnp.exp(dot2) + lam_init
    o = acc1_ref[...] / l1_ref[...] - lam * (acc2_ref[...] / l2_ref[...])
    y = o * lax.rsqrt(jnp.mean(o * o, axis=-1, keepdims=True) + NORM_EPS)
    o_ref[...] = ((y * gain_ref[...]) * (1.0 - lam_init)).astype(o_ref.dtype)


def diff_attention(rope_a, plain, lam_vecs, subln_gain, lam_init, seq, tq, tk):
    b = rope_a.shape[0]
    state = [pltpu.VMEM((tq, 1), F32), pltpu.VMEM((tq, 1), F32), pltpu.VMEM((tq, HEAD_DIM), F32)]
    return pl.pallas_call(
        functools.partial(_diff_kernel, tq=tq, tk=tk, lam_init=lam_init),
        grid=(b, H_DIFF, seq // tq),
        in_specs=[
            pl.BlockSpec((4, DK_DIFF), lambda bi, h, i: (0, 0)),
            pl.BlockSpec((1, HEAD_DIM), lambda bi, h, i: (0, 0)),
            pl.BlockSpec((None, tq, HEAD_DIM), lambda bi, h, i: (bi, i, h)),
            pl.BlockSpec((None, seq, HEAD_DIM), lambda bi, h, i: (bi, 0, H_DIFF + h)),
            pl.BlockSpec((None, seq, HEAD_DIM), lambda bi, h, i: (bi, 0, h)),
        ],
        out_specs=pl.BlockSpec((None, tq, HEAD_DIM), lambda bi, h, i: (bi, i, h)),
        out_shape=jax.ShapeDtypeStruct((b, seq, W_DIFF), BF16),
        scratch_shapes=state + state,
        compiler_params=_params("arbitrary", "arbitrary", "arbitrary"),
        name="diff_attention",
    )(lam_vecs, subln_gain.reshape(1, HEAD_DIM), rope_a, rope_a, plain)


def _dsa_kernel(q_ref, k_ref, v_ref, qi_ref, ki_ref, w_ref, o_ref,
                keys_ref, qm_ref, wb_ref, thr_ref, m_ref, l_ref, acc_ref, *, tq, tk, seq, n_sel):
    qb = pl.program_id(1)
    n_full = (qb * tq) // tk
    n_tiles = ((qb + 1) * tq + tk - 1) // tk
    reps = tk // LANES

    lane = lax.broadcasted_iota(jnp.int32, (tq, LANES), 1)
    w = w_ref[...]
    for h in range(H_IDX):
        chunk = qi_ref[:, (h // 2) * LANES:(h // 2 + 1) * LANES]
        keep = (lane < D_IDX) if h % 2 == 0 else (lane >= D_IDX)
        qm_ref[h] = jnp.where(keep, chunk, jnp.zeros_like(chunk))
        wb_ref[h] = jnp.broadcast_to(w[:, 8 + h:9 + h], (tq, LANES))

    def score_tile(kj, masked):
        off = pl.multiple_of(kj * tk, tk)
        kk = ki_ref[pl.ds(off, tk), :]
        score = jnp.zeros((tq, tk), F32)
        for h in range(H_IDX):
            rel = jnp.maximum(_nt_dot(qm_ref[h], kk), 0.0)
            score = score + jnp.tile(wb_ref[h], (1, reps)) * rel
        if masked:
            score = jnp.where(_causal_tile_mask(qb, kj, tq, tk), score, NEG_INF)
        bits = lax.bitcast_convert_type(score, jnp.int32)
        keys_ref[kj] = bits ^ ((bits >> 31) & 0x7FFFFFFF)

    _sweep_key_tiles(qb, tq, tk, score_tile)

    n_beyond = (seq - n_tiles * tk).astype(F32)

    def lane_fold(x):
        folded = x[:, 0:LANES]
        for r in range(1, reps):
            folded = folded + x[:, r * LANES:(r + 1) * LANES]
        return folded

    def count_ge(cand):
        def body(kj, part):
            hit = jnp.where(keys_ref[kj] >= cand, 1.0, 0.0)
            return part + lane_fold(hit)
        part = lax.fori_loop(0, n_tiles, body, jnp.zeros((tq, LANES), F32))
        return jnp.sum(part, axis=1, keepdims=True) + jnp.where(cand <= KEY_NEG_INF, n_beyond, 0.0)

    def bisect(step, thr):
        cand = thr + jnp.left_shift(jnp.int32(1), 31 - step)
        return jnp.where(count_ge(cand) >= n_sel, cand, thr)

    thr = lax.fori_loop(0, 32, bisect, jnp.full((tq, 1), INT32_MIN, jnp.int32))
    thr_ref[...] = thr

    surplus = count_ge(thr) - n_sel
    any_surplus = jnp.max(surplus) > 0.0

    @pl.when(any_surplus)
    def _():
        need = jnp.where(surplus > 0.0, n_sel - count_ge(thr + 1), float(seq + 1))

        def count_tied_below(cut):
            def body(kj, part):
                idx = kj * tk + lax.broadcasted_iota(jnp.int32, (tq, tk), 1)
                hit = jnp.where((keys_ref[kj] == thr) & (idx < cut), 1.0, 0.0)
                return part + lane_fold(hit)
            part = lax.fori_loop(0, n_tiles, body, jnp.zeros((tq, LANES), F32))
            return jnp.sum(part, axis=1, keepdims=True)

        n_bits = max(1, (seq - 1).bit_length())

        def bisect_cut(step, cut):
            cand = cut + jnp.left_shift(jnp.int32(1), n_bits - 1 - step)
            return jnp.where(count_tied_below(cand) < need, cand, cut)

        cut = lax.fori_loop(0, n_bits, bisect_cut, jnp.zeros((tq, 1), jnp.int32))

        def demote(kj, carry):
            idx = kj * tk + lax.broadcasted_iota(jnp.int32, (tq, tk), 1)
            t = keys_ref[kj]
            keys_ref[kj] = jnp.where((t == thr) & (idx > cut), thr - 1, t)
            return carry

        lax.fori_loop(0, n_tiles, demote, 0)

    scale = HEAD_DIM ** -0.5
    for h in range(H_DSA):
        cols = slice(h * HEAD_DIM, (h + 1) * HEAD_DIM)
        q = q_ref[:, cols]
        _init_softmax_state(m_ref, l_ref, acc_ref)

        def att_tile(kj, masked, q=q, cols=cols):
            off = pl.multiple_of(kj * tk, tk)
            k = k_ref[pl.ds(off, tk), cols]
            v = v_ref[pl.ds(off, tk), cols]
            sel = keys_ref[kj] >= thr_ref[...]
            if masked:
                sel = sel & _causal_tile_mask(qb, kj, tq, tk)
            s = jnp.where(sel, _nt_dot(q, k) * scale, NEG_INF)
            _online_softmax_step(s, v, m_ref, l_ref, acc_ref)

        _sweep_key_tiles(qb, tq, tk, att_tile)
        o_ref[:, cols] = (acc_ref[...] / l_ref[...]).astype(o_ref.dtype)


def dsa_attention(rope_c, plain, rope_a, wi, seq, tq, tk):
    b = rope_c.shape[0]
    n_sel = min(TOPK_MAX, seq // 4)
    once = pl.Buffered(1)
    return pl.pallas_call(
        functools.partial(_dsa_kernel, tq=tq, tk=tk, seq=seq, n_sel=n_sel),
        grid=(b, seq // tq),
        in_specs=[
            pl.BlockSpec((None, tq, W_DSA), lambda bi, i: (bi, i, 0)),
            pl.BlockSpec((None, seq, W_DSA), lambda bi, i: (bi, 0, 1), pipeline_mode=once),
            pl.BlockSpec((None, seq, W_DSA), lambda bi, i: (bi, 0, (W_DIFF + 3 * W_FOX) // W_DSA), pipeline_mode=once),
            pl.BlockSpec((None, tq, W_IDX), lambda bi, i: (bi, i, (2 * W_DIFF) // W_IDX)),
            pl.BlockSpec((None, seq, LANES), lambda bi, i: (bi, 0, (2 * W_DIFF + W_IDX) // LANES), pipeline_mode=once),
            pl.BlockSpec((None, tq, LANES), lambda bi, i: (bi, i, 0)),
        ],
        out_specs=pl.BlockSpec((None, tq, W_DSA), lambda bi, i: (bi, i, 0)),
        out_shape=jax.ShapeDtypeStruct((b, seq, W_DSA), BF16),
        scratch_shapes=[
            pltpu.VMEM((seq // tk, tq, tk), jnp.int32),
            pltpu.VMEM((H_IDX, tq, LANES), BF16),
            pltpu.VMEM((H_IDX, tq, LANES), F32),
            pltpu.VMEM((tq, 1), jnp.int32),
            pltpu.VMEM((tq, 1), F32),
            pltpu.VMEM((tq, 1), F32),
            pltpu.VMEM((tq, HEAD_DIM), F32),
        ],
        compiler_params=_params("arbitrary", "arbitrary"),
        name="dsa_attention",
    )(rope_c, rope_c, plain, rope_a, rope_a, wi)


def _merge_kernel(h_ref, oa_ref, ob_ref, oc_ref, mw0_ref, mw1_ref, mw2_ref, mb_ref, wa_ref, wb_ref, wc_ref, o_ref):
    h = h_ref[...]
    merged = None
    for i, (mw_ref, o_b_ref, w_b_ref) in enumerate(((mw0_ref, oa_ref, wa_ref), (mw1_ref, ob_ref, wb_ref),
                                                    (mw2_ref, oc_ref, wc_ref))):
        gate = jax.nn.sigmoid(_dot(h, mw_ref[...]) + mb_ref[i])
        term = gate * _dot(o_b_ref[...], w_b_ref[...])
        merged = term if merged is None else merged + term
    o_ref[...] = merged.astype(o_ref.dtype)


def merge_branches(h, oa, ob, oc, merge_w, merge_b, wa, wb, wc, seq):
    n, d = h.shape
    tm = min(512, seq)
    tn = min(512, d)
    nj = d // tn
    row = lambda width: pl.BlockSpec((tm, width), lambda i, j: (i, 0))
    col = lambda rows: pl.BlockSpec((rows, tn), lambda i, j: (0, j))
    gate_w = lambda g: pl.BlockSpec((d, tn), lambda i, j: (0, g * nj + j))
    return pl.pallas_call(
        _merge_kernel,
        grid=(n // tm, nj),
        in_specs=[
            row(d), row(oa.shape[1]), row(ob.shape[1]), row(oc.shape[1]),
            gate_w(0), gate_w(1), gate_w(2),
            pl.BlockSpec((N_BRANCH, 1, tn), lambda i, j: (0, 0, j)),
            col(wa.shape[0]), col(wb.shape[0]), col(wc.shape[0]),
        ],
        out_specs=pl.BlockSpec((tm, tn), lambda i, j: (i, j)),
        out_shape=jax.ShapeDtypeStruct((n, d), BF16),
        compiler_params=_params("arbitrary", "arbitrary"),
        name="merge_branches",
    )(h, oa, ob, oc, merge_w, merge_w, merge_w, merge_b.reshape(N_BRANCH, 1, d), wa, wb, wc)


def _rope_tables(seq, dim, group_pattern):
    inv_freq = 1.0 / (ROPE_THETA ** (jnp.arange(0, dim, 2, dtype=F32) / dim))
    ang = jnp.arange(seq, dtype=F32)[:, None] * inv_freq[None, :]
    cos, sin = jnp.cos(ang), jnp.sin(ang)
    reps = LANES // dim
    cos_t = jnp.tile(jnp.concatenate([cos, cos], axis=1), (1, reps))
    sin_t = jnp.tile(jnp.concatenate([-sin, sin], axis=1), (1, reps))
    return cos_t, sin_t


def _split_w_in(w_in):
    sizes = (W_DIFF, W_DIFF, W_DIFF, W_FOX, W_FOX, W_FOX, H_FOX, W_DSA, W_DSA, W_DSA, W_IDX, D_IDX, H_IDX)
    parts, start = [], 0
    for n in sizes:
        parts.append(w_in[:, start:start + n])
        start += n
    return parts


def kernel(x, c, w_ada, b_ada, norm_ffn1, ffn1_w1, ffn1_w3, ffn1_w2, norm_mix, w_in, b_forget, lam_q1, lam_k1, lam_q2, lam_k2, subln_gain, merge_w, merge_b, w_branch_a, w_branch_b, w_branch_c, w_out, norm_ffn2, ffn2_w1, ffn2_w3, ffn2_w2, norm_final):
    batch, seq, d = x.shape
    depth = w_ada.shape[0]
    n = batch * seq
    tq = min(512, seq)
    tq_dsa = min(256, seq)

    cos_a, sin_a = _rope_tables(seq, DK_DIFF, None)
    cos_c, sin_c = _rope_tables(seq, HEAD_DIM, None)
    mod = adaln_mod(c, w_ada, b_ada)
    xf = x.reshape(n, d)

    for l in range(depth):
        lam_init = 0.8 - 0.6 * math.exp(-0.3 * l)
        modl = mod[l].reshape(batch * N_MOD, 1, d)
        bf = lambda a: a.astype(BF16)

        h = norm_mod(xf, norm_ffn1[l], modl, 0, seq)
        u = ffn_up(h, bf(ffn1_w1[l]), bf(ffn1_w3[l]))
        xf = resid_mm(u, bf(ffn1_w2[l]), xf, modl, 2, 0.5, seq)

        h = norm_mod(xf, norm_mix[l], modl, 3, seq)
        qa, ka, va, qb, kb, vb, fb, qc, kc, vc, qi, ki, wi = _split_w_in(w_in[l])
        w_rope_a = bf(jnp.concatenate([qa, ka, qi, ki, ki], axis=1))
        w_rope_c = bf(jnp.concatenate([qc, kc], axis=1))
        w_plain = bf(jnp.concatenate([va, qb, kb, vb, vc], axis=1))
        w_small = bf(jnp.concatenate([fb, jnp.zeros((d, 8 - H_FOX), F32), wi,
                                      jnp.zeros((d, LANES - 8 - H_IDX), F32)], axis=1))
        b_forget8 = jnp.concatenate([b_forget[l], jnp.zeros((8 - H_FOX,), F32)]).reshape(8, 1)

        rope_a = proj(h, w_rope_a, cos_a, sin_a, DK_DIFF, seq).reshape(batch, seq, -1)
        rope_c = proj(h, w_rope_c, cos_c, sin_c, HEAD_DIM, seq).reshape(batch, seq, -1)
        plain = proj(h, w_plain, cos_a, sin_a, 0, seq).reshape(batch, seq, -1)
        logf_t, wi_s = small_proj(h, w_small, w_small[:, :16].T, b_forget8, seq)
        neg_cum = neg_cumsum(logf_t, batch, seq)

        lam_vecs = jnp.stack([lam_q1[l], lam_k1[l], lam_q2[l], lam_k2[l]])
        oa = diff_attention(rope_a, plain, lam_vecs, subln_gain[l], lam_init, seq, tq, tq)
        ob = fox_attention(plain, neg_cum, seq, tq, tq)
        oc = dsa_attention(rope_c, plain, rope_a, wi_s.reshape(batch, seq, LANES), seq, tq_dsa, tq_dsa)

        merged = merge_branches(h, oa.reshape(n, -1), ob.reshape(n, -1), oc.reshape(n, -1), bf(merge_w[l]),
                                merge_b[l], bf(w_branch_a[l]), bf(w_branch_b[l]), bf(w_branch_c[l]), seq)
        xf = resid_mm(merged, bf(w_out[l]), xf, modl, 5, 1.0, seq)

        h = norm_mod(xf, norm_ffn2[l], modl, 6, seq)
        u = ffn_up(h, bf(ffn2_w1[l]), bf(ffn2_w3[l]))
        xf = resid_mm(u, bf(ffn2_w2[l]), xf, modl, 8, 0.5, seq)

    return final_norm(xf, norm_final).reshape(batch, seq, d)
```

```python
import functools
import math

import numpy as np
import jax
import jax.numpy as jnp
from jax import lax
from jax.experimental import pallas as pl
from jax.experimental.pallas import tpu as pltpu

HEAD_DIM = 128
H_DIFF = 6
DK_DIFF = HEAD_DIM // 2
H_FOX = 6
H_DSA = 4
H_IDX = 8
D_IDX = 64
TOPK_MAX = 256
ROPE_THETA = 10000.0
NORM_EPS = 1e-6
N_BRANCH = 3
N_MOD = 9
NEG_INF = -1e30
IDX_W_SCALE = (H_IDX ** -0.5) * (D_IDX ** -0.5)
LOG2E = math.log2(math.e)

W_DIFF = H_DIFF * HEAD_DIM
W_FOX = H_FOX * HEAD_DIM
W_DSA = H_DSA * HEAD_DIM
W_IDX = H_IDX * D_IDX

LANES = 128
MXU_WIDTH = 256
VMEM_LIMIT = 56 * 1024 * 1024
TOKEN_TILE = 512
INT32_MIN = -(2 ** 31)

BF16 = jnp.bfloat16
F32 = jnp.float32


def _order_key_of(value):
    bits = int(np.array(value, np.float32).view(np.int32))
    return bits ^ ((bits >> 31) & 0x7FFFFFFF)


KEY_NEG_INF = _order_key_of(NEG_INF)


def _params(*semantics):
    return pltpu.CompilerParams(dimension_semantics=semantics, vmem_limit_bytes=VMEM_LIMIT)


def _nt_dot(a, b):
    return lax.dot_general(a, b, (((1,), (1,)), ((), ())), preferred_element_type=F32)


def _dot(a, b):
    return jnp.dot(a, b, preferred_element_type=F32)


def _adaln_kernel(c_ref, w_ref, b_ref, o_ref):
    c = c_ref[...]
    c_act = (c * jax.nn.sigmoid(c)).astype(BF16)
    o_ref[...] = _dot(c_act, w_ref[...].astype(BF16)) + b_ref[...]


def adaln_mod(c, w_ada, b_ada):
    depth, d, nd = w_ada.shape
    b = c.shape[0]
    tn = min(1024, d)
    return pl.pallas_call(
        _adaln_kernel,
        grid=(depth, nd // tn),
        in_specs=[
            pl.BlockSpec((b, d), lambda l, j: (0, 0)),
            pl.BlockSpec((None, d, tn), lambda l, j: (l, 0, j)),
            pl.BlockSpec((None, 1, tn), lambda l, j: (l, 0, j)),
        ],
        out_specs=pl.BlockSpec((None, b, tn), lambda l, j: (l, 0, j)),
        out_shape=jax.ShapeDtypeStruct((depth, b, nd), F32),
        compiler_params=_params("arbitrary", "arbitrary"),
        name="adaln_mod",
    )(c, w_ada, b_ada.reshape(depth, 1, nd))


def _norm_mod_kernel(x_ref, gain_ref, sc_ref, sh_ref, o_ref):
    x = x_ref[...]
    y = x * lax.rsqrt(jnp.mean(x * x, axis=-1, keepdims=True) + NORM_EPS)
    o_ref[...] = ((y * gain_ref[...]) * (1.0 + sc_ref[...]) + sh_ref[...]).astype(o_ref.dtype)


def norm_mod(x, gain, modl, i_shift, seq):
    n, d = x.shape
    tm = min(512, seq)
    return pl.pallas_call(
        _norm_mod_kernel,
        grid=(n // tm,),
        in_specs=[
            pl.BlockSpec((tm, d), lambda i: (i, 0)),
            pl.BlockSpec((1, d), lambda i: (0, 0)),
            pl.BlockSpec((None, 1, d), lambda i: ((i * tm) // seq * N_MOD + i_shift + 1, 0, 0)),
            pl.BlockSpec((None, 1, d), lambda i: ((i * tm) // seq * N_MOD + i_shift, 0, 0)),
        ],
        out_specs=pl.BlockSpec((tm, d), lambda i: (i, 0)),
        out_shape=jax.ShapeDtypeStruct((n, d), BF16),
        compiler_params=_params("arbitrary"),
        name="norm_mod",
    )(x, gain.reshape(1, d), modl, modl)


def _final_norm_kernel(x_ref, gain_ref, o_ref):
    x = x_ref[...]
    y = x * lax.rsqrt(jnp.mean(x * x, axis=-1, keepdims=True) + NORM_EPS)
    o_ref[...] = y * gain_ref[...]


def final_norm(x, gain):
    n, d = x.shape
    tm = min(512, n)
    return pl.pallas_call(
        _final_norm_kernel,
        grid=(n // tm,),
        in_specs=[pl.BlockSpec((tm, d), lambda i: (i, 0)), pl.BlockSpec((1, d), lambda i: (0, 0))],
        out_specs=pl.BlockSpec((tm, d), lambda i: (i, 0)),
        out_shape=jax.ShapeDtypeStruct((n, d), F32),
        compiler_params=_params("arbitrary"),
        name="final_norm",
    )(x, gain.reshape(1, d))


def _ffn_up_kernel(h_ref, w1_ref, w3_ref, o_ref):
    h = h_ref[...]
    a = _dot(h, w1_ref[...])
    b = _dot(h, w3_ref[...])
    o_ref[...] = ((a * jax.nn.sigmoid(a)) * b).astype(o_ref.dtype)


def ffn_up(h, w1, w3):
    n, d = h.shape
    f = w1.shape[1]
    tm = min(1024, n)
    tn = 512 if f % 512 == 0 else f
    return pl.pallas_call(
        _ffn_up_kernel,
        grid=(n // tm, f // tn),
        in_specs=[
            pl.BlockSpec((tm, d), lambda i, j: (i, 0)),
            pl.BlockSpec((d, tn), lambda i, j: (0, j)),
            pl.BlockSpec((d, tn), lambda i, j: (0, j)),
        ],
        out_specs=pl.BlockSpec((tm, tn), lambda i, j: (i, j)),
        out_shape=jax.ShapeDtypeStruct((n, f), BF16),
        compiler_params=_params("arbitrary", "arbitrary"),
        name="ffn_up",
    )(h, w1, w3)


def _resid_mm_kernel(a_ref, w_ref, x_ref, g_ref, o_ref, *, gscale):
    y = _dot(a_ref[...], w_ref[...])
    o_ref[...] = x_ref[...] + (gscale * g_ref[...]) * y


def resid_mm(a, w, x, modl, i_gate, gscale, seq):
    n, k = a.shape
    d = w.shape[1]
    tm = min(512, seq)
    tn = min(512, d)
    return pl.pallas_call(
        functools.partial(_resid_mm_kernel, gscale=gscale),
        grid=(n // tm, d // tn),
        in_specs=[
            pl.BlockSpec((tm, k), lambda i, j: (i, 0)),
            pl.BlockSpec((k, tn), lambda i, j: (0, j)),
            pl.BlockSpec((tm, tn), lambda i, j: (i, j)),
            pl.BlockSpec((None, 1, tn), lambda i, j: ((i * tm) // seq * N_MOD + i_gate, 0, j)),
        ],
        out_specs=pl.BlockSpec((tm, tn), lambda i, j: (i, j)),
        out_shape=jax.ShapeDtypeStruct((n, d), F32),
        compiler_params=_params("arbitrary", "arbitrary"),
        name="resid_mm",
    )(a, w, x, modl)


def _widest_tile(n_chunks, chunk, bytes_per_unit, limit_bytes):
    return chunk * max(t for t in range(1, n_chunks + 1)
                       if n_chunks % t == 0 and (t == 1 or t * chunk * bytes_per_unit <= limit_bytes))


def _swap_halves(z, group):
    if group == LANES:
        return pltpu.roll(z, LANES // 2, axis=1)
    half = group // 2
    lane = lax.broadcasted_iota(jnp.int32, z.shape, 1)
    from_above = pltpu.roll(z, LANES - half, axis=1)
    from_below = pltpu.roll(z, half, axis=1)
    return jnp.where((lane & (group - 1)) < half, from_above, from_below)


def _proj_kernel(h_ref, w_ref, cos_ref, sin_ref, o_ref, *, group):
    z = _dot(h_ref[...], w_ref[...])
    if group == 0:
        o_ref[...] = z.astype(o_ref.dtype)
        return
    cos = cos_ref[...]
    sin = sin_ref[...]
    for c in range(z.shape[1] // LANES):
        zc = z[:, c * LANES:(c + 1) * LANES]
        o_ref[:, c * LANES:(c + 1) * LANES] = (zc * cos + _swap_halves(zc, group) * sin).astype(o_ref.dtype)


def proj(h, w, cos, sin, group, seq):
    n, d = h.shape
    cols = w.shape[1]
    tm = min(512, seq)
    tn = _widest_tile(cols // LANES, LANES, d * 2, 9 * 2 ** 20)
    s_blocks = seq // tm
    return pl.pallas_call(
        functools.partial(_proj_kernel, group=group),
        grid=(cols // tn, n // tm),
        in_specs=[
            pl.BlockSpec((tm, d), lambda j, i: (i, 0)),
            pl.BlockSpec((d, tn), lambda j, i: (0, j)),
            pl.BlockSpec((tm, LANES), lambda j, i: (i % s_blocks, 0)),
            pl.BlockSpec((tm, LANES), lambda j, i: (i % s_blocks, 0)),
        ],
        out_specs=pl.BlockSpec((tm, tn), lambda j, i: (i, j)),
        out_shape=jax.ShapeDtypeStruct((n, cols), BF16),
        compiler_params=_params("arbitrary", "arbitrary"),
        name=f"proj_rope{group}",
    )(h, w, cos, sin)


def _proj_fm_kernel(wt_ref, h_ref, cos_ref, sin_ref, o_ref, *, group):
    zt = _nt_dot(wt_ref[...], h_ref[...])
    if group == 0:
        o_ref[...] = zt.astype(o_ref.dtype)
        return
    cos = cos_ref[...]
    sin = sin_ref[...]
    half = group // 2
    for c in range(zt.shape[0] // group):
        blk = zt[c * group:(c + 1) * group]
        swapped = jnp.concatenate([blk[half:], blk[:half]], axis=0)
        o_ref[c * group:(c + 1) * group, :] = (blk * cos + swapped * sin).astype(o_ref.dtype)


def proj_fm(h, wt, cos_fm, sin_fm, group, seq):
    n, d = h.shape
    rows = wt.shape[0]
    tm = min(TOKEN_TILE, seq)
    unit = max(group, LANES)
    tn = _widest_tile(rows // unit, unit, d * 2, 6 * 2 ** 20)
    s_blocks = seq // tm
    g = max(group, 8)
    return pl.pallas_call(
        functools.partial(_proj_fm_kernel, group=group),
        grid=(rows // tn, n // tm),
        in_specs=[
            pl.BlockSpec((tn, d), lambda j, i: (j, 0)),
            pl.BlockSpec((tm, d), lambda j, i: (i, 0)),
            pl.BlockSpec((g, tm), lambda j, i: (0, i % s_blocks)),
            pl.BlockSpec((g, tm), lambda j, i: (0, i % s_blocks)),
        ],
        out_specs=pl.BlockSpec((None, tn, tm), lambda j, i: (i, j, 0)),
        out_shape=jax.ShapeDtypeStruct((n // tm, rows, tm), BF16),
        compiler_params=_params("arbitrary", "arbitrary"),
        name=f"proj_fm_rope{group}",
    )(wt, h, cos_fm, sin_fm)


def _small_proj_kernel(h_ref, wt_ref, bf_ref, logf_ref, wi_ref):
    zt = _nt_dot(wt_ref[...], h_ref[...])
    logf_ref[...] = jax.nn.log_sigmoid(zt[0:8, :] + bf_ref[...])
    wi_ref[...] = zt[8:16, :] * IDX_W_SCALE


def small_proj(h, w_small_t, b_forget8, seq):
    n, d = h.shape
    tm = min(512, seq)
    return pl.pallas_call(
        _small_proj_kernel,
        grid=(n // tm,),
        in_specs=[
            pl.BlockSpec((tm, d), lambda i: (i, 0)),
            pl.BlockSpec((16, d), lambda i: (0, 0)),
            pl.BlockSpec((8, 1), lambda i: (0, 0)),
        ],
        out_specs=[pl.BlockSpec((8, tm), lambda i: (0, i)), pl.BlockSpec((8, tm), lambda i: (0, i))],
        out_shape=[jax.ShapeDtypeStruct((8, n), F32), jax.ShapeDtypeStruct((8, n), F32)],
        compiler_params=_params("arbitrary"),
        name="small_proj",
    )(h, w_small_t, b_forget8)


def _split3(x):
    x1 = x.astype(BF16)
    r1 = x - x1.astype(F32)
    x2 = r1.astype(BF16)
    x3 = (r1 - x2.astype(F32)).astype(BF16)
    return x1, x2, x3


def _neg_cumsum_kernel(x_ref, o_ref):
    x = x_ref[...]
    chunks = x.shape[0]
    r = lax.broadcasted_iota(jnp.int32, (LANES, LANES), 0)
    c = lax.broadcasted_iota(jnp.int32, (LANES, LANES), 1)
    upper = (r <= c).astype(BF16)
    within = sum(_dot(p, upper) for p in _split3(x))
    totals = jnp.broadcast_to(within[:, LANES - 1:LANES], (chunks, LANES))
    rr = lax.broadcasted_iota(jnp.int32, (chunks, chunks), 0)
    cc = lax.broadcasted_iota(jnp.int32, (chunks, chunks), 1)
    strict_lower = (cc < rr).astype(BF16)
    offset = sum(_dot(strict_lower, p) for p in _split3(totals))
    o_ref[...] = -(within + offset) * LOG2E


def neg_cumsum(logf_t, batch, seq):
    rows = logf_t.shape[0] * batch
    chunks = seq // LANES
    x = logf_t.reshape(rows, chunks, LANES)
    out = pl.pallas_call(
        _neg_cumsum_kernel,
        grid=(rows,),
        in_specs=[pl.BlockSpec((None, chunks, LANES), lambda i: (i, 0, 0))],
        out_specs=pl.BlockSpec((None, chunks, LANES), lambda i: (i, 0, 0)),
        out_shape=jax.ShapeDtypeStruct((rows, chunks, LANES), F32),
        compiler_params=_params("arbitrary"),
        name="neg_cumsum",
    )(x)
    return out.reshape(logf_t.shape[0], batch, seq)


def _query_streams(tq):
    width = min(MXU_WIDTH, tq)
    return [slice(lo, lo + width) for lo in range(0, tq, width)]


def _softmax_update(s, vt, state):
    m_prev, l_prev, acc_prev = state
    m_new = jnp.maximum(m_prev, jnp.max(s, axis=0, keepdims=True))
    alpha = jnp.exp2(m_prev - m_new)
    p = jnp.exp2(s - m_new)
    l_new = alpha * l_prev + jnp.sum(p, axis=0, keepdims=True)
    return m_new, l_new, alpha * acc_prev + _dot(vt, p.astype(vt.dtype))


def _run_streams(streams, logits_fn):
    prev = [tuple(r[:, qs] for r in refs) for refs, qs, _ in streams]
    scores = [logits_fn(i) for i in range(len(streams))]
    new = [_softmax_update(scores[i], vt, prev[i]) for i, (_, _, vt) in enumerate(streams)]
    for (refs, qs, _), vals in zip(streams, new):
        for r, v in zip(refs, vals):
            r[:, qs] = v


def _init_softmax_state(m_ref, l_ref, acc_ref):
    m_ref[...] = jnp.full(m_ref.shape, -jnp.inf, F32)
    l_ref[...] = jnp.zeros(l_ref.shape, F32)
    acc_ref[...] = jnp.zeros(acc_ref.shape, F32)


def _causal_mask(key0, query0, n_keys, n_queries):
    key = key0 + lax.broadcasted_iota(jnp.int32, (n_keys, n_queries), 0)
    query = query0 + lax.broadcasted_iota(jnp.int32, (n_keys, n_queries), 1)
    return key <= query


def _sweep_key_tiles(q_block, tq, tk, tile_fn):
    n_full = (q_block * tq) // tk
    n_tiles = ((q_block + 1) * tq + tk - 1) // tk

    def run(masked):
        def body(kj, carry):
            tile_fn(kj, masked)
            return carry
        return body

    lax.fori_loop(0, n_full, run(False), 0)
    lax.fori_loop(n_full, n_tiles, run(True), 0)


def _softmax_scratch(tq):
    return [pltpu.VMEM((1, tq), F32), pltpu.VMEM((1, tq), F32), pltpu.VMEM((HEAD_DIM, tq), F32)]


def _fox_kernel(qt_ref, k_ref, vt_ref, bias_ref, o_ref, m_ref, l_ref, acc_ref, *, tq, tk):
    qb = pl.program_id(2)
    c = (HEAD_DIM ** -0.5) * LOG2E
    _init_softmax_state(m_ref, l_ref, acc_ref)

    def tile(kj, masked):
        off = pl.multiple_of(kj * tk, tk)
        k = k_ref[pl.ds(off, tk), :]
        vt = vt_ref[kj]
        bias = bias_ref[kj]
        slices = _query_streams(tq)

        def logits(i):
            qs = slices[i]
            width = qs.stop - qs.start
            s = _dot(k, qt_ref[:, qs]) * c + jnp.tile(bias, (1, width // LANES))
            if masked:
                s = jnp.where(_causal_mask(kj * tk, qb * tq + qs.start, tk, width), s, NEG_INF)
            return s

        _run_streams([((m_ref, l_ref, acc_ref), qs, vt) for qs in slices], logits)

    _sweep_key_tiles(qb, tq, tk, tile)
    o_ref[...] = (acc_ref[...] / l_ref[...]).T.astype(o_ref.dtype)


def fox_attention(q_fm, q_blk0, k_rows, v_fm, v_blk0, bias, seq):
    b = k_rows.shape[0]
    tq = tk = q_fm.shape[2]
    nq = seq // tq
    return pl.pallas_call(
        functools.partial(_fox_kernel, tq=tq, tk=tk),
        grid=(b, H_FOX, nq),
        in_specs=[
            pl.BlockSpec((None, HEAD_DIM, tq), lambda bi, h, i: (bi * nq + i, q_blk0 + h, 0)),
            pl.BlockSpec((None, seq, HEAD_DIM), lambda bi, h, i: (bi, 0, h)),
            pl.BlockSpec((nq, HEAD_DIM, tk), lambda bi, h, i: (bi, v_blk0 + h, 0)),
            pl.BlockSpec((None, None, nq, tk, LANES), lambda bi, h, i: (h, bi, 0, 0, 0)),
        ],
        out_specs=pl.BlockSpec((None, tq, HEAD_DIM), lambda bi, h, i: (bi, i, h)),
        out_shape=jax.ShapeDtypeStruct((b, seq, W_FOX), BF16),
        scratch_shapes=_softmax_scratch(tq),
        compiler_params=_params("arbitrary", "arbitrary", "arbitrary"),
        name="fox_attention",
    )(q_fm, k_rows, v_fm, bias)


def _diff_kernel(lam_ref, gain_ref, qt_ref, k_ref, vt_ref, o_ref,
                 m1_ref, l1_ref, acc1_ref, m2_ref, l2_ref, acc2_ref, q1_ref, q2_ref, *, tq, tk, lam_init):
    qb = pl.program_id(2)
    c = (DK_DIFF ** -0.5) * LOG2E
    zeros = jnp.zeros((DK_DIFF, tq), BF16)
    q1_ref[...] = jnp.concatenate([qt_ref[0:DK_DIFF, :], zeros], axis=0)
    q2_ref[...] = jnp.concatenate([zeros, qt_ref[DK_DIFF:HEAD_DIM, :]], axis=0)
    _init_softmax_state(m1_ref, l1_ref, acc1_ref)
    _init_softmax_state(m2_ref, l2_ref, acc2_ref)

    def tile(kj, masked):
        off = pl.multiple_of(kj * tk, tk)
        k = k_ref[pl.ds(off, tk), :]
        vt = vt_ref[kj]
        streams = [(refs, qs, vt) for qs in _query_streams(tq)
                   for refs in ((m1_ref, l1_ref, acc1_ref), (m2_ref, l2_ref, acc2_ref))]
        q_refs = [q_ref for _ in _query_streams(tq) for q_ref in (q1_ref, q2_ref)]

        def logits(i):
            qs = streams[i][1]
            s = _dot(k, q_refs[i][:, qs]) * c
            if masked:
                s = jnp.where(_causal_mask(kj * tk, qb * tq + qs.start, tk, qs.stop - qs.start), s, NEG_INF)
            return s

        _run_streams(streams, logits)

    _sweep_key_tiles(qb, tq, tk, tile)

    lam_vecs = lam_ref[...]
    dot1 = jnp.sum(lam_vecs[0:1] * lam_vecs[1:2], axis=1, keepdims=True)
    dot2 = jnp.sum(lam_vecs[2:3] * lam_vecs[3:4], axis=1, keepdims=True)
    lam = jnp.exp(dot1) - jnp.exp(dot2) + lam_init
    o = (acc1_ref[...] / l1_ref[...] - lam * (acc2_ref[...] / l2_ref[...])).T
    y = o * lax.rsqrt(jnp.mean(o * o, axis=-1, keepdims=True) + NORM_EPS)
    o_ref[...] = ((y * gain_ref[...]) * (1.0 - lam_init)).astype(o_ref.dtype)


def diff_attention(q_fm, q_blk0, k_rows, v_fm, v_blk0, lam_vecs, subln_gain, lam_init, seq):
    b = k_rows.shape[0]
    tq = tk = q_fm.shape[2]
    nq = seq // tq
    return pl.pallas_call(
        functools.partial(_diff_kernel, tq=tq, tk=tk, lam_init=lam_init),
        grid=(b, H_DIFF, nq),
        in_specs=[
            pl.BlockSpec((4, DK_DIFF), lambda bi, h, i: (0, 0)),
            pl.BlockSpec((1, HEAD_DIM), lambda bi, h, i: (0, 0)),
            pl.BlockSpec((None, HEAD_DIM, tq), lambda bi, h, i: (bi * nq + i, q_blk0 + h, 0)),
            pl.BlockSpec((None, seq, HEAD_DIM), lambda bi, h, i: (bi, 0, h)),
            pl.BlockSpec((nq, HEAD_DIM, tk), lambda bi, h, i: (bi, v_blk0 + h, 0)),
        ],
        out_specs=pl.BlockSpec((None, tq, HEAD_DIM), lambda bi, h, i: (bi, i, h)),
        out_shape=jax.ShapeDtypeStruct((b, seq, W_DIFF), BF16),
        scratch_shapes=_softmax_scratch(tq) + _softmax_scratch(tq)
        + [pltpu.VMEM((HEAD_DIM, tq), BF16), pltpu.VMEM((HEAD_DIM, tq), BF16)],
        compiler_params=_params("arbitrary", "arbitrary", "arbitrary"),
        name="diff_attention",
    )(lam_vecs, subln_gain.reshape(1, HEAD_DIM), q_fm, k_rows, v_fm)


def _dsa_kernel(qt_ref, k_ref, vt_ref, qit_ref, ki_ref, wt_ref, o_ref,
                keys_ref, qm_ref, thr_ref, m_ref, l_ref, acc_ref, m2_ref, l2_ref, acc2_ref,
                *, tq, tk, seq, n_sel):
    qb = pl.program_id(1)
    n_full = (qb * tq) // tk
    n_tiles = ((qb + 1) * tq + tk - 1) // tk

    zeros = jnp.zeros((D_IDX, tq), BF16)
    for h in range(H_IDX):
        qm_ref[h] = jnp.concatenate([qit_ref[h * D_IDX:(h + 1) * D_IDX, :], zeros], axis=0)

    def score_tile(kj, masked):
        off = pl.multiple_of(kj * tk, tk)
        kk = ki_ref[pl.ds(off, tk), :]
        w = wt_ref[...]
        score = jnp.zeros((tk, tq), F32)
        for h in range(H_IDX):
            score = score + w[h:h + 1, :] * jnp.maximum(_dot(kk, qm_ref[h]), 0.0)
        if masked:
            score = jnp.where(_causal_mask(kj * tk, qb * tq, tk, tq), score, NEG_INF)
        bits = lax.bitcast_convert_type(score, jnp.int32)
        keys_ref[kj] = bits ^ ((bits >> 31) & 0x7FFFFFFF)

    _sweep_key_tiles(qb, tq, tk, score_tile)

    n_beyond = (seq - n_tiles * tk).astype(F32)

    def count_ge(cand):
        def body(kj, part):
            return part + jnp.sum(jnp.where(keys_ref[kj] >= cand, 1.0, 0.0), axis=0, keepdims=True)
        part = lax.fori_loop(0, n_tiles, body, jnp.zeros((1, tq), F32))
        return part + jnp.where(cand <= KEY_NEG_INF, n_beyond, 0.0)

    def bisect(step, thr):
        cand = thr + jnp.left_shift(jnp.int32(1), 31 - step)
        return jnp.where(count_ge(cand) >= n_sel, cand, thr)

    thr = lax.fori_loop(0, 32, bisect, jnp.full((1, tq), INT32_MIN, jnp.int32))
    thr_ref[...] = thr

    surplus = count_ge(thr) - n_sel
    any_surplus = jnp.max(surplus) > 0.0

    @pl.when(any_surplus)
    def _():
        need = jnp.where(surplus > 0.0, n_sel - count_ge(thr + 1), float(seq + 1))

        def count_tied_below(cut):
            def body(kj, part):
                idx = kj * tk + lax.broadcasted_iota(jnp.int32, (tk, tq), 0)
                hit = jnp.where((keys_ref[kj] == thr) & (idx < cut), 1.0, 0.0)
                return part + jnp.sum(hit, axis=0, keepdims=True)
            return lax.fori_loop(0, n_tiles, body, jnp.zeros((1, tq), F32))

        n_bits = max(1, (seq - 1).bit_length())

        def bisect_cut(step, cut):
            cand = cut + jnp.left_shift(jnp.int32(1), n_bits - 1 - step)
            return jnp.where(count_tied_below(cand) < need, cand, cut)

        cut = lax.fori_loop(0, n_bits, bisect_cut, jnp.zeros((1, tq), jnp.int32))

        def demote(kj, carry):
            idx = kj * tk + lax.broadcasted_iota(jnp.int32, (tk, tq), 0)
            t = keys_ref[kj]
            keys_ref[kj] = jnp.where((t == thr) & (idx > cut), thr - 1, t)
            return carry

        lax.fori_loop(0, n_tiles, demote, 0)

    c = (HEAD_DIM ** -0.5) * LOG2E
    states = ((m_ref, l_ref, acc_ref), (m2_ref, l2_ref, acc2_ref))
    qs = slice(0, tq)
    for h0 in range(0, H_DSA, 2):
        for st in states:
            _init_softmax_state(*st)

        def att_tile(kj, masked, h0=h0):
            off = pl.multiple_of(kj * tk, tk)
            sel = keys_ref[kj] >= thr_ref[...]
            if masked:
                sel = sel & _causal_mask(kj * tk, qb * tq, tk, tq)
            head_rows = [slice(h * HEAD_DIM, (h + 1) * HEAD_DIM) for h in (h0, h0 + 1)]

            def logits(i):
                k = k_ref[pl.ds(off, tk), head_rows[i]]
                return jnp.where(sel, _dot(k, qt_ref[head_rows[i], :]) * c, NEG_INF)

            _run_streams([(st, qs, vt_ref[kj, rows, :]) for st, rows in zip(states, head_rows)], logits)

        _sweep_key_tiles(qb, tq, tk, att_tile)
        for (m_r, l_r, acc_r), h in zip(states, (h0, h0 + 1)):
            o_ref[:, h * HEAD_DIM:(h + 1) * HEAD_DIM] = (acc_r[...] / l_r[...]).T.astype(o_ref.dtype)


def dsa_attention(q_fm, k_rows, v_fm, qi_fm, ki_rows, ki_blk, wi_t, seq, tq):
    b = k_rows.shape[0]
    tk = q_fm.shape[2]
    nk = seq // tk
    nq = seq // tq
    sub = tk // tq
    n_sel = min(TOPK_MAX, seq // 4)
    once = pl.Buffered(1)
    fm_q = lambda rows: pl.BlockSpec((None, rows, tq), lambda bi, i: (bi * nk + i // sub, 0, i % sub))
    return pl.pallas_call(
        functools.partial(_dsa_kernel, tq=tq, tk=tk, seq=seq, n_sel=n_sel),
        grid=(b, nq),
        in_specs=[
            fm_q(W_DSA),
            pl.BlockSpec((None, seq, W_DSA), lambda bi, i: (bi, 0, 0), pipeline_mode=once),
            pl.BlockSpec((nk, W_DSA, tk), lambda bi, i: (bi, 0, 0), pipeline_mode=once),
            fm_q(W_IDX),
            pl.BlockSpec((None, seq, LANES), lambda bi, i: (bi, 0, ki_blk), pipeline_mode=once),
            pl.BlockSpec((H_IDX, tq), lambda bi, i: (0, bi * nq + i)),
        ],
        out_specs=pl.BlockSpec((None, tq, W_DSA), lambda bi, i: (bi, i, 0)),
        out_shape=jax.ShapeDtypeStruct((b, seq, W_DSA), BF16),
        scratch_shapes=[
            pltpu.VMEM((nk, tk, tq), jnp.int32),
            pltpu.VMEM((H_IDX, 2 * D_IDX, tq), BF16),
            pltpu.VMEM((1, tq), jnp.int32),
        ] + _softmax_scratch(tq) + _softmax_scratch(tq),
        compiler_params=_params("arbitrary", "arbitrary"),
        name="dsa_attention",
    )(q_fm, k_rows, v_fm, qi_fm, ki_rows, wi_t)


def _merge_kernel(h_ref, oa_ref, ob_ref, oc_ref, mw0_ref, mw1_ref, mw2_ref, mb_ref, wa_ref, wb_ref, wc_ref, o_ref):
    h = h_ref[...]
    merged = None
    for i, (mw_ref, o_b_ref, w_b_ref) in enumerate(((mw0_ref, oa_ref, wa_ref), (mw1_ref, ob_ref, wb_ref),
                                                    (mw2_ref, oc_ref, wc_ref))):
        gate = jax.nn.sigmoid(_dot(h, mw_ref[...]) + mb_ref[i])
        term = gate * _dot(o_b_ref[...], w_b_ref[...])
        merged = term if merged is None else merged + term
    o_ref[...] = merged.astype(o_ref.dtype)


def merge_branches(h, oa, ob, oc, merge_w, merge_b, wa, wb, wc, seq):
    n, d = h.shape
    tm = min(512, seq)
    tn = min(512, d)
    nj = d // tn
    row = lambda width: pl.BlockSpec((tm, width), lambda i, j: (i, 0))
    col = lambda rows: pl.BlockSpec((rows, tn), lambda i, j: (0, j))
    gate_w = lambda g: pl.BlockSpec((d, tn), lambda i, j: (0, g * nj + j))
    return pl.pallas_call(
        _merge_kernel,
        grid=(n // tm, nj),
        in_specs=[
            row(d), row(oa.shape[1]), row(ob.shape[1]), row(oc.shape[1]),
            gate_w(0), gate_w(1), gate_w(2),
            pl.BlockSpec((N_BRANCH, 1, tn), lambda i, j: (0, 0, j)),
            col(wa.shape[0]), col(wb.shape[0]), col(wc.shape[0]),
        ],
        out_specs=pl.BlockSpec((tm, tn), lambda i, j: (i, j)),
        out_shape=jax.ShapeDtypeStruct((n, d), BF16),
        compiler_params=_params("arbitrary", "arbitrary"),
        name="merge_branches",
    )(h, oa, ob, oc, merge_w, merge_w, merge_w, merge_b.reshape(N_BRANCH, 1, d), wa, wb, wc)


def _rope_tables(seq, dim):
    inv_freq = 1.0 / (ROPE_THETA ** (jnp.arange(0, dim, 2, dtype=F32) / dim))
    ang = jnp.arange(seq, dtype=F32)[:, None] * inv_freq[None, :]
    cos, sin = jnp.cos(ang), jnp.sin(ang)
    cos_g = jnp.concatenate([cos, cos], axis=1)
    sin_g = jnp.concatenate([-sin, sin], axis=1)
    reps = LANES // dim
    return jnp.tile(cos_g, (1, reps)), jnp.tile(sin_g, (1, reps)), cos_g.T, sin_g.T


def _split_w_in(w_in):
    sizes = (W_DIFF, W_DIFF, W_DIFF, W_FOX, W_FOX, W_FOX, H_FOX, W_DSA, W_DSA, W_DSA, W_IDX, D_IDX, H_IDX)
    parts, start = [], 0
    for n in sizes:
        parts.append(w_in[:, start:start + n])
        start += n
    return parts


def kernel(x, c, w_ada, b_ada, norm_ffn1, ffn1_w1, ffn1_w3, ffn1_w2, norm_mix, w_in, b_forget, lam_q1, lam_k1, lam_q2, lam_k2, subln_gain, merge_w, merge_b, w_branch_a, w_branch_b, w_branch_c, w_out, norm_ffn2, ffn2_w1, ffn2_w3, ffn2_w2, norm_final):
    batch, seq, d = x.shape
    depth = w_ada.shape[0]
    n = batch * seq
    tk = min(TOKEN_TILE, seq)
    tq_dsa = min(256, seq)
    bf = lambda a: a.astype(BF16)

    cos64, sin64, cos64_fm, sin64_fm = _rope_tables(seq, DK_DIFF)
    cos128, sin128, cos128_fm, sin128_fm = _rope_tables(seq, HEAD_DIM)
    mod = adaln_mod(c, w_ada, b_ada)
    xf = x.reshape(n, d)

    for l in range(depth):
        lam_init = 0.8 - 0.6 * math.exp(-0.3 * l)
        modl = mod[l].reshape(batch * N_MOD, 1, d)

        h = norm_mod(xf, norm_ffn1[l], modl, 0, seq)
        u = ffn_up(h, bf(ffn1_w1[l]), bf(ffn1_w3[l]))
        xf = resid_mm(u, bf(ffn1_w2[l]), xf, modl, 2, 0.5, seq)

        h = norm_mod(xf, norm_mix[l], modl, 3, seq)
        qa, ka, va, qb, kb, vb, fb, qc, kc, vc, qi, ki, wi = _split_w_in(w_in[l])
        k64 = proj(h, bf(jnp.concatenate([ka, ki, ki], axis=1)), cos64, sin64, DK_DIFF, seq).reshape(batch, seq, -1)
        k128 = proj(h, bf(kc), cos128, sin128, HEAD_DIM, seq).reshape(batch, seq, -1)
        k0 = proj(h, bf(kb), cos64, sin64, 0, seq).reshape(batch, seq, -1)
        q64_fm = proj_fm(h, bf(jnp.concatenate([qi, qa], axis=1).T), cos64_fm, sin64_fm, DK_DIFF, seq)
        q128_fm = proj_fm(h, bf(qc.T), cos128_fm, sin128_fm, HEAD_DIM, seq)
        p0_fm = proj_fm(h, bf(jnp.concatenate([vc, qb, va, vb], axis=1).T), cos64_fm, sin64_fm, 0, seq)
        blk = lambda rows: rows // HEAD_DIM
        qa_blk, qb_blk, va_blk, vb_blk = blk(W_IDX), blk(W_DSA), blk(W_DSA + W_FOX), blk(W_DSA + W_FOX + W_DIFF)

        w_small_t = bf(jnp.concatenate([fb, jnp.zeros((d, 8 - H_FOX), F32), wi], axis=1).T)
        b_forget8 = jnp.concatenate([b_forget[l], jnp.zeros((8 - H_FOX,), F32)]).reshape(8, 1)
        logf_t, wi_t = small_proj(h, w_small_t, b_forget8, seq)
        neg_cum = neg_cumsum(logf_t, batch, seq)
        bias = jnp.broadcast_to(neg_cum.reshape(8, batch, seq // tk, tk, 1), (8, batch, seq // tk, tk, LANES))

        lam_vecs = jnp.stack([lam_q1[l], lam_k1[l], lam_q2[l], lam_k2[l]])
        oa = diff_attention(q64_fm, qa_blk, k64, p0_fm, va_blk, lam_vecs, subln_gain[l], lam_init, seq)
        ob = fox_attention(p0_fm, qb_blk, k0, p0_fm, vb_blk, bias, seq)
        oc = dsa_attention(q128_fm, k128, p0_fm, q64_fm, k64, blk(W_DIFF), wi_t, seq, tq_dsa)

        merged = merge_branches(h, oa.reshape(n, -1), ob.reshape(n, -1), oc.reshape(n, -1), bf(merge_w[l]),
                                merge_b[l], bf(w_branch_a[l]), bf(w_branch_b[l]), bf(w_branch_c[l]), seq)
        xf = resid_mm(merged, bf(w_out[l]), xf, modl, 5, 1.0, seq)

        h = norm_mod(xf, norm_ffn2[l], modl, 6, seq)
        u = ffn_up(h, bf(ffn2_w1[l]), bf(ffn2_w3[l]))
        xf = resid_mm(u, bf(ffn2_w2[l]), xf, modl, 8, 0.5, seq)

    return final_norm(xf, norm_final).reshape(batch, seq, d)
```

```python
import functools
import math

import numpy as np
import jax
import jax.numpy as jnp
from jax import lax
from jax.experimental import pallas as pl
from jax.experimental.pallas import tpu as pltpu

HEAD_DIM = 128
H_DIFF = 6
DK_DIFF = HEAD_DIM // 2
H_FOX = 6
H_DSA = 4
H_IDX = 8
D_IDX = 64
TOPK_MAX = 256
ROPE_THETA = 10000.0
NORM_EPS = 1e-6
N_BRANCH = 3
N_MOD = 9
NEG_INF = -1e30
IDX_W_SCALE = (H_IDX ** -0.5) * (D_IDX ** -0.5)
LOG2E = math.log2(math.e)

W_DIFF = H_DIFF * HEAD_DIM
W_FOX = H_FOX * HEAD_DIM
W_DSA = H_DSA * HEAD_DIM
W_IDX = H_IDX * D_IDX

LANES = 128
MXU_WIDTH = 256
VMEM_LIMIT = 56 * 1024 * 1024
TOKEN_TILE = 512
INT32_MIN = -(2 ** 31)

BF16 = jnp.bfloat16
F32 = jnp.float32


def _order_key_of(value):
    bits = int(np.array(value, np.float32).view(np.int32))
    return bits ^ ((bits >> 31) & 0x7FFFFFFF)


KEY_NEG_INF = _order_key_of(NEG_INF)


def _params(*semantics):
    return pltpu.CompilerParams(dimension_semantics=semantics, vmem_limit_bytes=VMEM_LIMIT)


def _nt_dot(a, b):
    return lax.dot_general(a, b, (((1,), (1,)), ((), ())), preferred_element_type=F32)


def _dot(a, b):
    return jnp.dot(a, b, preferred_element_type=F32)


def _adaln_kernel(c_ref, w_ref, b_ref, o_ref):
    c = c_ref[...]
    c_act = (c * jax.nn.sigmoid(c)).astype(BF16)
    o_ref[...] = _dot(c_act, w_ref[...].astype(BF16)) + b_ref[...]


def adaln_mod(c, w_ada, b_ada):
    depth, d, nd = w_ada.shape
    b = c.shape[0]
    tn = min(1024, d)
    return pl.pallas_call(
        _adaln_kernel,
        grid=(depth, nd // tn),
        in_specs=[
            pl.BlockSpec((b, d), lambda l, j: (0, 0)),
            pl.BlockSpec((None, d, tn), lambda l, j: (l, 0, j)),
            pl.BlockSpec((None, 1, tn), lambda l, j: (l, 0, j)),
        ],
        out_specs=pl.BlockSpec((None, b, tn), lambda l, j: (l, 0, j)),
        out_shape=jax.ShapeDtypeStruct((depth, b, nd), F32),
        compiler_params=_params("arbitrary", "arbitrary"),
        name="adaln_mod",
    )(c, w_ada, b_ada.reshape(depth, 1, nd))


def _norm_mod_kernel(x_ref, gain_ref, sc_ref, sh_ref, o_ref):
    x = x_ref[...]
    y = x * lax.rsqrt(jnp.mean(x * x, axis=-1, keepdims=True) + NORM_EPS)
    o_ref[...] = ((y * gain_ref[...]) * (1.0 + sc_ref[...]) + sh_ref[...]).astype(o_ref.dtype)


def norm_mod(x, gain, modl, i_shift, seq):
    n, d = x.shape
    tm = min(512, seq)
    return pl.pallas_call(
        _norm_mod_kernel,
        grid=(n // tm,),
        in_specs=[
            pl.BlockSpec((tm, d), lambda i: (i, 0)),
            pl.BlockSpec((1, d), lambda i: (0, 0)),
            pl.BlockSpec((None, 1, d), lambda i: ((i * tm) // seq * N_MOD + i_shift + 1, 0, 0)),
            pl.BlockSpec((None, 1, d), lambda i: ((i * tm) // seq * N_MOD + i_shift, 0, 0)),
        ],
        out_specs=pl.BlockSpec((tm, d), lambda i: (i, 0)),
        out_shape=jax.ShapeDtypeStruct((n, d), BF16),
        compiler_params=_params("arbitrary"),
        name="norm_mod",
    )(x, gain.reshape(1, d), modl, modl)


def _final_norm_kernel(x_ref, gain_ref, o_ref):
    x = x_ref[...]
    y = x * lax.rsqrt(jnp.mean(x * x, axis=-1, keepdims=True) + NORM_EPS)
    o_ref[...] = y * gain_ref[...]


def final_norm(x, gain):
    n, d = x.shape
    tm = min(512, n)
    return pl.pallas_call(
        _final_norm_kernel,
        grid=(n // tm,),
        in_specs=[pl.BlockSpec((tm, d), lambda i: (i, 0)), pl.BlockSpec((1, d), lambda i: (0, 0))],
        out_specs=pl.BlockSpec((tm, d), lambda i: (i, 0)),
        out_shape=jax.ShapeDtypeStruct((n, d), F32),
        compiler_params=_params("arbitrary"),
        name="final_norm",
    )(x, gain.reshape(1, d))


def _ffn_up_kernel(h_ref, w1_ref, w3_ref, o_ref):
    h = h_ref[...]
    a = _dot(h, w1_ref[...])
    b = _dot(h, w3_ref[...])
    o_ref[...] = ((a * jax.nn.sigmoid(a)) * b).astype(o_ref.dtype)


def ffn_up(h, w1, w3):
    n, d = h.shape
    f = w1.shape[1]
    tm = min(1024, n)
    tn = 512 if f % 512 == 0 else f
    return pl.pallas_call(
        _ffn_up_kernel,
        grid=(n // tm, f // tn),
        in_specs=[
            pl.BlockSpec((tm, d), lambda i, j: (i, 0)),
            pl.BlockSpec((d, tn), lambda i, j: (0, j)),
            pl.BlockSpec((d, tn), lambda i, j: (0, j)),
        ],
        out_specs=pl.BlockSpec((tm, tn), lambda i, j: (i, j)),
        out_shape=jax.ShapeDtypeStruct((n, f), BF16),
        compiler_params=_params("arbitrary", "arbitrary"),
        name="ffn_up",
    )(h, w1, w3)


def _resid_mm_kernel(a_ref, w_ref, x_ref, g_ref, o_ref, *, gscale):
    y = _dot(a_ref[...], w_ref[...])
    o_ref[...] = x_ref[...] + (gscale * g_ref[...]) * y


def resid_mm(a, w, x, modl, i_gate, gscale, seq):
    n, k = a.shape
    d = w.shape[1]
    tm = min(512, seq)
    tn = min(512, d)
    return pl.pallas_call(
        functools.partial(_resid_mm_kernel, gscale=gscale),
        grid=(n // tm, d // tn),
        in_specs=[
            pl.BlockSpec((tm, k), lambda i, j: (i, 0)),
            pl.BlockSpec((k, tn), lambda i, j: (0, j)),
            pl.BlockSpec((tm, tn), lambda i, j: (i, j)),
            pl.BlockSpec((None, 1, tn), lambda i, j: ((i * tm) // seq * N_MOD + i_gate, 0, j)),
        ],
        out_specs=pl.BlockSpec((tm, tn), lambda i, j: (i, j)),
        out_shape=jax.ShapeDtypeStruct((n, d), F32),
        compiler_params=_params("arbitrary", "arbitrary"),
        name="resid_mm",
    )(a, w, x, modl)


def _widest_tile(n_chunks, chunk, bytes_per_unit, limit_bytes):
    return chunk * max(t for t in range(1, n_chunks + 1)
                       if n_chunks % t == 0 and (t == 1 or t * chunk * bytes_per_unit <= limit_bytes))


def _swap_halves(z, group):
    if group == LANES:
        return pltpu.roll(z, LANES // 2, axis=1)
    half = group // 2
    lane = lax.broadcasted_iota(jnp.int32, z.shape, 1)
    from_above = pltpu.roll(z, LANES - half, axis=1)
    from_below = pltpu.roll(z, half, axis=1)
    return jnp.where((lane & (group - 1)) < half, from_above, from_below)


def _proj_kernel(h_ref, w_ref, cos_ref, sin_ref, o_ref, *, group):
    z = _dot(h_ref[...], w_ref[...])
    if group == 0:
        o_ref[...] = z.astype(o_ref.dtype)
        return
    cos = cos_ref[...]
    sin = sin_ref[...]
    for c in range(z.shape[1] // LANES):
        zc = z[:, c * LANES:(c + 1) * LANES]
        o_ref[:, c * LANES:(c + 1) * LANES] = (zc * cos + _swap_halves(zc, group) * sin).astype(o_ref.dtype)


def proj(h, w, cos, sin, group, seq):
    n, d = h.shape
    cols = w.shape[1]
    tm = min(512, seq)
    tn = _widest_tile(cols // LANES, LANES, d * 2, 9 * 2 ** 20)
    s_blocks = seq // tm
    return pl.pallas_call(
        functools.partial(_proj_kernel, group=group),
        grid=(cols // tn, n // tm),
        in_specs=[
            pl.BlockSpec((tm, d), lambda j, i: (i, 0)),
            pl.BlockSpec((d, tn), lambda j, i: (0, j)),
            pl.BlockSpec((tm, LANES), lambda j, i: (i % s_blocks, 0)),
            pl.BlockSpec((tm, LANES), lambda j, i: (i % s_blocks, 0)),
        ],
        out_specs=pl.BlockSpec((tm, tn), lambda j, i: (i, j)),
        out_shape=jax.ShapeDtypeStruct((n, cols), BF16),
        compiler_params=_params("arbitrary", "arbitrary"),
        name=f"proj_rope{group}",
    )(h, w, cos, sin)


def _proj_fm_kernel(wt_ref, h_ref, cos_ref, sin_ref, o_ref, *, group):
    zt = _nt_dot(wt_ref[...], h_ref[...])
    if group == 0:
        o_ref[...] = zt.astype(o_ref.dtype)
        return
    cos = cos_ref[...]
    sin = sin_ref[...]
    half = group // 2
    for c in range(zt.shape[0] // group):
        blk = zt[c * group:(c + 1) * group]
        swapped = jnp.concatenate([blk[half:], blk[:half]], axis=0)
        o_ref[c * group:(c + 1) * group, :] = (blk * cos + swapped * sin).astype(o_ref.dtype)


def proj_fm(h, wt, cos_fm, sin_fm, group, seq):
    n, d = h.shape
    rows = wt.shape[0]
    tm = min(TOKEN_TILE, seq)
    unit = max(group, LANES)
    tn = _widest_tile(rows // unit, unit, d * 2, 6 * 2 ** 20)
    s_blocks = seq // tm
    g = max(group, 8)
    return pl.pallas_call(
        functools.partial(_proj_fm_kernel, group=group),
        grid=(rows // tn, n // tm),
        in_specs=[
            pl.BlockSpec((tn, d), lambda j, i: (j, 0)),
            pl.BlockSpec((tm, d), lambda j, i: (i, 0)),
            pl.BlockSpec((g, tm), lambda j, i: (0, i % s_blocks)),
            pl.BlockSpec((g, tm), lambda j, i: (0, i % s_blocks)),
        ],
        out_specs=pl.BlockSpec((None, tn, tm), lambda j, i: (i, j, 0)),
        out_shape=jax.ShapeDtypeStruct((n // tm, rows, tm), BF16),
        compiler_params=_params("arbitrary", "arbitrary"),
        name=f"proj_fm_rope{group}",
    )(wt, h, cos_fm, sin_fm)


def _small_proj_kernel(h_ref, wt_ref, bf_ref, logf_ref, wi_ref):
    zt = _nt_dot(wt_ref[...], h_ref[...])
    logf_ref[...] = jax.nn.log_sigmoid(zt[0:8, :] + bf_ref[...])
    wi_ref[...] = zt[8:16, :] * IDX_W_SCALE


def small_proj(h, w_small_t, b_forget8, seq):
    n, d = h.shape
    tm = min(512, seq)
    return pl.pallas_call(
        _small_proj_kernel,
        grid=(n // tm,),
        in_specs=[
            pl.BlockSpec((tm, d), lambda i: (i, 0)),
            pl.BlockSpec((16, d), lambda i: (0, 0)),
            pl.BlockSpec((8, 1), lambda i: (0, 0)),
        ],
        out_specs=[pl.BlockSpec((8, tm), lambda i: (0, i)), pl.BlockSpec((8, tm), lambda i: (0, i))],
        out_shape=[jax.ShapeDtypeStruct((8, n), F32), jax.ShapeDtypeStruct((8, n), F32)],
        compiler_params=_params("arbitrary"),
        name="small_proj",
    )(h, w_small_t, b_forget8)


def _split3(x):
    x1 = x.astype(BF16)
    r1 = x - x1.astype(F32)
    x2 = r1.astype(BF16)
    x3 = (r1 - x2.astype(F32)).astype(BF16)
    return x1, x2, x3


def _neg_cumsum_kernel(x_ref, o_ref):
    x = x_ref[...]
    chunks = x.shape[0]
    r = lax.broadcasted_iota(jnp.int32, (LANES, LANES), 0)
    c = lax.broadcasted_iota(jnp.int32, (LANES, LANES), 1)
    upper = (r <= c).astype(BF16)
    within = sum(_dot(p, upper) for p in _split3(x))
    totals = jnp.broadcast_to(within[:, LANES - 1:LANES], (chunks, LANES))
    rr = lax.broadcasted_iota(jnp.int32, (chunks, chunks), 0)
    cc = lax.broadcasted_iota(jnp.int32, (chunks, chunks), 1)
    strict_lower = (cc < rr).astype(BF16)
    offset = sum(_dot(strict_lower, p) for p in _split3(totals))
    o_ref[...] = -(within + offset) * LOG2E


def neg_cumsum(logf_t, batch, seq):
    rows = logf_t.shape[0] * batch
    chunks = seq // LANES
    x = logf_t.reshape(rows, chunks, LANES)
    out = pl.pallas_call(
        _neg_cumsum_kernel,
        grid=(rows,),
        in_specs=[pl.BlockSpec((None, chunks, LANES), lambda i: (i, 0, 0))],
        out_specs=pl.BlockSpec((None, chunks, LANES), lambda i: (i, 0, 0)),
        out_shape=jax.ShapeDtypeStruct((rows, chunks, LANES), F32),
        compiler_params=_params("arbitrary"),
        name="neg_cumsum",
    )(x)
    return out.reshape(logf_t.shape[0], batch, seq)


def _query_streams(tq):
    width = min(MXU_WIDTH, tq)
    return [slice(lo, lo + width) for lo in range(0, tq, width)]


def _softmax_stats(s, m_prev, l_prev):
    m_new = jnp.maximum(m_prev, jnp.max(s, axis=0, keepdims=True))
    alpha = jnp.exp2(m_prev - m_new)
    p = jnp.exp2(s - m_new)
    return m_new, alpha * l_prev + jnp.sum(p, axis=0, keepdims=True), alpha, p.astype(BF16)


def _causal_mask(key0, query0, n_keys, n_queries):
    key = key0 + lax.broadcasted_iota(jnp.int32, (n_keys, n_queries), 0)
    query = query0 + lax.broadcasted_iota(jnp.int32, (n_keys, n_queries), 1)
    return key <= query


def _sweep_key_tiles(q_block, tq, tk, tile_fn):
    n_full = (q_block * tq) // tk

    def body(kj, carry):
        tile_fn(kj, False)
        return carry

    lax.fori_loop(0, n_full, body, 0)
    tile_fn(n_full, True)


def _attention_sweep(q_block, tq, tk, streams, raw_scores, logits, values, pipe):
    assert tk % tq == 0
    n_full = (q_block * tq) // tk
    n_trips = n_full // 2
    ids = range(len(streams))
    slot_a, slot_b = pipe[:3], pipe[3:]

    for (m_ref, l_ref, acc_ref), qs in streams:
        m_ref[:, qs] = jnp.full((1, qs.stop - qs.start), -jnp.inf, F32)
        l_ref[:, qs] = jnp.zeros((1, qs.stop - qs.start), F32)
        acc_ref[:, qs] = jnp.zeros((acc_ref.shape[0], qs.stop - qs.start), F32)
    slot_b[1][...] = jnp.zeros(slot_b[1].shape, BF16)
    slot_b[2][...] = jnp.ones(slot_b[2].shape, F32)

    def add_pv(kj, slot):
        pv = [_dot(values(kj, i), slot[1][i]) for i in ids]
        for i in ids:
            (_, _, acc_ref), qs = streams[i]
            acc_ref[:, qs] = slot[2][i] * acc_ref[:, qs] + pv[i]

    def softmax_into(kj, raw, slot, i, masked):
        (m_ref, l_ref, _), qs = streams[i]
        m_new, l_new, alpha, p = _softmax_stats(logits(kj, i, raw, masked), m_ref[:, qs], l_ref[:, qs])
        m_ref[:, qs] = m_new
        l_ref[:, qs] = l_new
        slot[1][i] = p
        slot[2][i] = alpha

    def pipe_step(kj, cur, nxt):
        raw_next = raw_scores(kj + 1)
        pv_prev = [_dot(values(jnp.maximum(kj - 1, 0), i), nxt[1][i]) for i in ids]
        rescale_prev = [nxt[2][i] for i in ids]
        for i in ids:
            nxt[0][i] = raw_next[i]
        for i in ids:
            softmax_into(kj, cur[0][i], cur, i, False)
        for i in ids:
            (_, _, acc_ref), qs = streams[i]
            acc_ref[:, qs] = rescale_prev[i] * acc_ref[:, qs] + pv_prev[i]

    def simple_step(kj, masked):
        raw = raw_scores(kj)
        for i in ids:
            softmax_into(kj, raw[i], slot_a, i, masked)
        add_pv(kj, slot_a)

    raw0 = raw_scores(0)
    for i in ids:
        slot_a[0][i] = raw0[i]

    def trip(t, carry):
        pipe_step(2 * t, slot_a, slot_b)
        pipe_step(2 * t + 1, slot_b, slot_a)
        return carry

    lax.fori_loop(0, n_trips, trip, 0)
    add_pv(jnp.maximum(2 * n_trips - 1, 0), slot_b)

    @pl.when(n_full % 2 == 1)
    def _():
        simple_step(n_full - 1, False)

    simple_step(n_full, True)


def _softmax_scratch(tq):
    return [pltpu.VMEM((1, tq), F32), pltpu.VMEM((1, tq), F32), pltpu.VMEM((HEAD_DIM, tq), F32)]


def _pipe_scratch(n_streams, tk, width):
    slot = lambda: [pltpu.VMEM((n_streams, tk, width), F32), pltpu.VMEM((n_streams, tk, width), BF16),
                    pltpu.VMEM((n_streams, 1, width), F32)]
    return slot() + slot()


def _fox_kernel(qt_ref, k_ref, vt_ref, bias_ref, o_ref, m_ref, l_ref, acc_ref, *pipe, tq, tk):
    qb = pl.program_id(2)
    c = (HEAD_DIM ** -0.5) * LOG2E
    slices = _query_streams(tq)

    def raw_scores(kj):
        k = k_ref[pl.ds(pl.multiple_of(kj * tk, tk), tk), :]
        return [_dot(k, qt_ref[:, qs]) for qs in slices]

    def logits(kj, i, raw, masked):
        qs = slices[i]
        width = qs.stop - qs.start
        s = raw * c + jnp.tile(bias_ref[kj], (1, width // LANES))
        if masked:
            s = jnp.where(_causal_mask(kj * tk, qb * tq + qs.start, tk, width), s, NEG_INF)
        return s

    _attention_sweep(qb, tq, tk, [((m_ref, l_ref, acc_ref), qs) for qs in slices], raw_scores, logits,
                     lambda kj, i: vt_ref[kj], pipe)
    o_ref[...] = (acc_ref[...] / l_ref[...]).T.astype(o_ref.dtype)


def fox_attention(q_fm, q_blk0, k_rows, v_fm, v_blk0, bias, seq):
    b = k_rows.shape[0]
    tq = tk = q_fm.shape[2]
    nq = seq // tq
    slices = _query_streams(tq)
    return pl.pallas_call(
        functools.partial(_fox_kernel, tq=tq, tk=tk),
        grid=(b, H_FOX, nq),
        in_specs=[
            pl.BlockSpec((None, HEAD_DIM, tq), lambda bi, h, i: (bi * nq + i, q_blk0 + h, 0)),
            pl.BlockSpec((None, seq, HEAD_DIM), lambda bi, h, i: (bi, 0, h)),
            pl.BlockSpec((nq, HEAD_DIM, tk), lambda bi, h, i: (bi, v_blk0 + h, 0)),
            pl.BlockSpec((None, None, nq, tk, LANES), lambda bi, h, i: (h, bi, 0, 0, 0)),
        ],
        out_specs=pl.BlockSpec((None, tq, HEAD_DIM), lambda bi, h, i: (bi, i, h)),
        out_shape=jax.ShapeDtypeStruct((b, seq, W_FOX), BF16),
        scratch_shapes=_softmax_scratch(tq) + _pipe_scratch(len(slices), tk, slices[0].stop),
        compiler_params=_params("arbitrary", "arbitrary", "arbitrary"),
        name="fox_attention",
    )(q_fm, k_rows, v_fm, bias)


def _diff_kernel(lam_ref, gain_ref, qt_ref, k_ref, vt_ref, o_ref,
                 m1_ref, l1_ref, acc1_ref, m2_ref, l2_ref, acc2_ref, q1_ref, q2_ref, *pipe, tq, tk, lam_init):
    qb = pl.program_id(2)
    c = (DK_DIFF ** -0.5) * LOG2E
    zeros = jnp.zeros((DK_DIFF, tq), BF16)
    q1_ref[...] = jnp.concatenate([qt_ref[0:DK_DIFF, :], zeros], axis=0)
    q2_ref[...] = jnp.concatenate([zeros, qt_ref[DK_DIFF:HEAD_DIM, :]], axis=0)
    streams, q_refs = [], []
    for qs in _query_streams(tq):
        for refs, q_ref in (((m1_ref, l1_ref, acc1_ref), q1_ref), ((m2_ref, l2_ref, acc2_ref), q2_ref)):
            streams.append((refs, qs))
            q_refs.append(q_ref)

    def raw_scores(kj):
        k = k_ref[pl.ds(pl.multiple_of(kj * tk, tk), tk), :]
        return [_dot(k, q_ref[:, qs]) for q_ref, (_, qs) in zip(q_refs, streams)]

    def logits(kj, i, raw, masked):
        qs = streams[i][1]
        s = raw * c
        if masked:
            s = jnp.where(_causal_mask(kj * tk, qb * tq + qs.start, tk, qs.stop - qs.start), s, NEG_INF)
        return s

    _attention_sweep(qb, tq, tk, streams, raw_scores, logits, lambda kj, i: vt_ref[kj], pipe)

    lam_vecs = lam_ref[...]
    dot1 = jnp.sum(lam_vecs[0:1] * lam_vecs[1:2], axis=1, keepdims=True)
    dot2 = jnp.sum(lam_vecs[2:3] * lam_vecs[3:4], axis=1, keepdims=True)
    lam = jnp.exp(dot1) - jnp.exp(dot2) + lam_init
    o = (acc1_ref[...] / l1_ref[...] - lam * (acc2_ref[...] / l2_ref[...])).T
    y = o * lax.rsqrt(jnp.mean(o * o, axis=-1, keepdims=True) + NORM_EPS)
    o_ref[...] = ((y * gain_ref[...]) * (1.0 - lam_init)).astype(o_ref.dtype)


def diff_attention(q_fm, q_blk0, k_rows, v_fm, v_blk0, lam_vecs, subln_gain, lam_init, seq):
    b = k_rows.shape[0]
    tq = tk = q_fm.shape[2]
    nq = seq // tq
    slices = _query_streams(tq)
    return pl.pallas_call(
        functools.partial(_diff_kernel, tq=tq, tk=tk, lam_init=lam_init),
        grid=(b, H_DIFF, nq),
        in_specs=[
            pl.BlockSpec((4, DK_DIFF), lambda bi, h, i: (0, 0)),
            pl.BlockSpec((1, HEAD_DIM), lambda bi, h, i: (0, 0)),
            pl.BlockSpec((None, HEAD_DIM, tq), lambda bi, h, i: (bi * nq + i, q_blk0 + h, 0)),
            pl.BlockSpec((None, seq, HEAD_DIM), lambda bi, h, i: (bi, 0, h)),
            pl.BlockSpec((nq, HEAD_DIM, tk), lambda bi, h, i: (bi, v_blk0 + h, 0)),
        ],
        out_specs=pl.BlockSpec((None, tq, HEAD_DIM), lambda bi, h, i: (bi, i, h)),
        out_shape=jax.ShapeDtypeStruct((b, seq, W_DIFF), BF16),
        scratch_shapes=_softmax_scratch(tq) + _softmax_scratch(tq)
        + [pltpu.VMEM((HEAD_DIM, tq), BF16), pltpu.VMEM((HEAD_DIM, tq), BF16)]
        + _pipe_scratch(2 * len(slices), tk, slices[0].stop),
        compiler_params=_params("arbitrary", "arbitrary", "arbitrary"),
        name="diff_attention",
    )(lam_vecs, subln_gain.reshape(1, HEAD_DIM), q_fm, k_rows, v_fm)


def _dsa_kernel(qt_ref, k_ref, vt_ref, qit_ref, ki_ref, wt_ref, o_ref,
                keys_ref, qm_ref, thr_ref, m_ref, l_ref, acc_ref, m2_ref, l2_ref, acc2_ref, *pipe,
                tq, tk, seq, n_sel):
    qb = pl.program_id(1)
    n_tiles = (qb * tq) // tk + 1
    slices = _query_streams(tq)

    zeros = jnp.zeros((D_IDX, tq), BF16)
    for h in range(H_IDX):
        qm_ref[h] = jnp.concatenate([qit_ref[h * D_IDX:(h + 1) * D_IDX, :], zeros], axis=0)

    def score_tile(kj, masked):
        kk = ki_ref[pl.ds(pl.multiple_of(kj * tk, tk), tk), :]
        for qs in slices:
            width = qs.stop - qs.start
            rel_q = [_dot(kk, qm_ref[h, :, qs]) for h in range(H_IDX)]
            score = jnp.zeros((tk, width), F32)
            for h in range(H_IDX):
                score = score + wt_ref[h:h + 1, qs] * jnp.maximum(rel_q[h], 0.0)
            if masked:
                score = jnp.where(_causal_mask(kj * tk, qb * tq + qs.start, tk, width), score, NEG_INF)
            bits = lax.bitcast_convert_type(score, jnp.int32)
            keys_ref[kj, :, qs] = bits ^ ((bits >> 31) & 0x7FFFFFFF)

    _sweep_key_tiles(qb, tq, tk, score_tile)

    n_beyond = (seq - n_tiles * tk).astype(F32)

    def count_ge(cand):
        def body(kj, part):
            return part + jnp.sum(jnp.where(keys_ref[kj] >= cand, 1.0, 0.0), axis=0, keepdims=True)
        part = lax.fori_loop(0, n_tiles, body, jnp.zeros((1, tq), F32))
        return part + jnp.where(cand <= KEY_NEG_INF, n_beyond, 0.0)

    def bisect(step, thr):
        cand = thr + jnp.left_shift(jnp.int32(1), 31 - step)
        return jnp.where(count_ge(cand) >= n_sel, cand, thr)

    thr = lax.fori_loop(0, 32, bisect, jnp.full((1, tq), INT32_MIN, jnp.int32))
    thr_ref[...] = thr

    surplus = count_ge(thr) - n_sel
    any_surplus = jnp.max(surplus) > 0.0

    @pl.when(any_surplus)
    def _():
        need = jnp.where(surplus > 0.0, n_sel - count_ge(thr + 1), float(seq + 1))

        def count_tied_below(cut):
            def body(kj, part):
                idx = kj * tk + lax.broadcasted_iota(jnp.int32, (tk, tq), 0)
                hit = jnp.where((keys_ref[kj] == thr) & (idx < cut), 1.0, 0.0)
                return part + jnp.sum(hit, axis=0, keepdims=True)
            return lax.fori_loop(0, n_tiles, body, jnp.zeros((1, tq), F32))

        n_bits = max(1, (seq - 1).bit_length())

        def bisect_cut(step, cut):
            cand = cut + jnp.left_shift(jnp.int32(1), n_bits - 1 - step)
            return jnp.where(count_tied_below(cand) < need, cand, cut)

        cut = lax.fori_loop(0, n_bits, bisect_cut, jnp.zeros((1, tq), jnp.int32))

        def demote(kj, carry):
            idx = kj * tk + lax.broadcasted_iota(jnp.int32, (tk, tq), 0)
            t = keys_ref[kj]
            keys_ref[kj] = jnp.where((t == thr) & (idx > cut), thr - 1, t)
            return carry

        lax.fori_loop(0, n_tiles, demote, 0)

    c = (HEAD_DIM ** -0.5) * LOG2E
    states = ((m_ref, l_ref, acc_ref), (m2_ref, l2_ref, acc2_ref))
    for h0 in range(0, H_DSA, 2):
        streams, rows = [], []
        for st, h in zip(states, (h0, h0 + 1)):
            for qs in slices:
                streams.append((st, qs))
                rows.append(slice(h * HEAD_DIM, (h + 1) * HEAD_DIM))

        def raw_scores(kj, streams=streams, rows=rows):
            off = pl.multiple_of(kj * tk, tk)
            return [_dot(k_ref[pl.ds(off, tk), r], qt_ref[r, qs]) for (_, qs), r in zip(streams, rows)]

        def logits(kj, i, raw, masked, streams=streams):
            qs = streams[i][1]
            sel = keys_ref[kj, :, qs] >= thr_ref[:, qs]
            if masked:
                sel = sel & _causal_mask(kj * tk, qb * tq + qs.start, tk, qs.stop - qs.start)
            return jnp.where(sel, raw * c, NEG_INF)

        _attention_sweep(qb, tq, tk, streams, raw_scores, logits,
                         lambda kj, i, rows=rows: vt_ref[kj, rows[i], :], pipe)
        for (_, l_r, acc_r), h in zip(states, (h0, h0 + 1)):
            o_ref[:, h * HEAD_DIM:(h + 1) * HEAD_DIM] = (acc_r[...] / l_r[...]).T.astype(o_ref.dtype)


def dsa_attention(q_fm, k_rows, v_fm, qi_fm, ki_rows, ki_blk, wi_t, seq):
    b = k_rows.shape[0]
    tq = tk = q_fm.shape[2]
    nq = seq // tq
    n_sel = min(TOPK_MAX, seq // 4)
    once = pl.Buffered(1)
    slices = _query_streams(tq)
    fm_q = lambda rows: pl.BlockSpec((None, rows, tq), lambda bi, i: (bi * nq + i, 0, 0))
    return pl.pallas_call(
        functools.partial(_dsa_kernel, tq=tq, tk=tk, seq=seq, n_sel=n_sel),
        grid=(b, nq),
        in_specs=[
            fm_q(W_DSA),
            pl.BlockSpec((None, seq, W_DSA), lambda bi, i: (bi, 0, 0), pipeline_mode=once),
            pl.BlockSpec((nq, W_DSA, tk), lambda bi, i: (bi, 0, 0), pipeline_mode=once),
            fm_q(W_IDX),
            pl.BlockSpec((None, seq, LANES), lambda bi, i: (bi, 0, ki_blk), pipeline_mode=once),
            pl.BlockSpec((H_IDX, tq), lambda bi, i: (0, bi * nq + i)),
        ],
        out_specs=pl.BlockSpec((None, tq, W_DSA), lambda bi, i: (bi, i, 0)),
        out_shape=jax.ShapeDtypeStruct((b, seq, W_DSA), BF16),
        scratch_shapes=[
            pltpu.VMEM((nq, tk, tq), jnp.int32),
            pltpu.VMEM((H_IDX, 2 * D_IDX, tq), BF16),
            pltpu.VMEM((1, tq), jnp.int32),
        ] + _softmax_scratch(tq) + _softmax_scratch(tq) + _pipe_scratch(2 * len(slices), tk, slices[0].stop),
        compiler_params=_params("arbitrary", "arbitrary"),
        name="dsa_attention",
    )(q_fm, k_rows, v_fm, qi_fm, ki_rows, wi_t)


def _merge_kernel(h_ref, oa_ref, ob_ref, oc_ref, mw0_ref, mw1_ref, mw2_ref, mb_ref, wa_ref, wb_ref, wc_ref, o_ref):
    h = h_ref[...]
    merged = None
    for i, (mw_ref, o_b_ref, w_b_ref) in enumerate(((mw0_ref, oa_ref, wa_ref), (mw1_ref, ob_ref, wb_ref),
                                                    (mw2_ref, oc_ref, wc_ref))):
        gate = jax.nn.sigmoid(_dot(h, mw_ref[...]) + mb_ref[i])
        term = gate * _dot(o_b_ref[...], w_b_ref[...])
        merged = term if merged is None else merged + term
    o_ref[...] = merged.astype(o_ref.dtype)


def merge_branches(h, oa, ob, oc, merge_w, merge_b, wa, wb, wc, seq):
    n, d = h.shape
    tm = min(512, seq)
    tn = min(512, d)
    nj = d // tn
    row = lambda width: pl.BlockSpec((tm, width), lambda i, j: (i, 0))
    col = lambda rows: pl.BlockSpec((rows, tn), lambda i, j: (0, j))
    gate_w = lambda g: pl.BlockSpec((d, tn), lambda i, j: (0, g * nj + j))
    return pl.pallas_call(
        _merge_kernel,
        grid=(n // tm, nj),
        in_specs=[
            row(d), row(oa.shape[1]), row(ob.shape[1]), row(oc.shape[1]),
            gate_w(0), gate_w(1), gate_w(2),
            pl.BlockSpec((N_BRANCH, 1, tn), lambda i, j: (0, 0, j)),
            col(wa.shape[0]), col(wb.shape[0]), col(wc.shape[0]),
        ],
        out_specs=pl.BlockSpec((tm, tn), lambda i, j: (i, j)),
        out_shape=jax.ShapeDtypeStruct((n, d), BF16),
        compiler_params=_params("arbitrary", "arbitrary"),
        name="merge_branches",
    )(h, oa, ob, oc, merge_w, merge_w, merge_w, merge_b.reshape(N_BRANCH, 1, d), wa, wb, wc)


def _rope_tables(seq, dim):
    inv_freq = 1.0 / (ROPE_THETA ** (jnp.arange(0, dim, 2, dtype=F32) / dim))
    ang = jnp.arange(seq, dtype=F32)[:, None] * inv_freq[None, :]
    cos, sin = jnp.cos(ang), jnp.sin(ang)
    cos_g = jnp.concatenate([cos, cos], axis=1)
    sin_g = jnp.concatenate([-sin, sin], axis=1)
    reps = LANES // dim
    return jnp.tile(cos_g, (1, reps)), jnp.tile(sin_g, (1, reps)), cos_g.T, sin_g.T


def _split_w_in(w_in):
    sizes = (W_DIFF, W_DIFF, W_DIFF, W_FOX, W_FOX, W_FOX, H_FOX, W_DSA, W_DSA, W_DSA, W_IDX, D_IDX, H_IDX)
    parts, start = [], 0
    for n in sizes:
        parts.append(w_in[:, start:start + n])
        start += n
    return parts


def kernel(x, c, w_ada, b_ada, norm_ffn1, ffn1_w1, ffn1_w3, ffn1_w2, norm_mix, w_in, b_forget, lam_q1, lam_k1, lam_q2, lam_k2, subln_gain, merge_w, merge_b, w_branch_a, w_branch_b, w_branch_c, w_out, norm_ffn2, ffn2_w1, ffn2_w3, ffn2_w2, norm_final):
    batch, seq, d = x.shape
    depth = w_ada.shape[0]
    n = batch * seq
    tk = min(TOKEN_TILE, seq)
    bf = lambda a: a.astype(BF16)

    cos64, sin64, cos64_fm, sin64_fm = _rope_tables(seq, DK_DIFF)
    cos128, sin128, cos128_fm, sin128_fm = _rope_tables(seq, HEAD_DIM)
    mod = adaln_mod(c, w_ada, b_ada)
    xf = x.reshape(n, d)

    for l in range(depth):
        lam_init = 0.8 - 0.6 * math.exp(-0.3 * l)
        modl = mod[l].reshape(batch * N_MOD, 1, d)

        h = norm_mod(xf, norm_ffn1[l], modl, 0, seq)
        u = ffn_up(h, bf(ffn1_w1[l]), bf(ffn1_w3[l]))
        xf = resid_mm(u, bf(ffn1_w2[l]), xf, modl, 2, 0.5, seq)

        h = norm_mod(xf, norm_mix[l], modl, 3, seq)
        qa, ka, va, qb, kb, vb, fb, qc, kc, vc, qi, ki, wi = _split_w_in(w_in[l])
        k64 = proj(h, bf(jnp.concatenate([ka, ki, ki], axis=1)), cos64, sin64, DK_DIFF, seq).reshape(batch, seq, -1)
        k128 = proj(h, bf(kc), cos128, sin128, HEAD_DIM, seq).reshape(batch, seq, -1)
        k0 = proj(h, bf(kb), cos64, sin64, 0, seq).reshape(batch, seq, -1)
        q64_fm = proj_fm(h, bf(jnp.concatenate([qi, qa], axis=1).T), cos64_fm, sin64_fm, DK_DIFF, seq)
        q128_fm = proj_fm(h, bf(qc.T), cos128_fm, sin128_fm, HEAD_DIM, seq)
        p0_fm = proj_fm(h, bf(jnp.concatenate([vc, qb, va, vb], axis=1).T), cos64_fm, sin64_fm, 0, seq)
        blk = lambda rows: rows // HEAD_DIM
        qa_blk, qb_blk, va_blk, vb_blk = blk(W_IDX), blk(W_DSA), blk(W_DSA + W_FOX), blk(W_DSA + W_FOX + W_DIFF)

        w_small_t = bf(jnp.concatenate([fb, jnp.zeros((d, 8 - H_FOX), F32), wi], axis=1).T)
        b_forget8 = jnp.concatenate([b_forget[l], jnp.zeros((8 - H_FOX,), F32)]).reshape(8, 1)
        logf_t, wi_t = small_proj(h, w_small_t, b_forget8, seq)
        neg_cum = neg_cumsum(logf_t, batch, seq)
        bias = jnp.broadcast_to(neg_cum.reshape(8, batch, seq // tk, tk, 1), (8, batch, seq // tk, tk, LANES))

        lam_vecs = jnp.stack([lam_q1[l], lam_k1[l], lam_q2[l], lam_k2[l]])
        oa = diff_attention(q64_fm, qa_blk, k64, p0_fm, va_blk, lam_vecs, subln_gain[l], lam_init, seq)
        ob = fox_attention(p0_fm, qb_blk, k0, p0_fm, vb_blk, bias, seq)
        oc = dsa_attention(q128_fm, k128, p0_fm, q64_fm, k64, blk(W_DIFF), wi_t, seq)

        merged = merge_branches(h, oa.reshape(n, -1), ob.reshape(n, -1), oc.reshape(n, -1), bf(merge_w[l]),
                                merge_b[l], bf(w_branch_a[l]), bf(w_branch_b[l]), bf(w_branch_c[l]), seq)
        xf = resid_mm(merged, bf(w_out[l]), xf, modl, 5, 1.0, seq)

        h = norm_mod(xf, norm_ffn2[l], modl, 6, seq)
        u = ffn_up(h, bf(ffn2_w1[l]), bf(ffn2_w3[l]))
        xf = resid_mm(u, bf(ffn2_w2[l]), xf, modl, 8, 0.5, seq)

    return final_norm(xf, norm_final).reshape(batch, seq, d)
```

```python
import functools
import math

import numpy as np
import jax
import jax.numpy as jnp
from jax import lax
from jax.experimental import pallas as pl
from jax.experimental.pallas import tpu as pltpu

HEAD_DIM = 128
H_DIFF = 6
DK_DIFF = HEAD_DIM // 2
H_FOX = 6
H_DSA = 4
H_IDX = 8
D_IDX = 64
TOPK_MAX = 256
ROPE_THETA = 10000.0
NORM_EPS = 1e-6
N_BRANCH = 3
N_MOD = 9
NEG_INF = -1e30
IDX_W_SCALE = (H_IDX ** -0.5) * (D_IDX ** -0.5)
LOG2E = math.log2(math.e)

W_DIFF = H_DIFF * HEAD_DIM
W_FOX = H_FOX * HEAD_DIM
W_DSA = H_DSA * HEAD_DIM
W_IDX = H_IDX * D_IDX

LANES = 128
MXU_WIDTH = 256
VMEM_LIMIT = 56 * 1024 * 1024
TOKEN_TILE = 512
INT32_MIN = -(2 ** 31)

BF16 = jnp.bfloat16
F32 = jnp.float32


def _order_key_of(value):
    bits = int(np.array(value, np.float32).view(np.int32))
    return bits ^ ((bits >> 31) & 0x7FFFFFFF)


KEY_NEG_INF = _order_key_of(NEG_INF)


def _params(*semantics):
    return pltpu.CompilerParams(dimension_semantics=semantics, vmem_limit_bytes=VMEM_LIMIT)


def _nt_dot(a, b):
    return lax.dot_general(a, b, (((1,), (1,)), ((), ())), preferred_element_type=F32)


def _dot(a, b):
    return jnp.dot(a, b, preferred_element_type=F32)


def _adaln_kernel(c_ref, w_ref, b_ref, o_ref):
    c = c_ref[...]
    c_act = (c * jax.nn.sigmoid(c)).astype(BF16)
    o_ref[...] = _dot(c_act, w_ref[...].astype(BF16)) + b_ref[...]


def adaln_mod(c, w_ada, b_ada):
    depth, d, nd = w_ada.shape
    b = c.shape[0]
    tn = min(1024, d)
    return pl.pallas_call(
        _adaln_kernel,
        grid=(depth, nd // tn),
        in_specs=[
            pl.BlockSpec((b, d), lambda l, j: (0, 0)),
            pl.BlockSpec((None, d, tn), lambda l, j: (l, 0, j)),
            pl.BlockSpec((None, 1, tn), lambda l, j: (l, 0, j)),
        ],
        out_specs=pl.BlockSpec((None, b, tn), lambda l, j: (l, 0, j)),
        out_shape=jax.ShapeDtypeStruct((depth, b, nd), F32),
        compiler_params=_params("arbitrary", "arbitrary"),
        name="adaln_mod",
    )(c, w_ada, b_ada.reshape(depth, 1, nd))


def _norm_mod_kernel(x_ref, gain_ref, sc_ref, sh_ref, o_ref):
    x = x_ref[...]
    y = x * lax.rsqrt(jnp.mean(x * x, axis=-1, keepdims=True) + NORM_EPS)
    o_ref[...] = ((y * gain_ref[...]) * (1.0 + sc_ref[...]) + sh_ref[...]).astype(o_ref.dtype)


def norm_mod(x, gain, modl, i_shift, seq):
    n, d = x.shape
    tm = min(512, seq)
    return pl.pallas_call(
        _norm_mod_kernel,
        grid=(n // tm,),
        in_specs=[
            pl.BlockSpec((tm, d), lambda i: (i, 0)),
            pl.BlockSpec((1, d), lambda i: (0, 0)),
            pl.BlockSpec((None, 1, d), lambda i: ((i * tm) // seq * N_MOD + i_shift + 1, 0, 0)),
            pl.BlockSpec((None, 1, d), lambda i: ((i * tm) // seq * N_MOD + i_shift, 0, 0)),
        ],
        out_specs=pl.BlockSpec((tm, d), lambda i: (i, 0)),
        out_shape=jax.ShapeDtypeStruct((n, d), BF16),
        compiler_params=_params("arbitrary"),
        name="norm_mod",
    )(x, gain.reshape(1, d), modl, modl)


def _final_norm_kernel(x_ref, gain_ref, o_ref):
    x = x_ref[...]
    y = x * lax.rsqrt(jnp.mean(x * x, axis=-1, keepdims=True) + NORM_EPS)
    o_ref[...] = y * gain_ref[...]


def final_norm(x, gain):
    n, d = x.shape
    tm = min(512, n)
    return pl.pallas_call(
        _final_norm_kernel,
        grid=(n // tm,),
        in_specs=[pl.BlockSpec((tm, d), lambda i: (i, 0)), pl.BlockSpec((1, d), lambda i: (0, 0))],
        out_specs=pl.BlockSpec((tm, d), lambda i: (i, 0)),
        out_shape=jax.ShapeDtypeStruct((n, d), F32),
        compiler_params=_params("arbitrary"),
        name="final_norm",
    )(x, gain.reshape(1, d))


def _ffn_up_kernel(h_ref, w1_ref, w3_ref, o_ref):
    h = h_ref[...]
    a = _dot(h, w1_ref[...])
    b = _dot(h, w3_ref[...])
    o_ref[...] = ((a * jax.nn.sigmoid(a)) * b).astype(o_ref.dtype)


def ffn_up(h, w1, w3):
    n, d = h.shape
    f = w1.shape[1]
    tm = min(1024, n)
    tn = 512 if f % 512 == 0 else f
    return pl.pallas_call(
        _ffn_up_kernel,
        grid=(n // tm, f // tn),
        in_specs=[
            pl.BlockSpec((tm, d), lambda i, j: (i, 0)),
            pl.BlockSpec((d, tn), lambda i, j: (0, j)),
            pl.BlockSpec((d, tn), lambda i, j: (0, j)),
        ],
        out_specs=pl.BlockSpec((tm, tn), lambda i, j: (i, j)),
        out_shape=jax.ShapeDtypeStruct((n, f), BF16),
        compiler_params=_params("arbitrary", "arbitrary"),
        name="ffn_up",
    )(h, w1, w3)


def _resid_mm_kernel(a_ref, w_ref, x_ref, g_ref, o_ref, *, gscale):
    y = _dot(a_ref[...], w_ref[...])
    o_ref[...] = x_ref[...] + (gscale * g_ref[...]) * y


def resid_mm(a, w, x, modl, i_gate, gscale, seq):
    n, k = a.shape
    d = w.shape[1]
    tm = min(512, seq)
    tn = min(512, d)
    return pl.pallas_call(
        functools.partial(_resid_mm_kernel, gscale=gscale),
        grid=(n // tm, d // tn),
        in_specs=[
            pl.BlockSpec((tm, k), lambda i, j: (i, 0)),
            pl.BlockSpec((k, tn), lambda i, j: (0, j)),
            pl.BlockSpec((tm, tn), lambda i, j: (i, j)),
            pl.BlockSpec((None, 1, tn), lambda i, j: ((i * tm) // seq * N_MOD + i_gate, 0, j)),
        ],
        out_specs=pl.BlockSpec((tm, tn), lambda i, j: (i, j)),
        out_shape=jax.ShapeDtypeStruct((n, d), F32),
        compiler_params=_params("arbitrary", "arbitrary"),
        name="resid_mm",
    )(a, w, x, modl)


def _widest_tile(n_chunks, chunk, bytes_per_unit, limit_bytes):
    return chunk * max(t for t in range(1, n_chunks + 1)
                       if n_chunks % t == 0 and (t == 1 or t * chunk * bytes_per_unit <= limit_bytes))


def _swap_halves(z, group):
    if group == LANES:
        return pltpu.roll(z, LANES // 2, axis=1)
    half = group // 2
    lane = lax.broadcasted_iota(jnp.int32, z.shape, 1)
    from_above = pltpu.roll(z, LANES - half, axis=1)
    from_below = pltpu.roll(z, half, axis=1)
    return jnp.where((lane & (group - 1)) < half, from_above, from_below)


def _proj_kernel(h_ref, w_ref, cos_ref, sin_ref, o_ref, *, group):
    z = _dot(h_ref[...], w_ref[...])
    if group == 0:
        o_ref[...] = z.astype(o_ref.dtype)
        return
    cos = cos_ref[...]
    sin = sin_ref[...]
    for c in range(z.shape[1] // LANES):
        zc = z[:, c * LANES:(c + 1) * LANES]
        o_ref[:, c * LANES:(c + 1) * LANES] = (zc * cos + _swap_halves(zc, group) * sin).astype(o_ref.dtype)


def proj(h, w, cos, sin, group, seq):
    n, d = h.shape
    cols = w.shape[1]
    tm = min(512, seq)
    tn = _widest_tile(cols // LANES, LANES, d * 2, 9 * 2 ** 20)
    s_blocks = seq // tm
    return pl.pallas_call(
        functools.partial(_proj_kernel, group=group),
        grid=(cols // tn, n // tm),
        in_specs=[
            pl.BlockSpec((tm, d), lambda j, i: (i, 0)),
            pl.BlockSpec((d, tn), lambda j, i: (0, j)),
            pl.BlockSpec((tm, LANES), lambda j, i: (i % s_blocks, 0)),
            pl.BlockSpec((tm, LANES), lambda j, i: (i % s_blocks, 0)),
        ],
        out_specs=pl.BlockSpec((tm, tn), lambda j, i: (i, j)),
        out_shape=jax.ShapeDtypeStruct((n, cols), BF16),
        compiler_params=_params("arbitrary", "arbitrary"),
        name=f"proj_rope{group}",
    )(h, w, cos, sin)


def _proj_fm_kernel(wt_ref, h_ref, cos_ref, sin_ref, o_ref, *, group, row_scales):
    zt = _nt_dot(wt_ref[...], h_ref[...])
    step = group if group else LANES
    half = group // 2
    scale_of_row = [s for n_rows, s in row_scales for _ in range(n_rows // step)]
    if len(row_scales) == 1:
        scale_of_row = scale_of_row[:1] * (zt.shape[0] // step)
    for c in range(zt.shape[0] // step):
        blk = zt[c * step:(c + 1) * step]
        if group:
            swapped = jnp.concatenate([blk[half:], blk[:half]], axis=0)
            blk = blk * cos_ref[...] + swapped * sin_ref[...]
        if scale_of_row[c] != 1.0:
            blk = blk * scale_of_row[c]
        o_ref[c * step:(c + 1) * step, :] = blk.astype(o_ref.dtype)


def proj_fm(h, wt, cos_fm, sin_fm, group, seq, row_scales=None):
    n, d = h.shape
    rows = wt.shape[0]
    tm = min(TOKEN_TILE, seq)
    unit = max(group, LANES)
    row_scales = row_scales or ((rows, 1.0),)
    limit = 6 * 2 ** 20 if len(row_scales) == 1 else rows * d * 2
    tn = _widest_tile(rows // unit, unit, d * 2, limit)
    s_blocks = seq // tm
    g = max(group, 8)
    return pl.pallas_call(
        functools.partial(_proj_fm_kernel, group=group, row_scales=row_scales),
        grid=(rows // tn, n // tm),
        in_specs=[
            pl.BlockSpec((tn, d), lambda j, i: (j, 0)),
            pl.BlockSpec((tm, d), lambda j, i: (i, 0)),
            pl.BlockSpec((g, tm), lambda j, i: (0, i % s_blocks)),
            pl.BlockSpec((g, tm), lambda j, i: (0, i % s_blocks)),
        ],
        out_specs=pl.BlockSpec((None, tn, tm), lambda j, i: (i, j, 0)),
        out_shape=jax.ShapeDtypeStruct((n // tm, rows, tm), BF16),
        compiler_params=_params("arbitrary", "arbitrary"),
        name=f"proj_fm_rope{group}",
    )(wt, h, cos_fm, sin_fm)


def _small_proj_kernel(h_ref, wt_ref, bf_ref, logf_ref, wi_ref):
    zt = _nt_dot(wt_ref[...], h_ref[...])
    logf_ref[...] = jax.nn.log_sigmoid(zt[0:8, :] + bf_ref[...])
    wi_ref[...] = zt[8:16, :] * IDX_W_SCALE


def small_proj(h, w_small_t, b_forget8, seq):
    n, d = h.shape
    tm = min(512, seq)
    return pl.pallas_call(
        _small_proj_kernel,
        grid=(n // tm,),
        in_specs=[
            pl.BlockSpec((tm, d), lambda i: (i, 0)),
            pl.BlockSpec((16, d), lambda i: (0, 0)),
            pl.BlockSpec((8, 1), lambda i: (0, 0)),
        ],
        out_specs=[pl.BlockSpec((8, tm), lambda i: (0, i)), pl.BlockSpec((8, tm), lambda i: (0, i))],
        out_shape=[jax.ShapeDtypeStruct((8, n), F32), jax.ShapeDtypeStruct((8, n), F32)],
        compiler_params=_params("arbitrary"),
        name="small_proj",
    )(h, w_small_t, b_forget8)


def _split3(x):
    x1 = x.astype(BF16)
    r1 = x - x1.astype(F32)
    x2 = r1.astype(BF16)
    x3 = (r1 - x2.astype(F32)).astype(BF16)
    return x1, x2, x3


def _neg_cumsum_kernel(x_ref, hi_ref, mid_ref, lo_ref):
    x = x_ref[...]
    chunks = x.shape[0]
    r = lax.broadcasted_iota(jnp.int32, (LANES, LANES), 0)
    c = lax.broadcasted_iota(jnp.int32, (LANES, LANES), 1)
    upper = (r <= c).astype(BF16)
    within = sum(_dot(p, upper) for p in _split3(x))
    totals = jnp.broadcast_to(within[:, LANES - 1:LANES], (chunks, LANES))
    rr = lax.broadcasted_iota(jnp.int32, (chunks, chunks), 0)
    cc = lax.broadcasted_iota(jnp.int32, (chunks, chunks), 1)
    strict_lower = (cc < rr).astype(BF16)
    offset = sum(_dot(strict_lower, p) for p in _split3(totals))
    hi_ref[...], mid_ref[...], lo_ref[...] = _split3(-(within + offset) * LOG2E)


def neg_cumsum(logf_t, batch, seq):
    rows = logf_t.shape[0] * batch
    chunks = seq // LANES
    x = logf_t.reshape(rows, chunks, LANES)
    spec = pl.BlockSpec((None, chunks, LANES), lambda i: (i, 0, 0))
    pieces = pl.pallas_call(
        _neg_cumsum_kernel,
        grid=(rows,),
        in_specs=[spec],
        out_specs=[spec] * 3,
        out_shape=[jax.ShapeDtypeStruct((rows, chunks, LANES), BF16)] * 3,
        compiler_params=_params("arbitrary"),
        name="neg_cumsum",
    )(x)
    return jnp.stack([p.reshape(logf_t.shape[0], batch, seq) for p in pieces], axis=-1)


def _query_streams(tq):
    width = min(MXU_WIDTH, tq)
    return [slice(lo, lo + width) for lo in range(0, tq, width)]


ONES_ROWS = 16
ACC_ROWS = HEAD_DIM + ONES_ROWS


def _softmax_stats(s, m_prev):
    m_new = jnp.maximum(m_prev, jnp.max(s, axis=0, keepdims=True))
    return m_new, jnp.exp2(m_prev - m_new), jnp.exp2(s - m_new).astype(BF16)


def _with_ones_rows(vt):
    return jnp.concatenate([vt, jnp.ones((ONES_ROWS, vt.shape[1]), vt.dtype)], axis=0)


def _normalized(acc):
    return acc[0:HEAD_DIM] / acc[HEAD_DIM:HEAD_DIM + 1]


def _causal_mask(key0, query0, n_keys, n_queries):
    key = key0 + lax.broadcasted_iota(jnp.int32, (n_keys, n_queries), 0)
    query = query0 + lax.broadcasted_iota(jnp.int32, (n_keys, n_queries), 1)
    return key <= query


def _sweep_key_tiles(q_block, tq, tk, tile_fn):
    n_full = (q_block * tq) // tk

    def body(kj, carry):
        tile_fn(kj, False)
        return carry

    lax.fori_loop(0, n_full, body, 0)
    tile_fn(n_full, True)


def _attention_sweep(q_block, tq, tk, streams, raw_scores, logits, values, pipe):
    assert tk % tq == 0
    n_full = (q_block * tq) // tk
    n_trips = n_full // 2
    ids = range(len(streams))
    slot_a, slot_b = pipe[:3], pipe[3:]

    for (m_ref, acc_ref), qs in streams:
        m_ref[:, qs] = jnp.full((1, qs.stop - qs.start), -jnp.inf, F32)
        acc_ref[:, qs] = jnp.zeros((ACC_ROWS, qs.stop - qs.start), F32)
    slot_b[1][...] = jnp.zeros(slot_b[1].shape, BF16)
    slot_b[2][...] = jnp.ones(slot_b[2].shape, F32)

    def products(kj, slot):
        vts = values(kj)
        return [_dot(vts[i], slot[1][i]) for i in ids]

    def accumulate(pv, rescale):
        for i in ids:
            (_, acc_ref), qs = streams[i]
            acc_ref[:, qs] = rescale[i] * acc_ref[:, qs] + pv[i]

    def softmax_into(kj, raw, slot, i, masked):
        (m_ref, _), qs = streams[i]
        m_new, alpha, p = _softmax_stats(logits(kj, i, raw, masked), m_ref[:, qs])
        m_ref[:, qs] = m_new
        slot[1][i] = p
        slot[2][i] = alpha

    def add_pv(kj, slot):
        accumulate(products(kj, slot), [slot[2][i] for i in ids])

    def pipe_step(kj, cur, nxt):
        raw_next = raw_scores(kj + 1)
        pv_prev = products(jnp.maximum(kj - 1, 0), nxt)
        rescale_prev = [nxt[2][i] for i in ids]
        for i in ids:
            nxt[0][i] = raw_next[i]
        for i in ids:
            softmax_into(kj, cur[0][i], cur, i, False)
        accumulate(pv_prev, rescale_prev)

    def simple_step(kj, masked):
        raw = raw_scores(kj)
        for i in ids:
            softmax_into(kj, raw[i], slot_a, i, masked)
        add_pv(kj, slot_a)

    raw0 = raw_scores(0)
    for i in ids:
        slot_a[0][i] = raw0[i]

    def trip(t, carry):
        pipe_step(2 * t, slot_a, slot_b)
        pipe_step(2 * t + 1, slot_b, slot_a)
        return carry

    lax.fori_loop(0, n_trips, trip, 0)
    add_pv(jnp.maximum(2 * n_trips - 1, 0), slot_b)

    @pl.when(n_full % 2 == 1)
    def _():
        simple_step(n_full - 1, False)

    simple_step(n_full, True)


def _softmax_scratch(tq):
    return [pltpu.VMEM((1, tq), F32), pltpu.VMEM((ACC_ROWS, tq), F32)]


def _pipe_scratch(n_streams, tk, width):
    slot = lambda: [pltpu.VMEM((n_streams, tk, width), F32), pltpu.VMEM((n_streams, tk, width), BF16),
                    pltpu.VMEM((n_streams, 1, width), F32)]
    return slot() + slot()


N_BIAS_PIECES = 3


def _fox_kernel(qt_ref, k_ref, vt_ref, o_ref, m_ref, acc_ref, qa_ref, *pipe, tq, tk):
    qb = pl.program_id(2)
    slices = _query_streams(tq)
    row = lax.broadcasted_iota(jnp.int32, (HEAD_DIM, tq), 0)
    qa_ref[...] = jnp.concatenate([qt_ref[...], jnp.where(row < N_BIAS_PIECES, 1.0, 0.0).astype(BF16)], axis=0)

    def raw_scores(kj):
        k = k_ref[pl.ds(pl.multiple_of(kj * tk, tk), tk), :]
        return [_dot(k, qa_ref[:, qs]) for qs in slices]

    def logits(kj, i, raw, masked):
        qs = slices[i]
        if masked:
            raw = jnp.where(_causal_mask(kj * tk, qb * tq + qs.start, tk, qs.stop - qs.start), raw, NEG_INF)
        return raw

    def values(kj):
        return [_with_ones_rows(vt_ref[kj])] * len(slices)

    _attention_sweep(qb, tq, tk, [((m_ref, acc_ref), qs) for qs in slices], raw_scores, logits, values, pipe)
    o_ref[...] = _normalized(acc_ref[...]).T.astype(o_ref.dtype)


def fox_attention(q_fm, q_blk0, k_aug, v_fm, v_blk0, seq):
    b = k_aug.shape[0]
    tq = tk = q_fm.shape[2]
    nq = seq // tq
    slices = _query_streams(tq)
    return pl.pallas_call(
        functools.partial(_fox_kernel, tq=tq, tk=tk),
        grid=(b, H_FOX, nq),
        in_specs=[
            pl.BlockSpec((None, HEAD_DIM, tq), lambda bi, h, i: (bi * nq + i, q_blk0 + h, 0)),
            pl.BlockSpec((None, seq, 2 * HEAD_DIM), lambda bi, h, i: (bi, 0, h)),
            pl.BlockSpec((nq, HEAD_DIM, tk), lambda bi, h, i: (bi, v_blk0 + h, 0)),
        ],
        out_specs=pl.BlockSpec((None, tq, HEAD_DIM), lambda bi, h, i: (bi, i, h)),
        out_shape=jax.ShapeDtypeStruct((b, seq, W_FOX), BF16),
        scratch_shapes=_softmax_scratch(tq) + [pltpu.VMEM((2 * HEAD_DIM, tq), BF16)]
        + _pipe_scratch(len(slices), tk, slices[0].stop),
        compiler_params=_params("arbitrary", "arbitrary", "arbitrary"),
        name="fox_attention",
    )(q_fm, k_aug, v_fm)


def _diff_kernel(lam_ref, gain_ref, qt_ref, k_ref, vt_ref, o_ref,
                 m1_ref, acc1_ref, m2_ref, acc2_ref, q1_ref, q2_ref, *pipe, tq, tk, lam_init):
    qb = pl.program_id(2)
    zeros = jnp.zeros((DK_DIFF, tq), BF16)
    q1_ref[...] = jnp.concatenate([qt_ref[0:DK_DIFF, :], zeros], axis=0)
    q2_ref[...] = jnp.concatenate([zeros, qt_ref[DK_DIFF:HEAD_DIM, :]], axis=0)
    streams, q_refs = [], []
    for qs in _query_streams(tq):
        for refs, q_ref in (((m1_ref, acc1_ref), q1_ref), ((m2_ref, acc2_ref), q2_ref)):
            streams.append((refs, qs))
            q_refs.append(q_ref)

    def raw_scores(kj):
        k = k_ref[pl.ds(pl.multiple_of(kj * tk, tk), tk), :]
        return [_dot(k, q_ref[:, qs]) for q_ref, (_, qs) in zip(q_refs, streams)]

    def logits(kj, i, raw, masked):
        qs = streams[i][1]
        if masked:
            raw = jnp.where(_causal_mask(kj * tk, qb * tq + qs.start, tk, qs.stop - qs.start), raw, NEG_INF)
        return raw

    def values(kj):
        return [_with_ones_rows(vt_ref[kj])] * len(streams)

    _attention_sweep(qb, tq, tk, streams, raw_scores, logits, values, pipe)

    lam_vecs = lam_ref[...]
    dot1 = jnp.sum(lam_vecs[0:1] * lam_vecs[1:2], axis=1, keepdims=True)
    dot2 = jnp.sum(lam_vecs[2:3] * lam_vecs[3:4], axis=1, keepdims=True)
    lam = jnp.exp(dot1) - jnp.exp(dot2) + lam_init
    o = (_normalized(acc1_ref[...]) - lam * _normalized(acc2_ref[...])).T
    y = o * lax.rsqrt(jnp.mean(o * o, axis=-1, keepdims=True) + NORM_EPS)
    o_ref[...] = ((y * gain_ref[...]) * (1.0 - lam_init)).astype(o_ref.dtype)


def diff_attention(q_fm, q_blk0, k_rows, v_fm, v_blk0, lam_vecs, subln_gain, lam_init, seq):
    b = k_rows.shape[0]
    tq = tk = q_fm.shape[2]
    nq = seq // tq
    slices = _query_streams(tq)
    return pl.pallas_call(
        functools.partial(_diff_kernel, tq=tq, tk=tk, lam_init=lam_init),
        grid=(b, H_DIFF, nq),
        in_specs=[
            pl.BlockSpec((4, DK_DIFF), lambda bi, h, i: (0, 0)),
            pl.BlockSpec((1, HEAD_DIM), lambda bi, h, i: (0, 0)),
            pl.BlockSpec((None, HEAD_DIM, tq), lambda bi, h, i: (bi * nq + i, q_blk0 + h, 0)),
            pl.BlockSpec((None, seq, HEAD_DIM), lambda bi, h, i: (bi, 0, h)),
            pl.BlockSpec((nq, HEAD_DIM, tk), lambda bi, h, i: (bi, v_blk0 + h, 0)),
        ],
        out_specs=pl.BlockSpec((None, tq, HEAD_DIM), lambda bi, h, i: (bi, i, h)),
        out_shape=jax.ShapeDtypeStruct((b, seq, W_DIFF), BF16),
        scratch_shapes=_softmax_scratch(tq) + _softmax_scratch(tq)
        + [pltpu.VMEM((HEAD_DIM, tq), BF16), pltpu.VMEM((HEAD_DIM, tq), BF16)]
        + _pipe_scratch(2 * len(slices), tk, slices[0].stop),
        compiler_params=_params("arbitrary", "arbitrary", "arbitrary"),
        name="diff_attention",
    )(lam_vecs, subln_gain.reshape(1, HEAD_DIM), q_fm, k_rows, v_fm)


def _dsa_kernel(qt_ref, k_ref, vt_ref, qit_ref, ki_ref, wt_ref, o_ref,
                keys_ref, qm_ref, thr_ref, m_ref, acc_ref, m2_ref, acc2_ref, *pipe,
                tq, tk, seq, n_sel):
    qb = pl.program_id(1)
    n_tiles = (qb * tq) // tk + 1
    slices = _query_streams(tq)

    zeros = jnp.zeros((D_IDX, tq), BF16)
    for h in range(H_IDX):
        qm_ref[h] = jnp.concatenate([qit_ref[h * D_IDX:(h + 1) * D_IDX, :], zeros], axis=0)

    def score_tile(kj, masked):
        kk = ki_ref[pl.ds(pl.multiple_of(kj * tk, tk), tk), :]
        for qs in slices:
            width = qs.stop - qs.start
            rel_q = [_dot(kk, qm_ref[h, :, qs]) for h in range(H_IDX)]
            score = jnp.zeros((tk, width), F32)
            for h in range(H_IDX):
                score = score + wt_ref[h:h + 1, qs] * jnp.maximum(rel_q[h], 0.0)
            if masked:
                score = jnp.where(_causal_mask(kj * tk, qb * tq + qs.start, tk, width), score, NEG_INF)
            bits = lax.bitcast_convert_type(score, jnp.int32)
            keys_ref[kj, :, qs] = bits ^ ((bits >> 31) & 0x7FFFFFFF)

    _sweep_key_tiles(qb, tq, tk, score_tile)

    n_beyond = (seq - n_tiles * tk).astype(F32)

    def count_ge(cand):
        def body(kj, part):
            return part + jnp.sum(jnp.where(keys_ref[kj] >= cand, 1.0, 0.0), axis=0, keepdims=True)
        part = lax.fori_loop(0, n_tiles, body, jnp.zeros((1, tq), F32))
        return part + jnp.where(cand <= KEY_NEG_INF, n_beyond, 0.0)

    def bisect(step, thr):
        cand = thr + jnp.left_shift(jnp.int32(1), 31 - step)
        return jnp.where(count_ge(cand) >= n_sel, cand, thr)

    thr = lax.fori_loop(0, 32, bisect, jnp.full((1, tq), INT32_MIN, jnp.int32))
    thr_ref[...] = thr

    surplus = count_ge(thr) - n_sel
    any_surplus = jnp.max(surplus) > 0.0

    @pl.when(any_surplus)
    def _():
        need = jnp.where(surplus > 0.0, n_sel - count_ge(thr + 1), float(seq + 1))

        def count_tied_below(cut):
            def body(kj, part):
                idx = kj * tk + lax.broadcasted_iota(jnp.int32, (tk, tq), 0)
                hit = jnp.where((keys_ref[kj] == thr) & (idx < cut), 1.0, 0.0)
                return part + jnp.sum(hit, axis=0, keepdims=True)
            return lax.fori_loop(0, n_tiles, body, jnp.zeros((1, tq), F32))

        n_bits = max(1, (seq - 1).bit_length())

        def bisect_cut(step, cut):
            cand = cut + jnp.left_shift(jnp.int32(1), n_bits - 1 - step)
            return jnp.where(count_tied_below(cand) < need, cand, cut)

        cut = lax.fori_loop(0, n_bits, bisect_cut, jnp.zeros((1, tq), jnp.int32))

        def demote(kj, carry):
            idx = kj * tk + lax.broadcasted_iota(jnp.int32, (tk, tq), 0)
            t = keys_ref[kj]
            keys_ref[kj] = jnp.where((t == thr) & (idx > cut), thr - 1, t)
            return carry

        lax.fori_loop(0, n_tiles, demote, 0)

    states = ((m_ref, acc_ref), (m2_ref, acc2_ref))
    for h0 in range(0, H_DSA, 2):
        streams, rows = [], []
        for st, h in zip(states, (h0, h0 + 1)):
            for qs in slices:
                streams.append((st, qs))
                rows.append(slice(h * HEAD_DIM, (h + 1) * HEAD_DIM))

        def raw_scores(kj, streams=streams, rows=rows):
            off = pl.multiple_of(kj * tk, tk)
            return [_dot(k_ref[pl.ds(off, tk), r], qt_ref[r, qs]) for (_, qs), r in zip(streams, rows)]

        def logits(kj, i, raw, masked, streams=streams):
            qs = streams[i][1]
            sel = keys_ref[kj, :, qs] >= thr_ref[:, qs]
            if masked:
                sel = sel & _causal_mask(kj * tk, qb * tq + qs.start, tk, qs.stop - qs.start)
            return jnp.where(sel, raw, NEG_INF)

        def values(kj, h0=h0):
            per_head = [_with_ones_rows(vt_ref[kj, h * HEAD_DIM:(h + 1) * HEAD_DIM, :]) for h in (h0, h0 + 1)]
            return [v for v in per_head for _ in slices]

        _attention_sweep(qb, tq, tk, streams, raw_scores, logits, values, pipe)
        for (_, acc_r), h in zip(states, (h0, h0 + 1)):
            o_ref[:, h * HEAD_DIM:(h + 1) * HEAD_DIM] = _normalized(acc_r[...]).T.astype(o_ref.dtype)


def dsa_attention(q_fm, k_rows, v_fm, qi_fm, ki_rows, ki_blk, wi_t, seq):
    b = k_rows.shape[0]
    tq = tk = q_fm.shape[2]
    nq = seq // tq
    n_sel = min(TOPK_MAX, seq // 4)
    once = pl.Buffered(1)
    slices = _query_streams(tq)
    fm_q = lambda rows: pl.BlockSpec((None, rows, tq), lambda bi, i: (bi * nq + i, 0, 0))
    return pl.pallas_call(
        functools.partial(_dsa_kernel, tq=tq, tk=tk, seq=seq, n_sel=n_sel),
        grid=(b, nq),
        in_specs=[
            fm_q(W_DSA),
            pl.BlockSpec((None, seq, W_DSA), lambda bi, i: (bi, 0, 0), pipeline_mode=once),
            pl.BlockSpec((nq, W_DSA, tk), lambda bi, i: (bi, 0, 0), pipeline_mode=once),
            fm_q(W_IDX),
            pl.BlockSpec((None, seq, LANES), lambda bi, i: (bi, 0, ki_blk), pipeline_mode=once),
            pl.BlockSpec((H_IDX, tq), lambda bi, i: (0, bi * nq + i)),
        ],
        out_specs=pl.BlockSpec((None, tq, W_DSA), lambda bi, i: (bi, i, 0)),
        out_shape=jax.ShapeDtypeStruct((b, seq, W_DSA), BF16),
        scratch_shapes=[
            pltpu.VMEM((nq, tk, tq), jnp.int32),
            pltpu.VMEM((H_IDX, 2 * D_IDX, tq), BF16),
            pltpu.VMEM((1, tq), jnp.int32),
        ] + _softmax_scratch(tq) + _softmax_scratch(tq) + _pipe_scratch(2 * len(slices), tk, slices[0].stop),
        compiler_params=_params("arbitrary", "arbitrary"),
        name="dsa_attention",
    )(q_fm, k_rows, v_fm, qi_fm, ki_rows, wi_t)


def _merge_kernel(h_ref, oa_ref, ob_ref, oc_ref, mw0_ref, mw1_ref, mw2_ref, mb_ref, wa_ref, wb_ref, wc_ref, o_ref):
    h = h_ref[...]
    merged = None
    for i, (mw_ref, o_b_ref, w_b_ref) in enumerate(((mw0_ref, oa_ref, wa_ref), (mw1_ref, ob_ref, wb_ref),
                                                    (mw2_ref, oc_ref, wc_ref))):
        gate = jax.nn.sigmoid(_dot(h, mw_ref[...]) + mb_ref[i])
        term = gate * _dot(o_b_ref[...], w_b_ref[...])
        merged = term if merged is None else merged + term
    o_ref[...] = merged.astype(o_ref.dtype)


def merge_branches(h, oa, ob, oc, merge_w, merge_b, wa, wb, wc, seq):
    n, d = h.shape
    tm = min(512, seq)
    tn = min(512, d)
    nj = d // tn
    row = lambda width: pl.BlockSpec((tm, width), lambda i, j: (i, 0))
    col = lambda rows: pl.BlockSpec((rows, tn), lambda i, j: (0, j))
    gate_w = lambda g: pl.BlockSpec((d, tn), lambda i, j: (0, g * nj + j))
    return pl.pallas_call(
        _merge_kernel,
        grid=(n // tm, nj),
        in_specs=[
            row(d), row(oa.shape[1]), row(ob.shape[1]), row(oc.shape[1]),
            gate_w(0), gate_w(1), gate_w(2),
            pl.BlockSpec((N_BRANCH, 1, tn), lambda i, j: (0, 0, j)),
            col(wa.shape[0]), col(wb.shape[0]), col(wc.shape[0]),
        ],
        out_specs=pl.BlockSpec((tm, tn), lambda i, j: (i, j)),
        out_shape=jax.ShapeDtypeStruct((n, d), BF16),
        compiler_params=_params("arbitrary", "arbitrary"),
        name="merge_branches",
    )(h, oa, ob, oc, merge_w, merge_w, merge_w, merge_b.reshape(N_BRANCH, 1, d), wa, wb, wc)


def _rope_tables(seq, dim):
    inv_freq = 1.0 / (ROPE_THETA ** (jnp.arange(0, dim, 2, dtype=F32) / dim))
    ang = jnp.arange(seq, dtype=F32)[:, None] * inv_freq[None, :]
    cos, sin = jnp.cos(ang), jnp.sin(ang)
    cos_g = jnp.concatenate([cos, cos], axis=1)
    sin_g = jnp.concatenate([-sin, sin], axis=1)
    reps = LANES // dim
    return jnp.tile(cos_g, (1, reps)), jnp.tile(sin_g, (1, reps)), cos_g.T, sin_g.T


def _split_w_in(w_in):
    sizes = (W_DIFF, W_DIFF, W_DIFF, W_FOX, W_FOX, W_FOX, H_FOX, W_DSA, W_DSA, W_DSA, W_IDX, D_IDX, H_IDX)
    parts, start = [], 0
    for n in sizes:
        parts.append(w_in[:, start:start + n])
        start += n
    return parts


def kernel(x, c, w_ada, b_ada, norm_ffn1, ffn1_w1, ffn1_w3, ffn1_w2, norm_mix, w_in, b_forget, lam_q1, lam_k1, lam_q2, lam_k2, subln_gain, merge_w, merge_b, w_branch_a, w_branch_b, w_branch_c, w_out, norm_ffn2, ffn2_w1, ffn2_w3, ffn2_w2, norm_final):
    batch, seq, d = x.shape
    depth = w_ada.shape[0]
    n = batch * seq
    tk = min(TOKEN_TILE, seq)
    bf = lambda a: a.astype(BF16)

    cos64, sin64, cos64_fm, sin64_fm = _rope_tables(seq, DK_DIFF)
    cos128, sin128, cos128_fm, sin128_fm = _rope_tables(seq, HEAD_DIM)
    mod = adaln_mod(c, w_ada, b_ada)
    xf = x.reshape(n, d)

    for l in range(depth):
        lam_init = 0.8 - 0.6 * math.exp(-0.3 * l)
        modl = mod[l].reshape(batch * N_MOD, 1, d)

        h = norm_mod(xf, norm_ffn1[l], modl, 0, seq)
        u = ffn_up(h, bf(ffn1_w1[l]), bf(ffn1_w3[l]))
        xf = resid_mm(u, bf(ffn1_w2[l]), xf, modl, 2, 0.5, seq)

        h = norm_mod(xf, norm_mix[l], modl, 3, seq)
        qa, ka, va, qb, kb, vb, fb, qc, kc, vc, qi, ki, wi = _split_w_in(w_in[l])
        k64 = proj(h, bf(jnp.concatenate([ka, ki, ki], axis=1)), cos64, sin64, DK_DIFF, seq).reshape(batch, seq, -1)
        k128 = proj(h, bf(kc), cos128, sin128, HEAD_DIM, seq).reshape(batch, seq, -1)
        k0 = proj(h, bf(kb), cos64, sin64, 0, seq).reshape(batch, seq, -1)
        log2e_over_sqrt = lambda width: (width ** -0.5) * LOG2E
        q64_fm = proj_fm(h, bf(jnp.concatenate([qi, qa], axis=1).T), cos64_fm, sin64_fm, DK_DIFF, seq,
                         ((W_IDX, 1.0), (W_DIFF, log2e_over_sqrt(DK_DIFF))))
        q128_fm = proj_fm(h, bf(qc.T), cos128_fm, sin128_fm, HEAD_DIM, seq, ((W_DSA, log2e_over_sqrt(HEAD_DIM)),))
        qb_fm = proj_fm(h, bf(qb.T), cos64_fm, sin64_fm, 0, seq, ((W_FOX, log2e_over_sqrt(HEAD_DIM)),))
        v_fm = proj_fm(h, bf(jnp.concatenate([vc, va, vb], axis=1).T), cos64_fm, sin64_fm, 0, seq)
        blk = lambda rows: rows // HEAD_DIM
        qa_blk, va_blk, vb_blk = blk(W_IDX), blk(W_DSA), blk(W_DSA + W_DIFF)

        w_small_t = bf(jnp.concatenate([fb, jnp.zeros((d, 8 - H_FOX), F32), wi], axis=1).T)
        b_forget8 = jnp.concatenate([b_forget[l], jnp.zeros((8 - H_FOX,), F32)]).reshape(8, 1)
        logf_t, wi_t = small_proj(h, w_small_t, b_forget8, seq)
        bias_pieces = jnp.transpose(neg_cumsum(logf_t, batch, seq)[:H_FOX], (1, 2, 0, 3))
        bias_cols = jnp.pad(bias_pieces, ((0, 0), (0, 0), (0, 0), (0, HEAD_DIM - N_BIAS_PIECES)))
        k_aug = jnp.concatenate([k0.reshape(batch, seq, H_FOX, HEAD_DIM), bias_cols], axis=-1).reshape(batch, seq, -1)

        lam_vecs = jnp.stack([lam_q1[l], lam_k1[l], lam_q2[l], lam_k2[l]])
        oa = diff_attention(q64_fm, qa_blk, k64, v_fm, va_blk, lam_vecs, subln_gain[l], lam_init, seq)
        ob = fox_attention(qb_fm, 0, k_aug, v_fm, vb_blk, seq)
        oc = dsa_attention(q128_fm, k128, v_fm, q64_fm, k64, blk(W_DIFF), wi_t, seq)

        merged = merge_branches(h, oa.reshape(n, -1), ob.reshape(n, -1), oc.reshape(n, -1), bf(merge_w[l]),
                                merge_b[l], bf(w_branch_a[l]), bf(w_branch_b[l]), bf(w_branch_c[l]), seq)
        xf = resid_mm(merged, bf(w_out[l]), xf, modl, 5, 1.0, seq)

        h = norm_mod(xf, norm_ffn2[l], modl, 6, seq)
        u = ffn_up(h, bf(ffn2_w1[l]), bf(ffn2_w3[l]))
        xf = resid_mm(u, bf(ffn2_w2[l]), xf, modl, 8, 0.5, seq)

    return final_norm(xf, norm_final).reshape(batch, seq, d)
```

```python
import functools
import math

import numpy as np
import jax
import jax.numpy as jnp
from jax import lax
from jax.experimental import pallas as pl
from jax.experimental.pallas import tpu as pltpu

HEAD_DIM = 128
H_DIFF = 6
DK_DIFF = HEAD_DIM // 2
H_FOX = 6
H_DSA = 4
H_IDX = 8
D_IDX = 64
TOPK_MAX = 256
ROPE_THETA = 10000.0
NORM_EPS = 1e-6
N_BRANCH = 3
N_MOD = 9
NEG_INF = -1e30
IDX_W_SCALE = (H_IDX ** -0.5) * (D_IDX ** -0.5)
LOG2E = math.log2(math.e)

W_DIFF = H_DIFF * HEAD_DIM
W_FOX = H_FOX * HEAD_DIM
W_DSA = H_DSA * HEAD_DIM
W_IDX = H_IDX * D_IDX

LANES = 128
MXU_WIDTH = 256
VMEM_LIMIT = 56 * 1024 * 1024
TOKEN_TILE = 512
INT32_MIN = -(2 ** 31)

BF16 = jnp.bfloat16
F32 = jnp.float32


def _order_key_of(value):
    bits = int(np.array(value, np.float32).view(np.int32))
    return bits ^ ((bits >> 31) & 0x7FFFFFFF)


KEY_NEG_INF = _order_key_of(NEG_INF)


def _params(*semantics):
    return pltpu.CompilerParams(dimension_semantics=semantics, vmem_limit_bytes=VMEM_LIMIT)


def _nt_dot(a, b):
    return lax.dot_general(a, b, (((1,), (1,)), ((), ())), preferred_element_type=F32)


def _dot(a, b):
    return jnp.dot(a, b, preferred_element_type=F32)


def _adaln_kernel(c_ref, w_ref, b_ref, o_ref):
    c = c_ref[...]
    c_act = (c * jax.nn.sigmoid(c)).astype(BF16)
    o_ref[...] = _dot(c_act, w_ref[...].astype(BF16)) + b_ref[...]


def adaln_mod(c, w_ada, b_ada):
    depth, d, nd = w_ada.shape
    b = c.shape[0]
    tn = min(1024, d)
    return pl.pallas_call(
        _adaln_kernel,
        grid=(depth, nd // tn),
        in_specs=[
            pl.BlockSpec((b, d), lambda l, j: (0, 0)),
            pl.BlockSpec((None, d, tn), lambda l, j: (l, 0, j)),
            pl.BlockSpec((None, 1, tn), lambda l, j: (l, 0, j)),
        ],
        out_specs=pl.BlockSpec((None, b, tn), lambda l, j: (l, 0, j)),
        out_shape=jax.ShapeDtypeStruct((depth, b, nd), F32),
        compiler_params=_params("arbitrary", "arbitrary"),
        name="adaln_mod",
    )(c, w_ada, b_ada.reshape(depth, 1, nd))


def _norm_mod_kernel(x_ref, gain_ref, sc_ref, sh_ref, o_ref):
    x = x_ref[...]
    y = x * lax.rsqrt(jnp.mean(x * x, axis=-1, keepdims=True) + NORM_EPS)
    o_ref[...] = ((y * gain_ref[...]) * (1.0 + sc_ref[...]) + sh_ref[...]).astype(o_ref.dtype)


def norm_mod(x, gain, modl, i_shift, seq):
    n, d = x.shape
    tm = min(512, seq)
    return pl.pallas_call(
        _norm_mod_kernel,
        grid=(n // tm,),
        in_specs=[
            pl.BlockSpec((tm, d), lambda i: (i, 0)),
            pl.BlockSpec((1, d), lambda i: (0, 0)),
            pl.BlockSpec((None, 1, d), lambda i: ((i * tm) // seq * N_MOD + i_shift + 1, 0, 0)),
            pl.BlockSpec((None, 1, d), lambda i: ((i * tm) // seq * N_MOD + i_shift, 0, 0)),
        ],
        out_specs=pl.BlockSpec((tm, d), lambda i: (i, 0)),
        out_shape=jax.ShapeDtypeStruct((n, d), BF16),
        compiler_params=_params("arbitrary"),
        name="norm_mod",
    )(x, gain.reshape(1, d), modl, modl)


def _final_norm_kernel(x_ref, gain_ref, o_ref):
    x = x_ref[...]
    y = x * lax.rsqrt(jnp.mean(x * x, axis=-1, keepdims=True) + NORM_EPS)
    o_ref[...] = y * gain_ref[...]


def final_norm(x, gain):
    n, d = x.shape
    tm = min(512, n)
    return pl.pallas_call(
        _final_norm_kernel,
        grid=(n // tm,),
        in_specs=[pl.BlockSpec((tm, d), lambda i: (i, 0)), pl.BlockSpec((1, d), lambda i: (0, 0))],
        out_specs=pl.BlockSpec((tm, d), lambda i: (i, 0)),
        out_shape=jax.ShapeDtypeStruct((n, d), F32),
        compiler_params=_params("arbitrary"),
        name="final_norm",
    )(x, gain.reshape(1, d))


def _cast_on_first_row_step(pairs):
    @pl.when(pl.program_id(1) == 0)
    def _():
        for src_ref, dst_ref in pairs:
            dst_ref[...] = src_ref[...].astype(dst_ref.dtype)


def _ffn_up_kernel(h_ref, w1_ref, w3_ref, o_ref, w1b_ref, w3b_ref):
    _cast_on_first_row_step(((w1_ref, w1b_ref), (w3_ref, w3b_ref)))
    h = h_ref[...]
    a = _dot(h, w1b_ref[...])
    b = _dot(h, w3b_ref[...])
    o_ref[...] = ((a * jax.nn.sigmoid(a)) * b).astype(o_ref.dtype)


def ffn_up(h, w1, w3, layer):
    n, d = h.shape
    f = w1.shape[2]
    tm = min(1024, n)
    tn = 512 if f % 512 == 0 else f
    w_spec = pl.BlockSpec((None, d, tn), lambda j, i: (layer, 0, j))
    return pl.pallas_call(
        _ffn_up_kernel,
        grid=(f // tn, n // tm),
        in_specs=[pl.BlockSpec((tm, d), lambda j, i: (i, 0)), w_spec, w_spec],
        out_specs=pl.BlockSpec((tm, tn), lambda j, i: (i, j)),
        out_shape=jax.ShapeDtypeStruct((n, f), BF16),
        scratch_shapes=[pltpu.VMEM((d, tn), BF16), pltpu.VMEM((d, tn), BF16)],
        compiler_params=_params("arbitrary", "arbitrary"),
        name="ffn_up",
    )(h, w1, w3)


def _resid_mm_kernel(a_ref, w_ref, x_ref, g_ref, o_ref, wb_ref, *, gscale):
    _cast_on_first_row_step(((w_ref, wb_ref),))
    y = _dot(a_ref[...], wb_ref[...])
    o_ref[...] = x_ref[...] + (gscale * g_ref[...]) * y


def resid_mm(a, w, layer, x, modl, i_gate, gscale, seq):
    n, k = a.shape
    d = w.shape[2]
    tm = min(512, seq)
    tn = min(512, d)
    return pl.pallas_call(
        functools.partial(_resid_mm_kernel, gscale=gscale),
        grid=(d // tn, n // tm),
        in_specs=[
            pl.BlockSpec((tm, k), lambda j, i: (i, 0)),
            pl.BlockSpec((None, k, tn), lambda j, i: (layer, 0, j)),
            pl.BlockSpec((tm, tn), lambda j, i: (i, j)),
            pl.BlockSpec((None, 1, tn), lambda j, i: ((i * tm) // seq * N_MOD + i_gate, 0, j)),
        ],
        out_specs=pl.BlockSpec((tm, tn), lambda j, i: (i, j)),
        out_shape=jax.ShapeDtypeStruct((n, d), F32),
        scratch_shapes=[pltpu.VMEM((k, tn), BF16)],
        compiler_params=_params("arbitrary", "arbitrary"),
        name="resid_mm",
    )(a, w, x, modl)


def _widest_tile(n_chunks, chunk, bytes_per_unit, limit_bytes):
    return chunk * max(t for t in range(1, n_chunks + 1)
                       if n_chunks % t == 0 and (t == 1 or t * chunk * bytes_per_unit <= limit_bytes))


def _swap_halves(z, group):
    if group == LANES:
        return pltpu.roll(z, LANES // 2, axis=1)
    half = group // 2
    lane = lax.broadcasted_iota(jnp.int32, z.shape, 1)
    from_above = pltpu.roll(z, LANES - half, axis=1)
    from_below = pltpu.roll(z, half, axis=1)
    return jnp.where((lane & (group - 1)) < half, from_above, from_below)


def _proj_kernel(h_ref, w_ref, cos_ref, sin_ref, o_ref, wb_ref, *, group):
    _cast_on_first_row_step(((w_ref, wb_ref),))
    z = _dot(h_ref[...], wb_ref[...])
    if group == 0:
        o_ref[...] = z.astype(o_ref.dtype)
        return
    cos = cos_ref[...]
    sin = sin_ref[...]
    for c in range(z.shape[1] // LANES):
        zc = z[:, c * LANES:(c + 1) * LANES]
        o_ref[:, c * LANES:(c + 1) * LANES] = (zc * cos + _swap_halves(zc, group) * sin).astype(o_ref.dtype)


def proj(h, w, cos, sin, group, seq):
    n, d = h.shape
    cols = w.shape[1]
    tm = min(512, seq)
    tn = _widest_tile(cols // LANES, LANES, d * 4, 12 * 2 ** 20)
    s_blocks = seq // tm
    return pl.pallas_call(
        functools.partial(_proj_kernel, group=group),
        grid=(cols // tn, n // tm),
        in_specs=[
            pl.BlockSpec((tm, d), lambda j, i: (i, 0)),
            pl.BlockSpec((d, tn), lambda j, i: (0, j)),
            pl.BlockSpec((tm, LANES), lambda j, i: (i % s_blocks, 0)),
            pl.BlockSpec((tm, LANES), lambda j, i: (i % s_blocks, 0)),
        ],
        out_specs=pl.BlockSpec((tm, tn), lambda j, i: (i, j)),
        out_shape=jax.ShapeDtypeStruct((n, cols), BF16),
        scratch_shapes=[pltpu.VMEM((d, tn), BF16)],
        compiler_params=_params("arbitrary", "arbitrary"),
        name=f"proj_rope{group}",
    )(h, w, cos, sin)


def _proj_fm_kernel(w_ref, h_ref, cos_ref, sin_ref, o_ref, wt_ref, *, group, row_scales):
    @pl.when(pl.program_id(1) == 0)
    def _():
        wt_ref[...] = w_ref[...].T.astype(wt_ref.dtype)

    zt = _nt_dot(wt_ref[...], h_ref[...])
    step = group if group else LANES
    half = group // 2
    scale_of_row = [s for n_rows, s in row_scales for _ in range(n_rows // step)]
    if len(row_scales) == 1:
        scale_of_row = scale_of_row[:1] * (zt.shape[0] // step)
    for c in range(zt.shape[0] // step):
        blk = zt[c * step:(c + 1) * step]
        if group:
            swapped = jnp.concatenate([blk[half:], blk[:half]], axis=0)
            blk = blk * cos_ref[...] + swapped * sin_ref[...]
        if scale_of_row[c] != 1.0:
            blk = blk * scale_of_row[c]
        o_ref[c * step:(c + 1) * step, :] = blk.astype(o_ref.dtype)


def proj_fm(h, w, cos_fm, sin_fm, group, seq, row_scales=None):
    n, d = h.shape
    rows = w.shape[1]
    tm = min(TOKEN_TILE, seq)
    unit = max(group, LANES)
    row_scales = row_scales or ((rows, 1.0),)
    limit = 12 * 2 ** 20 if len(row_scales) == 1 else rows * d * 4
    tn = _widest_tile(rows // unit, unit, d * 4, limit)
    s_blocks = seq // tm
    g = max(group, 8)
    return pl.pallas_call(
        functools.partial(_proj_fm_kernel, group=group, row_scales=row_scales),
        grid=(rows // tn, n // tm),
        in_specs=[
            pl.BlockSpec((d, tn), lambda j, i: (0, j)),
            pl.BlockSpec((tm, d), lambda j, i: (i, 0)),
            pl.BlockSpec((g, tm), lambda j, i: (0, i % s_blocks)),
            pl.BlockSpec((g, tm), lambda j, i: (0, i % s_blocks)),
        ],
        out_specs=pl.BlockSpec((None, tn, tm), lambda j, i: (i, j, 0)),
        out_shape=jax.ShapeDtypeStruct((n // tm, rows, tm), BF16),
        scratch_shapes=[pltpu.VMEM((tn, d), BF16)],
        compiler_params=_params("arbitrary", "arbitrary"),
        name=f"proj_fm_rope{group}",
    )(w, h, cos_fm, sin_fm)


def _small_proj_kernel(h_ref, wt_ref, bf_ref, logf_ref, wi_ref):
    zt = _nt_dot(wt_ref[...], h_ref[...])
    logf_ref[...] = jax.nn.log_sigmoid(zt[0:8, :] + bf_ref[...])
    wi_ref[...] = zt[8:16, :] * IDX_W_SCALE


def small_proj(h, w_small_t, b_forget8, seq):
    n, d = h.shape
    tm = min(512, seq)
    return pl.pallas_call(
        _small_proj_kernel,
        grid=(n // tm,),
        in_specs=[
            pl.BlockSpec((tm, d), lambda i: (i, 0)),
            pl.BlockSpec((16, d), lambda i: (0, 0)),
            pl.BlockSpec((8, 1), lambda i: (0, 0)),
        ],
        out_specs=[pl.BlockSpec((8, tm), lambda i: (0, i)), pl.BlockSpec((8, tm), lambda i: (0, i))],
        out_shape=[jax.ShapeDtypeStruct((8, n), F32), jax.ShapeDtypeStruct((8, n), F32)],
        compiler_params=_params("arbitrary"),
        name="small_proj",
    )(h, w_small_t, b_forget8)


def _split3(x):
    x1 = x.astype(BF16)
    r1 = x - x1.astype(F32)
    x2 = r1.astype(BF16)
    x3 = (r1 - x2.astype(F32)).astype(BF16)
    return x1, x2, x3


def _neg_cumsum_kernel(x_ref, hi_ref, mid_ref, lo_ref):
    x = x_ref[...]
    chunks = x.shape[0]
    r = lax.broadcasted_iota(jnp.int32, (LANES, LANES), 0)
    c = lax.broadcasted_iota(jnp.int32, (LANES, LANES), 1)
    upper = (r <= c).astype(BF16)
    within = sum(_dot(p, upper) for p in _split3(x))
    totals = jnp.broadcast_to(within[:, LANES - 1:LANES], (chunks, LANES))
    rr = lax.broadcasted_iota(jnp.int32, (chunks, chunks), 0)
    cc = lax.broadcasted_iota(jnp.int32, (chunks, chunks), 1)
    strict_lower = (cc < rr).astype(BF16)
    offset = sum(_dot(strict_lower, p) for p in _split3(totals))
    hi_ref[...], mid_ref[...], lo_ref[...] = _split3(-(within + offset) * LOG2E)


def neg_cumsum(logf_t, batch, seq):
    rows = logf_t.shape[0] * batch
    chunks = seq // LANES
    x = logf_t.reshape(rows, chunks, LANES)
    spec = pl.BlockSpec((None, chunks, LANES), lambda i: (i, 0, 0))
    pieces = pl.pallas_call(
        _neg_cumsum_kernel,
        grid=(rows,),
        in_specs=[spec],
        out_specs=[spec] * 3,
        out_shape=[jax.ShapeDtypeStruct((rows, chunks, LANES), BF16)] * 3,
        compiler_params=_params("arbitrary"),
        name="neg_cumsum",
    )(x)
    return jnp.stack([p.reshape(logf_t.shape[0], batch, seq) for p in pieces], axis=-1)


def _query_streams(tq):
    width = min(MXU_WIDTH, tq)
    return [slice(lo, lo + width) for lo in range(0, tq, width)]


ONES_ROWS = 16
ACC_ROWS = HEAD_DIM + ONES_ROWS


def _softmax_stats(s, m_prev):
    m_new = jnp.maximum(m_prev, jnp.max(s, axis=0, keepdims=True))
    return m_new, jnp.exp2(m_prev - m_new), jnp.exp2(s - m_new).astype(BF16)


def _with_ones_rows(vt):
    return jnp.concatenate([vt, jnp.ones((ONES_ROWS, vt.shape[1]), vt.dtype)], axis=0)


def _normalized(acc):
    return acc[0:HEAD_DIM] / acc[HEAD_DIM:HEAD_DIM + 1]


def _causal_mask(key0, query0, n_keys, n_queries):
    key = key0 + lax.broadcasted_iota(jnp.int32, (n_keys, n_queries), 0)
    query = query0 + lax.broadcasted_iota(jnp.int32, (n_keys, n_queries), 1)
    return key <= query


def _sweep_key_tiles(q_block, tq, tk, tile_fn):
    n_full = (q_block * tq) // tk

    def body(kj, carry):
        tile_fn(kj, False)
        return carry

    lax.fori_loop(0, n_full, body, 0)
    tile_fn(n_full, True)


def _attention_sweep(q_block, tq, tk, streams, raw_scores, logits, values, pipe):
    assert tk % tq == 0
    n_full = (q_block * tq) // tk
    n_trips = n_full // 2
    ids = range(len(streams))
    slot_a, slot_b = pipe[:3], pipe[3:]

    for (m_ref, acc_ref), qs in streams:
        m_ref[:, qs] = jnp.full((1, qs.stop - qs.start), -jnp.inf, F32)
        acc_ref[:, qs] = jnp.zeros((ACC_ROWS, qs.stop - qs.start), F32)
    slot_b[1][...] = jnp.zeros(slot_b[1].shape, BF16)
    slot_b[2][...] = jnp.ones(slot_b[2].shape, F32)

    def products(kj, slot):
        vts = values(kj)
        return [_dot(vts[i], slot[1][i]) for i in ids]

    def accumulate(pv, rescale):
        for i in ids:
            (_, acc_ref), qs = streams[i]
            acc_ref[:, qs] = rescale[i] * acc_ref[:, qs] + pv[i]

    def softmax_into(kj, raw, slot, i, masked):
        (m_ref, _), qs = streams[i]
        m_new, alpha, p = _softmax_stats(logits(kj, i, raw, masked), m_ref[:, qs])
        m_ref[:, qs] = m_new
        slot[1][i] = p
        slot[2][i] = alpha

    def add_pv(kj, slot):
        accumulate(products(kj, slot), [slot[2][i] for i in ids])

    def pipe_step(kj, cur, nxt):
        raw_next = raw_scores(kj + 1)
        pv_prev = products(jnp.maximum(kj - 1, 0), nxt)
        rescale_prev = [nxt[2][i] for i in ids]
        for i in ids:
            nxt[0][i] = raw_next[i]
        for i in ids:
            softmax_into(kj, cur[0][i], cur, i, False)
        accumulate(pv_prev, rescale_prev)

    def simple_step(kj, masked):
        raw = raw_scores(kj)
        for i in ids:
            softmax_into(kj, raw[i], slot_a, i, masked)
        add_pv(kj, slot_a)

    raw0 = raw_scores(0)
    for i in ids:
        slot_a[0][i] = raw0[i]

    def trip(t, carry):
        pipe_step(2 * t, slot_a, slot_b)
        pipe_step(2 * t + 1, slot_b, slot_a)
        return carry

    lax.fori_loop(0, n_trips, trip, 0)
    add_pv(jnp.maximum(2 * n_trips - 1, 0), slot_b)

    @pl.when(n_full % 2 == 1)
    def _():
        simple_step(n_full - 1, False)

    simple_step(n_full, True)


def _softmax_scratch(tq):
    return [pltpu.VMEM((1, tq), F32), pltpu.VMEM((ACC_ROWS, tq), F32)]


def _pipe_scratch(n_streams, tk, width):
    slot = lambda: [pltpu.VMEM((n_streams, tk, width), F32), pltpu.VMEM((n_streams, tk, width), BF16),
                    pltpu.VMEM((n_streams, 1, width), F32)]
    return slot() + slot()


N_BIAS_PIECES = 3


def _fox_kernel(qt_ref, k_ref, vt_ref, o_ref, m_ref, acc_ref, qa_ref, *pipe, tq, tk):
    qb = pl.program_id(2)
    slices = _query_streams(tq)
    row = lax.broadcasted_iota(jnp.int32, (HEAD_DIM, tq), 0)
    qa_ref[...] = jnp.concatenate([qt_ref[...], jnp.where(row < N_BIAS_PIECES, 1.0, 0.0).astype(BF16)], axis=0)

    def raw_scores(kj):
        k = k_ref[pl.ds(pl.multiple_of(kj * tk, tk), tk), :]
        return [_dot(k, qa_ref[:, qs]) for qs in slices]

    def logits(kj, i, raw, masked):
        qs = slices[i]
        if masked:
            raw = jnp.where(_causal_mask(kj * tk, qb * tq + qs.start, tk, qs.stop - qs.start), raw, NEG_INF)
        return raw

    def values(kj):
        return [_with_ones_rows(vt_ref[kj])] * len(slices)

    _attention_sweep(qb, tq, tk, [((m_ref, acc_ref), qs) for qs in slices], raw_scores, logits, values, pipe)
    o_ref[...] = _normalized(acc_ref[...]).T.astype(o_ref.dtype)


def fox_attention(q_fm, q_blk0, k_aug, v_fm, v_blk0, seq):
    b = k_aug.shape[0]
    tq = tk = q_fm.shape[2]
    nq = seq // tq
    slices = _query_streams(tq)
    return pl.pallas_call(
        functools.partial(_fox_kernel, tq=tq, tk=tk),
        grid=(b, H_FOX, nq),
        in_specs=[
            pl.BlockSpec((None, HEAD_DIM, tq), lambda bi, h, i: (bi * nq + i, q_blk0 + h, 0)),
            pl.BlockSpec((None, seq, 2 * HEAD_DIM), lambda bi, h, i: (bi, 0, h)),
            pl.BlockSpec((nq, HEAD_DIM, tk), lambda bi, h, i: (bi, v_blk0 + h, 0)),
        ],
        out_specs=pl.BlockSpec((None, tq, HEAD_DIM), lambda bi, h, i: (bi, i, h)),
        out_shape=jax.ShapeDtypeStruct((b, seq, W_FOX), BF16),
        scratch_shapes=_softmax_scratch(tq) + [pltpu.VMEM((2 * HEAD_DIM, tq), BF16)]
        + _pipe_scratch(len(slices), tk, slices[0].stop),
        compiler_params=_params("arbitrary", "arbitrary", "arbitrary"),
        name="fox_attention",
    )(q_fm, k_aug, v_fm)


def _diff_kernel(lam_ref, gain_ref, qt_ref, k_ref, vt_ref, o_ref,
                 m1_ref, acc1_ref, m2_ref, acc2_ref, q1_ref, q2_ref, *pipe, tq, tk, lam_init):
    qb = pl.program_id(2)
    zeros = jnp.zeros((DK_DIFF, tq), BF16)
    q1_ref[...] = jnp.concatenate([qt_ref[0:DK_DIFF, :], zeros], axis=0)
    q2_ref[...] = jnp.concatenate([zeros, qt_ref[DK_DIFF:HEAD_DIM, :]], axis=0)
    streams, q_refs = [], []
    for qs in _query_streams(tq):
        for refs, q_ref in (((m1_ref, acc1_ref), q1_ref), ((m2_ref, acc2_ref), q2_ref)):
            streams.append((refs, qs))
            q_refs.append(q_ref)

    def raw_scores(kj):
        k = k_ref[pl.ds(pl.multiple_of(kj * tk, tk), tk), :]
        return [_dot(k, q_ref[:, qs]) for q_ref, (_, qs) in zip(q_refs, streams)]

    def logits(kj, i, raw, masked):
        qs = streams[i][1]
        if masked:
            raw = jnp.where(_causal_mask(kj * tk, qb * tq + qs.start, tk, qs.stop - qs.start), raw, NEG_INF)
        return raw

    def values(kj):
        return [_with_ones_rows(vt_ref[kj])] * len(streams)

    _attention_sweep(qb, tq, tk, streams, raw_scores, logits, values, pipe)

    lam_vecs = lam_ref[...]
    dot1 = jnp.sum(lam_vecs[0:1] * lam_vecs[1:2], axis=1, keepdims=True)
    dot2 = jnp.sum(lam_vecs[2:3] * lam_vecs[3:4], axis=1, keepdims=True)
    lam = jnp.exp(dot1) - jnp.exp(dot2) + lam_init
    o = (_normalized(acc1_ref[...]) - lam * _normalized(acc2_ref[...])).T
    y = o * lax.rsqrt(jnp.mean(o * o, axis=-1, keepdims=True) + NORM_EPS)
    o_ref[...] = ((y * gain_ref[...]) * (1.0 - lam_init)).astype(o_ref.dtype)


def diff_attention(q_fm, q_blk0, k_rows, v_fm, v_blk0, lam_vecs, subln_gain, lam_init, seq):
    b = k_rows.shape[0]
    tq = tk = q_fm.shape[2]
    nq = seq // tq
    slices = _query_streams(tq)
    return pl.pallas_call(
        functools.partial(_diff_kernel, tq=tq, tk=tk, lam_init=lam_init),
        grid=(b, H_DIFF, nq),
        in_specs=[
            pl.BlockSpec((4, DK_DIFF), lambda bi, h, i: (0, 0)),
            pl.BlockSpec((1, HEAD_DIM), lambda bi, h, i: (0, 0)),
            pl.BlockSpec((None, HEAD_DIM, tq), lambda bi, h, i: (bi * nq + i, q_blk0 + h, 0)),
            pl.BlockSpec((None, seq, HEAD_DIM), lambda bi, h, i: (bi, 0, h)),
            pl.BlockSpec((nq, HEAD_DIM, tk), lambda bi, h, i: (bi, v_blk0 + h, 0)),
        ],
        out_specs=pl.BlockSpec((None, tq, HEAD_DIM), lambda bi, h, i: (bi, i, h)),
        out_shape=jax.ShapeDtypeStruct((b, seq, W_DIFF), BF16),
        scratch_shapes=_softmax_scratch(tq) + _softmax_scratch(tq)
        + [pltpu.VMEM((HEAD_DIM, tq), BF16), pltpu.VMEM((HEAD_DIM, tq), BF16)]
        + _pipe_scratch(2 * len(slices), tk, slices[0].stop),
        compiler_params=_params("arbitrary", "arbitrary", "arbitrary"),
        name="diff_attention",
    )(lam_vecs, subln_gain.reshape(1, HEAD_DIM), q_fm, k_rows, v_fm)


def _dsa_kernel(qt_ref, k_ref, vt_ref, qit_ref, ki_ref, wt_ref, o_ref,
                keys_ref, qm_ref, thr_ref, m_ref, acc_ref, m2_ref, acc2_ref, *pipe,
                tq, tk, seq, n_sel):
    qb = pl.program_id(1)
    n_tiles = (qb * tq) // tk + 1
    slices = _query_streams(tq)

    zeros = jnp.zeros((D_IDX, tq), BF16)
    for h in range(H_IDX):
        qm_ref[h] = jnp.concatenate([qit_ref[h * D_IDX:(h + 1) * D_IDX, :], zeros], axis=0)

    def score_tile(kj, masked):
        kk = ki_ref[pl.ds(pl.multiple_of(kj * tk, tk), tk), :]
        for qs in slices:
            width = qs.stop - qs.start
            rel_q = [_dot(kk, qm_ref[h, :, qs]) for h in range(H_IDX)]
            score = jnp.zeros((tk, width), F32)
            for h in range(H_IDX):
                score = score + wt_ref[h:h + 1, qs] * jnp.maximum(rel_q[h], 0.0)
            if masked:
                score = jnp.where(_causal_mask(kj * tk, qb * tq + qs.start, tk, width), score, NEG_INF)
            bits = lax.bitcast_convert_type(score, jnp.int32)
            keys_ref[kj, :, qs] = bits ^ ((bits >> 31) & 0x7FFFFFFF)

    _sweep_key_tiles(qb, tq, tk, score_tile)

    n_beyond = (seq - n_tiles * tk).astype(F32)

    def count_ge(cand):
        def body(kj, part):
            return part + jnp.sum(jnp.where(keys_ref[kj] >= cand, 1.0, 0.0), axis=0, keepdims=True)
        part = lax.fori_loop(0, n_tiles, body, jnp.zeros((1, tq), F32))
        return part + jnp.where(cand <= KEY_NEG_INF, n_beyond, 0.0)

    def bisect(step, thr):
        cand = thr + jnp.left_shift(jnp.int32(1), 31 - step)
        return jnp.where(count_ge(cand) >= n_sel, cand, thr)

    thr = lax.fori_loop(0, 32, bisect, jnp.full((1, tq), INT32_MIN, jnp.int32))
    thr_ref[...] = thr

    surplus = count_ge(thr) - n_sel
    any_surplus = jnp.max(surplus) > 0.0

    @pl.when(any_surplus)
    def _():
        need = jnp.where(surplus > 0.0, n_sel - count_ge(thr + 1), float(seq + 1))

        def count_tied_below(cut):
            def body(kj, part):
                idx = kj * tk + lax.broadcasted_iota(jnp.int32, (tk, tq), 0)
                hit = jnp.where((keys_ref[kj] == thr) & (idx < cut), 1.0, 0.0)
                return part + jnp.sum(hit, axis=0, keepdims=True)
            return lax.fori_loop(0, n_tiles, body, jnp.zeros((1, tq), F32))

        n_bits = max(1, (seq - 1).bit_length())

        def bisect_cut(step, cut):
            cand = cut + jnp.left_shift(jnp.int32(1), n_bits - 1 - step)
            return jnp.where(count_tied_below(cand) < need, cand, cut)

        cut = lax.fori_loop(0, n_bits, bisect_cut, jnp.zeros((1, tq), jnp.int32))

        def demote(kj, carry):
            idx = kj * tk + lax.broadcasted_iota(jnp.int32, (tk, tq), 0)
            t = keys_ref[kj]
            keys_ref[kj] = jnp.where((t == thr) & (idx > cut), thr - 1, t)
            return carry

        lax.fori_loop(0, n_tiles, demote, 0)

    states = ((m_ref, acc_ref), (m2_ref, acc2_ref))
    for h0 in range(0, H_DSA, 2):
        streams, rows = [], []
        for st, h in zip(states, (h0, h0 + 1)):
            for qs in slices:
                streams.append((st, qs))
                rows.append(slice(h * HEAD_DIM, (h + 1) * HEAD_DIM))

        def raw_scores(kj, streams=streams, rows=rows):
            off = pl.multiple_of(kj * tk, tk)
            return [_dot(k_ref[pl.ds(off, tk), r], qt_ref[r, qs]) for (_, qs), r in zip(streams, rows)]

        def logits(kj, i, raw, masked, streams=streams):
            qs = streams[i][1]
            sel = keys_ref[kj, :, qs] >= thr_ref[:, qs]
            if masked:
                sel = sel & _causal_mask(kj * tk, qb * tq + qs.start, tk, qs.stop - qs.start)
            return jnp.where(sel, raw, NEG_INF)

        def values(kj, h0=h0):
            per_head = [_with_ones_rows(vt_ref[kj, h * HEAD_DIM:(h + 1) * HEAD_DIM, :]) for h in (h0, h0 + 1)]
            return [v for v in per_head for _ in slices]

        _attention_sweep(qb, tq, tk, streams, raw_scores, logits, values, pipe)
        for (_, acc_r), h in zip(states, (h0, h0 + 1)):
            o_ref[:, h * HEAD_DIM:(h + 1) * HEAD_DIM] = _normalized(acc_r[...]).T.astype(o_ref.dtype)


def dsa_attention(q_fm, k_rows, v_fm, qi_fm, ki_rows, ki_blk, wi_t, seq):
    b = k_rows.shape[0]
    tq = tk = q_fm.shape[2]
    nq = seq // tq
    n_sel = min(TOPK_MAX, seq // 4)
    once = pl.Buffered(1)
    slices = _query_streams(tq)
    fm_q = lambda rows: pl.BlockSpec((None, rows, tq), lambda bi, i: (bi * nq + i, 0, 0))
    return pl.pallas_call(
        functools.partial(_dsa_kernel, tq=tq, tk=tk, seq=seq, n_sel=n_sel),
        grid=(b, nq),
        in_specs=[
            fm_q(W_DSA),
            pl.BlockSpec((None, seq, W_DSA), lambda bi, i: (bi, 0, 0), pipeline_mode=once),
            pl.BlockSpec((nq, W_DSA, tk), lambda bi, i: (bi, 0, 0), pipeline_mode=once),
            fm_q(W_IDX),
            pl.BlockSpec((None, seq, LANES), lambda bi, i: (bi, 0, ki_blk), pipeline_mode=once),
            pl.BlockSpec((H_IDX, tq), lambda bi, i: (0, bi * nq + i)),
        ],
        out_specs=pl.BlockSpec((None, tq, W_DSA), lambda bi, i: (bi, i, 0)),
        out_shape=jax.ShapeDtypeStruct((b, seq, W_DSA), BF16),
        scratch_shapes=[
            pltpu.VMEM((nq, tk, tq), jnp.int32),
            pltpu.VMEM((H_IDX, 2 * D_IDX, tq), BF16),
            pltpu.VMEM((1, tq), jnp.int32),
        ] + _softmax_scratch(tq) + _softmax_scratch(tq) + _pipe_scratch(2 * len(slices), tk, slices[0].stop),
        compiler_params=_params("arbitrary", "arbitrary"),
        name="dsa_attention",
    )(q_fm, k_rows, v_fm, qi_fm, ki_rows, wi_t)


def _merge_kernel(h_ref, oa_ref, ob_ref, oc_ref, mw0_ref, mw1_ref, mw2_ref, mb_ref, wa_ref, wb_ref, wc_ref, o_ref,
                  *bf16_refs):
    f32_refs = (mw0_ref, mw1_ref, mw2_ref, wa_ref, wb_ref, wc_ref)
    _cast_on_first_row_step(tuple(zip(f32_refs, bf16_refs)))
    gate_w, branch_w = bf16_refs[:N_BRANCH], bf16_refs[N_BRANCH:]
    h = h_ref[...]
    merged = None
    for i, o_b_ref in enumerate((oa_ref, ob_ref, oc_ref)):
        gate = jax.nn.sigmoid(_dot(h, gate_w[i][...]) + mb_ref[i])
        term = gate * _dot(o_b_ref[...], branch_w[i][...])
        merged = term if merged is None else merged + term
    o_ref[...] = merged.astype(o_ref.dtype)


def merge_branches(h, oa, ob, oc, merge_w, merge_b, wa, wb, wc, layer, seq):
    n, d = h.shape
    tm = min(512, seq)
    tn = min(256, d)
    nj = d // tn
    row = lambda width: pl.BlockSpec((tm, width), lambda j, i: (i, 0))
    col = lambda rows: pl.BlockSpec((None, rows, tn), lambda j, i: (layer, 0, j))
    gate_w = lambda g: pl.BlockSpec((None, d, tn), lambda j, i: (layer, 0, g * nj + j))
    widths = (oa.shape[1], ob.shape[1], oc.shape[1])
    return pl.pallas_call(
        _merge_kernel,
        grid=(nj, n // tm),
        in_specs=[
            row(d), row(widths[0]), row(widths[1]), row(widths[2]),
            gate_w(0), gate_w(1), gate_w(2),
            pl.BlockSpec((None, N_BRANCH, 1, tn), lambda j, i: (layer, 0, 0, j)),
            col(widths[0]), col(widths[1]), col(widths[2]),
        ],
        out_specs=pl.BlockSpec((tm, tn), lambda j, i: (i, j)),
        out_shape=jax.ShapeDtypeStruct((n, d), BF16),
        scratch_shapes=[pltpu.VMEM((d, tn), BF16)] * N_BRANCH + [pltpu.VMEM((w, tn), BF16) for w in widths],
        compiler_params=_params("arbitrary", "arbitrary"),
        name="merge_branches",
    )(h, oa, ob, oc, merge_w, merge_w, merge_w, merge_b.reshape(merge_b.shape[0], N_BRANCH, 1, d), wa, wb, wc)


def _rope_tables(seq, dim):
    inv_freq = 1.0 / (ROPE_THETA ** (jnp.arange(0, dim, 2, dtype=F32) / dim))
    ang = jnp.arange(seq, dtype=F32)[:, None] * inv_freq[None, :]
    cos, sin = jnp.cos(ang), jnp.sin(ang)
    cos_g = jnp.concatenate([cos, cos], axis=1)
    sin_g = jnp.concatenate([-sin, sin], axis=1)
    reps = LANES // dim
    return jnp.tile(cos_g, (1, reps)), jnp.tile(sin_g, (1, reps)), cos_g.T, sin_g.T


def _split_w_in(w_in):
    sizes = (W_DIFF, W_DIFF, W_DIFF, W_FOX, W_FOX, W_FOX, H_FOX, W_DSA, W_DSA, W_DSA, W_IDX, D_IDX, H_IDX)
    parts, start = [], 0
    for n in sizes:
        parts.append(w_in[:, start:start + n])
        start += n
    return parts


def kernel(x, c, w_ada, b_ada, norm_ffn1, ffn1_w1, ffn1_w3, ffn1_w2, norm_mix, w_in, b_forget, lam_q1, lam_k1, lam_q2, lam_k2, subln_gain, merge_w, merge_b, w_branch_a, w_branch_b, w_branch_c, w_out, norm_ffn2, ffn2_w1, ffn2_w3, ffn2_w2, norm_final):
    batch, seq, d = x.shape
    depth = w_ada.shape[0]
    n = batch * seq
    tk = min(TOKEN_TILE, seq)
    bf = lambda a: a.astype(BF16)

    cos64, sin64, cos64_fm, sin64_fm = _rope_tables(seq, DK_DIFF)
    cos128, sin128, cos128_fm, sin128_fm = _rope_tables(seq, HEAD_DIM)
    mod = adaln_mod(c, w_ada, b_ada)
    xf = x.reshape(n, d)

    for l in range(depth):
        lam_init = 0.8 - 0.6 * math.exp(-0.3 * l)
        modl = mod[l].reshape(batch * N_MOD, 1, d)

        h = norm_mod(xf, norm_ffn1[l], modl, 0, seq)
        u = ffn_up(h, ffn1_w1, ffn1_w3, l)
        xf = resid_mm(u, ffn1_w2, l, xf, modl, 2, 0.5, seq)

        h = norm_mod(xf, norm_mix[l], modl, 3, seq)
        qa, ka, va, qb, kb, vb, fb, qc, kc, vc, qi, ki, wi = _split_w_in(w_in[l])
        k64 = proj(h, jnp.concatenate([ka, ki, ki], axis=1), cos64, sin64, DK_DIFF, seq).reshape(batch, seq, -1)
        k128 = proj(h, kc, cos128, sin128, HEAD_DIM, seq).reshape(batch, seq, -1)
        k0 = proj(h, kb, cos64, sin64, 0, seq).reshape(batch, seq, -1)
        log2e_over_sqrt = lambda width: (width ** -0.5) * LOG2E
        q64_fm = proj_fm(h, jnp.concatenate([qi, qa], axis=1), cos64_fm, sin64_fm, DK_DIFF, seq,
                         ((W_IDX, 1.0), (W_DIFF, log2e_over_sqrt(DK_DIFF))))
        q128_fm = proj_fm(h, qc, cos128_fm, sin128_fm, HEAD_DIM, seq, ((W_DSA, log2e_over_sqrt(HEAD_DIM)),))
        qb_fm = proj_fm(h, qb, cos64_fm, sin64_fm, 0, seq, ((W_FOX, log2e_over_sqrt(HEAD_DIM)),))
        v_fm = proj_fm(h, jnp.concatenate([vc, va, vb], axis=1), cos64_fm, sin64_fm, 0, seq)
        blk = lambda rows: rows // HEAD_DIM
        qa_blk, va_blk, vb_blk = blk(W_IDX), blk(W_DSA), blk(W_DSA + W_DIFF)

        w_small_t = bf(jnp.concatenate([fb, jnp.zeros((d, 8 - H_FOX), F32), wi], axis=1).T)
        b_forget8 = jnp.concatenate([b_forget[l], jnp.zeros((8 - H_FOX,), F32)]).reshape(8, 1)
        logf_t, wi_t = small_proj(h, w_small_t, b_forget8, seq)
        bias_pieces = jnp.transpose(neg_cumsum(logf_t, batch, seq)[:H_FOX], (1, 2, 0, 3))
        bias_cols = jnp.pad(bias_pieces, ((0, 0), (0, 0), (0, 0), (0, HEAD_DIM - N_BIAS_PIECES)))
        k_aug = jnp.concatenate([k0.reshape(batch, seq, H_FOX, HEAD_DIM), bias_cols], axis=-1).reshape(batch, seq, -1)

        lam_vecs = jnp.stack([lam_q1[l], lam_k1[l], lam_q2[l], lam_k2[l]])
        oa = diff_attention(q64_fm, qa_blk, k64, v_fm, va_blk, lam_vecs, subln_gain[l], lam_init, seq)
        ob = fox_attention(qb_fm, 0, k_aug, v_fm, vb_blk, seq)
        oc = dsa_attention(q128_fm, k128, v_fm, q64_fm, k64, blk(W_DIFF), wi_t, seq)

        merged = merge_branches(h, oa.reshape(n, -1), ob.reshape(n, -1), oc.reshape(n, -1), merge_w, merge_b,
                                w_branch_a, w_branch_b, w_branch_c, l, seq)
        xf = resid_mm(merged, w_out, l, xf, modl, 5, 1.0, seq)

        h = norm_mod(xf, norm_ffn2[l], modl, 6, seq)
        u = ffn_up(h, ffn2_w1, ffn2_w3, l)
        xf = resid_mm(u, ffn2_w2, l, xf, modl, 8, 0.5, seq)

    return final_norm(xf, norm_final).reshape(batch, seq, d)
```

```python
import functools
import math

import numpy as np
import jax
import jax.numpy as jnp
from jax import lax
from jax.experimental import pallas as pl
from jax.experimental.pallas import tpu as pltpu

HEAD_DIM = 128
H_DIFF = 6
DK_DIFF = HEAD_DIM // 2
H_FOX = 6
H_DSA = 4
H_IDX = 8
D_IDX = 64
TOPK_MAX = 256
ROPE_THETA = 10000.0
NORM_EPS = 1e-6
N_BRANCH = 3
N_MOD = 9
NEG_INF = -1e30
IDX_W_SCALE = (H_IDX ** -0.5) * (D_IDX ** -0.5)
LOG2E = math.log2(math.e)

W_DIFF = H_DIFF * HEAD_DIM
W_FOX = H_FOX * HEAD_DIM
W_DSA = H_DSA * HEAD_DIM
W_IDX = H_IDX * D_IDX

LANES = 128
MXU_WIDTH = 256
VMEM_LIMIT = 56 * 1024 * 1024
TOKEN_TILE = 512
INT32_MIN = -(2 ** 31)

BF16 = jnp.bfloat16
F32 = jnp.float32


def _order_key_of(value):
    bits = int(np.array(value, np.float32).view(np.int32))
    return bits ^ ((bits >> 31) & 0x7FFFFFFF)


KEY_NEG_INF = _order_key_of(NEG_INF)


def _params(*semantics):
    return pltpu.CompilerParams(dimension_semantics=semantics, vmem_limit_bytes=VMEM_LIMIT)


def _nt_dot(a, b):
    return lax.dot_general(a, b, (((1,), (1,)), ((), ())), preferred_element_type=F32)


def _dot(a, b):
    return jnp.dot(a, b, preferred_element_type=F32)


def _adaln_kernel(c_ref, w_ref, b_ref, o_ref):
    c = c_ref[...]
    c_act = (c * jax.nn.sigmoid(c)).astype(BF16)
    o_ref[...] = _dot(c_act, w_ref[...].astype(BF16)) + b_ref[...]


def adaln_mod(c, w_ada, b_ada):
    depth, d, nd = w_ada.shape
    b = c.shape[0]
    tn = min(1024, d)
    return pl.pallas_call(
        _adaln_kernel,
        grid=(depth, nd // tn),
        in_specs=[
            pl.BlockSpec((b, d), lambda l, j: (0, 0)),
            pl.BlockSpec((None, d, tn), lambda l, j: (l, 0, j)),
            pl.BlockSpec((None, 1, tn), lambda l, j: (l, 0, j)),
        ],
        out_specs=pl.BlockSpec((None, b, tn), lambda l, j: (l, 0, j)),
        out_shape=jax.ShapeDtypeStruct((depth, b, nd), F32),
        compiler_params=_params("arbitrary", "arbitrary"),
        name="adaln_mod",
    )(c, w_ada, b_ada.reshape(depth, 1, nd))


def _norm_mod_kernel(x_ref, gain_ref, sc_ref, sh_ref, o_ref):
    x = x_ref[...]
    y = x * lax.rsqrt(jnp.mean(x * x, axis=-1, keepdims=True) + NORM_EPS)
    o_ref[...] = ((y * gain_ref[...]) * (1.0 + sc_ref[...]) + sh_ref[...]).astype(o_ref.dtype)


def norm_mod(x, gain, modl, i_shift, seq):
    n, d = x.shape
    tm = min(512, seq)
    return pl.pallas_call(
        _norm_mod_kernel,
        grid=(n // tm,),
        in_specs=[
            pl.BlockSpec((tm, d), lambda i: (i, 0)),
            pl.BlockSpec((1, d), lambda i: (0, 0)),
            pl.BlockSpec((None, 1, d), lambda i: ((i * tm) // seq * N_MOD + i_shift + 1, 0, 0)),
            pl.BlockSpec((None, 1, d), lambda i: ((i * tm) // seq * N_MOD + i_shift, 0, 0)),
        ],
        out_specs=pl.BlockSpec((tm, d), lambda i: (i, 0)),
        out_shape=jax.ShapeDtypeStruct((n, d), BF16),
        compiler_params=_params("arbitrary"),
        name="norm_mod",
    )(x, gain.reshape(1, d), modl, modl)


def _final_norm_kernel(x_ref, gain_ref, o_ref):
    x = x_ref[...]
    y = x * lax.rsqrt(jnp.mean(x * x, axis=-1, keepdims=True) + NORM_EPS)
    o_ref[...] = y * gain_ref[...]


def final_norm(x, gain):
    n, d = x.shape
    tm = min(512, n)
    return pl.pallas_call(
        _final_norm_kernel,
        grid=(n // tm,),
        in_specs=[pl.BlockSpec((tm, d), lambda i: (i, 0)), pl.BlockSpec((1, d), lambda i: (0, 0))],
        out_specs=pl.BlockSpec((tm, d), lambda i: (i, 0)),
        out_shape=jax.ShapeDtypeStruct((n, d), F32),
        compiler_params=_params("arbitrary"),
        name="final_norm",
    )(x, gain.reshape(1, d))


def _cast_on_first_row_step(pairs):
    @pl.when(pl.program_id(1) == 0)
    def _():
        for src_ref, dst_ref in pairs:
            dst_ref[...] = src_ref[...].astype(dst_ref.dtype)


def _ffn_up_kernel(h_ref, w1_ref, w3_ref, o_ref, w1b_ref, w3b_ref):
    _cast_on_first_row_step(((w1_ref, w1b_ref), (w3_ref, w3b_ref)))
    h = h_ref[...]
    a = _dot(h, w1b_ref[...])
    b = _dot(h, w3b_ref[...])
    o_ref[...] = ((a * jax.nn.sigmoid(a)) * b).astype(o_ref.dtype)


def ffn_up(h, w1, w3, layer):
    n, d = h.shape
    f = w1.shape[2]
    tm = min(1024, n)
    tn = 512 if f % 512 == 0 else f
    w_spec = pl.BlockSpec((None, d, tn), lambda j, i: (layer, 0, j))
    return pl.pallas_call(
        _ffn_up_kernel,
        grid=(f // tn, n // tm),
        in_specs=[pl.BlockSpec((tm, d), lambda j, i: (i, 0)), w_spec, w_spec],
        out_specs=pl.BlockSpec((tm, tn), lambda j, i: (i, j)),
        out_shape=jax.ShapeDtypeStruct((n, f), BF16),
        scratch_shapes=[pltpu.VMEM((d, tn), BF16), pltpu.VMEM((d, tn), BF16)],
        compiler_params=_params("arbitrary", "arbitrary"),
        name="ffn_up",
    )(h, w1, w3)


def _resid_mm_kernel(a_ref, w_ref, x_ref, g_ref, o_ref, wb_ref, *, gscale):
    _cast_on_first_row_step(((w_ref, wb_ref),))
    y = _dot(a_ref[...], wb_ref[...])
    o_ref[...] = x_ref[...] + (gscale * g_ref[...]) * y


def resid_mm(a, w, layer, x, modl, i_gate, gscale, seq):
    n, k = a.shape
    d = w.shape[2]
    tm = min(512, seq)
    tn = min(512, d)
    return pl.pallas_call(
        functools.partial(_resid_mm_kernel, gscale=gscale),
        grid=(d // tn, n // tm),
        in_specs=[
            pl.BlockSpec((tm, k), lambda j, i: (i, 0)),
            pl.BlockSpec((None, k, tn), lambda j, i: (layer, 0, j)),
            pl.BlockSpec((tm, tn), lambda j, i: (i, j)),
            pl.BlockSpec((None, 1, tn), lambda j, i: ((i * tm) // seq * N_MOD + i_gate, 0, j)),
        ],
        out_specs=pl.BlockSpec((tm, tn), lambda j, i: (i, j)),
        out_shape=jax.ShapeDtypeStruct((n, d), F32),
        scratch_shapes=[pltpu.VMEM((k, tn), BF16)],
        compiler_params=_params("arbitrary", "arbitrary"),
        name="resid_mm",
    )(a, w, x, modl)


def _widest_tile(n_chunks, chunk, bytes_per_unit, limit_bytes):
    return chunk * max(t for t in range(1, n_chunks + 1)
                       if n_chunks % t == 0 and (t == 1 or t * chunk * bytes_per_unit <= limit_bytes))


def _swap_halves(z, group):
    if group == LANES:
        return pltpu.roll(z, LANES // 2, axis=1)
    half = group // 2
    lane = lax.broadcasted_iota(jnp.int32, z.shape, 1)
    from_above = pltpu.roll(z, LANES - half, axis=1)
    from_below = pltpu.roll(z, half, axis=1)
    return jnp.where((lane & (group - 1)) < half, from_above, from_below)


def _proj_kernel(h_ref, w_ref, cos_ref, sin_ref, o_ref, wb_ref, *, group):
    _cast_on_first_row_step(((w_ref, wb_ref),))
    z = _dot(h_ref[...], wb_ref[...])
    if group == 0:
        o_ref[...] = z.astype(o_ref.dtype)
        return
    cos = cos_ref[...]
    sin = sin_ref[...]
    for c in range(z.shape[1] // LANES):
        zc = z[:, c * LANES:(c + 1) * LANES]
        o_ref[:, c * LANES:(c + 1) * LANES] = (zc * cos + _swap_halves(zc, group) * sin).astype(o_ref.dtype)


def proj(h, w, cos, sin, group, seq):
    n, d = h.shape
    cols = w.shape[1]
    tm = min(512, seq)
    tn = _widest_tile(cols // LANES, LANES, d * 4, 12 * 2 ** 20)
    s_blocks = seq // tm
    return pl.pallas_call(
        functools.partial(_proj_kernel, group=group),
        grid=(cols // tn, n // tm),
        in_specs=[
            pl.BlockSpec((tm, d), lambda j, i: (i, 0)),
            pl.BlockSpec((d, tn), lambda j, i: (0, j)),
            pl.BlockSpec((tm, LANES), lambda j, i: (i % s_blocks, 0)),
            pl.BlockSpec((tm, LANES), lambda j, i: (i % s_blocks, 0)),
        ],
        out_specs=pl.BlockSpec((tm, tn), lambda j, i: (i, j)),
        out_shape=jax.ShapeDtypeStruct((n, cols), BF16),
        scratch_shapes=[pltpu.VMEM((d, tn), BF16)],
        compiler_params=_params("arbitrary", "arbitrary"),
        name=f"proj_rope{group}",
    )(h, w, cos, sin)


def _proj_fm_kernel(w_ref, h_ref, cos_ref, sin_ref, o_ref, wt_ref, *, group, row_scales):
    @pl.when(pl.program_id(1) == 0)
    def _():
        wt_ref[...] = w_ref[...].T.astype(wt_ref.dtype)

    zt = _nt_dot(wt_ref[...], h_ref[...])
    step = group if group else LANES
    half = group // 2
    scale_of_row = [s for n_rows, s in row_scales for _ in range(n_rows // step)]
    if len(row_scales) == 1:
        scale_of_row = scale_of_row[:1] * (zt.shape[0] // step)
    for c in range(zt.shape[0] // step):
        blk = zt[c * step:(c + 1) * step]
        if group:
            swapped = jnp.concatenate([blk[half:], blk[:half]], axis=0)
            blk = blk * cos_ref[...] + swapped * sin_ref[...]
        if scale_of_row[c] != 1.0:
            blk = blk * scale_of_row[c]
        o_ref[c * step:(c + 1) * step, :] = blk.astype(o_ref.dtype)


def proj_fm(h, w, cos_fm, sin_fm, group, seq, row_scales=None):
    n, d = h.shape
    rows = w.shape[1]
    tm = min(TOKEN_TILE, seq)
    unit = max(group, LANES)
    row_scales = row_scales or ((rows, 1.0),)
    limit = 12 * 2 ** 20 if len(row_scales) == 1 else rows * d * 4
    tn = _widest_tile(rows // unit, unit, d * 4, limit)
    s_blocks = seq // tm
    g = max(group, 8)
    return pl.pallas_call(
        functools.partial(_proj_fm_kernel, group=group, row_scales=row_scales),
        grid=(rows // tn, n // tm),
        in_specs=[
            pl.BlockSpec((d, tn), lambda j, i: (0, j)),
            pl.BlockSpec((tm, d), lambda j, i: (i, 0)),
            pl.BlockSpec((g, tm), lambda j, i: (0, i % s_blocks)),
            pl.BlockSpec((g, tm), lambda j, i: (0, i % s_blocks)),
        ],
        out_specs=pl.BlockSpec((None, tn, tm), lambda j, i: (i, j, 0)),
        out_shape=jax.ShapeDtypeStruct((n // tm, rows, tm), BF16),
        scratch_shapes=[pltpu.VMEM((tn, d), BF16)],
        compiler_params=_params("arbitrary", "arbitrary"),
        name=f"proj_fm_rope{group}",
    )(w, h, cos_fm, sin_fm)


def _small_proj_kernel(h_ref, wt_ref, bf_ref, logf_ref, wi_ref):
    zt = _nt_dot(wt_ref[...], h_ref[...])
    logf_ref[...] = jax.nn.log_sigmoid(zt[0:8, :] + bf_ref[...])
    wi_ref[...] = zt[8:16, :] * IDX_W_SCALE


def small_proj(h, w_small_t, b_forget8, seq):
    n, d = h.shape
    tm = min(512, seq)
    return pl.pallas_call(
        _small_proj_kernel,
        grid=(n // tm,),
        in_specs=[
            pl.BlockSpec((tm, d), lambda i: (i, 0)),
            pl.BlockSpec((16, d), lambda i: (0, 0)),
            pl.BlockSpec((8, 1), lambda i: (0, 0)),
        ],
        out_specs=[pl.BlockSpec((8, tm), lambda i: (0, i)), pl.BlockSpec((8, tm), lambda i: (0, i))],
        out_shape=[jax.ShapeDtypeStruct((8, n), F32), jax.ShapeDtypeStruct((8, n), F32)],
        compiler_params=_params("arbitrary"),
        name="small_proj",
    )(h, w_small_t, b_forget8)


def _split3(x):
    x1 = x.astype(BF16)
    r1 = x - x1.astype(F32)
    x2 = r1.astype(BF16)
    x3 = (r1 - x2.astype(F32)).astype(BF16)
    return x1, x2, x3


def _neg_cumsum_kernel(x_ref, hi_ref, mid_ref, lo_ref):
    x = x_ref[...]
    chunks = x.shape[0]
    r = lax.broadcasted_iota(jnp.int32, (LANES, LANES), 0)
    c = lax.broadcasted_iota(jnp.int32, (LANES, LANES), 1)
    upper = (r <= c).astype(BF16)
    within = sum(_dot(p, upper) for p in _split3(x))
    totals = jnp.broadcast_to(within[:, LANES - 1:LANES], (chunks, LANES))
    rr = lax.broadcasted_iota(jnp.int32, (chunks, chunks), 0)
    cc = lax.broadcasted_iota(jnp.int32, (chunks, chunks), 1)
    strict_lower = (cc < rr).astype(BF16)
    offset = sum(_dot(strict_lower, p) for p in _split3(totals))
    hi_ref[...], mid_ref[...], lo_ref[...] = _split3(-(within + offset) * LOG2E)


def neg_cumsum(logf_t, batch, seq):
    rows = logf_t.shape[0] * batch
    chunks = seq // LANES
    x = logf_t.reshape(rows, chunks, LANES)
    spec = pl.BlockSpec((None, chunks, LANES), lambda i: (i, 0, 0))
    pieces = pl.pallas_call(
        _neg_cumsum_kernel,
        grid=(rows,),
        in_specs=[spec],
        out_specs=[spec] * 3,
        out_shape=[jax.ShapeDtypeStruct((rows, chunks, LANES), BF16)] * 3,
        compiler_params=_params("arbitrary"),
        name="neg_cumsum",
    )(x)
    rows3 = jnp.stack([p.reshape(logf_t.shape[0], batch, seq) for p in pieces], axis=2)
    return jnp.pad(rows3, ((0, 0), (0, 0), (0, 8 - N_BIAS_PIECES), (0, 0)))


def _query_streams(tq):
    width = min(MXU_WIDTH, tq)
    return [slice(lo, lo + width) for lo in range(0, tq, width)]


ONES_ROWS = 16
ACC_ROWS = HEAD_DIM + ONES_ROWS


def _softmax_stats(s, m_prev):
    m_new = jnp.maximum(m_prev, jnp.max(s, axis=0, keepdims=True))
    return m_new, jnp.exp2(m_prev - m_new), jnp.exp2(s - m_new).astype(BF16)


def _with_ones_rows(vt):
    return jnp.concatenate([vt, jnp.ones((ONES_ROWS, vt.shape[1]), vt.dtype)], axis=0)


def _normalized(acc):
    return acc[0:HEAD_DIM] / acc[HEAD_DIM:HEAD_DIM + 1]


def _causal_mask(key0, query0, n_keys, n_queries):
    key = key0 + lax.broadcasted_iota(jnp.int32, (n_keys, n_queries), 0)
    query = query0 + lax.broadcasted_iota(jnp.int32, (n_keys, n_queries), 1)
    return key <= query


def _sweep_key_tiles(q_block, tq, tk, tile_fn):
    n_full = (q_block * tq) // tk

    def body(kj, carry):
        tile_fn(kj, False)
        return carry

    lax.fori_loop(0, n_full, body, 0)
    tile_fn(n_full, True)


def _attention_sweep(q_block, tq, tk, streams, raw_scores, logits, values, pipe):
    assert tk % tq == 0
    n_full = (q_block * tq) // tk
    n_trips = n_full // 2
    ids = range(len(streams))
    slot_a, slot_b = pipe[:3], pipe[3:]

    for (m_ref, acc_ref), qs in streams:
        m_ref[:, qs] = jnp.full((1, qs.stop - qs.start), -jnp.inf, F32)
        acc_ref[:, qs] = jnp.zeros((ACC_ROWS, qs.stop - qs.start), F32)
    slot_b[1][...] = jnp.zeros(slot_b[1].shape, BF16)
    slot_b[2][...] = jnp.ones(slot_b[2].shape, F32)

    def products(kj, slot):
        vts = values(kj)
        return [_dot(vts[i], slot[1][i]) for i in ids]

    def accumulate(pv, rescale):
        for i in ids:
            (_, acc_ref), qs = streams[i]
            acc_ref[:, qs] = rescale[i] * acc_ref[:, qs] + pv[i]

    def softmax_into(kj, raw, slot, i, masked):
        (m_ref, _), qs = streams[i]
        m_new, alpha, p = _softmax_stats(logits(kj, i, raw, masked), m_ref[:, qs])
        m_ref[:, qs] = m_new
        slot[1][i] = p
        slot[2][i] = alpha

    def add_pv(kj, slot):
        accumulate(products(kj, slot), [slot[2][i] for i in ids])

    def pipe_step(kj, cur, nxt, last=False):
        raw_next = None if last else raw_scores(kj + 1)
        pv_prev = products(jnp.maximum(kj - 1, 0), nxt)
        rescale_prev = [nxt[2][i] for i in ids]
        if not last:
            for i in ids:
                nxt[0][i] = raw_next[i]
        for i in ids:
            softmax_into(kj, cur[0][i], cur, i, last)
        accumulate(pv_prev, rescale_prev)

    raw0 = raw_scores(0)
    for i in ids:
        slot_a[0][i] = raw0[i]

    def trip(t, carry):
        pipe_step(2 * t, slot_a, slot_b)
        pipe_step(2 * t + 1, slot_b, slot_a)
        return carry

    lax.fori_loop(0, n_trips, trip, 0)

    @pl.when(n_full % 2 == 0)
    def _():
        pipe_step(n_full, slot_a, slot_b, last=True)
        add_pv(n_full, slot_a)

    @pl.when(n_full % 2 == 1)
    def _():
        pipe_step(n_full - 1, slot_a, slot_b)
        pipe_step(n_full, slot_b, slot_a, last=True)
        add_pv(n_full, slot_b)


def _softmax_scratch(tq):
    return [pltpu.VMEM((1, tq), F32), pltpu.VMEM((ACC_ROWS, tq), F32)]


def _pipe_scratch(n_streams, tk, width):
    slot = lambda: [pltpu.VMEM((n_streams, tk, width), F32), pltpu.VMEM((n_streams, tk, width), BF16),
                    pltpu.VMEM((n_streams, 1, width), F32)]
    return slot() + slot()


N_BIAS_PIECES = 3


def _fox_kernel(qt_ref, k_ref, bias_ref, vt_ref, o_ref, m_ref, acc_ref, qa_ref, kb_ref, *pipe, tq, tk):
    qb = pl.program_id(2)
    slices = _query_streams(tq)
    row = lax.broadcasted_iota(jnp.int32, (HEAD_DIM, tq), 0)
    qa_ref[...] = jnp.concatenate([qt_ref[...], jnp.where(row < N_BIAS_PIECES, 1.0, 0.0).astype(BF16)], axis=0)

    @pl.when(qb == 0)
    def _():
        zeros = jnp.zeros((HEAD_DIM - 8, tk), F32)
        for t in range(kb_ref.shape[0]):
            rows = bias_ref[:, t * tk:(t + 1) * tk].astype(F32)
            kb_ref[t] = jnp.concatenate([rows, zeros], axis=0).T.astype(BF16)

    def raw_scores(kj):
        k = jnp.concatenate([k_ref[pl.ds(pl.multiple_of(kj * tk, tk), tk), :], kb_ref[kj]], axis=1)
        return [_dot(k, qa_ref[:, qs]) for qs in slices]

    def logits(kj, i, raw, masked):
        qs = slices[i]
        if masked:
            raw = jnp.where(_causal_mask(kj * tk, qb * tq + qs.start, tk, qs.stop - qs.start), raw, NEG_INF)
        return raw

    def values(kj):
        return [_with_ones_rows(vt_ref[kj])] * len(slices)

    _attention_sweep(qb, tq, tk, [((m_ref, acc_ref), qs) for qs in slices], raw_scores, logits, values, pipe)
    o_ref[...] = _normalized(acc_ref[...]).T.astype(o_ref.dtype)


def fox_attention(q_fm, q_blk0, k_rows, bias_rows, v_fm, v_blk0, seq):
    b = k_rows.shape[0]
    tq = tk = q_fm.shape[2]
    nq = seq // tq
    slices = _query_streams(tq)
    return pl.pallas_call(
        functools.partial(_fox_kernel, tq=tq, tk=tk),
        grid=(b, H_FOX, nq),
        in_specs=[
            pl.BlockSpec((None, HEAD_DIM, tq), lambda bi, h, i: (bi * nq + i, q_blk0 + h, 0)),
            pl.BlockSpec((None, seq, HEAD_DIM), lambda bi, h, i: (bi, 0, h)),
            pl.BlockSpec((None, None, 8, seq), lambda bi, h, i: (h, bi, 0, 0)),
            pl.BlockSpec((nq, HEAD_DIM, tk), lambda bi, h, i: (bi, v_blk0 + h, 0)),
        ],
        out_specs=pl.BlockSpec((None, tq, HEAD_DIM), lambda bi, h, i: (bi, i, h)),
        out_shape=jax.ShapeDtypeStruct((b, seq, W_FOX), BF16),
        scratch_shapes=_softmax_scratch(tq)
        + [pltpu.VMEM((2 * HEAD_DIM, tq), BF16), pltpu.VMEM((nq, tk, HEAD_DIM), BF16)]
        + _pipe_scratch(len(slices), tk, slices[0].stop),
        compiler_params=_params("arbitrary", "arbitrary", "arbitrary"),
        name="fox_attention",
    )(q_fm, k_rows, bias_rows, v_fm)


def _diff_kernel(lam_ref, gain_ref, qt_ref, k_ref, vt_ref, o_ref,
                 m1_ref, acc1_ref, m2_ref, acc2_ref, q1_ref, q2_ref, *pipe, tq, tk, lam_init):
    qb = pl.program_id(2)
    zeros = jnp.zeros((DK_DIFF, tq), BF16)
    q1_ref[...] = jnp.concatenate([qt_ref[0:DK_DIFF, :], zeros], axis=0)
    q2_ref[...] = jnp.concatenate([zeros, qt_ref[DK_DIFF:HEAD_DIM, :]], axis=0)
    streams, q_refs = [], []
    for qs in _query_streams(tq):
        for refs, q_ref in (((m1_ref, acc1_ref), q1_ref), ((m2_ref, acc2_ref), q2_ref)):
            streams.append((refs, qs))
            q_refs.append(q_ref)

    def raw_scores(kj):
        k = k_ref[pl.ds(pl.multiple_of(kj * tk, tk), tk), :]
        return [_dot(k, q_ref[:, qs]) for q_ref, (_, qs) in zip(q_refs, streams)]

    def logits(kj, i, raw, masked):
        qs = streams[i][1]
        if masked:
            raw = jnp.where(_causal_mask(kj * tk, qb * tq + qs.start, tk, qs.stop - qs.start), raw, NEG_INF)
        return raw

    def values(kj):
        return [_with_ones_rows(vt_ref[kj])] * len(streams)

    _attention_sweep(qb, tq, tk, streams, raw_scores, logits, values, pipe)

    lam_vecs = lam_ref[...]
    dot1 = jnp.sum(lam_vecs[0:1] * lam_vecs[1:2], axis=1, keepdims=True)
    dot2 = jnp.sum(lam_vecs[2:3] * lam_vecs[3:4], axis=1, keepdims=True)
    lam = jnp.exp(dot1) - jnp.exp(dot2) + lam_init
    o = (_normalized(acc1_ref[...]) - lam * _normalized(acc2_ref[...])).T
    y = o * lax.rsqrt(jnp.mean(o * o, axis=-1, keepdims=True) + NORM_EPS)
    o_ref[...] = ((y * gain_ref[...]) * (1.0 - lam_init)).astype(o_ref.dtype)


def diff_attention(q_fm, q_blk0, k_rows, v_fm, v_blk0, lam_vecs, subln_gain, lam_init, seq):
    b = k_rows.shape[0]
    tq = tk = q_fm.shape[2]
    nq = seq // tq
    slices = _query_streams(tq)
    return pl.pallas_call(
        functools.partial(_diff_kernel, tq=tq, tk=tk, lam_init=lam_init),
        grid=(b, H_DIFF, nq),
        in_specs=[
            pl.BlockSpec((4, DK_DIFF), lambda bi, h, i: (0, 0)),
            pl.BlockSpec((1, HEAD_DIM), lambda bi, h, i: (0, 0)),
            pl.BlockSpec((None, HEAD_DIM, tq), lambda bi, h, i: (bi * nq + i, q_blk0 + h, 0)),
            pl.BlockSpec((None, seq, HEAD_DIM), lambda bi, h, i: (bi, 0, h)),
            pl.BlockSpec((nq, HEAD_DIM, tk), lambda bi, h, i: (bi, v_blk0 + h, 0)),
        ],
        out_specs=pl.BlockSpec((None, tq, HEAD_DIM), lambda bi, h, i: (bi, i, h)),
        out_shape=jax.ShapeDtypeStruct((b, seq, W_DIFF), BF16),
        scratch_shapes=_softmax_scratch(tq) + _softmax_scratch(tq)
        + [pltpu.VMEM((HEAD_DIM, tq), BF16), pltpu.VMEM((HEAD_DIM, tq), BF16)]
        + _pipe_scratch(2 * len(slices), tk, slices[0].stop),
        compiler_params=_params("arbitrary", "arbitrary", "arbitrary"),
        name="diff_attention",
    )(lam_vecs, subln_gain.reshape(1, HEAD_DIM), q_fm, k_rows, v_fm)


def _dsa_kernel(qt_ref, k_ref, vt_ref, qit_ref, ki_ref, wt_ref, o_ref,
                keys_ref, qm_ref, thr_ref, m_ref, acc_ref, m2_ref, acc2_ref, *pipe,
                tq, tk, seq, n_sel):
    qb = pl.program_id(1)
    n_tiles = (qb * tq) // tk + 1
    slices = _query_streams(tq)

    zeros = jnp.zeros((D_IDX, tq), BF16)
    for h in range(H_IDX):
        qm_ref[h] = jnp.concatenate([qit_ref[h * D_IDX:(h + 1) * D_IDX, :], zeros], axis=0)

    def score_tile(kj, masked):
        kk = ki_ref[pl.ds(pl.multiple_of(kj * tk, tk), tk), :]
        for qs in slices:
            width = qs.stop - qs.start
            rel_q = [_dot(kk, qm_ref[h, :, qs]) for h in range(H_IDX)]
            score = jnp.zeros((tk, width), F32)
            for h in range(H_IDX):
                score = score + wt_ref[h:h + 1, qs] * jnp.maximum(rel_q[h], 0.0)
            if masked:
                score = jnp.where(_causal_mask(kj * tk, qb * tq + qs.start, tk, width), score, NEG_INF)
            bits = lax.bitcast_convert_type(score, jnp.int32)
            keys_ref[kj, :, qs] = bits ^ ((bits >> 31) & 0x7FFFFFFF)

    _sweep_key_tiles(qb, tq, tk, score_tile)

    n_beyond = (seq - n_tiles * tk).astype(F32)

    def count_ge(cand):
        def body(kj, part):
            return part + jnp.sum(jnp.where(keys_ref[kj] >= cand, 1.0, 0.0), axis=0, keepdims=True)
        part = lax.fori_loop(0, n_tiles, body, jnp.zeros((1, tq), F32))
        return part + jnp.where(cand <= KEY_NEG_INF, n_beyond, 0.0)

    def bisect(step, thr):
        cand = thr + jnp.left_shift(jnp.int32(1), 31 - step)
        return jnp.where(count_ge(cand) >= n_sel, cand, thr)

    thr = lax.fori_loop(0, 32, bisect, jnp.full((1, tq), INT32_MIN, jnp.int32))
    thr_ref[...] = thr

    surplus = count_ge(thr) - n_sel
    any_surplus = jnp.max(surplus) > 0.0

    @pl.when(any_surplus)
    def _():
        need = jnp.where(surplus > 0.0, n_sel - count_ge(thr + 1), float(seq + 1))

        def count_tied_below(cut):
            def body(kj, part):
                idx = kj * tk + lax.broadcasted_iota(jnp.int32, (tk, tq), 0)
                hit = jnp.where((keys_ref[kj] == thr) & (idx < cut), 1.0, 0.0)
                return part + jnp.sum(hit, axis=0, keepdims=True)
            return lax.fori_loop(0, n_tiles, body, jnp.zeros((1, tq), F32))

        n_bits = max(1, (seq - 1).bit_length())

        def bisect_cut(step, cut):
            cand = cut + jnp.left_shift(jnp.int32(1), n_bits - 1 - step)
            return jnp.where(count_tied_below(cand) < need, cand, cut)

        cut = lax.fori_loop(0, n_bits, bisect_cut, jnp.zeros((1, tq), jnp.int32))

        def demote(kj, carry):
            idx = kj * tk + lax.broadcasted_iota(jnp.int32, (tk, tq), 0)
            t = keys_ref[kj]
            keys_ref[kj] = jnp.where((t == thr) & (idx > cut), thr - 1, t)
            return carry

        lax.fori_loop(0, n_tiles, demote, 0)

    states = ((m_ref, acc_ref), (m2_ref, acc2_ref))
    for h0 in range(0, H_DSA, 2):
        streams, rows = [], []
        for st, h in zip(states, (h0, h0 + 1)):
            for qs in slices:
                streams.append((st, qs))
                rows.append(slice(h * HEAD_DIM, (h + 1) * HEAD_DIM))

        def raw_scores(kj, streams=streams, rows=rows):
            off = pl.multiple_of(kj * tk, tk)
            return [_dot(k_ref[pl.ds(off, tk), r], qt_ref[r, qs]) for (_, qs), r in zip(streams, rows)]

        def logits(kj, i, raw, masked, streams=streams):
            qs = streams[i][1]
            sel = keys_ref[kj, :, qs] >= thr_ref[:, qs]
            if masked:
                sel = sel & _causal_mask(kj * tk, qb * tq + qs.start, tk, qs.stop - qs.start)
            return jnp.where(sel, raw, NEG_INF)

        def values(kj, h0=h0):
            per_head = [_with_ones_rows(vt_ref[kj, h * HEAD_DIM:(h + 1) * HEAD_DIM, :]) for h in (h0, h0 + 1)]
            return [v for v in per_head for _ in slices]

        _attention_sweep(qb, tq, tk, streams, raw_scores, logits, values, pipe)
        for (_, acc_r), h in zip(states, (h0, h0 + 1)):
            o_ref[:, h * HEAD_DIM:(h + 1) * HEAD_DIM] = _normalized(acc_r[...]).T.astype(o_ref.dtype)


def dsa_attention(q_fm, k_rows, v_fm, qi_fm, ki_rows, ki_blk, wi_t, seq):
    b = k_rows.shape[0]
    tq = tk = q_fm.shape[2]
    nq = seq // tq
    n_sel = min(TOPK_MAX, seq // 4)
    once = pl.Buffered(1)
    slices = _query_streams(tq)
    fm_q = lambda rows: pl.BlockSpec((None, rows, tq), lambda bi, i: (bi * nq + i, 0, 0))
    return pl.pallas_call(
        functools.partial(_dsa_kernel, tq=tq, tk=tk, seq=seq, n_sel=n_sel),
        grid=(b, nq),
        in_specs=[
            fm_q(W_DSA),
            pl.BlockSpec((None, seq, W_DSA), lambda bi, i: (bi, 0, 0), pipeline_mode=once),
            pl.BlockSpec((nq, W_DSA, tk), lambda bi, i: (bi, 0, 0), pipeline_mode=once),
            fm_q(W_IDX),
            pl.BlockSpec((None, seq, LANES), lambda bi, i: (bi, 0, ki_blk), pipeline_mode=once),
            pl.BlockSpec((H_IDX, tq), lambda bi, i: (0, bi * nq + i)),
        ],
        out_specs=pl.BlockSpec((None, tq, W_DSA), lambda bi, i: (bi, i, 0)),
        out_shape=jax.ShapeDtypeStruct((b, seq, W_DSA), BF16),
        scratch_shapes=[
            pltpu.VMEM((nq, tk, tq), jnp.int32),
            pltpu.VMEM((H_IDX, 2 * D_IDX, tq), BF16),
            pltpu.VMEM((1, tq), jnp.int32),
        ] + _softmax_scratch(tq) + _softmax_scratch(tq) + _pipe_scratch(2 * len(slices), tk, slices[0].stop),
        compiler_params=_params("arbitrary", "arbitrary"),
        name="dsa_attention",
    )(q_fm, k_rows, v_fm, qi_fm, ki_rows, wi_t)


def _merge_kernel(h_ref, oa_ref, ob_ref, oc_ref, mw0_ref, mw1_ref, mw2_ref, mb_ref, wa_ref, wb_ref, wc_ref, o_ref,
                  *bf16_refs):
    f32_refs = (mw0_ref, mw1_ref, mw2_ref, wa_ref, wb_ref, wc_ref)
    _cast_on_first_row_step(tuple(zip(f32_refs, bf16_refs)))
    gate_w, branch_w = bf16_refs[:N_BRANCH], bf16_refs[N_BRANCH:]
    h = h_ref[...]
    merged = None
    for i, o_b_ref in enumerate((oa_ref, ob_ref, oc_ref)):
        gate = jax.nn.sigmoid(_dot(h, gate_w[i][...]) + mb_ref[i])
        term = gate * _dot(o_b_ref[...], branch_w[i][...])
        merged = term if merged is None else merged + term
    o_ref[...] = merged.astype(o_ref.dtype)


def merge_branches(h, oa, ob, oc, merge_w, merge_b, wa, wb, wc, layer, seq):
    n, d = h.shape
    tm = min(1024, seq)
    tn = min(256, d)
    nj = d // tn
    row = lambda width: pl.BlockSpec((tm, width), lambda j, i: (i, 0))
    col = lambda rows: pl.BlockSpec((None, rows, tn), lambda j, i: (layer, 0, j))
    gate_w = lambda g: pl.BlockSpec((None, d, tn), lambda j, i: (layer, 0, g * nj + j))
    widths = (oa.shape[1], ob.shape[1], oc.shape[1])
    return pl.pallas_call(
        _merge_kernel,
        grid=(nj, n // tm),
        in_specs=[
            row(d), row(widths[0]), row(widths[1]), row(widths[2]),
            gate_w(0), gate_w(1), gate_w(2),
            pl.BlockSpec((None, N_BRANCH, 1, tn), lambda j, i: (layer, 0, 0, j)),
            col(widths[0]), col(widths[1]), col(widths[2]),
        ],
        out_specs=pl.BlockSpec((tm, tn), lambda j, i: (i, j)),
        out_shape=jax.ShapeDtypeStruct((n, d), BF16),
        scratch_shapes=[pltpu.VMEM((d, tn), BF16)] * N_BRANCH + [pltpu.VMEM((w, tn), BF16) for w in widths],
        compiler_params=_params("arbitrary", "arbitrary"),
        name="merge_branches",
    )(h, oa, ob, oc, merge_w, merge_w, merge_w, merge_b.reshape(merge_b.shape[0], N_BRANCH, 1, d), wa, wb, wc)


def _rope_tables(seq, dim):
    inv_freq = 1.0 / (ROPE_THETA ** (jnp.arange(0, dim, 2, dtype=F32) / dim))
    ang = jnp.arange(seq, dtype=F32)[:, None] * inv_freq[None, :]
    cos, sin = jnp.cos(ang), jnp.sin(ang)
    cos_g = jnp.concatenate([cos, cos], axis=1)
    sin_g = jnp.concatenate([-sin, sin], axis=1)
    reps = LANES // dim
    return jnp.tile(cos_g, (1, reps)), jnp.tile(sin_g, (1, reps)), cos_g.T, sin_g.T


def _split_w_in(w_in):
    sizes = (W_DIFF, W_DIFF, W_DIFF, W_FOX, W_FOX, W_FOX, H_FOX, W_DSA, W_DSA, W_DSA, W_IDX, D_IDX, H_IDX)
    parts, start = [], 0
    for n in sizes:
        parts.append(w_in[:, start:start + n])
        start += n
    return parts


def kernel(x, c, w_ada, b_ada, norm_ffn1, ffn1_w1, ffn1_w3, ffn1_w2, norm_mix, w_in, b_forget, lam_q1, lam_k1, lam_q2, lam_k2, subln_gain, merge_w, merge_b, w_branch_a, w_branch_b, w_branch_c, w_out, norm_ffn2, ffn2_w1, ffn2_w3, ffn2_w2, norm_final):
    batch, seq, d = x.shape
    depth = w_ada.shape[0]
    n = batch * seq
    tk = min(TOKEN_TILE, seq)
    bf = lambda a: a.astype(BF16)

    cos64, sin64, cos64_fm, sin64_fm = _rope_tables(seq, DK_DIFF)
    cos128, sin128, cos128_fm, sin128_fm = _rope_tables(seq, HEAD_DIM)
    mod = adaln_mod(c, w_ada, b_ada)
    xf = x.reshape(n, d)

    for l in range(depth):
        lam_init = 0.8 - 0.6 * math.exp(-0.3 * l)
        modl = mod[l].reshape(batch * N_MOD, 1, d)

        h = norm_mod(xf, norm_ffn1[l], modl, 0, seq)
        u = ffn_up(h, ffn1_w1, ffn1_w3, l)
        xf = resid_mm(u, ffn1_w2, l, xf, modl, 2, 0.5, seq)

        h = norm_mod(xf, norm_mix[l], modl, 3, seq)
        qa, ka, va, qb, kb, vb, fb, qc, kc, vc, qi, ki, wi = _split_w_in(w_in[l])
        k64 = proj(h, jnp.concatenate([ka, ki, ki], axis=1), cos64, sin64, DK_DIFF, seq).reshape(batch, seq, -1)
        k128 = proj(h, kc, cos128, sin128, HEAD_DIM, seq).reshape(batch, seq, -1)
        k0 = proj(h, kb, cos64, sin64, 0, seq).reshape(batch, seq, -1)
        log2e_over_sqrt = lambda width: (width ** -0.5) * LOG2E
        q64_fm = proj_fm(h, jnp.concatenate([qi, qa], axis=1), cos64_fm, sin64_fm, DK_DIFF, seq,
                         ((W_IDX, 1.0), (W_DIFF, log2e_over_sqrt(DK_DIFF))))
        q128_fm = proj_fm(h, qc, cos128_fm, sin128_fm, HEAD_DIM, seq, ((W_DSA, log2e_over_sqrt(HEAD_DIM)),))
        qb_fm = proj_fm(h, qb, cos64_fm, sin64_fm, 0, seq, ((W_FOX, log2e_over_sqrt(HEAD_DIM)),))
        v_fm = proj_fm(h, jnp.concatenate([vc, va, vb], axis=1), cos64_fm, sin64_fm, 0, seq)
        blk = lambda rows: rows // HEAD_DIM
        qa_blk, va_blk, vb_blk = blk(W_IDX), blk(W_DSA), blk(W_DSA + W_DIFF)

        w_small_t = bf(jnp.concatenate([fb, jnp.zeros((d, 8 - H_FOX), F32), wi], axis=1).T)
        b_forget8 = jnp.concatenate([b_forget[l], jnp.zeros((8 - H_FOX,), F32)]).reshape(8, 1)
        logf_t, wi_t = small_proj(h, w_small_t, b_forget8, seq)
        bias_rows = neg_cumsum(logf_t, batch, seq)

        lam_vecs = jnp.stack([lam_q1[l], lam_k1[l], lam_q2[l], lam_k2[l]])
        oa = diff_attention(q64_fm, qa_blk, k64, v_fm, va_blk, lam_vecs, subln_gain[l], lam_init, seq)
        ob = fox_attention(qb_fm, 0, k0, bias_rows, v_fm, vb_blk, seq)
        oc = dsa_attention(q128_fm, k128, v_fm, q64_fm, k64, blk(W_DIFF), wi_t, seq)

        merged = merge_branches(h, oa.reshape(n, -1), ob.reshape(n, -1), oc.reshape(n, -1), merge_w, merge_b,
                                w_branch_a, w_branch_b, w_branch_c, l, seq)
        xf = resid_mm(merged, w_out, l, xf, modl, 5, 1.0, seq)

        h = norm_mod(xf, norm_ffn2[l], modl, 6, seq)
        u = ffn_up(h, ffn2_w1, ffn2_w3, l)
        xf = resid_mm(u, ffn2_w2, l, xf, modl, 8, 0.5, seq)

    return final_norm(xf, norm_final).reshape(batch, seq, d)
```

```python
import functools
import math

import numpy as np
import jax
import jax.numpy as jnp
from jax import lax
from jax.experimental import pallas as pl
from jax.experimental.pallas import tpu as pltpu

HEAD_DIM = 128
H_DIFF = 6
DK_DIFF = HEAD_DIM // 2
H_FOX = 6
H_DSA = 4
H_IDX = 8
D_IDX = 64
TOPK_MAX = 256
ROPE_THETA = 10000.0
NORM_EPS = 1e-6
N_BRANCH = 3
N_MOD = 9
NEG_INF = -1e30
IDX_W_SCALE = (H_IDX ** -0.5) * (D_IDX ** -0.5)
LOG2E = math.log2(math.e)

W_DIFF = H_DIFF * HEAD_DIM
W_FOX = H_FOX * HEAD_DIM
W_DSA = H_DSA * HEAD_DIM
W_IDX = H_IDX * D_IDX

LANES = 128
MXU_WIDTH = 256
VMEM_LIMIT = 56 * 1024 * 1024
TOKEN_TILE = 512
I16_MIN, I16_MAX = -(2 ** 15), 2 ** 15 - 1

BF16 = jnp.bfloat16
F32 = jnp.float32


def _order_key_of(value):
    bits = int(np.array(value, np.float32).view(np.int32))
    return bits ^ ((bits >> 31) & 0x7FFFFFFF)


KEY_NEG_INF = _order_key_of(NEG_INF)


def _params(*semantics):
    return pltpu.CompilerParams(dimension_semantics=semantics, vmem_limit_bytes=VMEM_LIMIT)


def _nt_dot(a, b):
    return lax.dot_general(a, b, (((1,), (1,)), ((), ())), preferred_element_type=F32)


def _dot(a, b):
    return jnp.dot(a, b, preferred_element_type=F32)


def _adaln_kernel(c_ref, w_ref, b_ref, o_ref):
    c = c_ref[...]
    c_act = (c * jax.nn.sigmoid(c)).astype(BF16)
    o_ref[...] = _dot(c_act, w_ref[...].astype(BF16)) + b_ref[...]


def adaln_mod(c, w_ada, b_ada):
    depth, d, nd = w_ada.shape
    b = c.shape[0]
    tn = min(1024, d)
    return pl.pallas_call(
        _adaln_kernel,
        grid=(depth, nd // tn),
        in_specs=[
            pl.BlockSpec((b, d), lambda l, j: (0, 0)),
            pl.BlockSpec((None, d, tn), lambda l, j: (l, 0, j)),
            pl.BlockSpec((None, 1, tn), lambda l, j: (l, 0, j)),
        ],
        out_specs=pl.BlockSpec((None, b, tn), lambda l, j: (l, 0, j)),
        out_shape=jax.ShapeDtypeStruct((depth, b, nd), F32),
        compiler_params=_params("arbitrary", "arbitrary"),
        name="adaln_mod",
    )(c, w_ada, b_ada.reshape(depth, 1, nd))


def _norm_mod_kernel(x_ref, gain_ref, sc_ref, sh_ref, o_ref):
    x = x_ref[...]
    y = x * lax.rsqrt(jnp.mean(x * x, axis=-1, keepdims=True) + NORM_EPS)
    o_ref[...] = ((y * gain_ref[...]) * (1.0 + sc_ref[...]) + sh_ref[...]).astype(o_ref.dtype)


def norm_mod(x, gain, modl, i_shift, seq):
    n, d = x.shape
    tm = min(512, seq)
    return pl.pallas_call(
        _norm_mod_kernel,
        grid=(n // tm,),
        in_specs=[
            pl.BlockSpec((tm, d), lambda i: (i, 0)),
            pl.BlockSpec((1, d), lambda i: (0, 0)),
            pl.BlockSpec((None, 1, d), lambda i: ((i * tm) // seq * N_MOD + i_shift + 1, 0, 0)),
            pl.BlockSpec((None, 1, d), lambda i: ((i * tm) // seq * N_MOD + i_shift, 0, 0)),
        ],
        out_specs=pl.BlockSpec((tm, d), lambda i: (i, 0)),
        out_shape=jax.ShapeDtypeStruct((n, d), BF16),
        compiler_params=_params("arbitrary"),
        name="norm_mod",
    )(x, gain.reshape(1, d), modl, modl)


def _final_norm_kernel(x_ref, gain_ref, o_ref):
    x = x_ref[...]
    y = x * lax.rsqrt(jnp.mean(x * x, axis=-1, keepdims=True) + NORM_EPS)
    o_ref[...] = y * gain_ref[...]


def final_norm(x, gain):
    n, d = x.shape
    tm = min(512, n)
    return pl.pallas_call(
        _final_norm_kernel,
        grid=(n // tm,),
        in_specs=[pl.BlockSpec((tm, d), lambda i: (i, 0)), pl.BlockSpec((1, d), lambda i: (0, 0))],
        out_specs=pl.BlockSpec((tm, d), lambda i: (i, 0)),
        out_shape=jax.ShapeDtypeStruct((n, d), F32),
        compiler_params=_params("arbitrary"),
        name="final_norm",
    )(x, gain.reshape(1, d))


def _cast_on_first_row_step(pairs):
    @pl.when(pl.program_id(1) == 0)
    def _():
        for src_ref, dst_ref in pairs:
            dst_ref[...] = src_ref[...].astype(dst_ref.dtype)


def _ffn_up_kernel(h_ref, w1_ref, w3_ref, o_ref, w1b_ref, w3b_ref):
    _cast_on_first_row_step(((w1_ref, w1b_ref), (w3_ref, w3b_ref)))
    h = h_ref[...]
    a = _dot(h, w1b_ref[...])
    b = _dot(h, w3b_ref[...])
    o_ref[...] = ((a * jax.nn.sigmoid(a)) * b).astype(o_ref.dtype)


def ffn_up(h, w1, w3, layer):
    n, d = h.shape
    f = w1.shape[2]
    tm = min(1024, n)
    tn = 512 if f % 512 == 0 else f
    w_spec = pl.BlockSpec((None, d, tn), lambda j, i: (layer, 0, j))
    return pl.pallas_call(
        _ffn_up_kernel,
        grid=(f // tn, n // tm),
        in_specs=[pl.BlockSpec((tm, d), lambda j, i: (i, 0)), w_spec, w_spec],
        out_specs=pl.BlockSpec((tm, tn), lambda j, i: (i, j)),
        out_shape=jax.ShapeDtypeStruct((n, f), BF16),
        scratch_shapes=[pltpu.VMEM((d, tn), BF16), pltpu.VMEM((d, tn), BF16)],
        compiler_params=_params("arbitrary", "arbitrary"),
        name="ffn_up",
    )(h, w1, w3)


def _resid_mm_kernel(a_ref, w_ref, x_ref, g_ref, o_ref, wb_ref, *, gscale):
    _cast_on_first_row_step(((w_ref, wb_ref),))
    y = _dot(a_ref[...], wb_ref[...])
    o_ref[...] = x_ref[...] + (gscale * g_ref[...]) * y


def resid_mm(a, w, layer, x, modl, i_gate, gscale, seq):
    n, k = a.shape
    d = w.shape[2]
    tm = min(512, seq)
    tn = min(512, d)
    return pl.pallas_call(
        functools.partial(_resid_mm_kernel, gscale=gscale),
        grid=(d // tn, n // tm),
        in_specs=[
            pl.BlockSpec((tm, k), lambda j, i: (i, 0)),
            pl.BlockSpec((None, k, tn), lambda j, i: (layer, 0, j)),
            pl.BlockSpec((tm, tn), lambda j, i: (i, j)),
            pl.BlockSpec((None, 1, tn), lambda j, i: ((i * tm) // seq * N_MOD + i_gate, 0, j)),
        ],
        out_specs=pl.BlockSpec((tm, tn), lambda j, i: (i, j)),
        out_shape=jax.ShapeDtypeStruct((n, d), F32),
        scratch_shapes=[pltpu.VMEM((k, tn), BF16)],
        compiler_params=_params("arbitrary", "arbitrary"),
        name="resid_mm",
    )(a, w, x, modl)


def _widest_tile(n_chunks, chunk, bytes_per_unit, limit_bytes):
    return chunk * max(t for t in range(1, n_chunks + 1)
                       if n_chunks % t == 0 and (t == 1 or t * chunk * bytes_per_unit <= limit_bytes))


def _swap_halves(z, group):
    if group == LANES:
        return pltpu.roll(z, LANES // 2, axis=1)
    half = group // 2
    lane = lax.broadcasted_iota(jnp.int32, z.shape, 1)
    from_above = pltpu.roll(z, LANES - half, axis=1)
    from_below = pltpu.roll(z, half, axis=1)
    return jnp.where((lane & (group - 1)) < half, from_above, from_below)


def _proj_kernel(h_ref, w_ref, cos_ref, sin_ref, o_ref, wb_ref, *, group):
    _cast_on_first_row_step(((w_ref, wb_ref),))
    z = _dot(h_ref[...], wb_ref[...])
    if group == 0:
        o_ref[...] = z.astype(o_ref.dtype)
        return
    cos = cos_ref[...]
    sin = sin_ref[...]
    for c in range(z.shape[1] // LANES):
        zc = z[:, c * LANES:(c + 1) * LANES]
        o_ref[:, c * LANES:(c + 1) * LANES] = (zc * cos + _swap_halves(zc, group) * sin).astype(o_ref.dtype)


def proj(h, w, cos, sin, group, seq):
    n, d = h.shape
    cols = w.shape[1]
    tm = min(512, seq)
    tn = _widest_tile(cols // LANES, LANES, d * 4, 12 * 2 ** 20)
    s_blocks = seq // tm
    return pl.pallas_call(
        functools.partial(_proj_kernel, group=group),
        grid=(cols // tn, n // tm),
        in_specs=[
            pl.BlockSpec((tm, d), lambda j, i: (i, 0)),
            pl.BlockSpec((d, tn), lambda j, i: (0, j)),
            pl.BlockSpec((tm, LANES), lambda j, i: (i % s_blocks, 0)),
            pl.BlockSpec((tm, LANES), lambda j, i: (i % s_blocks, 0)),
        ],
        out_specs=pl.BlockSpec((tm, tn), lambda j, i: (i, j)),
        out_shape=jax.ShapeDtypeStruct((n, cols), BF16),
        scratch_shapes=[pltpu.VMEM((d, tn), BF16)],
        compiler_params=_params("arbitrary", "arbitrary"),
        name=f"proj_rope{group}",
    )(h, w, cos, sin)


def _proj_fm_kernel(w_ref, h_ref, cos_ref, sin_ref, o_ref, wt_ref, *, group, row_scales):
    @pl.when(pl.program_id(1) == 0)
    def _():
        wt_ref[...] = w_ref[...].T.astype(wt_ref.dtype)

    zt = _nt_dot(wt_ref[...], h_ref[...])
    step = group if group else LANES
    half = group // 2
    scale_of_row = [s for n_rows, s in row_scales for _ in range(n_rows // step)]
    if len(row_scales) == 1:
        scale_of_row = scale_of_row[:1] * (zt.shape[0] // step)
    for c in range(zt.shape[0] // step):
        blk = zt[c * step:(c + 1) * step]
        if group:
            swapped = jnp.concatenate([blk[half:], blk[:half]], axis=0)
            blk = blk * cos_ref[...] + swapped * sin_ref[...]
        if scale_of_row[c] != 1.0:
            blk = blk * scale_of_row[c]
        o_ref[c * step:(c + 1) * step, :] = blk.astype(o_ref.dtype)


def proj_fm(h, w, cos_fm, sin_fm, group, seq, row_scales=None):
    n, d = h.shape
    rows = w.shape[1]
    tm = min(TOKEN_TILE, seq)
    unit = max(group, LANES)
    row_scales = row_scales or ((rows, 1.0),)
    limit = 12 * 2 ** 20 if len(row_scales) == 1 else rows * d * 4
    tn = _widest_tile(rows // unit, unit, d * 4, limit)
    s_blocks = seq // tm
    g = max(group, 8)
    return pl.pallas_call(
        functools.partial(_proj_fm_kernel, group=group, row_scales=row_scales),
        grid=(rows // tn, n // tm),
        in_specs=[
            pl.BlockSpec((d, tn), lambda j, i: (0, j)),
            pl.BlockSpec((tm, d), lambda j, i: (i, 0)),
            pl.BlockSpec((g, tm), lambda j, i: (0, i % s_blocks)),
            pl.BlockSpec((g, tm), lambda j, i: (0, i % s_blocks)),
        ],
        out_specs=pl.BlockSpec((None, tn, tm), lambda j, i: (i, j, 0)),
        out_shape=jax.ShapeDtypeStruct((n // tm, rows, tm), BF16),
        scratch_shapes=[pltpu.VMEM((tn, d), BF16)],
        compiler_params=_params("arbitrary", "arbitrary"),
        name=f"proj_fm_rope{group}",
    )(w, h, cos_fm, sin_fm)


def _small_proj_kernel(h_ref, wt_ref, bf_ref, logf_ref, wi_ref):
    zt = _nt_dot(wt_ref[...], h_ref[...])
    logf_ref[...] = jax.nn.log_sigmoid(zt[0:8, :] + bf_ref[...])
    wi_ref[...] = zt[8:16, :] * IDX_W_SCALE


def small_proj(h, w_small_t, b_forget8, seq):
    n, d = h.shape
    tm = min(512, seq)
    return pl.pallas_call(
        _small_proj_kernel,
        grid=(n // tm,),
        in_specs=[
            pl.BlockSpec((tm, d), lambda i: (i, 0)),
            pl.BlockSpec((16, d), lambda i: (0, 0)),
            pl.BlockSpec((8, 1), lambda i: (0, 0)),
        ],
        out_specs=[pl.BlockSpec((8, tm), lambda i: (0, i)), pl.BlockSpec((8, tm), lambda i: (0, i))],
        out_shape=[jax.ShapeDtypeStruct((8, n), F32), jax.ShapeDtypeStruct((8, n), F32)],
        compiler_params=_params("arbitrary"),
        name="small_proj",
    )(h, w_small_t, b_forget8)


def _split3(x):
    x1 = x.astype(BF16)
    r1 = x - x1.astype(F32)
    x2 = r1.astype(BF16)
    x3 = (r1 - x2.astype(F32)).astype(BF16)
    return x1, x2, x3


def _neg_cumsum_kernel(x_ref, hi_ref, mid_ref, lo_ref):
    x = x_ref[...]
    chunks = x.shape[0]
    r = lax.broadcasted_iota(jnp.int32, (LANES, LANES), 0)
    c = lax.broadcasted_iota(jnp.int32, (LANES, LANES), 1)
    upper = (r <= c).astype(BF16)
    within = sum(_dot(p, upper) for p in _split3(x))
    totals = jnp.broadcast_to(within[:, LANES - 1:LANES], (chunks, LANES))
    rr = lax.broadcasted_iota(jnp.int32, (chunks, chunks), 0)
    cc = lax.broadcasted_iota(jnp.int32, (chunks, chunks), 1)
    strict_lower = (cc < rr).astype(BF16)
    offset = sum(_dot(strict_lower, p) for p in _split3(totals))
    hi_ref[...], mid_ref[...], lo_ref[...] = _split3(-(within + offset) * LOG2E)


def neg_cumsum(logf_t, batch, seq):
    rows = logf_t.shape[0] * batch
    chunks = seq // LANES
    x = logf_t.reshape(rows, chunks, LANES)
    spec = pl.BlockSpec((None, chunks, LANES), lambda i: (i, 0, 0))
    pieces = pl.pallas_call(
        _neg_cumsum_kernel,
        grid=(rows,),
        in_specs=[spec],
        out_specs=[spec] * 3,
        out_shape=[jax.ShapeDtypeStruct((rows, chunks, LANES), BF16)] * 3,
        compiler_params=_params("arbitrary"),
        name="neg_cumsum",
    )(x)
    rows3 = jnp.stack([p.reshape(logf_t.shape[0], batch, seq) for p in pieces], axis=2)
    return jnp.pad(rows3, ((0, 0), (0, 0), (0, 8 - N_BIAS_PIECES), (0, 0)))


def _query_streams(tq):
    width = min(MXU_WIDTH, tq)
    return [slice(lo, lo + width) for lo in range(0, tq, width)]


ONES_ROWS = 16
ACC_ROWS = HEAD_DIM + ONES_ROWS


def _softmax_stats(s, m_prev):
    m_new = jnp.maximum(m_prev, jnp.max(s, axis=0, keepdims=True))
    return m_new, jnp.exp2(m_prev - m_new), jnp.exp2(s - m_new).astype(BF16)


def _with_ones_rows(vt):
    return jnp.concatenate([vt, jnp.ones((ONES_ROWS, vt.shape[1]), vt.dtype)], axis=0)


def _normalized(acc):
    return acc[0:HEAD_DIM] / acc[HEAD_DIM:HEAD_DIM + 1]


def _causal_mask(key0, query0, n_keys, n_queries):
    key = key0 + lax.broadcasted_iota(jnp.int32, (n_keys, n_queries), 0)
    query = query0 + lax.broadcasted_iota(jnp.int32, (n_keys, n_queries), 1)
    return key <= query


def _sweep_key_tiles(q_block, tq, tk, tile_fn):
    n_full = (q_block * tq) // tk

    def body(kj, carry):
        tile_fn(kj, False)
        return carry

    lax.fori_loop(0, n_full, body, 0)
    tile_fn(n_full, True)


def _attention_sweep(q_block, tq, tk, streams, raw_scores, logits, values, pipe):
    assert tk % tq == 0
    n_full = (q_block * tq) // tk
    n_trips = n_full // 2
    ids = range(len(streams))
    slot_a, slot_b = pipe[:3], pipe[3:]

    for (m_ref, acc_ref), qs in streams:
        m_ref[:, qs] = jnp.full((1, qs.stop - qs.start), -jnp.inf, F32)
        acc_ref[:, qs] = jnp.zeros((ACC_ROWS, qs.stop - qs.start), F32)
    slot_b[1][...] = jnp.zeros(slot_b[1].shape, BF16)
    slot_b[2][...] = jnp.ones(slot_b[2].shape, F32)

    def products(kj, slot):
        vts = values(kj)
        return [_dot(vts[i], slot[1][i]) for i in ids]

    def accumulate(pv, rescale):
        for i in ids:
            (_, acc_ref), qs = streams[i]
            acc_ref[:, qs] = rescale[i] * acc_ref[:, qs] + pv[i]

    def softmax_into(kj, raw, slot, i, masked):
        (m_ref, _), qs = streams[i]
        m_new, alpha, p = _softmax_stats(logits(kj, i, raw, masked), m_ref[:, qs])
        m_ref[:, qs] = m_new
        slot[1][i] = p
        slot[2][i] = alpha

    def add_pv(kj, slot):
        accumulate(products(kj, slot), [slot[2][i] for i in ids])

    def pipe_step(kj, cur, nxt, last=False):
        raw_next = None if last else raw_scores(kj + 1)
        pv_prev = products(jnp.maximum(kj - 1, 0), nxt)
        rescale_prev = [nxt[2][i] for i in ids]
        if not last:
            for i in ids:
                nxt[0][i] = raw_next[i]
        for i in ids:
            softmax_into(kj, cur[0][i], cur, i, last)
        accumulate(pv_prev, rescale_prev)

    raw0 = raw_scores(0)
    for i in ids:
        slot_a[0][i] = raw0[i]

    def trip(t, carry):
        pipe_step(2 * t, slot_a, slot_b)
        pipe_step(2 * t + 1, slot_b, slot_a)
        return carry

    lax.fori_loop(0, n_trips, trip, 0)

    @pl.when(n_full % 2 == 0)
    def _():
        pipe_step(n_full, slot_a, slot_b, last=True)
        add_pv(n_full, slot_a)

    @pl.when(n_full % 2 == 1)
    def _():
        pipe_step(n_full - 1, slot_a, slot_b)
        pipe_step(n_full, slot_b, slot_a, last=True)
        add_pv(n_full, slot_b)


def _softmax_scratch(tq):
    return [pltpu.VMEM((1, tq), F32), pltpu.VMEM((ACC_ROWS, tq), F32)]


def _pipe_scratch(n_streams, tk, width):
    slot = lambda: [pltpu.VMEM((n_streams, tk, width), F32), pltpu.VMEM((n_streams, tk, width), BF16),
                    pltpu.VMEM((n_streams, 1, width), F32)]
    return slot() + slot()


N_BIAS_PIECES = 3


def _fox_kernel(qt_ref, k_ref, bias_ref, vt_ref, o_ref, m_ref, acc_ref, qa_ref, kb_ref, *pipe, tq, tk):
    qb = pl.program_id(2)
    slices = _query_streams(tq)
    row = lax.broadcasted_iota(jnp.int32, (HEAD_DIM, tq), 0)
    qa_ref[...] = jnp.concatenate([qt_ref[...], jnp.where(row < N_BIAS_PIECES, 1.0, 0.0).astype(BF16)], axis=0)

    @pl.when(qb == 0)
    def _():
        zeros = jnp.zeros((HEAD_DIM - 8, tk), F32)
        for t in range(kb_ref.shape[0]):
            rows = bias_ref[:, t * tk:(t + 1) * tk].astype(F32)
            kb_ref[t] = jnp.concatenate([rows, zeros], axis=0).T.astype(BF16)

    def raw_scores(kj):
        k = jnp.concatenate([k_ref[pl.ds(pl.multiple_of(kj * tk, tk), tk), :], kb_ref[kj]], axis=1)
        return [_dot(k, qa_ref[:, qs]) for qs in slices]

    def logits(kj, i, raw, masked):
        qs = slices[i]
        if masked:
            raw = jnp.where(_causal_mask(kj * tk, qb * tq + qs.start, tk, qs.stop - qs.start), raw, NEG_INF)
        return raw

    def values(kj):
        return [_with_ones_rows(vt_ref[kj])] * len(slices)

    _attention_sweep(qb, tq, tk, [((m_ref, acc_ref), qs) for qs in slices], raw_scores, logits, values, pipe)
    o_ref[...] = _normalized(acc_ref[...]).T.astype(o_ref.dtype)


def fox_attention(q_fm, q_blk0, k_rows, bias_rows, v_fm, v_blk0, seq):
    b = k_rows.shape[0]
    tq = tk = q_fm.shape[2]
    nq = seq // tq
    slices = _query_streams(tq)
    return pl.pallas_call(
        functools.partial(_fox_kernel, tq=tq, tk=tk),
        grid=(b, H_FOX, nq),
        in_specs=[
            pl.BlockSpec((None, HEAD_DIM, tq), lambda bi, h, i: (bi * nq + i, q_blk0 + h, 0)),
            pl.BlockSpec((None, seq, HEAD_DIM), lambda bi, h, i: (bi, 0, h)),
            pl.BlockSpec((None, None, 8, seq), lambda bi, h, i: (h, bi, 0, 0)),
            pl.BlockSpec((nq, HEAD_DIM, tk), lambda bi, h, i: (bi, v_blk0 + h, 0)),
        ],
        out_specs=pl.BlockSpec((None, tq, HEAD_DIM), lambda bi, h, i: (bi, i, h)),
        out_shape=jax.ShapeDtypeStruct((b, seq, W_FOX), BF16),
        scratch_shapes=_softmax_scratch(tq)
        + [pltpu.VMEM((2 * HEAD_DIM, tq), BF16), pltpu.VMEM((nq, tk, HEAD_DIM), BF16)]
        + _pipe_scratch(len(slices), tk, slices[0].stop),
        compiler_params=_params("arbitrary", "arbitrary", "arbitrary"),
        name="fox_attention",
    )(q_fm, k_rows, bias_rows, v_fm)


def _diff_kernel(lam_ref, gain_ref, qt_ref, k_ref, vt_ref, o_ref,
                 m1_ref, acc1_ref, m2_ref, acc2_ref, q1_ref, q2_ref, *pipe, tq, tk, lam_init):
    qb = pl.program_id(2)
    zeros = jnp.zeros((DK_DIFF, tq), BF16)
    q1_ref[...] = jnp.concatenate([qt_ref[0:DK_DIFF, :], zeros], axis=0)
    q2_ref[...] = jnp.concatenate([zeros, qt_ref[DK_DIFF:HEAD_DIM, :]], axis=0)
    streams, q_refs = [], []
    for qs in _query_streams(tq):
        for refs, q_ref in (((m1_ref, acc1_ref), q1_ref), ((m2_ref, acc2_ref), q2_ref)):
            streams.append((refs, qs))
            q_refs.append(q_ref)

    def raw_scores(kj):
        k = k_ref[pl.ds(pl.multiple_of(kj * tk, tk), tk), :]
        return [_dot(k, q_ref[:, qs]) for q_ref, (_, qs) in zip(q_refs, streams)]

    def logits(kj, i, raw, masked):
        qs = streams[i][1]
        if masked:
            raw = jnp.where(_causal_mask(kj * tk, qb * tq + qs.start, tk, qs.stop - qs.start), raw, NEG_INF)
        return raw

    def values(kj):
        return [_with_ones_rows(vt_ref[kj])] * len(streams)

    _attention_sweep(qb, tq, tk, streams, raw_scores, logits, values, pipe)

    lam_vecs = lam_ref[...]
    dot1 = jnp.sum(lam_vecs[0:1] * lam_vecs[1:2], axis=1, keepdims=True)
    dot2 = jnp.sum(lam_vecs[2:3] * lam_vecs[3:4], axis=1, keepdims=True)
    lam = jnp.exp(dot1) - jnp.exp(dot2) + lam_init
    o = (_normalized(acc1_ref[...]) - lam * _normalized(acc2_ref[...])).T
    y = o * lax.rsqrt(jnp.mean(o * o, axis=-1, keepdims=True) + NORM_EPS)
    o_ref[...] = ((y * gain_ref[...]) * (1.0 - lam_init)).astype(o_ref.dtype)


def diff_attention(q_fm, q_blk0, k_rows, v_fm, v_blk0, lam_vecs, subln_gain, lam_init, seq):
    b = k_rows.shape[0]
    tq = tk = q_fm.shape[2]
    nq = seq // tq
    slices = _query_streams(tq)
    return pl.pallas_call(
        functools.partial(_diff_kernel, tq=tq, tk=tk, lam_init=lam_init),
        grid=(b, H_DIFF, nq),
        in_specs=[
            pl.BlockSpec((4, DK_DIFF), lambda bi, h, i: (0, 0)),
            pl.BlockSpec((1, HEAD_DIM), lambda bi, h, i: (0, 0)),
            pl.BlockSpec((None, HEAD_DIM, tq), lambda bi, h, i: (bi * nq + i, q_blk0 + h, 0)),
            pl.BlockSpec((None, seq, HEAD_DIM), lambda bi, h, i: (bi, 0, h)),
            pl.BlockSpec((nq, HEAD_DIM, tk), lambda bi, h, i: (bi, v_blk0 + h, 0)),
        ],
        out_specs=pl.BlockSpec((None, tq, HEAD_DIM), lambda bi, h, i: (bi, i, h)),
        out_shape=jax.ShapeDtypeStruct((b, seq, W_DIFF), BF16),
        scratch_shapes=_softmax_scratch(tq) + _softmax_scratch(tq)
        + [pltpu.VMEM((HEAD_DIM, tq), BF16), pltpu.VMEM((HEAD_DIM, tq), BF16)]
        + _pipe_scratch(2 * len(slices), tk, slices[0].stop),
        compiler_params=_params("arbitrary", "arbitrary", "arbitrary"),
        name="diff_attention",
    )(lam_vecs, subln_gain.reshape(1, HEAD_DIM), q_fm, k_rows, v_fm)


def _dsa_kernel(qt_ref, k_ref, vt_ref, qit_ref, ki_ref, wt_ref, o_ref,
                hi_ref, lo_ref, qm_ref, cut_ref, m_ref, acc_ref, m2_ref, acc2_ref, *pipe,
                tq, tk, seq, n_sel):
    qb = pl.program_id(1)
    n_tiles = (qb * tq) // tk + 1
    slices = _query_streams(tq)

    zeros = jnp.zeros((D_IDX, tq), BF16)
    for h in range(H_IDX):
        qm_ref[h] = jnp.concatenate([qit_ref[h * D_IDX:(h + 1) * D_IDX, :], zeros], axis=0)

    def score_tile(kj, masked):
        kk = ki_ref[pl.ds(pl.multiple_of(kj * tk, tk), tk), :]
        for qs in slices:
            width = qs.stop - qs.start
            rel_q = [_dot(kk, qm_ref[h, :, qs]) for h in range(H_IDX)]
            score = jnp.zeros((tk, width), F32)
            for h in range(H_IDX):
                score = score + wt_ref[h:h + 1, qs] * jnp.maximum(rel_q[h], 0.0)
            if masked:
                score = jnp.where(_causal_mask(kj * tk, qb * tq + qs.start, tk, width), score, NEG_INF)
            bits = lax.bitcast_convert_type(score, jnp.int32)
            key = bits ^ ((bits >> 31) & 0x7FFFFFFF)
            hi_ref[kj, :, qs] = (key >> 16).astype(jnp.int16)
            lo_ref[kj, :, qs] = ((key & 0xFFFF) + I16_MIN).astype(jnp.int16)

    _sweep_key_tiles(qb, tq, tk, score_tile)

    n_beyond = (seq - n_tiles * tk).astype(F32)
    neg_hi, neg_lo = KEY_NEG_INF >> 16, (KEY_NEG_INF & 0xFFFF) + I16_MIN
    one, zero = jnp.int16(1), jnp.int16(0)
    as16 = lambda v: v.astype(jnp.int16)

    def count_hits(hit_fn):
        def body(kj, total):
            hit = jnp.where(hit_fn(kj), one, zero)
            part = hit[0:16]
            for r in range(1, tk // 16):
                part = part + hit[r * 16:(r + 1) * 16]
            return total + jnp.sum(part.astype(F32), axis=0, keepdims=True)
        return lax.fori_loop(0, n_tiles, body, jnp.zeros((1, tq), F32))

    def bisect16(enough):
        def step(s, t):
            cand = t + jnp.left_shift(jnp.int32(1), 15 - s)
            return jnp.where(enough(cand), cand, t)
        return lax.fori_loop(0, 16, step, jnp.full((1, tq), I16_MIN, jnp.int32))

    def count_hi_ge(cand):
        n = count_hits(lambda kj, c=as16(cand): hi_ref[kj] >= c)
        return n + jnp.where(cand <= neg_hi, n_beyond, 0.0)

    t_hi = bisect16(lambda cand: count_hi_ge(cand) >= n_sel)
    t_hi16 = as16(t_hi)
    n_above = jnp.where(t_hi < I16_MAX, count_hi_ge(jnp.minimum(t_hi + 1, I16_MAX)), 0.0)
    need_lo = n_sel - n_above

    def mask_lo(kj, carry):
        lo_ref[kj] = jnp.where(hi_ref[kj] == t_hi16, lo_ref[kj], jnp.int16(I16_MIN))
        return carry

    lax.fori_loop(0, n_tiles, mask_lo, 0)
    beyond_in_group = jnp.where(t_hi == neg_hi, n_beyond, 0.0)

    def count_lo_ge(cand):
        n = count_hits(lambda kj, c=as16(cand): lo_ref[kj] >= c)
        return n + jnp.where(cand <= neg_lo, beyond_in_group, 0.0)

    t_lo = bisect16(lambda cand: count_lo_ge(cand) >= need_lo)
    t_lo16 = as16(t_lo)

    def in_group(kj):
        return hi_ref[kj] == t_hi16

    n_ge = n_above + count_hits(lambda kj: in_group(kj) & (lo_ref[kj] >= t_lo16)) \
        + jnp.where(t_lo <= neg_lo, beyond_in_group, 0.0)
    surplus = n_ge - n_sel
    cut_ref[...] = jnp.full((1, tq), I16_MAX, jnp.int32)

    def key_index(kj):
        return as16(kj * tk + lax.broadcasted_iota(jnp.int32, (tk, tq), 0))

    @pl.when(jnp.max(surplus) > 0.0)
    def _():
        n_gt = n_above + count_hits(lambda kj: in_group(kj) & (lo_ref[kj] > t_lo16)) \
            + jnp.where(t_lo < neg_lo, beyond_in_group, 0.0)
        need = jnp.where(surplus > 0.0, n_sel - n_gt, float(seq + 1))

        def count_tied_below(cut):
            c = as16(cut)
            return count_hits(lambda kj: in_group(kj) & (lo_ref[kj] == t_lo16) & (key_index(kj) < c))

        n_bits = max(1, (seq - 1).bit_length())

        def bisect_cut(step, cut):
            cand = cut + jnp.left_shift(jnp.int32(1), n_bits - 1 - step)
            return jnp.where(count_tied_below(cand) < need, cand, cut)

        cut_ref[...] = lax.fori_loop(0, n_bits, bisect_cut, jnp.zeros((1, tq), jnp.int32))

    cut16 = as16(cut_ref[...])
    zero_bias, neg_bias = jnp.zeros((), BF16), jnp.asarray(NEG_INF, BF16)

    def write_bias(kj, carry):
        hi, lo = hi_ref[kj], lo_ref[kj]
        tied_ok = (lo == t_lo16) & (key_index(kj) <= cut16)
        sel = (hi > t_hi16) | ((hi == t_hi16) & ((lo > t_lo16) | tied_ok))
        hi_ref[kj] = lax.bitcast_convert_type(jnp.where(sel, zero_bias, neg_bias), jnp.int16)
        return carry

    lax.fori_loop(0, n_tiles, write_bias, 0)

    states = ((m_ref, acc_ref), (m2_ref, acc2_ref))
    for h0 in range(0, H_DSA, 2):
        streams, rows = [], []
        for st, h in zip(states, (h0, h0 + 1)):
            for qs in slices:
                streams.append((st, qs))
                rows.append(slice(h * HEAD_DIM, (h + 1) * HEAD_DIM))

        def raw_scores(kj, streams=streams, rows=rows):
            off = pl.multiple_of(kj * tk, tk)
            return [_dot(k_ref[pl.ds(off, tk), r], qt_ref[r, qs]) for (_, qs), r in zip(streams, rows)]

        def logits(kj, i, raw, masked, streams=streams):
            qs = streams[i][1]
            s = raw + lax.bitcast_convert_type(hi_ref[kj, :, qs], BF16).astype(F32)
            if masked:
                s = jnp.where(_causal_mask(kj * tk, qb * tq + qs.start, tk, qs.stop - qs.start), s, NEG_INF)
            return s

        def values(kj, h0=h0):
            per_head = [_with_ones_rows(vt_ref[kj, h * HEAD_DIM:(h + 1) * HEAD_DIM, :]) for h in (h0, h0 + 1)]
            return [v for v in per_head for _ in slices]

        _attention_sweep(qb, tq, tk, streams, raw_scores, logits, values, pipe)
        for (_, acc_r), h in zip(states, (h0, h0 + 1)):
            o_ref[:, h * HEAD_DIM:(h + 1) * HEAD_DIM] = _normalized(acc_r[...]).T.astype(o_ref.dtype)


def dsa_attention(q_fm, k_rows, v_fm, qi_fm, ki_rows, ki_blk, wi_t, seq):
    b = k_rows.shape[0]
    tq = tk = q_fm.shape[2]
    nq = seq // tq
    n_sel = min(TOPK_MAX, seq // 4)
    once = pl.Buffered(1)
    slices = _query_streams(tq)
    fm_q = lambda rows: pl.BlockSpec((None, rows, tq), lambda bi, i: (bi * nq + i, 0, 0))
    return pl.pallas_call(
        functools.partial(_dsa_kernel, tq=tq, tk=tk, seq=seq, n_sel=n_sel),
        grid=(b, nq),
        in_specs=[
            fm_q(W_DSA),
            pl.BlockSpec((None, seq, W_DSA), lambda bi, i: (bi, 0, 0), pipeline_mode=once),
            pl.BlockSpec((nq, W_DSA, tk), lambda bi, i: (bi, 0, 0), pipeline_mode=once),
            fm_q(W_IDX),
            pl.BlockSpec((None, seq, LANES), lambda bi, i: (bi, 0, ki_blk), pipeline_mode=once),
            pl.BlockSpec((H_IDX, tq), lambda bi, i: (0, bi * nq + i)),
        ],
        out_specs=pl.BlockSpec((None, tq, W_DSA), lambda bi, i: (bi, i, 0)),
        out_shape=jax.ShapeDtypeStruct((b, seq, W_DSA), BF16),
        scratch_shapes=[
            pltpu.VMEM((nq, tk, tq), jnp.int16),
            pltpu.VMEM((nq, tk, tq), jnp.int16),
            pltpu.VMEM((H_IDX, 2 * D_IDX, tq), BF16),
            pltpu.VMEM((1, tq), jnp.int32),
        ] + _softmax_scratch(tq) + _softmax_scratch(tq) + _pipe_scratch(2 * len(slices), tk, slices[0].stop),
        compiler_params=_params("arbitrary", "arbitrary"),
        name="dsa_attention",
    )(q_fm, k_rows, v_fm, qi_fm, ki_rows, wi_t)


def _merge_kernel(h_ref, oa_ref, ob_ref, oc_ref, mw0_ref, mw1_ref, mw2_ref, mb_ref, wa_ref, wb_ref, wc_ref, o_ref,
                  *bf16_refs):
    f32_refs = (mw0_ref, mw1_ref, mw2_ref, wa_ref, wb_ref, wc_ref)
    _cast_on_first_row_step(tuple(zip(f32_refs, bf16_refs)))
    gate_w, branch_w = bf16_refs[:N_BRANCH], bf16_refs[N_BRANCH:]
    h = h_ref[...]
    merged = None
    for i, o_b_ref in enumerate((oa_ref, ob_ref, oc_ref)):
        gate = jax.nn.sigmoid(_dot(h, gate_w[i][...]) + mb_ref[i])
        term = gate * _dot(o_b_ref[...], branch_w[i][...])
        merged = term if merged is None else merged + term
    o_ref[...] = merged.astype(o_ref.dtype)


def merge_branches(h, oa, ob, oc, merge_w, merge_b, wa, wb, wc, layer, seq):
    n, d = h.shape
    tm = min(1024, seq)
    tn = min(256, d)
    nj = d // tn
    row = lambda width: pl.BlockSpec((tm, width), lambda j, i: (i, 0))
    col = lambda rows: pl.BlockSpec((None, rows, tn), lambda j, i: (layer, 0, j))
    gate_w = lambda g: pl.BlockSpec((None, d, tn), lambda j, i: (layer, 0, g * nj + j))
    widths = (oa.shape[1], ob.shape[1], oc.shape[1])
    return pl.pallas_call(
        _merge_kernel,
        grid=(nj, n // tm),
        in_specs=[
            row(d), row(widths[0]), row(widths[1]), row(widths[2]),
            gate_w(0), gate_w(1), gate_w(2),
            pl.BlockSpec((None, N_BRANCH, 1, tn), lambda j, i: (layer, 0, 0, j)),
            col(widths[0]), col(widths[1]), col(widths[2]),
        ],
        out_specs=pl.BlockSpec((tm, tn), lambda j, i: (i, j)),
        out_shape=jax.ShapeDtypeStruct((n, d), BF16),
        scratch_shapes=[pltpu.VMEM((d, tn), BF16)] * N_BRANCH + [pltpu.VMEM((w, tn), BF16) for w in widths],
        compiler_params=_params("arbitrary", "arbitrary"),
        name="merge_branches",
    )(h, oa, ob, oc, merge_w, merge_w, merge_w, merge_b.reshape(merge_b.shape[0], N_BRANCH, 1, d), wa, wb, wc)


def _rope_tables(seq, dim):
    inv_freq = 1.0 / (ROPE_THETA ** (jnp.arange(0, dim, 2, dtype=F32) / dim))
    ang = jnp.arange(seq, dtype=F32)[:, None] * inv_freq[None, :]
    cos, sin = jnp.cos(ang), jnp.sin(ang)
    cos_g = jnp.concatenate([cos, cos], axis=1)
    sin_g = jnp.concatenate([-sin, sin], axis=1)
    reps = LANES // dim
    return jnp.tile(cos_g, (1, reps)), jnp.tile(sin_g, (1, reps)), cos_g.T, sin_g.T


def _split_w_in(w_in):
    sizes = (W_DIFF, W_DIFF, W_DIFF, W_FOX, W_FOX, W_FOX, H_FOX, W_DSA, W_DSA, W_DSA, W_IDX, D_IDX, H_IDX)
    parts, start = [], 0
    for n in sizes:
        parts.append(w_in[:, start:start + n])
        start += n
    return parts


def kernel(x, c, w_ada, b_ada, norm_ffn1, ffn1_w1, ffn1_w3, ffn1_w2, norm_mix, w_in, b_forget, lam_q1, lam_k1, lam_q2, lam_k2, subln_gain, merge_w, merge_b, w_branch_a, w_branch_b, w_branch_c, w_out, norm_ffn2, ffn2_w1, ffn2_w3, ffn2_w2, norm_final):
    batch, seq, d = x.shape
    depth = w_ada.shape[0]
    n = batch * seq
    tk = min(TOKEN_TILE, seq)
    bf = lambda a: a.astype(BF16)

    cos64, sin64, cos64_fm, sin64_fm = _rope_tables(seq, DK_DIFF)
    cos128, sin128, cos128_fm, sin128_fm = _rope_tables(seq, HEAD_DIM)
    mod = adaln_mod(c, w_ada, b_ada)
    xf = x.reshape(n, d)

    for l in range(depth):
        lam_init = 0.8 - 0.6 * math.exp(-0.3 * l)
        modl = mod[l].reshape(batch * N_MOD, 1, d)

        h = norm_mod(xf, norm_ffn1[l], modl, 0, seq)
        u = ffn_up(h, ffn1_w1, ffn1_w3, l)
        xf = resid_mm(u, ffn1_w2, l, xf, modl, 2, 0.5, seq)

        h = norm_mod(xf, norm_mix[l], modl, 3, seq)
        qa, ka, va, qb, kb, vb, fb, qc, kc, vc, qi, ki, wi = _split_w_in(w_in[l])
        k64 = proj(h, jnp.concatenate([ka, ki, ki], axis=1), cos64, sin64, DK_DIFF, seq).reshape(batch, seq, -1)
        k128 = proj(h, kc, cos128, sin128, HEAD_DIM, seq).reshape(batch, seq, -1)
        k0 = proj(h, kb, cos64, sin64, 0, seq).reshape(batch, seq, -1)
        log2e_over_sqrt = lambda width: (width ** -0.5) * LOG2E
        q64_fm = proj_fm(h, jnp.concatenate([qi, qa], axis=1), cos64_fm, sin64_fm, DK_DIFF, seq,
                         ((W_IDX, 1.0), (W_DIFF, log2e_over_sqrt(DK_DIFF))))
        q128_fm = proj_fm(h, qc, cos128_fm, sin128_fm, HEAD_DIM, seq, ((W_DSA, log2e_over_sqrt(HEAD_DIM)),))
        qb_fm = proj_fm(h, qb, cos64_fm, sin64_fm, 0, seq, ((W_FOX, log2e_over_sqrt(HEAD_DIM)),))
        v_fm = proj_fm(h, jnp.concatenate([vc, va, vb], axis=1), cos64_fm, sin64_fm, 0, seq)
        blk = lambda rows: rows // HEAD_DIM
        qa_blk, va_blk, vb_blk = blk(W_IDX), blk(W_DSA), blk(W_DSA + W_DIFF)

        w_small_t = bf(jnp.concatenate([fb, jnp.zeros((d, 8 - H_FOX), F32), wi], axis=1).T)
        b_forget8 = jnp.concatenate([b_forget[l], jnp.zeros((8 - H_FOX,), F32)]).reshape(8, 1)
        logf_t, wi_t = small_proj(h, w_small_t, b_forget8, seq)
        bias_rows = neg_cumsum(logf_t, batch, seq)

        lam_vecs = jnp.stack([lam_q1[l], lam_k1[l], lam_q2[l], lam_k2[l]])
        oa = diff_attention(q64_fm, qa_blk, k64, v_fm, va_blk, lam_vecs, subln_gain[l], lam_init, seq)
        ob = fox_attention(qb_fm, 0, k0, bias_rows, v_fm, vb_blk, seq)
        oc = dsa_attention(q128_fm, k128, v_fm, q64_fm, k64, blk(W_DIFF), wi_t, seq)

        merged = merge_branches(h, oa.reshape(n, -1), ob.reshape(n, -1), oc.reshape(n, -1), merge_w, merge_b,
                                w_branch_a, w_branch_b, w_branch_c, l, seq)
        xf = resid_mm(merged, w_out, l, xf, modl, 5, 1.0, seq)

        h = norm_mod(xf, norm_ffn2[l], modl, 6, seq)
        u = ffn_up(h, ffn2_w1, ffn2_w3, l)
        xf = resid_mm(u, ffn2_w2, l, xf, modl, 8, 0.5, seq)

    return final_norm(xf, norm_final).reshape(batch, seq, d)
```

```python
import functools
import math

import numpy as np
import jax
import jax.numpy as jnp
from jax import lax
from jax.experimental import pallas as pl
from jax.experimental.pallas import tpu as pltpu

HEAD_DIM = 128
H_DIFF = 6
DK_DIFF = HEAD_DIM // 2
H_FOX = 6
H_DSA = 4
H_IDX = 8
D_IDX = 64
TOPK_MAX = 256
ROPE_THETA = 10000.0
NORM_EPS = 1e-6
N_BRANCH = 3
N_MOD = 9
NEG_INF = -1e30
IDX_W_SCALE = (H_IDX ** -0.5) * (D_IDX ** -0.5)
LOG2E = math.log2(math.e)

W_DIFF = H_DIFF * HEAD_DIM
W_FOX = H_FOX * HEAD_DIM
W_DSA = H_DSA * HEAD_DIM
W_IDX = H_IDX * D_IDX

LANES = 128
MXU_WIDTH = 256
VMEM_LIMIT = 56 * 1024 * 1024
TOKEN_TILE = 512
I16_MIN, I16_MAX = -(2 ** 15), 2 ** 15 - 1

BF16 = jnp.bfloat16
F32 = jnp.float32


def _order_key_of(value):
    bits = int(np.array(value, np.float32).view(np.int32))
    return bits ^ ((bits >> 31) & 0x7FFFFFFF)


KEY_NEG_INF = _order_key_of(NEG_INF)


def _params(*semantics):
    return pltpu.CompilerParams(dimension_semantics=semantics, vmem_limit_bytes=VMEM_LIMIT)


def _nt_dot(a, b):
    return lax.dot_general(a, b, (((1,), (1,)), ((), ())), preferred_element_type=F32)


def _dot(a, b):
    return jnp.dot(a, b, preferred_element_type=F32)


def _adaln_kernel(c_ref, w_ref, b_ref, o_ref):
    c = c_ref[...]
    c_act = (c * jax.nn.sigmoid(c)).astype(BF16)
    o_ref[...] = _dot(c_act, w_ref[...].astype(BF16)) + b_ref[...]


def adaln_mod(c, w_ada, b_ada):
    depth, d, nd = w_ada.shape
    b = c.shape[0]
    tn = min(1024, d)
    return pl.pallas_call(
        _adaln_kernel,
        grid=(depth, nd // tn),
        in_specs=[
            pl.BlockSpec((b, d), lambda l, j: (0, 0)),
            pl.BlockSpec((None, d, tn), lambda l, j: (l, 0, j)),
            pl.BlockSpec((None, 1, tn), lambda l, j: (l, 0, j)),
        ],
        out_specs=pl.BlockSpec((None, b, tn), lambda l, j: (l, 0, j)),
        out_shape=jax.ShapeDtypeStruct((depth, b, nd), F32),
        compiler_params=_params("arbitrary", "arbitrary"),
        name="adaln_mod",
    )(c, w_ada, b_ada.reshape(depth, 1, nd))


def _norm_mod_kernel(x_ref, gain_ref, sc_ref, sh_ref, o_ref):
    x = x_ref[...]
    y = x * lax.rsqrt(jnp.mean(x * x, axis=-1, keepdims=True) + NORM_EPS)
    o_ref[...] = ((y * gain_ref[...]) * (1.0 + sc_ref[...]) + sh_ref[...]).astype(o_ref.dtype)


def norm_mod(x, gain, modl, i_shift, seq):
    n, d = x.shape
    tm = min(512, seq)
    return pl.pallas_call(
        _norm_mod_kernel,
        grid=(n // tm,),
        in_specs=[
            pl.BlockSpec((tm, d), lambda i: (i, 0)),
            pl.BlockSpec((1, d), lambda i: (0, 0)),
            pl.BlockSpec((None, 1, d), lambda i: ((i * tm) // seq * N_MOD + i_shift + 1, 0, 0)),
            pl.BlockSpec((None, 1, d), lambda i: ((i * tm) // seq * N_MOD + i_shift, 0, 0)),
        ],
        out_specs=pl.BlockSpec((tm, d), lambda i: (i, 0)),
        out_shape=jax.ShapeDtypeStruct((n, d), BF16),
        compiler_params=_params("arbitrary"),
        name="norm_mod",
    )(x, gain.reshape(1, d), modl, modl)


def _final_norm_kernel(x_ref, gain_ref, o_ref):
    x = x_ref[...]
    y = x * lax.rsqrt(jnp.mean(x * x, axis=-1, keepdims=True) + NORM_EPS)
    o_ref[...] = y * gain_ref[...]


def final_norm(x, gain):
    n, d = x.shape
    tm = min(512, n)
    return pl.pallas_call(
        _final_norm_kernel,
        grid=(n // tm,),
        in_specs=[pl.BlockSpec((tm, d), lambda i: (i, 0)), pl.BlockSpec((1, d), lambda i: (0, 0))],
        out_specs=pl.BlockSpec((tm, d), lambda i: (i, 0)),
        out_shape=jax.ShapeDtypeStruct((n, d), F32),
        compiler_params=_params("arbitrary"),
        name="final_norm",
    )(x, gain.reshape(1, d))


def _cast_on_first_row_step(pairs):
    @pl.when(pl.program_id(1) == 0)
    def _():
        for src_ref, dst_ref in pairs:
            dst_ref[...] = src_ref[...].astype(dst_ref.dtype)


def _ffn_up_kernel(h_ref, w1_ref, w3_ref, o_ref, w1b_ref, w3b_ref):
    _cast_on_first_row_step(((w1_ref, w1b_ref), (w3_ref, w3b_ref)))
    h = h_ref[...]
    a = _dot(h, w1b_ref[...])
    b = _dot(h, w3b_ref[...])
    o_ref[...] = ((a * jax.nn.sigmoid(a)) * b).astype(o_ref.dtype)


def ffn_up(h, w1, w3, layer):
    n, d = h.shape
    f = w1.shape[2]
    tm = min(1024, n)
    tn = 512 if f % 512 == 0 else f
    w_spec = pl.BlockSpec((None, d, tn), lambda j, i: (layer, 0, j))
    return pl.pallas_call(
        _ffn_up_kernel,
        grid=(f // tn, n // tm),
        in_specs=[pl.BlockSpec((tm, d), lambda j, i: (i, 0)), w_spec, w_spec],
        out_specs=pl.BlockSpec((tm, tn), lambda j, i: (i, j)),
        out_shape=jax.ShapeDtypeStruct((n, f), BF16),
        scratch_shapes=[pltpu.VMEM((d, tn), BF16), pltpu.VMEM((d, tn), BF16)],
        compiler_params=_params("arbitrary", "arbitrary"),
        name="ffn_up",
    )(h, w1, w3)


def _resid_mm_kernel(a_ref, w_ref, x_ref, g_ref, o_ref, wb_ref, *, gscale):
    _cast_on_first_row_step(((w_ref, wb_ref),))
    y = _dot(a_ref[...], wb_ref[...])
    o_ref[...] = x_ref[...] + (gscale * g_ref[...]) * y


def resid_mm(a, w, layer, x, modl, i_gate, gscale, seq):
    n, k = a.shape
    d = w.shape[2]
    tm = min(512, seq)
    tn = min(512, d)
    return pl.pallas_call(
        functools.partial(_resid_mm_kernel, gscale=gscale),
        grid=(d // tn, n // tm),
        in_specs=[
            pl.BlockSpec((tm, k), lambda j, i: (i, 0)),
            pl.BlockSpec((None, k, tn), lambda j, i: (layer, 0, j)),
            pl.BlockSpec((tm, tn), lambda j, i: (i, j)),
            pl.BlockSpec((None, 1, tn), lambda j, i: ((i * tm) // seq * N_MOD + i_gate, 0, j)),
        ],
        out_specs=pl.BlockSpec((tm, tn), lambda j, i: (i, j)),
        out_shape=jax.ShapeDtypeStruct((n, d), F32),
        scratch_shapes=[pltpu.VMEM((k, tn), BF16)],
        compiler_params=_params("arbitrary", "arbitrary"),
        name="resid_mm",
    )(a, w, x, modl)


def _widest_tile(n_chunks, chunk, bytes_per_unit, limit_bytes):
    return chunk * max(t for t in range(1, n_chunks + 1)
                       if n_chunks % t == 0 and (t == 1 or t * chunk * bytes_per_unit <= limit_bytes))


def _swap_halves(z, group):
    if group == LANES:
        return pltpu.roll(z, LANES // 2, axis=1)
    half = group // 2
    lane = lax.broadcasted_iota(jnp.int32, z.shape, 1)
    from_above = pltpu.roll(z, LANES - half, axis=1)
    from_below = pltpu.roll(z, half, axis=1)
    return jnp.where((lane & (group - 1)) < half, from_above, from_below)


def _proj_kernel(h_ref, w_ref, cos_ref, sin_ref, o_ref, wb_ref, *, group):
    _cast_on_first_row_step(((w_ref, wb_ref),))
    z = _dot(h_ref[...], wb_ref[...])
    if group == 0:
        o_ref[...] = z.astype(o_ref.dtype)
        return
    cos = cos_ref[...]
    sin = sin_ref[...]
    for c in range(z.shape[1] // LANES):
        zc = z[:, c * LANES:(c + 1) * LANES]
        o_ref[:, c * LANES:(c + 1) * LANES] = (zc * cos + _swap_halves(zc, group) * sin).astype(o_ref.dtype)


def proj(h, w, cos, sin, group, seq):
    n, d = h.shape
    cols = w.shape[1]
    tm = min(512, seq)
    tn = _widest_tile(cols // LANES, LANES, d * 4, 12 * 2 ** 20)
    s_blocks = seq // tm
    return pl.pallas_call(
        functools.partial(_proj_kernel, group=group),
        grid=(cols // tn, n // tm),
        in_specs=[
            pl.BlockSpec((tm, d), lambda j, i: (i, 0)),
            pl.BlockSpec((d, tn), lambda j, i: (0, j)),
            pl.BlockSpec((tm, LANES), lambda j, i: (i % s_blocks, 0)),
            pl.BlockSpec((tm, LANES), lambda j, i: (i % s_blocks, 0)),
        ],
        out_specs=pl.BlockSpec((tm, tn), lambda j, i: (i, j)),
        out_shape=jax.ShapeDtypeStruct((n, cols), BF16),
        scratch_shapes=[pltpu.VMEM((d, tn), BF16)],
        compiler_params=_params("arbitrary", "arbitrary"),
        name=f"proj_rope{group}",
    )(h, w, cos, sin)


def _proj_fm_kernel(w_ref, h_ref, cos_ref, sin_ref, o_ref, wt_ref, *, group, row_scales):
    @pl.when(pl.program_id(1) == 0)
    def _():
        wt_ref[...] = w_ref[...].T.astype(wt_ref.dtype)

    zt = _nt_dot(wt_ref[...], h_ref[...])
    step = group if group else LANES
    half = group // 2
    scale_of_row = [s for n_rows, s in row_scales for _ in range(n_rows // step)]
    if len(row_scales) == 1:
        scale_of_row = scale_of_row[:1] * (zt.shape[0] // step)
    for c in range(zt.shape[0] // step):
        blk = zt[c * step:(c + 1) * step]
        if group:
            swapped = jnp.concatenate([blk[half:], blk[:half]], axis=0)
            blk = blk * cos_ref[...] + swapped * sin_ref[...]
        if scale_of_row[c] != 1.0:
            blk = blk * scale_of_row[c]
        o_ref[c * step:(c + 1) * step, :] = blk.astype(o_ref.dtype)


def proj_fm(h, w, cos_fm, sin_fm, group, seq, row_scales=None):
    n, d = h.shape
    rows = w.shape[1]
    tm = min(TOKEN_TILE, seq)
    unit = max(group, LANES)
    row_scales = row_scales or ((rows, 1.0),)
    limit = 12 * 2 ** 20 if len(row_scales) == 1 else rows * d * 4
    tn = _widest_tile(rows // unit, unit, d * 4, limit)
    s_blocks = seq // tm
    g = max(group, 8)
    return pl.pallas_call(
        functools.partial(_proj_fm_kernel, group=group, row_scales=row_scales),
        grid=(rows // tn, n // tm),
        in_specs=[
            pl.BlockSpec((d, tn), lambda j, i: (0, j)),
            pl.BlockSpec((tm, d), lambda j, i: (i, 0)),
            pl.BlockSpec((g, tm), lambda j, i: (0, i % s_blocks)),
            pl.BlockSpec((g, tm), lambda j, i: (0, i % s_blocks)),
        ],
        out_specs=pl.BlockSpec((None, tn, tm), lambda j, i: (i, j, 0)),
        out_shape=jax.ShapeDtypeStruct((n // tm, rows, tm), BF16),
        scratch_shapes=[pltpu.VMEM((tn, d), BF16)],
        compiler_params=_params("arbitrary", "arbitrary"),
        name=f"proj_fm_rope{group}",
    )(w, h, cos_fm, sin_fm)


def _small_proj_kernel(h_ref, wt_ref, bf_ref, logf_ref, wi_ref):
    zt = _nt_dot(wt_ref[...], h_ref[...])
    logf_ref[...] = jax.nn.log_sigmoid(zt[0:8, :] + bf_ref[...])
    wi_ref[...] = zt[8:16, :] * IDX_W_SCALE


def small_proj(h, w_small_t, b_forget8, seq):
    n, d = h.shape
    tm = min(512, seq)
    return pl.pallas_call(
        _small_proj_kernel,
        grid=(n // tm,),
        in_specs=[
            pl.BlockSpec((tm, d), lambda i: (i, 0)),
            pl.BlockSpec((16, d), lambda i: (0, 0)),
            pl.BlockSpec((8, 1), lambda i: (0, 0)),
        ],
        out_specs=[pl.BlockSpec((8, tm), lambda i: (0, i)), pl.BlockSpec((8, tm), lambda i: (0, i))],
        out_shape=[jax.ShapeDtypeStruct((8, n), F32), jax.ShapeDtypeStruct((8, n), F32)],
        compiler_params=_params("arbitrary"),
        name="small_proj",
    )(h, w_small_t, b_forget8)


def _split3(x):
    x1 = x.astype(BF16)
    r1 = x - x1.astype(F32)
    x2 = r1.astype(BF16)
    x3 = (r1 - x2.astype(F32)).astype(BF16)
    return x1, x2, x3


def _neg_cumsum_kernel(x_ref, hi_ref, mid_ref, lo_ref):
    x = x_ref[...]
    chunks = x.shape[0]
    r = lax.broadcasted_iota(jnp.int32, (LANES, LANES), 0)
    c = lax.broadcasted_iota(jnp.int32, (LANES, LANES), 1)
    upper = (r <= c).astype(BF16)
    within = sum(_dot(p, upper) for p in _split3(x))
    totals = jnp.broadcast_to(within[:, LANES - 1:LANES], (chunks, LANES))
    rr = lax.broadcasted_iota(jnp.int32, (chunks, chunks), 0)
    cc = lax.broadcasted_iota(jnp.int32, (chunks, chunks), 1)
    strict_lower = (cc < rr).astype(BF16)
    offset = sum(_dot(strict_lower, p) for p in _split3(totals))
    hi_ref[...], mid_ref[...], lo_ref[...] = _split3(-(within + offset) * LOG2E)


def neg_cumsum(logf_t, batch, seq):
    rows = logf_t.shape[0] * batch
    chunks = seq // LANES
    x = logf_t.reshape(rows, chunks, LANES)
    spec = pl.BlockSpec((None, chunks, LANES), lambda i: (i, 0, 0))
    pieces = pl.pallas_call(
        _neg_cumsum_kernel,
        grid=(rows,),
        in_specs=[spec],
        out_specs=[spec] * 3,
        out_shape=[jax.ShapeDtypeStruct((rows, chunks, LANES), BF16)] * 3,
        compiler_params=_params("arbitrary"),
        name="neg_cumsum",
    )(x)
    rows3 = jnp.stack([p.reshape(logf_t.shape[0], batch, seq) for p in pieces], axis=2)
    return jnp.pad(rows3, ((0, 0), (0, 0), (0, 8 - N_BIAS_PIECES), (0, 0)))


def _query_streams(tq):
    width = min(MXU_WIDTH, tq)
    return [slice(lo, lo + width) for lo in range(0, tq, width)]


ONES_ROWS = 16
ACC_ROWS = HEAD_DIM + ONES_ROWS


def _softmax_stats(s, m_prev):
    m_new = jnp.maximum(m_prev, jnp.max(s, axis=0, keepdims=True))
    return m_new, jnp.exp2(m_prev - m_new), jnp.exp2(s - m_new).astype(BF16)


def _with_ones_rows(vt):
    return jnp.concatenate([vt, jnp.ones((ONES_ROWS, vt.shape[1]), vt.dtype)], axis=0)


def _normalized(acc):
    return acc[0:HEAD_DIM] / acc[HEAD_DIM:HEAD_DIM + 1]


def _causal_mask(key0, query0, n_keys, n_queries):
    key = key0 + lax.broadcasted_iota(jnp.int32, (n_keys, n_queries), 0)
    query = query0 + lax.broadcasted_iota(jnp.int32, (n_keys, n_queries), 1)
    return key <= query


def _sweep_key_tiles(q_block, tq, tk, tile_fn):
    n_full = (q_block * tq) // tk

    def body(kj, carry):
        tile_fn(kj, False)
        return carry

    lax.fori_loop(0, n_full, body, 0)
    tile_fn(n_full, True)


def _attention_sweep(q_block, tq, tk, streams, raw_scores, logits, values, pipe):
    assert tk % tq == 0
    n_full = (q_block * tq) // tk
    ids = range(len(streams))
    slot_a, slot_b = pipe[:3], pipe[3:]

    for (m_ref, acc_ref), qs in streams:
        m_ref[:, qs] = jnp.full((1, qs.stop - qs.start), -jnp.inf, F32)
        acc_ref[:, qs] = jnp.zeros((ACC_ROWS, qs.stop - qs.start), F32)
    def products(kj, slot):
        vts = values(kj)
        return [_dot(vts[i], slot[1][i]) for i in ids]

    def accumulate(pv, rescale):
        for i in ids:
            (_, acc_ref), qs = streams[i]
            acc_ref[:, qs] = rescale[i] * acc_ref[:, qs] + pv[i]

    def softmax_into(kj, raw, slot, i, masked):
        (m_ref, _), qs = streams[i]
        m_new, alpha, p = _softmax_stats(logits(kj, i, raw, masked), m_ref[:, qs])
        m_ref[:, qs] = m_new
        slot[1][i] = p
        slot[2][i] = alpha

    def add_pv(kj, slot):
        accumulate(products(kj, slot), [slot[2][i] for i in ids])

    def pipe_step(kj, cur, nxt, first=False, last=False):
        raw_next = None if last else raw_scores(kj + 1)
        if not first:
            pv_prev = products(kj - 1, nxt)
            rescale_prev = [nxt[2][i] for i in ids]
        if not last:
            for i in ids:
                nxt[0][i] = raw_next[i]
        for i in ids:
            softmax_into(kj, cur[0][i], cur, i, last)
        if not first:
            accumulate(pv_prev, rescale_prev)

    raw0 = raw_scores(0)
    for i in ids:
        slot_a[0][i] = raw0[i]

    @pl.when(n_full == 0)
    def _():
        pipe_step(0, slot_a, slot_b, first=True, last=True)
        add_pv(0, slot_a)

    @pl.when(n_full > 0)
    def _():
        pipe_step(0, slot_a, slot_b, first=True)

        def trip(t, carry):
            pipe_step(2 * t + 1, slot_b, slot_a)
            pipe_step(2 * t + 2, slot_a, slot_b)
            return carry

        lax.fori_loop(0, (n_full - 1) // 2, trip, 0)

        @pl.when(n_full % 2 == 1)
        def _():
            pipe_step(n_full, slot_b, slot_a, last=True)
            add_pv(n_full, slot_b)

        @pl.when(n_full % 2 == 0)
        def _():
            pipe_step(n_full - 1, slot_b, slot_a)
            pipe_step(n_full, slot_a, slot_b, last=True)
            add_pv(n_full, slot_a)


def _head_sweep(nq, tq, tk, streams, raw_scores, logits, values, pipe, steps_per_trip):
    assert tq == tk
    ids = range(len(streams))
    slot_a, slot_b = pipe[:3], pipe[3:]

    for (m_ref, acc_ref), qs in streams:
        m_ref[:, :, qs] = jnp.full((nq, 1, qs.stop - qs.start), -jnp.inf, F32)
        acc_ref[:, :, qs] = jnp.zeros((nq, ACC_ROWS, qs.stop - qs.start), F32)

    def accumulate(qi, pv, rescale):
        for i in ids:
            (_, acc_ref), qs = streams[i]
            acc_ref[qi, :, qs] = rescale[i] * acc_ref[qi, :, qs] + pv[i]

    def products(kj, slot):
        vts = values(kj)
        return [_dot(vts[i], slot[1][i]) for i in ids]

    def step(cur, prev, nxt, cur_slot, nxt_slot, masked):
        raw_next = raw_scores(*nxt)
        if prev is not None:
            pv_prev = products(prev[0], nxt_slot)
            rescale_prev = [nxt_slot[2][i] for i in ids]
        for i in ids:
            nxt_slot[0][i] = raw_next[i]
        kj, qi = cur
        for i in ids:
            (m_ref, _), qs = streams[i]
            m_new, alpha, p = _softmax_stats(logits(kj, qi, i, cur_slot[0][i], masked), m_ref[qi, :, qs])
            m_ref[qi, :, qs] = m_new
            cur_slot[1][i] = p
            cur_slot[2][i] = alpha
        if prev is not None:
            accumulate(prev[1], pv_prev, rescale_prev)

    def run(n_pairs, start, advance, masked):
        if n_pairs == 0:
            return
        slots = (slot_a, slot_b)
        raw0 = raw_scores(*start)
        for i in ids:
            slot_a[0][i] = raw0[i]
        second = advance(*start)
        step(start, None, second, slot_a, slot_b, masked)

        def steps(count, cur, prev):
            for u in range(count):
                nxt = advance(*cur)
                step(cur, prev, nxt, slots[(u + 1) % 2], slots[u % 2], masked)
                prev, cur = cur, nxt
            return cur, prev

        def trip(t, carry):
            cur, prev = steps(steps_per_trip, carry[:2], carry[2:])
            return (*cur, *prev)

        carry = lax.fori_loop(0, (n_pairs - 1) // steps_per_trip, trip, (*second, *start))
        _, prev = steps((n_pairs - 1) % steps_per_trip, carry[:2], carry[2:])
        last_slot = slots[(n_pairs - 1) % 2]
        accumulate(prev[1], products(prev[0], last_slot), [last_slot[2][i] for i in ids])

    last = nq - 1
    zero = jnp.int32(0)
    run(nq, (zero, zero), lambda kj, qi: (jnp.minimum(kj + 1, last), jnp.minimum(qi + 1, last)), True)

    def next_below_diagonal(kj, qi):
        wrap = qi == last
        dist = qi - kj
        return (jnp.where(wrap, 0, kj + 1), jnp.where(wrap, jnp.minimum(dist + 1, last), qi + 1))

    run(nq * (nq - 1) // 2, (zero, jnp.int32(min(1, last))), next_below_diagonal, False)


def _softmax_scratch(tq):
    return [pltpu.VMEM((1, tq), F32), pltpu.VMEM((ACC_ROWS, tq), F32)]


def _head_softmax_scratch(nq, tq):
    return [pltpu.VMEM((nq, 1, tq), F32), pltpu.VMEM((nq, ACC_ROWS, tq), F32)]


def _pipe_scratch(n_streams, tk, width):
    slot = lambda: [pltpu.VMEM((n_streams, tk, width), F32), pltpu.VMEM((n_streams, tk, width), BF16),
                    pltpu.VMEM((n_streams, 1, width), F32)]
    return slot() + slot()


N_BIAS_PIECES = 3


def _fox_kernel(qt_ref, k_ref, bias_ref, vt_ref, o_ref, m_ref, acc_ref, qa_ref, kb_ref, *pipe, nq, tq, tk):
    slices = _query_streams(tq)
    row = lax.broadcasted_iota(jnp.int32, (HEAD_DIM, tq), 0)
    ones_rows = jnp.where(row < N_BIAS_PIECES, 1.0, 0.0).astype(BF16)
    zeros = jnp.zeros((HEAD_DIM - 8, tk), F32)
    for j in range(nq):
        qa_ref[j] = jnp.concatenate([qt_ref[j], ones_rows], axis=0)
        rows = bias_ref[:, j * tk:(j + 1) * tk].astype(F32)
        kb_ref[j] = jnp.concatenate([rows, zeros], axis=0).T.astype(BF16)

    def raw_scores(kj, qi):
        k = jnp.concatenate([k_ref[pl.ds(pl.multiple_of(kj * tk, tk), tk), :], kb_ref[kj]], axis=1)
        return [_dot(k, qa_ref[qi, :, qs]) for qs in slices]

    def logits(kj, qi, i, raw, masked):
        qs = slices[i]
        if masked:
            raw = jnp.where(_causal_mask(kj * tk, qi * tq + qs.start, tk, qs.stop - qs.start), raw, NEG_INF)
        return raw

    def values(kj):
        return [_with_ones_rows(vt_ref[kj])] * len(slices)

    _head_sweep(nq, tq, tk, [((m_ref, acc_ref), qs) for qs in slices], raw_scores, logits, values, pipe, 4)
    for j in range(nq):
        o_ref[j * tq:(j + 1) * tq, :] = _normalized(acc_ref[j]).T.astype(o_ref.dtype)


def fox_attention(q_fm, q_blk0, k_rows, bias_rows, v_fm, v_blk0, seq):
    b = k_rows.shape[0]
    tq = tk = q_fm.shape[2]
    nq = seq // tq
    slices = _query_streams(tq)
    return pl.pallas_call(
        functools.partial(_fox_kernel, nq=nq, tq=tq, tk=tk),
        grid=(b, H_FOX),
        in_specs=[
            pl.BlockSpec((nq, HEAD_DIM, tq), lambda bi, h: (bi, q_blk0 + h, 0)),
            pl.BlockSpec((None, seq, HEAD_DIM), lambda bi, h: (bi, 0, h)),
            pl.BlockSpec((None, None, 8, seq), lambda bi, h: (h, bi, 0, 0)),
            pl.BlockSpec((nq, HEAD_DIM, tk), lambda bi, h: (bi, v_blk0 + h, 0)),
        ],
        out_specs=pl.BlockSpec((None, seq, HEAD_DIM), lambda bi, h: (bi, 0, h)),
        out_shape=jax.ShapeDtypeStruct((b, seq, W_FOX), BF16),
        scratch_shapes=_head_softmax_scratch(nq, tq)
        + [pltpu.VMEM((nq, 2 * HEAD_DIM, tq), BF16), pltpu.VMEM((nq, tk, HEAD_DIM), BF16)]
        + _pipe_scratch(len(slices), tk, slices[0].stop),
        compiler_params=_params("arbitrary", "arbitrary"),
        name="fox_attention",
    )(q_fm, k_rows, bias_rows, v_fm)


def _diff_kernel(lam_ref, gain_ref, qt_ref, k_ref, vt_ref, o_ref,
                 m1_ref, acc1_ref, m2_ref, acc2_ref, q1_ref, q2_ref, *pipe, nq, tq, tk, lam_init):
    zeros = jnp.zeros((DK_DIFF, tq), BF16)
    for j in range(nq):
        q1_ref[j] = jnp.concatenate([qt_ref[j, 0:DK_DIFF, :], zeros], axis=0)
        q2_ref[j] = jnp.concatenate([zeros, qt_ref[j, DK_DIFF:HEAD_DIM, :]], axis=0)
    streams, q_refs = [], []
    for qs in _query_streams(tq):
        for refs, q_ref in (((m1_ref, acc1_ref), q1_ref), ((m2_ref, acc2_ref), q2_ref)):
            streams.append((refs, qs))
            q_refs.append(q_ref)

    def raw_scores(kj, qi):
        k = k_ref[pl.ds(pl.multiple_of(kj * tk, tk), tk), :]
        return [_dot(k, q_ref[qi, :, qs]) for q_ref, (_, qs) in zip(q_refs, streams)]

    def logits(kj, qi, i, raw, masked):
        qs = streams[i][1]
        if masked:
            raw = jnp.where(_causal_mask(kj * tk, qi * tq + qs.start, tk, qs.stop - qs.start), raw, NEG_INF)
        return raw

    def values(kj):
        return [_with_ones_rows(vt_ref[kj])] * len(streams)

    _head_sweep(nq, tq, tk, streams, raw_scores, logits, values, pipe, 2)

    lam_vecs = lam_ref[...]
    dot1 = jnp.sum(lam_vecs[0:1] * lam_vecs[1:2], axis=1, keepdims=True)
    dot2 = jnp.sum(lam_vecs[2:3] * lam_vecs[3:4], axis=1, keepdims=True)
    lam = jnp.exp(dot1) - jnp.exp(dot2) + lam_init
    for j in range(nq):
        o = (_normalized(acc1_ref[j]) - lam * _normalized(acc2_ref[j])).T
        y = o * lax.rsqrt(jnp.mean(o * o, axis=-1, keepdims=True) + NORM_EPS)
        o_ref[j * tq:(j + 1) * tq, :] = ((y * gain_ref[...]) * (1.0 - lam_init)).astype(o_ref.dtype)


def diff_attention(q_fm, q_blk0, k_rows, v_fm, v_blk0, lam_vecs, subln_gain, lam_init, seq):
    b = k_rows.shape[0]
    tq = tk = q_fm.shape[2]
    nq = seq // tq
    slices = _query_streams(tq)
    return pl.pallas_call(
        functools.partial(_diff_kernel, nq=nq, tq=tq, tk=tk, lam_init=lam_init),
        grid=(b, H_DIFF),
        in_specs=[
            pl.BlockSpec((4, DK_DIFF), lambda bi, h: (0, 0)),
            pl.BlockSpec((1, HEAD_DIM), lambda bi, h: (0, 0)),
            pl.BlockSpec((nq, HEAD_DIM, tq), lambda bi, h: (bi, q_blk0 + h, 0)),
            pl.BlockSpec((None, seq, HEAD_DIM), lambda bi, h: (bi, 0, h)),
            pl.BlockSpec((nq, HEAD_DIM, tk), lambda bi, h: (bi, v_blk0 + h, 0)),
        ],
        out_specs=pl.BlockSpec((None, seq, HEAD_DIM), lambda bi, h: (bi, 0, h)),
        out_shape=jax.ShapeDtypeStruct((b, seq, W_DIFF), BF16),
        scratch_shapes=_head_softmax_scratch(nq, tq) + _head_softmax_scratch(nq, tq)
        + [pltpu.VMEM((nq, HEAD_DIM, tq), BF16), pltpu.VMEM((nq, HEAD_DIM, tq), BF16)]
        + _pipe_scratch(2 * len(slices), tk, slices[0].stop),
        compiler_params=_params("arbitrary", "arbitrary"),
        name="diff_attention",
    )(lam_vecs, subln_gain.reshape(1, HEAD_DIM), q_fm, k_rows, v_fm)


def _dsa_kernel(qt_ref, k_ref, vt_ref, qit_ref, ki_ref, wt_ref, o_ref,
                hi_ref, lo_ref, qm_ref, cut_ref, m_ref, acc_ref, m2_ref, acc2_ref, *pipe,
                tq, tk, seq, n_sel):
    qb = pl.program_id(1)
    n_tiles = (qb * tq) // tk + 1
    slices = _query_streams(tq)

    zeros = jnp.zeros((D_IDX, tq), BF16)
    for h in range(H_IDX):
        qm_ref[h] = jnp.concatenate([qit_ref[h * D_IDX:(h + 1) * D_IDX, :], zeros], axis=0)

    def score_tile(kj, masked):
        kk = ki_ref[pl.ds(pl.multiple_of(kj * tk, tk), tk), :]
        for qs in slices:
            width = qs.stop - qs.start
            rel_q = [_dot(kk, qm_ref[h, :, qs]) for h in range(H_IDX)]
            score = jnp.zeros((tk, width), F32)
            for h in range(H_IDX):
                score = score + wt_ref[h:h + 1, qs] * jnp.maximum(rel_q[h], 0.0)
            if masked:
                score = jnp.where(_causal_mask(kj * tk, qb * tq + qs.start, tk, width), score, NEG_INF)
            bits = lax.bitcast_convert_type(score, jnp.int32)
            key = bits ^ ((bits >> 31) & 0x7FFFFFFF)
            hi_ref[kj, :, qs] = (key >> 16).astype(jnp.int16)
            lo_ref[kj, :, qs] = ((key & 0xFFFF) + I16_MIN).astype(jnp.int16)

    _sweep_key_tiles(qb, tq, tk, score_tile)

    n_beyond = (seq - n_tiles * tk).astype(F32)
    neg_hi, neg_lo = KEY_NEG_INF >> 16, (KEY_NEG_INF & 0xFFFF) + I16_MIN
    one, zero = jnp.int16(1), jnp.int16(0)
    as16 = lambda v: v.astype(jnp.int16)

    def count_hits(hit_fn):
        def body(kj, total):
            hit = jnp.where(hit_fn(kj), one, zero)
            part = hit[0:16]
            for r in range(1, tk // 16):
                part = part + hit[r * 16:(r + 1) * 16]
            return total + jnp.sum(part.astype(F32), axis=0, keepdims=True)
        return lax.fori_loop(0, n_tiles, body, jnp.zeros((1, tq), F32))

    def bisect16(enough):
        def step(s, t):
            cand = t + jnp.left_shift(jnp.int32(1), 15 - s)
            return jnp.where(enough(cand), cand, t)
        return lax.fori_loop(0, 16, step, jnp.full((1, tq), I16_MIN, jnp.int32))

    def count_hi_ge(cand):
        n = count_hits(lambda kj, c=as16(cand): hi_ref[kj] >= c)
        return n + jnp.where(cand <= neg_hi, n_beyond, 0.0)

    t_hi = bisect16(lambda cand: count_hi_ge(cand) >= n_sel)
    t_hi16 = as16(t_hi)
    n_above = jnp.where(t_hi < I16_MAX, count_hi_ge(jnp.minimum(t_hi + 1, I16_MAX)), 0.0)
    need_lo = n_sel - n_above

    def mask_lo(kj, carry):
        lo_ref[kj] = jnp.where(hi_ref[kj] == t_hi16, lo_ref[kj], jnp.int16(I16_MIN))
        return carry

    lax.fori_loop(0, n_tiles, mask_lo, 0)
    beyond_in_group = jnp.where(t_hi == neg_hi, n_beyond, 0.0)

    def count_lo_ge(cand):
        n = count_hits(lambda kj, c=as16(cand): lo_ref[kj] >= c)
        return n + jnp.where(cand <= neg_lo, beyond_in_group, 0.0)

    t_lo = bisect16(lambda cand: count_lo_ge(cand) >= need_lo)
    t_lo16 = as16(t_lo)

    def in_group(kj):
        return hi_ref[kj] == t_hi16

    n_ge = n_above + count_hits(lambda kj: in_group(kj) & (lo_ref[kj] >= t_lo16)) \
        + jnp.where(t_lo <= neg_lo, beyond_in_group, 0.0)
    surplus = n_ge - n_sel
    cut_ref[...] = jnp.full((1, tq), I16_MAX, jnp.int32)

    def key_index(kj):
        return as16(kj * tk + lax.broadcasted_iota(jnp.int32, (tk, tq), 0))

    @pl.when(jnp.max(surplus) > 0.0)
    def _():
        n_gt = n_above + count_hits(lambda kj: in_group(kj) & (lo_ref[kj] > t_lo16)) \
            + jnp.where(t_lo < neg_lo, beyond_in_group, 0.0)
        need = jnp.where(surplus > 0.0, n_sel - n_gt, float(seq + 1))

        def count_tied_below(cut):
            c = as16(cut)
            return count_hits(lambda kj: in_group(kj) & (lo_ref[kj] == t_lo16) & (key_index(kj) < c))

        n_bits = max(1, (seq - 1).bit_length())

        def bisect_cut(step, cut):
            cand = cut + jnp.left_shift(jnp.int32(1), n_bits - 1 - step)
            return jnp.where(count_tied_below(cand) < need, cand, cut)

        cut_ref[...] = lax.fori_loop(0, n_bits, bisect_cut, jnp.zeros((1, tq), jnp.int32))

    cut16 = as16(cut_ref[...])
    zero_bias, neg_bias = jnp.zeros((), BF16), jnp.asarray(NEG_INF, BF16)

    def write_bias(kj, carry):
        hi, lo = hi_ref[kj], lo_ref[kj]
        tied_ok = (lo == t_lo16) & (key_index(kj) <= cut16)
        sel = (hi > t_hi16) | ((hi == t_hi16) & ((lo > t_lo16) | tied_ok))
        hi_ref[kj] = lax.bitcast_convert_type(jnp.where(sel, zero_bias, neg_bias), jnp.int16)
        return carry

    lax.fori_loop(0, n_tiles, write_bias, 0)

    states = ((m_ref, acc_ref), (m2_ref, acc2_ref))
    for h0 in range(0, H_DSA, 2):
        streams, rows = [], []
        for st, h in zip(states, (h0, h0 + 1)):
            for qs in slices:
                streams.append((st, qs))
                rows.append(slice(h * HEAD_DIM, (h + 1) * HEAD_DIM))

        def raw_scores(kj, streams=streams, rows=rows):
            off = pl.multiple_of(kj * tk, tk)
            return [_dot(k_ref[pl.ds(off, tk), r], qt_ref[r, qs]) for (_, qs), r in zip(streams, rows)]

        def logits(kj, i, raw, masked, streams=streams):
            qs = streams[i][1]
            s = raw + lax.bitcast_convert_type(hi_ref[kj, :, qs], BF16).astype(F32)
            if masked:
                s = jnp.where(_causal_mask(kj * tk, qb * tq + qs.start, tk, qs.stop - qs.start), s, NEG_INF)
            return s

        def values(kj, h0=h0):
            per_head = [_with_ones_rows(vt_ref[kj, h * HEAD_DIM:(h + 1) * HEAD_DIM, :]) for h in (h0, h0 + 1)]
            return [v for v in per_head for _ in slices]

        _attention_sweep(qb, tq, tk, streams, raw_scores, logits, values, pipe)
        for (_, acc_r), h in zip(states, (h0, h0 + 1)):
            o_ref[:, h * HEAD_DIM:(h + 1) * HEAD_DIM] = _normalized(acc_r[...]).T.astype(o_ref.dtype)


def dsa_attention(q_fm, k_rows, v_fm, qi_fm, ki_rows, ki_blk, wi_t, seq):
    b = k_rows.shape[0]
    tq = tk = q_fm.shape[2]
    nq = seq // tq
    n_sel = min(TOPK_MAX, seq // 4)
    once = pl.Buffered(1)
    slices = _query_streams(tq)
    fm_q = lambda rows: pl.BlockSpec((None, rows, tq), lambda bi, i: (bi * nq + i, 0, 0))
    return pl.pallas_call(
        functools.partial(_dsa_kernel, tq=tq, tk=tk, seq=seq, n_sel=n_sel),
        grid=(b, nq),
        in_specs=[
            fm_q(W_DSA),
            pl.BlockSpec((None, seq, W_DSA), lambda bi, i: (bi, 0, 0), pipeline_mode=once),
            pl.BlockSpec((nq, W_DSA, tk), lambda bi, i: (bi, 0, 0), pipeline_mode=once),
            fm_q(W_IDX),
            pl.BlockSpec((None, seq, LANES), lambda bi, i: (bi, 0, ki_blk), pipeline_mode=once),
            pl.BlockSpec((H_IDX, tq), lambda bi, i: (0, bi * nq + i)),
        ],
        out_specs=pl.BlockSpec((None, tq, W_DSA), lambda bi, i: (bi, i, 0)),
        out_shape=jax.ShapeDtypeStruct((b, seq, W_DSA), BF16),
        scratch_shapes=[
            pltpu.VMEM((nq, tk, tq), jnp.int16),
            pltpu.VMEM((nq, tk, tq), jnp.int16),
            pltpu.VMEM((H_IDX, 2 * D_IDX, tq), BF16),
            pltpu.VMEM((1, tq), jnp.int32),
        ] + _softmax_scratch(tq) + _softmax_scratch(tq) + _pipe_scratch(2 * len(slices), tk, slices[0].stop),
        compiler_params=_params("arbitrary", "arbitrary"),
        name="dsa_attention",
    )(q_fm, k_rows, v_fm, qi_fm, ki_rows, wi_t)


def _merge_kernel(h_ref, oa_ref, ob_ref, oc_ref, mw0_ref, mw1_ref, mw2_ref, mb_ref, wa_ref, wb_ref, wc_ref, o_ref,
                  *bf16_refs):
    f32_refs = (mw0_ref, mw1_ref, mw2_ref, wa_ref, wb_ref, wc_ref)
    _cast_on_first_row_step(tuple(zip(f32_refs, bf16_refs)))
    gate_w, branch_w = bf16_refs[:N_BRANCH], bf16_refs[N_BRANCH:]
    h = h_ref[...]
    merged = None
    for i, o_b_ref in enumerate((oa_ref, ob_ref, oc_ref)):
        gate = jax.nn.sigmoid(_dot(h, gate_w[i][...]) + mb_ref[i])
        term = gate * _dot(o_b_ref[...], branch_w[i][...])
        merged = term if merged is None else merged + term
    o_ref[...] = merged.astype(o_ref.dtype)


def merge_branches(h, oa, ob, oc, merge_w, merge_b, wa, wb, wc, layer, seq):
    n, d = h.shape
    tm = min(1024, seq)
    tn = min(256, d)
    nj = d // tn
    row = lambda width: pl.BlockSpec((tm, width), lambda j, i: (i, 0))
    col = lambda rows: pl.BlockSpec((None, rows, tn), lambda j, i: (layer, 0, j))
    gate_w = lambda g: pl.BlockSpec((None, d, tn), lambda j, i: (layer, 0, g * nj + j))
    widths = (oa.shape[1], ob.shape[1], oc.shape[1])
    return pl.pallas_call(
        _merge_kernel,
        grid=(nj, n // tm),
        in_specs=[
            row(d), row(widths[0]), row(widths[1]), row(widths[2]),
            gate_w(0), gate_w(1), gate_w(2),
            pl.BlockSpec((None, N_BRANCH, 1, tn), lambda j, i: (layer, 0, 0, j)),
            col(widths[0]), col(widths[1]), col(widths[2]),
        ],
        out_specs=pl.BlockSpec((tm, tn), lambda j, i: (i, j)),
        out_shape=jax.ShapeDtypeStruct((n, d), BF16),
        scratch_shapes=[pltpu.VMEM((d, tn), BF16)] * N_BRANCH + [pltpu.VMEM((w, tn), BF16) for w in widths],
        compiler_params=_params("arbitrary", "arbitrary"),
        name="merge_branches",
    )(h, oa, ob, oc, merge_w, merge_w, merge_w, merge_b.reshape(merge_b.shape[0], N_BRANCH, 1, d), wa, wb, wc)


def _rope_tables(seq, dim):
    inv_freq = 1.0 / (ROPE_THETA ** (jnp.arange(0, dim, 2, dtype=F32) / dim))
    ang = jnp.arange(seq, dtype=F32)[:, None] * inv_freq[None, :]
    cos, sin = lax.optimization_barrier((jnp.cos(ang), jnp.sin(ang)))
    cos_g = jnp.concatenate([cos, cos], axis=1)
    sin_g = jnp.concatenate([-sin, sin], axis=1)
    reps = LANES // dim
    return jnp.tile(cos_g, (1, reps)), jnp.tile(sin_g, (1, reps)), cos_g.T, sin_g.T


def _split_w_in(w_in):
    sizes = (W_DIFF, W_DIFF, W_DIFF, W_FOX, W_FOX, W_FOX, H_FOX, W_DSA, W_DSA, W_DSA, W_IDX, D_IDX, H_IDX)
    parts, start = [], 0
    for n in sizes:
        parts.append(w_in[:, start:start + n])
        start += n
    return parts


def kernel(x, c, w_ada, b_ada, norm_ffn1, ffn1_w1, ffn1_w3, ffn1_w2, norm_mix, w_in, b_forget, lam_q1, lam_k1, lam_q2, lam_k2, subln_gain, merge_w, merge_b, w_branch_a, w_branch_b, w_branch_c, w_out, norm_ffn2, ffn2_w1, ffn2_w3, ffn2_w2, norm_final):
    batch, seq, d = x.shape
    depth = w_ada.shape[0]
    n = batch * seq
    tk = min(TOKEN_TILE, seq)
    bf = lambda a: a.astype(BF16)

    cos64, sin64, cos64_fm, sin64_fm = _rope_tables(seq, DK_DIFF)
    cos128, sin128, cos128_fm, sin128_fm = _rope_tables(seq, HEAD_DIM)
    mod = adaln_mod(c, w_ada, b_ada)
    xf = x.reshape(n, d)

    for l in range(depth):
        lam_init = 0.8 - 0.6 * math.exp(-0.3 * l)
        modl = mod[l].reshape(batch * N_MOD, 1, d)

        h = norm_mod(xf, norm_ffn1[l], modl, 0, seq)
        u = ffn_up(h, ffn1_w1, ffn1_w3, l)
        xf = resid_mm(u, ffn1_w2, l, xf, modl, 2, 0.5, seq)

        h = norm_mod(xf, norm_mix[l], modl, 3, seq)
        qa, ka, va, qb, kb, vb, fb, qc, kc, vc, qi, ki, wi = _split_w_in(w_in[l])
        k64 = proj(h, jnp.concatenate([ka, ki, ki], axis=1), cos64, sin64, DK_DIFF, seq).reshape(batch, seq, -1)
        k128 = proj(h, kc, cos128, sin128, HEAD_DIM, seq).reshape(batch, seq, -1)
        k0 = proj(h, kb, cos64, sin64, 0, seq).reshape(batch, seq, -1)
        log2e_over_sqrt = lambda width: (width ** -0.5) * LOG2E
        q64_fm = proj_fm(h, jnp.concatenate([qi, qa], axis=1), cos64_fm, sin64_fm, DK_DIFF, seq,
                         ((W_IDX, 1.0), (W_DIFF, log2e_over_sqrt(DK_DIFF))))
        q128_fm = proj_fm(h, qc, cos128_fm, sin128_fm, HEAD_DIM, seq, ((W_DSA, log2e_over_sqrt(HEAD_DIM)),))
        qb_fm = proj_fm(h, qb, cos64_fm, sin64_fm, 0, seq, ((W_FOX, log2e_over_sqrt(HEAD_DIM)),))
        v_fm = proj_fm(h, jnp.concatenate([vc, va, vb], axis=1), cos64_fm, sin64_fm, 0, seq)
        blk = lambda rows: rows // HEAD_DIM
        qa_blk, va_blk, vb_blk = blk(W_IDX), blk(W_DSA), blk(W_DSA + W_DIFF)

        w_small_t = bf(jnp.concatenate([fb, jnp.zeros((d, 8 - H_FOX), F32), wi], axis=1).T)
        b_forget8 = jnp.concatenate([b_forget[l], jnp.zeros((8 - H_FOX,), F32)]).reshape(8, 1)
        logf_t, wi_t = small_proj(h, w_small_t, b_forget8, seq)
        bias_rows = neg_cumsum(logf_t, batch, seq)

        lam_vecs = jnp.stack([lam_q1[l], lam_k1[l], lam_q2[l], lam_k2[l]])
        oa = diff_attention(q64_fm, qa_blk, k64, v_fm, va_blk, lam_vecs, subln_gain[l], lam_init, seq)
        ob = fox_attention(qb_fm, 0, k0, bias_rows, v_fm, vb_blk, seq)
        oc = dsa_attention(q128_fm, k128, v_fm, q64_fm, k64, blk(W_DIFF), wi_t, seq)

        merged = merge_branches(h, oa.reshape(n, -1), ob.reshape(n, -1), oc.reshape(n, -1), merge_w, merge_b,
                                w_branch_a, w_branch_b, w_branch_c, l, seq)
        xf = resid_mm(merged, w_out, l, xf, modl, 5, 1.0, seq)

        h = norm_mod(xf, norm_ffn2[l], modl, 6, seq)
        u = ffn_up(h, ffn2_w1, ffn2_w3, l)
        xf = resid_mm(u, ffn2_w2, l, xf, modl, 8, 0.5, seq)

    return final_norm(xf, norm_final).reshape(batch, seq, d)
```

```python
import functools
import math

import numpy as np
import jax
import jax.numpy as jnp
from jax import lax
from jax.experimental import pallas as pl
from jax.experimental.pallas import tpu as pltpu

HEAD_DIM = 128
H_DIFF = 6
DK_DIFF = HEAD_DIM // 2
H_FOX = 6
H_DSA = 4
H_IDX = 8
D_IDX = 64
TOPK_MAX = 256
ROPE_THETA = 10000.0
NORM_EPS = 1e-6
N_BRANCH = 3
N_MOD = 9
NEG_INF = -1e30
IDX_W_SCALE = (H_IDX ** -0.5) * (D_IDX ** -0.5)
LOG2E = math.log2(math.e)

W_DIFF = H_DIFF * HEAD_DIM
W_FOX = H_FOX * HEAD_DIM
W_DSA = H_DSA * HEAD_DIM
W_IDX = H_IDX * D_IDX

LANES = 128
MXU_WIDTH = 256
VMEM_LIMIT = 56 * 1024 * 1024
TOKEN_TILE = 512
I16_MIN, I16_MAX = -(2 ** 15), 2 ** 15 - 1

BF16 = jnp.bfloat16
F32 = jnp.float32


def _order_key_of(value):
    bits = int(np.array(value, np.float32).view(np.int32))
    return bits ^ ((bits >> 31) & 0x7FFFFFFF)


KEY_NEG_INF = _order_key_of(NEG_INF)


def _params(*semantics):
    return pltpu.CompilerParams(dimension_semantics=semantics, vmem_limit_bytes=VMEM_LIMIT)


def _nt_dot(a, b):
    return lax.dot_general(a, b, (((1,), (1,)), ((), ())), preferred_element_type=F32)


def _dot(a, b):
    return jnp.dot(a, b, preferred_element_type=F32)


def _adaln_kernel(c_ref, w_ref, b_ref, o_ref):
    c = c_ref[...]
    c_act = (c * jax.nn.sigmoid(c)).astype(BF16)
    o_ref[...] = _dot(c_act, w_ref[...].astype(BF16)) + b_ref[...]


def adaln_mod(c, w_ada, b_ada):
    depth, d, nd = w_ada.shape
    b = c.shape[0]
    tn = min(1024, d)
    return pl.pallas_call(
        _adaln_kernel,
        grid=(depth, nd // tn),
        in_specs=[
            pl.BlockSpec((b, d), lambda l, j: (0, 0)),
            pl.BlockSpec((None, d, tn), lambda l, j: (l, 0, j)),
            pl.BlockSpec((None, 1, tn), lambda l, j: (l, 0, j)),
        ],
        out_specs=pl.BlockSpec((None, b, tn), lambda l, j: (l, 0, j)),
        out_shape=jax.ShapeDtypeStruct((depth, b, nd), F32),
        compiler_params=_params("arbitrary", "arbitrary"),
        name="adaln_mod",
    )(c, w_ada, b_ada.reshape(depth, 1, nd))


def _norm_mod_kernel(x_ref, gain_ref, sc_ref, sh_ref, o_ref):
    x = x_ref[...]
    y = x * lax.rsqrt(jnp.mean(x * x, axis=-1, keepdims=True) + NORM_EPS)
    o_ref[...] = ((y * gain_ref[...]) * (1.0 + sc_ref[...]) + sh_ref[...]).astype(o_ref.dtype)


def norm_mod(x, gain, modl, i_shift, seq):
    n, d = x.shape
    tm = min(512, seq)
    return pl.pallas_call(
        _norm_mod_kernel,
        grid=(n // tm,),
        in_specs=[
            pl.BlockSpec((tm, d), lambda i: (i, 0)),
            pl.BlockSpec((1, d), lambda i: (0, 0)),
            pl.BlockSpec((None, 1, d), lambda i: ((i * tm) // seq * N_MOD + i_shift + 1, 0, 0)),
            pl.BlockSpec((None, 1, d), lambda i: ((i * tm) // seq * N_MOD + i_shift, 0, 0)),
        ],
        out_specs=pl.BlockSpec((tm, d), lambda i: (i, 0)),
        out_shape=jax.ShapeDtypeStruct((n, d), BF16),
        compiler_params=_params("arbitrary"),
        name="norm_mod",
    )(x, gain.reshape(1, d), modl, modl)


def _final_norm_kernel(x_ref, gain_ref, o_ref):
    x = x_ref[...]
    y = x * lax.rsqrt(jnp.mean(x * x, axis=-1, keepdims=True) + NORM_EPS)
    o_ref[...] = y * gain_ref[...]


def final_norm(x, gain):
    n, d = x.shape
    tm = min(512, n)
    return pl.pallas_call(
        _final_norm_kernel,
        grid=(n // tm,),
        in_specs=[pl.BlockSpec((tm, d), lambda i: (i, 0)), pl.BlockSpec((1, d), lambda i: (0, 0))],
        out_specs=pl.BlockSpec((tm, d), lambda i: (i, 0)),
        out_shape=jax.ShapeDtypeStruct((n, d), F32),
        compiler_params=_params("arbitrary"),
        name="final_norm",
    )(x, gain.reshape(1, d))


def _cast_on_first_row_step(pairs):
    @pl.when(pl.program_id(1) == 0)
    def _():
        for src_ref, dst_ref in pairs:
            dst_ref[...] = src_ref[...].astype(dst_ref.dtype)


def _ffn_up_kernel(h_ref, w1_ref, w3_ref, o_ref, w1b_ref, w3b_ref):
    _cast_on_first_row_step(((w1_ref, w1b_ref), (w3_ref, w3b_ref)))
    h = h_ref[...]
    a = _dot(h, w1b_ref[...])
    b = _dot(h, w3b_ref[...])
    o_ref[...] = ((a * jax.nn.sigmoid(a)) * b).astype(o_ref.dtype)


def ffn_up(h, w1, w3, layer):
    n, d = h.shape
    f = w1.shape[2]
    tm = min(1024, n)
    tn = 512 if f % 512 == 0 else f
    w_spec = pl.BlockSpec((None, d, tn), lambda j, i: (layer, 0, j))
    return pl.pallas_call(
        _ffn_up_kernel,
        grid=(f // tn, n // tm),
        in_specs=[pl.BlockSpec((tm, d), lambda j, i: (i, 0)), w_spec, w_spec],
        out_specs=pl.BlockSpec((tm, tn), lambda j, i: (i, j)),
        out_shape=jax.ShapeDtypeStruct((n, f), BF16),
        scratch_shapes=[pltpu.VMEM((d, tn), BF16), pltpu.VMEM((d, tn), BF16)],
        compiler_params=_params("arbitrary", "arbitrary"),
        name="ffn_up",
    )(h, w1, w3)


def _resid_mm_kernel(a_ref, w_ref, x_ref, g_ref, o_ref, wb_ref, *, gscale):
    _cast_on_first_row_step(((w_ref, wb_ref),))
    y = _dot(a_ref[...], wb_ref[...])
    o_ref[...] = x_ref[...] + (gscale * g_ref[...]) * y


def resid_mm(a, w, layer, x, modl, i_gate, gscale, seq):
    n, k = a.shape
    d = w.shape[2]
    tm = min(512, seq)
    tn = min(512, d)
    return pl.pallas_call(
        functools.partial(_resid_mm_kernel, gscale=gscale),
        grid=(d // tn, n // tm),
        in_specs=[
            pl.BlockSpec((tm, k), lambda j, i: (i, 0)),
            pl.BlockSpec((None, k, tn), lambda j, i: (layer, 0, j)),
            pl.BlockSpec((tm, tn), lambda j, i: (i, j)),
            pl.BlockSpec((None, 1, tn), lambda j, i: ((i * tm) // seq * N_MOD + i_gate, 0, j)),
        ],
        out_specs=pl.BlockSpec((tm, tn), lambda j, i: (i, j)),
        out_shape=jax.ShapeDtypeStruct((n, d), F32),
        scratch_shapes=[pltpu.VMEM((k, tn), BF16)],
        compiler_params=_params("arbitrary", "arbitrary"),
        name="resid_mm",
    )(a, w, x, modl)


def _widest_tile(n_chunks, chunk, bytes_per_unit, limit_bytes):
    return chunk * max(t for t in range(1, n_chunks + 1)
                       if n_chunks % t == 0 and (t == 1 or t * chunk * bytes_per_unit <= limit_bytes))


def _swap_halves(z, group):
    if group == LANES:
        return pltpu.roll(z, LANES // 2, axis=1)
    half = group // 2
    lane = lax.broadcasted_iota(jnp.int32, z.shape, 1)
    from_above = pltpu.roll(z, LANES - half, axis=1)
    from_below = pltpu.roll(z, half, axis=1)
    return jnp.where((lane & (group - 1)) < half, from_above, from_below)


def _proj_kernel(h_ref, w_ref, cos_ref, sin_ref, o_ref, wb_ref, *, group):
    _cast_on_first_row_step(((w_ref, wb_ref),))
    z = _dot(h_ref[...], wb_ref[...])
    if group == 0:
        o_ref[...] = z.astype(o_ref.dtype)
        return
    cos = cos_ref[...]
    sin = sin_ref[...]
    for c in range(z.shape[1] // LANES):
        zc = z[:, c * LANES:(c + 1) * LANES]
        o_ref[:, c * LANES:(c + 1) * LANES] = (zc * cos + _swap_halves(zc, group) * sin).astype(o_ref.dtype)


def proj(h, w, cos, sin, group, seq):
    n, d = h.shape
    cols = w.shape[1]
    tm = min(512, seq)
    tn = _widest_tile(cols // LANES, LANES, d * 4, 12 * 2 ** 20)
    s_blocks = seq // tm
    return pl.pallas_call(
        functools.partial(_proj_kernel, group=group),
        grid=(cols // tn, n // tm),
        in_specs=[
            pl.BlockSpec((tm, d), lambda j, i: (i, 0)),
            pl.BlockSpec((d, tn), lambda j, i: (0, j)),
            pl.BlockSpec((tm, LANES), lambda j, i: (i % s_blocks, 0)),
            pl.BlockSpec((tm, LANES), lambda j, i: (i % s_blocks, 0)),
        ],
        out_specs=pl.BlockSpec((tm, tn), lambda j, i: (i, j)),
        out_shape=jax.ShapeDtypeStruct((n, cols), BF16),
        scratch_shapes=[pltpu.VMEM((d, tn), BF16)],
        compiler_params=_params("arbitrary", "arbitrary"),
        name=f"proj_rope{group}",
    )(h, w, cos, sin)


def _proj_fm_kernel(w_ref, h_ref, cos_ref, sin_ref, o_ref, wt_ref, *, group, row_scales):
    @pl.when(pl.program_id(1) == 0)
    def _():
        wt_ref[...] = w_ref[...].T.astype(wt_ref.dtype)

    zt = _nt_dot(wt_ref[...], h_ref[...])
    step = group if group else LANES
    half = group // 2
    scale_of_row = [s for n_rows, s in row_scales for _ in range(n_rows // step)]
    if len(row_scales) == 1:
        scale_of_row = scale_of_row[:1] * (zt.shape[0] // step)
    for c in range(zt.shape[0] // step):
        blk = zt[c * step:(c + 1) * step]
        if group:
            swapped = jnp.concatenate([blk[half:], blk[:half]], axis=0)
            blk = blk * cos_ref[...] + swapped * sin_ref[...]
        if scale_of_row[c] != 1.0:
            blk = blk * scale_of_row[c]
        o_ref[c * step:(c + 1) * step, :] = blk.astype(o_ref.dtype)


def proj_fm(h, w, cos_fm, sin_fm, group, seq, row_scales=None):
    n, d = h.shape
    rows = w.shape[1]
    tm = min(TOKEN_TILE, seq)
    unit = max(group, LANES)
    row_scales = row_scales or ((rows, 1.0),)
    limit = 12 * 2 ** 20 if len(row_scales) == 1 else rows * d * 4
    tn = _widest_tile(rows // unit, unit, d * 4, limit)
    s_blocks = seq // tm
    g = max(group, 8)
    return pl.pallas_call(
        functools.partial(_proj_fm_kernel, group=group, row_scales=row_scales),
        grid=(rows // tn, n // tm),
        in_specs=[
            pl.BlockSpec((d, tn), lambda j, i: (0, j)),
            pl.BlockSpec((tm, d), lambda j, i: (i, 0)),
            pl.BlockSpec((g, tm), lambda j, i: (0, i % s_blocks)),
            pl.BlockSpec((g, tm), lambda j, i: (0, i % s_blocks)),
        ],
        out_specs=pl.BlockSpec((None, tn, tm), lambda j, i: (i, j, 0)),
        out_shape=jax.ShapeDtypeStruct((n // tm, rows, tm), BF16),
        scratch_shapes=[pltpu.VMEM((tn, d), BF16)],
        compiler_params=_params("arbitrary", "arbitrary"),
        name=f"proj_fm_rope{group}",
    )(w, h, cos_fm, sin_fm)


def _small_proj_kernel(h_ref, wt_ref, bf_ref, logf_ref, wi_ref):
    zt = _nt_dot(wt_ref[...], h_ref[...])
    logf_ref[...] = jax.nn.log_sigmoid(zt[0:8, :] + bf_ref[...])
    wi_ref[...] = zt[8:16, :] * IDX_W_SCALE


def small_proj(h, w_small_t, b_forget8, seq):
    n, d = h.shape
    tm = min(512, seq)
    return pl.pallas_call(
        _small_proj_kernel,
        grid=(n // tm,),
        in_specs=[
            pl.BlockSpec((tm, d), lambda i: (i, 0)),
            pl.BlockSpec((16, d), lambda i: (0, 0)),
            pl.BlockSpec((8, 1), lambda i: (0, 0)),
        ],
        out_specs=[pl.BlockSpec((8, tm), lambda i: (0, i)), pl.BlockSpec((8, tm), lambda i: (0, i))],
        out_shape=[jax.ShapeDtypeStruct((8, n), F32), jax.ShapeDtypeStruct((8, n), F32)],
        compiler_params=_params("arbitrary"),
        name="small_proj",
    )(h, w_small_t, b_forget8)


def _split3(x):
    x1 = x.astype(BF16)
    r1 = x - x1.astype(F32)
    x2 = r1.astype(BF16)
    x3 = (r1 - x2.astype(F32)).astype(BF16)
    return x1, x2, x3


def _neg_cumsum_kernel(x_ref, hi_ref, mid_ref, lo_ref):
    x = x_ref[...]
    chunks = x.shape[0]
    r = lax.broadcasted_iota(jnp.int32, (LANES, LANES), 0)
    c = lax.broadcasted_iota(jnp.int32, (LANES, LANES), 1)
    upper = (r <= c).astype(BF16)
    within = sum(_dot(p, upper) for p in _split3(x))
    totals = jnp.broadcast_to(within[:, LANES - 1:LANES], (chunks, LANES))
    rr = lax.broadcasted_iota(jnp.int32, (chunks, chunks), 0)
    cc = lax.broadcasted_iota(jnp.int32, (chunks, chunks), 1)
    strict_lower = (cc < rr).astype(BF16)
    offset = sum(_dot(strict_lower, p) for p in _split3(totals))
    hi_ref[...], mid_ref[...], lo_ref[...] = _split3(-(within + offset) * LOG2E)


def neg_cumsum(logf_t, batch, seq):
    rows = logf_t.shape[0] * batch
    chunks = seq // LANES
    x = logf_t.reshape(rows, chunks, LANES)
    spec = pl.BlockSpec((None, chunks, LANES), lambda i: (i, 0, 0))
    pieces = pl.pallas_call(
        _neg_cumsum_kernel,
        grid=(rows,),
        in_specs=[spec],
        out_specs=[spec] * 3,
        out_shape=[jax.ShapeDtypeStruct((rows, chunks, LANES), BF16)] * 3,
        compiler_params=_params("arbitrary"),
        name="neg_cumsum",
    )(x)
    rows3 = jnp.stack([p.reshape(logf_t.shape[0], batch, seq) for p in pieces], axis=2)
    return jnp.pad(rows3, ((0, 0), (0, 0), (0, 8 - N_BIAS_PIECES), (0, 0)))


def _query_streams(tq):
    width = min(MXU_WIDTH, tq)
    return [slice(lo, lo + width) for lo in range(0, tq, width)]


ONES_ROWS = 16
ACC_ROWS = HEAD_DIM + ONES_ROWS


def _softmax_stats(s, m_prev):
    m_new = jnp.maximum(m_prev, jnp.max(s, axis=0, keepdims=True))
    return m_new, jnp.exp2(m_prev - m_new), jnp.exp2(s - m_new).astype(BF16)


def _with_ones_rows(vt):
    return jnp.concatenate([vt, jnp.ones((ONES_ROWS, vt.shape[1]), vt.dtype)], axis=0)


def _normalized(acc):
    return acc[0:HEAD_DIM] / acc[HEAD_DIM:HEAD_DIM + 1]


def _causal_mask(key0, query0, n_keys, n_queries):
    key = key0 + lax.broadcasted_iota(jnp.int32, (n_keys, n_queries), 0)
    query = query0 + lax.broadcasted_iota(jnp.int32, (n_keys, n_queries), 1)
    return key <= query


def _sweep_key_tiles(q_block, tq, tk, tile_fn):
    n_full = (q_block * tq) // tk

    def body(kj, carry):
        tile_fn(kj, False)
        return carry

    lax.fori_loop(0, n_full, body, 0)
    tile_fn(n_full, True)


def _head_sweep(nq, tq, tk, streams, raw_scores, logits, values, pipe, steps_per_trip):
    assert tq == tk
    ids = range(len(streams))
    slot_a, slot_b = pipe[:3], pipe[3:]

    for (m_ref, acc_ref), qs in streams:
        m_ref[:, :, qs] = jnp.full((nq, 1, qs.stop - qs.start), -jnp.inf, F32)
        acc_ref[:, :, qs] = jnp.zeros((nq, ACC_ROWS, qs.stop - qs.start), F32)

    def accumulate(qi, pv, rescale):
        for i in ids:
            (_, acc_ref), qs = streams[i]
            acc_ref[qi, :, qs] = rescale[i] * acc_ref[qi, :, qs] + pv[i]

    def products(kj, slot):
        vts = values(kj)
        return [_dot(vts[i], slot[1][i]) for i in ids]

    def step(cur, prev, nxt, cur_slot, nxt_slot, masked):
        raw_next = raw_scores(*nxt)
        if prev is not None:
            pv_prev = products(prev[0], nxt_slot)
            rescale_prev = [nxt_slot[2][i] for i in ids]
        for i in ids:
            nxt_slot[0][i] = raw_next[i]
        kj, qi = cur
        for i in ids:
            (m_ref, _), qs = streams[i]
            m_new, alpha, p = _softmax_stats(logits(kj, qi, i, cur_slot[0][i], masked), m_ref[qi, :, qs])
            m_ref[qi, :, qs] = m_new
            cur_slot[1][i] = p
            cur_slot[2][i] = alpha
        if prev is not None:
            accumulate(prev[1], pv_prev, rescale_prev)

    def run(n_pairs, start, advance, masked):
        if n_pairs == 0:
            return
        slots = (slot_a, slot_b)
        raw0 = raw_scores(*start)
        for i in ids:
            slot_a[0][i] = raw0[i]
        second = advance(*start)
        step(start, None, second, slot_a, slot_b, masked)

        def steps(count, cur, prev):
            for u in range(count):
                nxt = advance(*cur)
                step(cur, prev, nxt, slots[(u + 1) % 2], slots[u % 2], masked)
                prev, cur = cur, nxt
            return cur, prev

        def trip(t, carry):
            cur, prev = steps(steps_per_trip, carry[:2], carry[2:])
            return (*cur, *prev)

        carry = lax.fori_loop(0, (n_pairs - 1) // steps_per_trip, trip, (*second, *start))
        _, prev = steps((n_pairs - 1) % steps_per_trip, carry[:2], carry[2:])
        last_slot = slots[(n_pairs - 1) % 2]
        accumulate(prev[1], products(prev[0], last_slot), [last_slot[2][i] for i in ids])

    last = nq - 1
    zero = jnp.int32(0)
    run(nq, (zero, zero), lambda kj, qi: (jnp.minimum(kj + 1, last), jnp.minimum(qi + 1, last)), True)

    def next_below_diagonal(kj, qi):
        wrap = qi == last
        dist = qi - kj
        return (jnp.where(wrap, 0, kj + 1), jnp.where(wrap, jnp.minimum(dist + 1, last), qi + 1))

    run(nq * (nq - 1) // 2, (zero, jnp.int32(min(1, last))), next_below_diagonal, False)


def _head_softmax_scratch(nq, tq):
    return [pltpu.VMEM((nq, 1, tq), F32), pltpu.VMEM((nq, ACC_ROWS, tq), F32)]


def _pipe_scratch(n_streams, tk, width):
    slot = lambda: [pltpu.VMEM((n_streams, tk, width), F32), pltpu.VMEM((n_streams, tk, width), BF16),
                    pltpu.VMEM((n_streams, 1, width), F32)]
    return slot() + slot()


N_BIAS_PIECES = 3


def _fox_kernel(qt_ref, k_ref, bias_ref, vt_ref, o_ref, m_ref, acc_ref, qa_ref, kb_ref, *pipe, nq, tq, tk):
    slices = _query_streams(tq)
    row = lax.broadcasted_iota(jnp.int32, (HEAD_DIM, tq), 0)
    ones_rows = jnp.where(row < N_BIAS_PIECES, 1.0, 0.0).astype(BF16)
    zeros = jnp.zeros((HEAD_DIM - 8, tk), F32)
    for j in range(nq):
        qa_ref[j] = jnp.concatenate([qt_ref[j], ones_rows], axis=0)
        rows = bias_ref[:, j * tk:(j + 1) * tk].astype(F32)
        kb_ref[j] = jnp.concatenate([rows, zeros], axis=0).T.astype(BF16)

    def raw_scores(kj, qi):
        k = jnp.concatenate([k_ref[pl.ds(pl.multiple_of(kj * tk, tk), tk), :], kb_ref[kj]], axis=1)
        return [_dot(k, qa_ref[qi, :, qs]) for qs in slices]

    def logits(kj, qi, i, raw, masked):
        qs = slices[i]
        if masked:
            raw = jnp.where(_causal_mask(kj * tk, qi * tq + qs.start, tk, qs.stop - qs.start), raw, NEG_INF)
        return raw

    def values(kj):
        return [_with_ones_rows(vt_ref[kj])] * len(slices)

    _head_sweep(nq, tq, tk, [((m_ref, acc_ref), qs) for qs in slices], raw_scores, logits, values, pipe, 4)
    for j in range(nq):
        o_ref[j * tq:(j + 1) * tq, :] = _normalized(acc_ref[j]).T.astype(o_ref.dtype)


def fox_attention(q_fm, q_blk0, k_rows, bias_rows, v_fm, v_blk0, seq):
    b = k_rows.shape[0]
    tq = tk = q_fm.shape[2]
    nq = seq // tq
    slices = _query_streams(tq)
    return pl.pallas_call(
        functools.partial(_fox_kernel, nq=nq, tq=tq, tk=tk),
        grid=(b, H_FOX),
        in_specs=[
            pl.BlockSpec((nq, HEAD_DIM, tq), lambda bi, h: (bi, q_blk0 + h, 0)),
            pl.BlockSpec((None, seq, HEAD_DIM), lambda bi, h: (bi, 0, h)),
            pl.BlockSpec((None, None, 8, seq), lambda bi, h: (h, bi, 0, 0)),
            pl.BlockSpec((nq, HEAD_DIM, tk), lambda bi, h: (bi, v_blk0 + h, 0)),
        ],
        out_specs=pl.BlockSpec((None, seq, HEAD_DIM), lambda bi, h: (bi, 0, h)),
        out_shape=jax.ShapeDtypeStruct((b, seq, W_FOX), BF16),
        scratch_shapes=_head_softmax_scratch(nq, tq)
        + [pltpu.VMEM((nq, 2 * HEAD_DIM, tq), BF16), pltpu.VMEM((nq, tk, HEAD_DIM), BF16)]
        + _pipe_scratch(len(slices), tk, slices[0].stop),
        compiler_params=_params("arbitrary", "arbitrary"),
        name="fox_attention",
    )(q_fm, k_rows, bias_rows, v_fm)


def _diff_kernel(lam_ref, gain_ref, qt_ref, k_ref, vt_ref, o_ref,
                 m1_ref, acc1_ref, m2_ref, acc2_ref, q1_ref, q2_ref, *pipe, nq, tq, tk, lam_init):
    zeros = jnp.zeros((DK_DIFF, tq), BF16)
    for j in range(nq):
        q1_ref[j] = jnp.concatenate([qt_ref[j, 0:DK_DIFF, :], zeros], axis=0)
        q2_ref[j] = jnp.concatenate([zeros, qt_ref[j, DK_DIFF:HEAD_DIM, :]], axis=0)
    streams, q_refs = [], []
    for qs in _query_streams(tq):
        for refs, q_ref in (((m1_ref, acc1_ref), q1_ref), ((m2_ref, acc2_ref), q2_ref)):
            streams.append((refs, qs))
            q_refs.append(q_ref)

    def raw_scores(kj, qi):
        k = k_ref[pl.ds(pl.multiple_of(kj * tk, tk), tk), :]
        return [_dot(k, q_ref[qi, :, qs]) for q_ref, (_, qs) in zip(q_refs, streams)]

    def logits(kj, qi, i, raw, masked):
        qs = streams[i][1]
        if masked:
            raw = jnp.where(_causal_mask(kj * tk, qi * tq + qs.start, tk, qs.stop - qs.start), raw, NEG_INF)
        return raw

    def values(kj):
        return [_with_ones_rows(vt_ref[kj])] * len(streams)

    _head_sweep(nq, tq, tk, streams, raw_scores, logits, values, pipe, 2)

    lam_vecs = lam_ref[...]
    dot1 = jnp.sum(lam_vecs[0:1] * lam_vecs[1:2], axis=1, keepdims=True)
    dot2 = jnp.sum(lam_vecs[2:3] * lam_vecs[3:4], axis=1, keepdims=True)
    lam = jnp.exp(dot1) - jnp.exp(dot2) + lam_init
    for j in range(nq):
        o = (_normalized(acc1_ref[j]) - lam * _normalized(acc2_ref[j])).T
        y = o * lax.rsqrt(jnp.mean(o * o, axis=-1, keepdims=True) + NORM_EPS)
        o_ref[j * tq:(j + 1) * tq, :] = ((y * gain_ref[...]) * (1.0 - lam_init)).astype(o_ref.dtype)


def diff_attention(q_fm, q_blk0, k_rows, v_fm, v_blk0, lam_vecs, subln_gain, lam_init, seq):
    b = k_rows.shape[0]
    tq = tk = q_fm.shape[2]
    nq = seq // tq
    slices = _query_streams(tq)
    return pl.pallas_call(
        functools.partial(_diff_kernel, nq=nq, tq=tq, tk=tk, lam_init=lam_init),
        grid=(b, H_DIFF),
        in_specs=[
            pl.BlockSpec((4, DK_DIFF), lambda bi, h: (0, 0)),
            pl.BlockSpec((1, HEAD_DIM), lambda bi, h: (0, 0)),
            pl.BlockSpec((nq, HEAD_DIM, tq), lambda bi, h: (bi, q_blk0 + h, 0)),
            pl.BlockSpec((None, seq, HEAD_DIM), lambda bi, h: (bi, 0, h)),
            pl.BlockSpec((nq, HEAD_DIM, tk), lambda bi, h: (bi, v_blk0 + h, 0)),
        ],
        out_specs=pl.BlockSpec((None, seq, HEAD_DIM), lambda bi, h: (bi, 0, h)),
        out_shape=jax.ShapeDtypeStruct((b, seq, W_DIFF), BF16),
        scratch_shapes=_head_softmax_scratch(nq, tq) + _head_softmax_scratch(nq, tq)
        + [pltpu.VMEM((nq, HEAD_DIM, tq), BF16), pltpu.VMEM((nq, HEAD_DIM, tq), BF16)]
        + _pipe_scratch(2 * len(slices), tk, slices[0].stop),
        compiler_params=_params("arbitrary", "arbitrary"),
        name="diff_attention",
    )(lam_vecs, subln_gain.reshape(1, HEAD_DIM), q_fm, k_rows, v_fm)


def _dsa_kernel(qt_ref, k_ref, vt_ref, qit_ref, ki_ref, wt_ref, o_ref,
                hi_ref, lo_ref, qm_ref, cut_ref, m_ref, acc_ref, *pipe,
                tq, tk, seq, n_sel):
    qb = pl.program_id(1)
    n_tiles = (qb * tq) // tk + 1
    slices = _query_streams(tq)

    zeros = jnp.zeros((D_IDX, tq), BF16)
    for h in range(H_IDX):
        qm_ref[h] = jnp.concatenate([qit_ref[h * D_IDX:(h + 1) * D_IDX, :], zeros], axis=0)

    def score_tile(kj, masked):
        kk = ki_ref[pl.ds(pl.multiple_of(kj * tk, tk), tk), :]
        for qs in slices:
            width = qs.stop - qs.start
            rel_q = [_dot(kk, qm_ref[h, :, qs]) for h in range(H_IDX)]
            score = jnp.zeros((tk, width), F32)
            for h in range(H_IDX):
                score = score + wt_ref[h:h + 1, qs] * jnp.maximum(rel_q[h], 0.0)
            if masked:
                score = jnp.where(_causal_mask(kj * tk, qb * tq + qs.start, tk, width), score, NEG_INF)
            bits = lax.bitcast_convert_type(score, jnp.int32)
            key = bits ^ ((bits >> 31) & 0x7FFFFFFF)
            hi_ref[kj, :, qs] = (key >> 16).astype(jnp.int16)
            lo_ref[kj, :, qs] = ((key & 0xFFFF) + I16_MIN).astype(jnp.int16)

    _sweep_key_tiles(qb, tq, tk, score_tile)

    n_beyond = (seq - n_tiles * tk).astype(F32)
    neg_hi, neg_lo = KEY_NEG_INF >> 16, (KEY_NEG_INF & 0xFFFF) + I16_MIN
    one, zero = jnp.int16(1), jnp.int16(0)
    as16 = lambda v: v.astype(jnp.int16)

    def count_hits(hit_fn):
        def body(kj, total):
            hit = jnp.where(hit_fn(kj), one, zero)
            part = hit[0:16]
            for r in range(1, tk // 16):
                part = part + hit[r * 16:(r + 1) * 16]
            return total + jnp.sum(part.astype(F32), axis=0, keepdims=True)
        return lax.fori_loop(0, n_tiles, body, jnp.zeros((1, tq), F32))

    def bisect16(enough):
        def step(s, t):
            cand = t + jnp.left_shift(jnp.int32(1), 15 - s)
            return jnp.where(enough(cand), cand, t)
        return lax.fori_loop(0, 16, step, jnp.full((1, tq), I16_MIN, jnp.int32))

    def count_hi_ge(cand):
        n = count_hits(lambda kj, c=as16(cand): hi_ref[kj] >= c)
        return n + jnp.where(cand <= neg_hi, n_beyond, 0.0)

    t_hi = bisect16(lambda cand: count_hi_ge(cand) >= n_sel)
    t_hi16 = as16(t_hi)
    n_above = jnp.where(t_hi < I16_MAX, count_hi_ge(jnp.minimum(t_hi + 1, I16_MAX)), 0.0)
    need_lo = n_sel - n_above

    def mask_lo(kj, carry):
        lo_ref[kj] = jnp.where(hi_ref[kj] == t_hi16, lo_ref[kj], jnp.int16(I16_MIN))
        return carry

    lax.fori_loop(0, n_tiles, mask_lo, 0)
    beyond_in_group = jnp.where(t_hi == neg_hi, n_beyond, 0.0)

    def count_lo_ge(cand):
        n = count_hits(lambda kj, c=as16(cand): lo_ref[kj] >= c)
        return n + jnp.where(cand <= neg_lo, beyond_in_group, 0.0)

    t_lo = bisect16(lambda cand: count_lo_ge(cand) >= need_lo)
    t_lo16 = as16(t_lo)

    def in_group(kj):
        return hi_ref[kj] == t_hi16

    n_ge = n_above + count_hits(lambda kj: in_group(kj) & (lo_ref[kj] >= t_lo16)) \
        + jnp.where(t_lo <= neg_lo, beyond_in_group, 0.0)
    surplus = n_ge - n_sel
    cut_ref[...] = jnp.full((1, tq), I16_MAX, jnp.int32)

    def key_index(kj):
        return as16(kj * tk + lax.broadcasted_iota(jnp.int32, (tk, tq), 0))

    @pl.when(jnp.max(surplus) > 0.0)
    def _():
        n_gt = n_above + count_hits(lambda kj: in_group(kj) & (lo_ref[kj] > t_lo16)) \
            + jnp.where(t_lo < neg_lo, beyond_in_group, 0.0)
        need = jnp.where(surplus > 0.0, n_sel - n_gt, float(seq + 1))

        def count_tied_below(cut):
            c = as16(cut)
            return count_hits(lambda kj: in_group(kj) & (lo_ref[kj] == t_lo16) & (key_index(kj) < c))

        n_bits = max(1, (seq - 1).bit_length())

        def bisect_cut(step, cut):
            cand = cut + jnp.left_shift(jnp.int32(1), n_bits - 1 - step)
            return jnp.where(count_tied_below(cand) < need, cand, cut)

        cut_ref[...] = lax.fori_loop(0, n_bits, bisect_cut, jnp.zeros((1, tq), jnp.int32))

    cut16 = as16(cut_ref[...])
    zero_bias, neg_bias = jnp.zeros((), BF16), jnp.asarray(NEG_INF, BF16)

    def write_bias(kj, carry):
        hi, lo = hi_ref[kj], lo_ref[kj]
        tied_ok = (lo == t_lo16) & (key_index(kj) <= cut16)
        sel = (hi > t_hi16) | ((hi == t_hi16) & ((lo > t_lo16) | tied_ok))
        hi_ref[kj] = lax.bitcast_convert_type(jnp.where(sel, zero_bias, neg_bias), jnp.int16)
        return carry

    lax.fori_loop(0, n_tiles, write_bias, 0)

    n_full = n_tiles - 1
    ids = range(len(slices))
    slots = (pipe[:3], pipe[3:])
    head_rows = lambda h: slice(h * HEAD_DIM, (h + 1) * HEAD_DIM)
    m_ref[...] = jnp.full(m_ref.shape, -jnp.inf, F32)
    acc_ref[...] = jnp.zeros(acc_ref.shape, F32)

    def raw_scores(kj, h):
        k = k_ref[pl.ds(pl.multiple_of(kj * tk, tk), tk), head_rows(h)]
        return [_dot(k, qt_ref[head_rows(h), qs]) for qs in slices]

    def products(kj, h, slot):
        vt = _with_ones_rows(vt_ref[kj, head_rows(h), :])
        return [_dot(vt, slot[1][i]) for i in ids]

    def accumulate(h, pv, rescale):
        for i, qs in enumerate(slices):
            acc_ref[h, :, qs] = rescale[i] * acc_ref[h, :, qs] + pv[i]

    def step(kj, h, first=False, last=False, masked=False):
        cur_slot, nxt_slot = slots[h % 2], slots[(h + 1) % 2]
        nxt = (kj, h + 1) if h + 1 < H_DSA else (kj + 1, 0)
        prev = (kj, h - 1) if h > 0 else (kj - 1, H_DSA - 1)
        raw_next = None if last else raw_scores(*nxt)
        if not first:
            pv_prev = products(*prev, nxt_slot)
            rescale_prev = [nxt_slot[2][i] for i in ids]
        if not last:
            for i in ids:
                nxt_slot[0][i] = raw_next[i]
        for i, qs in enumerate(slices):
            s = cur_slot[0][i] + lax.bitcast_convert_type(hi_ref[kj, :, qs], BF16).astype(F32)
            if masked:
                s = jnp.where(_causal_mask(kj * tk, qb * tq + qs.start, tk, qs.stop - qs.start), s, NEG_INF)
            m_new, alpha, p = _softmax_stats(s, m_ref[h, :, qs])
            m_ref[h, :, qs] = m_new
            cur_slot[1][i] = p
            cur_slot[2][i] = alpha
        if not first:
            accumulate(prev[1], pv_prev, rescale_prev)

    def tile_steps(kj, first=False, diagonal=False):
        for h in range(H_DSA):
            step(kj, h, first=first and h == 0, last=diagonal and h == H_DSA - 1, masked=diagonal)

    raw0 = raw_scores(0, 0)
    for i in ids:
        slots[0][0][i] = raw0[i]

    @pl.when(n_full == 0)
    def _():
        tile_steps(0, first=True, diagonal=True)

    @pl.when(n_full > 0)
    def _():
        tile_steps(0, first=True)

        def body(kj, carry):
            tile_steps(kj)
            return carry

        lax.fori_loop(1, n_full, body, 0)
        tile_steps(n_full, diagonal=True)

    last_slot = slots[(H_DSA - 1) % 2]
    accumulate(H_DSA - 1, products(n_full, H_DSA - 1, last_slot), [last_slot[2][i] for i in ids])
    for h in range(H_DSA):
        o_ref[:, head_rows(h)] = _normalized(acc_ref[h]).T.astype(o_ref.dtype)


def dsa_attention(q_fm, k_rows, v_fm, qi_fm, ki_rows, ki_blk, wi_t, seq):
    b = k_rows.shape[0]
    tq = tk = q_fm.shape[2]
    nq = seq // tq
    n_sel = min(TOPK_MAX, seq // 4)
    once = pl.Buffered(1)
    slices = _query_streams(tq)
    fm_q = lambda rows: pl.BlockSpec((None, rows, tq), lambda bi, i: (bi * nq + i, 0, 0))
    return pl.pallas_call(
        functools.partial(_dsa_kernel, tq=tq, tk=tk, seq=seq, n_sel=n_sel),
        grid=(b, nq),
        in_specs=[
            fm_q(W_DSA),
            pl.BlockSpec((None, seq, W_DSA), lambda bi, i: (bi, 0, 0), pipeline_mode=once),
            pl.BlockSpec((nq, W_DSA, tk), lambda bi, i: (bi, 0, 0), pipeline_mode=once),
            fm_q(W_IDX),
            pl.BlockSpec((None, seq, LANES), lambda bi, i: (bi, 0, ki_blk), pipeline_mode=once),
            pl.BlockSpec((H_IDX, tq), lambda bi, i: (0, bi * nq + i)),
        ],
        out_specs=pl.BlockSpec((None, tq, W_DSA), lambda bi, i: (bi, i, 0)),
        out_shape=jax.ShapeDtypeStruct((b, seq, W_DSA), BF16),
        scratch_shapes=[
            pltpu.VMEM((nq, tk, tq), jnp.int16),
            pltpu.VMEM((nq, tk, tq), jnp.int16),
            pltpu.VMEM((H_IDX, 2 * D_IDX, tq), BF16),
            pltpu.VMEM((1, tq), jnp.int32),
        ] + _head_softmax_scratch(H_DSA, tq) + _pipe_scratch(len(slices), tk, slices[0].stop),
        compiler_params=_params("arbitrary", "arbitrary"),
        name="dsa_attention",
    )(q_fm, k_rows, v_fm, qi_fm, ki_rows, wi_t)


def _merge_kernel(h_ref, oa_ref, ob_ref, oc_ref, mw0_ref, mw1_ref, mw2_ref, mb_ref, wa_ref, wb_ref, wc_ref, o_ref,
                  *bf16_refs):
    f32_refs = (mw0_ref, mw1_ref, mw2_ref, wa_ref, wb_ref, wc_ref)
    _cast_on_first_row_step(tuple(zip(f32_refs, bf16_refs)))
    gate_w, branch_w = bf16_refs[:N_BRANCH], bf16_refs[N_BRANCH:]
    h = h_ref[...]
    merged = None
    for i, o_b_ref in enumerate((oa_ref, ob_ref, oc_ref)):
        gate = jax.nn.sigmoid(_dot(h, gate_w[i][...]) + mb_ref[i])
        term = gate * _dot(o_b_ref[...], branch_w[i][...])
        merged = term if merged is None else merged + term
    o_ref[...] = merged.astype(o_ref.dtype)


def merge_branches(h, oa, ob, oc, merge_w, merge_b, wa, wb, wc, layer, seq):
    n, d = h.shape
    tm = min(1024, seq)
    tn = min(256, d)
    nj = d // tn
    row = lambda width: pl.BlockSpec((tm, width), lambda j, i: (i, 0))
    col = lambda rows: pl.BlockSpec((None, rows, tn), lambda j, i: (layer, 0, j))
    gate_w = lambda g: pl.BlockSpec((None, d, tn), lambda j, i: (layer, 0, g * nj + j))
    widths = (oa.shape[1], ob.shape[1], oc.shape[1])
    return pl.pallas_call(
        _merge_kernel,
        grid=(nj, n // tm),
        in_specs=[
            row(d), row(widths[0]), row(widths[1]), row(widths[2]),
            gate_w(0), gate_w(1), gate_w(2),
            pl.BlockSpec((None, N_BRANCH, 1, tn), lambda j, i: (layer, 0, 0, j)),
            col(widths[0]), col(widths[1]), col(widths[2]),
        ],
        out_specs=pl.BlockSpec((tm, tn), lambda j, i: (i, j)),
        out_shape=jax.ShapeDtypeStruct((n, d), BF16),
        scratch_shapes=[pltpu.VMEM((d, tn), BF16)] * N_BRANCH + [pltpu.VMEM((w, tn), BF16) for w in widths],
        compiler_params=_params("arbitrary", "arbitrary"),
        name="merge_branches",
    )(h, oa, ob, oc, merge_w, merge_w, merge_w, merge_b.reshape(merge_b.shape[0], N_BRANCH, 1, d), wa, wb, wc)


def _rope_tables(seq, dim):
    inv_freq = 1.0 / (ROPE_THETA ** (jnp.arange(0, dim, 2, dtype=F32) / dim))
    ang = jnp.arange(seq, dtype=F32)[:, None] * inv_freq[None, :]
    cos, sin = lax.optimization_barrier((jnp.cos(ang), jnp.sin(ang)))
    cos_g = jnp.concatenate([cos, cos], axis=1)
    sin_g = jnp.concatenate([-sin, sin], axis=1)
    reps = LANES // dim
    return jnp.tile(cos_g, (1, reps)), jnp.tile(sin_g, (1, reps)), cos_g.T, sin_g.T


def _split_w_in(w_in):
    sizes = (W_DIFF, W_DIFF, W_DIFF, W_FOX, W_FOX, W_FOX, H_FOX, W_DSA, W_DSA, W_DSA, W_IDX, D_IDX, H_IDX)
    parts, start = [], 0
    for n in sizes:
        parts.append(w_in[:, start:start + n])
        start += n
    return parts


def kernel(x, c, w_ada, b_ada, norm_ffn1, ffn1_w1, ffn1_w3, ffn1_w2, norm_mix, w_in, b_forget, lam_q1, lam_k1, lam_q2, lam_k2, subln_gain, merge_w, merge_b, w_branch_a, w_branch_b, w_branch_c, w_out, norm_ffn2, ffn2_w1, ffn2_w3, ffn2_w2, norm_final):
    batch, seq, d = x.shape
    depth = w_ada.shape[0]
    n = batch * seq
    tk = min(TOKEN_TILE, seq)
    bf = lambda a: a.astype(BF16)

    cos64, sin64, cos64_fm, sin64_fm = _rope_tables(seq, DK_DIFF)
    cos128, sin128, cos128_fm, sin128_fm = _rope_tables(seq, HEAD_DIM)
    mod = adaln_mod(c, w_ada, b_ada)
    xf = x.reshape(n, d)

    for l in range(depth):
        lam_init = 0.8 - 0.6 * math.exp(-0.3 * l)
        modl = mod[l].reshape(batch * N_MOD, 1, d)

        h = norm_mod(xf, norm_ffn1[l], modl, 0, seq)
        u = ffn_up(h, ffn1_w1, ffn1_w3, l)
        xf = resid_mm(u, ffn1_w2, l, xf, modl, 2, 0.5, seq)

        h = norm_mod(xf, norm_mix[l], modl, 3, seq)
        qa, ka, va, qb, kb, vb, fb, qc, kc, vc, qi, ki, wi = _split_w_in(w_in[l])
        k64 = proj(h, jnp.concatenate([ka, ki, ki], axis=1), cos64, sin64, DK_DIFF, seq).reshape(batch, seq, -1)
        k128 = proj(h, kc, cos128, sin128, HEAD_DIM, seq).reshape(batch, seq, -1)
        k0 = proj(h, kb, cos64, sin64, 0, seq).reshape(batch, seq, -1)
        log2e_over_sqrt = lambda width: (width ** -0.5) * LOG2E
        q64_fm = proj_fm(h, jnp.concatenate([qi, qa], axis=1), cos64_fm, sin64_fm, DK_DIFF, seq,
                         ((W_IDX, 1.0), (W_DIFF, log2e_over_sqrt(DK_DIFF))))
        q128_fm = proj_fm(h, qc, cos128_fm, sin128_fm, HEAD_DIM, seq, ((W_DSA, log2e_over_sqrt(HEAD_DIM)),))
        qb_fm = proj_fm(h, qb, cos64_fm, sin64_fm, 0, seq, ((W_FOX, log2e_over_sqrt(HEAD_DIM)),))
        v_fm = proj_fm(h, jnp.concatenate([vc, va, vb], axis=1), cos64_fm, sin64_fm, 0, seq)
        blk = lambda rows: rows // HEAD_DIM
        qa_blk, va_blk, vb_blk = blk(W_IDX), blk(W_DSA), blk(W_DSA + W_DIFF)

        w_small_t = bf(jnp.concatenate([fb, jnp.zeros((d, 8 - H_FOX), F32), wi], axis=1).T)
        b_forget8 = jnp.concatenate([b_forget[l], jnp.zeros((8 - H_FOX,), F32)]).reshape(8, 1)
        logf_t, wi_t = small_proj(h, w_small_t, b_forget8, seq)
        bias_rows = neg_cumsum(logf_t, batch, seq)

        lam_vecs = jnp.stack([lam_q1[l], lam_k1[l], lam_q2[l], lam_k2[l]])
        oa = diff_attention(q64_fm, qa_blk, k64, v_fm, va_blk, lam_vecs, subln_gain[l], lam_init, seq)
        ob = fox_attention(qb_fm, 0, k0, bias_rows, v_fm, vb_blk, seq)
        oc = dsa_attention(q128_fm, k128, v_fm, q64_fm, k64, blk(W_DIFF), wi_t, seq)

        merged = merge_branches(h, oa.reshape(n, -1), ob.reshape(n, -1), oc.reshape(n, -1), merge_w, merge_b,
                                w_branch_a, w_branch_b, w_branch_c, l, seq)
        xf = resid_mm(merged, w_out, l, xf, modl, 5, 1.0, seq)

        h = norm_mod(xf, norm_ffn2[l], modl, 6, seq)
        u = ffn_up(h, ffn2_w1, ffn2_w3, l)
        xf = resid_mm(u, ffn2_w2, l, xf, modl, 8, 0.5, seq)

    return final_norm(xf, norm_final).reshape(batch, seq, d)
```

```python
import functools
import math

import numpy as np
import jax
import jax.numpy as jnp
from jax import lax
from jax.experimental import pallas as pl
from jax.experimental.pallas import tpu as pltpu

HEAD_DIM = 128
H_DIFF = 6
DK_DIFF = HEAD_DIM // 2
H_FOX = 6
H_DSA = 4
H_IDX = 8
D_IDX = 64
TOPK_MAX = 256
ROPE_THETA = 10000.0
NORM_EPS = 1e-6
N_BRANCH = 3
N_MOD = 9
NEG_INF = -1e30
IDX_W_SCALE = (H_IDX ** -0.5) * (D_IDX ** -0.5)
LOG2E = math.log2(math.e)

W_DIFF = H_DIFF * HEAD_DIM
W_FOX = H_FOX * HEAD_DIM
W_DSA = H_DSA * HEAD_DIM
W_IDX = H_IDX * D_IDX

LANES = 128
MXU_WIDTH = 256
VMEM_LIMIT = 56 * 1024 * 1024
TOKEN_TILE = 512
I16_MIN, I16_MAX = -(2 ** 15), 2 ** 15 - 1

BF16 = jnp.bfloat16
F32 = jnp.float32


def _order_key_of(value):
    bits = int(np.array(value, np.float32).view(np.int32))
    return bits ^ ((bits >> 31) & 0x7FFFFFFF)


KEY_NEG_INF = _order_key_of(NEG_INF)


def _params(*semantics):
    return pltpu.CompilerParams(dimension_semantics=semantics, vmem_limit_bytes=VMEM_LIMIT)


def _nt_dot(a, b):
    return lax.dot_general(a, b, (((1,), (1,)), ((), ())), preferred_element_type=F32)


def _dot(a, b):
    return jnp.dot(a, b, preferred_element_type=F32)


def _adaln_kernel(c_ref, w_ref, b_ref, o_ref):
    c = c_ref[...]
    c_act = (c * jax.nn.sigmoid(c)).astype(BF16)
    o_ref[...] = _dot(c_act, w_ref[...].astype(BF16)) + b_ref[...]


def adaln_mod(c, w_ada, b_ada):
    depth, d, nd = w_ada.shape
    b = c.shape[0]
    tn = min(1024, d)
    return pl.pallas_call(
        _adaln_kernel,
        grid=(depth, nd // tn),
        in_specs=[
            pl.BlockSpec((b, d), lambda l, j: (0, 0)),
            pl.BlockSpec((None, d, tn), lambda l, j: (l, 0, j)),
            pl.BlockSpec((None, 1, tn), lambda l, j: (l, 0, j)),
        ],
        out_specs=pl.BlockSpec((None, b, tn), lambda l, j: (l, 0, j)),
        out_shape=jax.ShapeDtypeStruct((depth, b, nd), F32),
        compiler_params=_params("arbitrary", "arbitrary"),
        name="adaln_mod",
    )(c, w_ada, b_ada.reshape(depth, 1, nd))


def _norm_mod_kernel(x_ref, gain_ref, sc_ref, sh_ref, o_ref):
    x = x_ref[...]
    y = x * lax.rsqrt(jnp.mean(x * x, axis=-1, keepdims=True) + NORM_EPS)
    o_ref[...] = ((y * gain_ref[...]) * (1.0 + sc_ref[...]) + sh_ref[...]).astype(o_ref.dtype)


def norm_mod(x, gain, modl, i_shift, seq):
    n, d = x.shape
    tm = min(512, seq)
    return pl.pallas_call(
        _norm_mod_kernel,
        grid=(n // tm,),
        in_specs=[
            pl.BlockSpec((tm, d), lambda i: (i, 0)),
            pl.BlockSpec((1, d), lambda i: (0, 0)),
            pl.BlockSpec((None, 1, d), lambda i: ((i * tm) // seq * N_MOD + i_shift + 1, 0, 0)),
            pl.BlockSpec((None, 1, d), lambda i: ((i * tm) // seq * N_MOD + i_shift, 0, 0)),
        ],
        out_specs=pl.BlockSpec((tm, d), lambda i: (i, 0)),
        out_shape=jax.ShapeDtypeStruct((n, d), BF16),
        compiler_params=_params("arbitrary"),
        name="norm_mod",
    )(x, gain.reshape(1, d), modl, modl)


def _final_norm_kernel(x_ref, gain_ref, o_ref):
    x = x_ref[...]
    y = x * lax.rsqrt(jnp.mean(x * x, axis=-1, keepdims=True) + NORM_EPS)
    o_ref[...] = y * gain_ref[...]


def final_norm(x, gain):
    n, d = x.shape
    tm = min(512, n)
    return pl.pallas_call(
        _final_norm_kernel,
        grid=(n // tm,),
        in_specs=[pl.BlockSpec((tm, d), lambda i: (i, 0)), pl.BlockSpec((1, d), lambda i: (0, 0))],
        out_specs=pl.BlockSpec((tm, d), lambda i: (i, 0)),
        out_shape=jax.ShapeDtypeStruct((n, d), F32),
        compiler_params=_params("arbitrary"),
        name="final_norm",
    )(x, gain.reshape(1, d))


def _cast_on_first_row_step(pairs):
    @pl.when(pl.program_id(1) == 0)
    def _():
        for src_ref, dst_ref in pairs:
            dst_ref[...] = src_ref[...].astype(dst_ref.dtype)


def _ffn_up_kernel(h_ref, w1_ref, w3_ref, o_ref, w1b_ref, w3b_ref):
    _cast_on_first_row_step(((w1_ref, w1b_ref), (w3_ref, w3b_ref)))
    h = h_ref[...]
    a = _dot(h, w1b_ref[...])
    b = _dot(h, w3b_ref[...])
    o_ref[...] = ((a * jax.nn.sigmoid(a)) * b).astype(o_ref.dtype)


def ffn_up(h, w1, w3, layer):
    n, d = h.shape
    f = w1.shape[2]
    tm = min(1024, n)
    tn = 512 if f % 512 == 0 else f
    w_spec = pl.BlockSpec((None, d, tn), lambda j, i: (layer, 0, j))
    return pl.pallas_call(
        _ffn_up_kernel,
        grid=(f // tn, n // tm),
        in_specs=[pl.BlockSpec((tm, d), lambda j, i: (i, 0)), w_spec, w_spec],
        out_specs=pl.BlockSpec((tm, tn), lambda j, i: (i, j)),
        out_shape=jax.ShapeDtypeStruct((n, f), BF16),
        scratch_shapes=[pltpu.VMEM((d, tn), BF16), pltpu.VMEM((d, tn), BF16)],
        compiler_params=_params("arbitrary", "arbitrary"),
        name="ffn_up",
    )(h, w1, w3)


def _resid_mm_kernel(a_ref, w_ref, x_ref, g_ref, o_ref, wb_ref, *, gscale):
    _cast_on_first_row_step(((w_ref, wb_ref),))
    y = _dot(a_ref[...], wb_ref[...])
    o_ref[...] = x_ref[...] + (gscale * g_ref[...]) * y


def resid_mm(a, w, layer, x, modl, i_gate, gscale, seq):
    n, k = a.shape
    d = w.shape[2]
    tm = min(seq, max(512, 2 ** int(math.log2(8 * 2 ** 20 // (2 * k)))))
    tn = min(512, d)
    return pl.pallas_call(
        functools.partial(_resid_mm_kernel, gscale=gscale),
        grid=(d // tn, n // tm),
        in_specs=[
            pl.BlockSpec((tm, k), lambda j, i: (i, 0)),
            pl.BlockSpec((None, k, tn), lambda j, i: (layer, 0, j)),
            pl.BlockSpec((tm, tn), lambda j, i: (i, j)),
            pl.BlockSpec((None, 1, tn), lambda j, i: ((i * tm) // seq * N_MOD + i_gate, 0, j)),
        ],
        out_specs=pl.BlockSpec((tm, tn), lambda j, i: (i, j)),
        out_shape=jax.ShapeDtypeStruct((n, d), F32),
        scratch_shapes=[pltpu.VMEM((k, tn), BF16)],
        compiler_params=_params("arbitrary", "arbitrary"),
        name="resid_mm",
    )(a, w, x, modl)


def _widest_tile(n_chunks, chunk, bytes_per_unit, limit_bytes):
    return chunk * max(t for t in range(1, n_chunks + 1)
                       if n_chunks % t == 0 and (t == 1 or t * chunk * bytes_per_unit <= limit_bytes))


def _swap_halves(z, group):
    if group == LANES:
        return pltpu.roll(z, LANES // 2, axis=1)
    half = group // 2
    lane = lax.broadcasted_iota(jnp.int32, z.shape, 1)
    from_above = pltpu.roll(z, LANES - half, axis=1)
    from_below = pltpu.roll(z, half, axis=1)
    return jnp.where((lane & (group - 1)) < half, from_above, from_below)


def _proj_kernel(h_ref, w_ref, cos_ref, sin_ref, o_ref, wb_ref, *, group):
    _cast_on_first_row_step(((w_ref, wb_ref),))
    z = _dot(h_ref[...], wb_ref[...])
    if group == 0:
        o_ref[...] = z.astype(o_ref.dtype)
        return
    cos = cos_ref[...]
    sin = sin_ref[...]
    for c in range(z.shape[1] // LANES):
        zc = z[:, c * LANES:(c + 1) * LANES]
        o_ref[:, c * LANES:(c + 1) * LANES] = (zc * cos + _swap_halves(zc, group) * sin).astype(o_ref.dtype)


def proj(h, w, cos, sin, group, seq):
    n, d = h.shape
    cols = w.shape[1]
    tm = min(1024, seq)
    tn = _widest_tile(cols // LANES, LANES, d * 4, 12 * 2 ** 20)
    s_blocks = seq // tm
    return pl.pallas_call(
        functools.partial(_proj_kernel, group=group),
        grid=(cols // tn, n // tm),
        in_specs=[
            pl.BlockSpec((tm, d), lambda j, i: (i, 0)),
            pl.BlockSpec((d, tn), lambda j, i: (0, j)),
            pl.BlockSpec((tm, LANES), lambda j, i: (i % s_blocks, 0)),
            pl.BlockSpec((tm, LANES), lambda j, i: (i % s_blocks, 0)),
        ],
        out_specs=pl.BlockSpec((tm, tn), lambda j, i: (i, j)),
        out_shape=jax.ShapeDtypeStruct((n, cols), BF16),
        scratch_shapes=[pltpu.VMEM((d, tn), BF16)],
        compiler_params=_params("arbitrary", "arbitrary"),
        name=f"proj_rope{group}",
    )(h, w, cos, sin)


def _proj_fm_kernel(w_ref, h_ref, cos_ref, sin_ref, o_ref, wt_ref, *, group, row_scales):
    @pl.when(pl.program_id(1) == 0)
    def _():
        wt_ref[...] = w_ref[...].T.astype(wt_ref.dtype)

    zt = _nt_dot(wt_ref[...], h_ref[...])
    step = group if group else LANES
    half = group // 2
    scale_of_row = [s for n_rows, s in row_scales for _ in range(n_rows // step)]
    if len(row_scales) == 1:
        scale_of_row = scale_of_row[:1] * (zt.shape[0] // step)
    for c in range(zt.shape[0] // step):
        blk = zt[c * step:(c + 1) * step]
        if group:
            swapped = jnp.concatenate([blk[half:], blk[:half]], axis=0)
            blk = blk * cos_ref[...] + swapped * sin_ref[...]
        if scale_of_row[c] != 1.0:
            blk = blk * scale_of_row[c]
        tile = o_ref.shape[2]
        for t in range(o_ref.shape[0]):
            o_ref[t, c * step:(c + 1) * step, :] = blk[:, t * tile:(t + 1) * tile].astype(o_ref.dtype)


def proj_fm(h, w, cos_fm, sin_fm, group, seq, row_scales=None):
    n, d = h.shape
    rows = w.shape[1]
    tile = min(TOKEN_TILE, seq)
    tm = min(2 * tile, seq)
    unit = max(group, LANES)
    row_scales = row_scales or ((rows, 1.0),)
    limit = 12 * 2 ** 20 if len(row_scales) == 1 else rows * d * 4
    tn = _widest_tile(rows // unit, unit, d * 4, limit)
    s_blocks = seq // tm
    g = max(group, 8)
    return pl.pallas_call(
        functools.partial(_proj_fm_kernel, group=group, row_scales=row_scales),
        grid=(rows // tn, n // tm),
        in_specs=[
            pl.BlockSpec((d, tn), lambda j, i: (0, j), pipeline_mode=pl.Buffered(1 if tn == rows else 2)),
            pl.BlockSpec((tm, d), lambda j, i: (i, 0)),
            pl.BlockSpec((g, tm), lambda j, i: (0, i % s_blocks)),
            pl.BlockSpec((g, tm), lambda j, i: (0, i % s_blocks)),
        ],
        out_specs=pl.BlockSpec((tm // tile, tn, tile), lambda j, i: (i, j, 0)),
        out_shape=jax.ShapeDtypeStruct((n // tile, rows, tile), BF16),
        scratch_shapes=[pltpu.VMEM((tn, d), BF16)],
        compiler_params=_params("arbitrary", "arbitrary"),
        name=f"proj_fm_rope{group}",
    )(w, h, cos_fm, sin_fm)


def _small_proj_kernel(h_ref, wt_ref, bf_ref, logf_ref, wi_ref):
    zt = _nt_dot(wt_ref[...], h_ref[...])
    logf_ref[...] = jax.nn.log_sigmoid(zt[0:8, :] + bf_ref[...])
    wi_ref[...] = zt[8:16, :] * IDX_W_SCALE


def small_proj(h, w_small_t, b_forget8, seq):
    n, d = h.shape
    tm = min(512, seq)
    return pl.pallas_call(
        _small_proj_kernel,
        grid=(n // tm,),
        in_specs=[
            pl.BlockSpec((tm, d), lambda i: (i, 0)),
            pl.BlockSpec((16, d), lambda i: (0, 0)),
            pl.BlockSpec((8, 1), lambda i: (0, 0)),
        ],
        out_specs=[pl.BlockSpec((8, tm), lambda i: (0, i)), pl.BlockSpec((8, tm), lambda i: (0, i))],
        out_shape=[jax.ShapeDtypeStruct((8, n), F32), jax.ShapeDtypeStruct((8, n), F32)],
        compiler_params=_params("arbitrary"),
        name="small_proj",
    )(h, w_small_t, b_forget8)


def _split3(x):
    x1 = x.astype(BF16)
    r1 = x - x1.astype(F32)
    x2 = r1.astype(BF16)
    x3 = (r1 - x2.astype(F32)).astype(BF16)
    return x1, x2, x3


def _neg_cumsum_kernel(x_ref, hi_ref, mid_ref, lo_ref):
    x = x_ref[...]
    chunks = x.shape[0]
    r = lax.broadcasted_iota(jnp.int32, (LANES, LANES), 0)
    c = lax.broadcasted_iota(jnp.int32, (LANES, LANES), 1)
    upper = (r <= c).astype(BF16)
    within = sum(_dot(p, upper) for p in _split3(x))
    totals = jnp.broadcast_to(within[:, LANES - 1:LANES], (chunks, LANES))
    rr = lax.broadcasted_iota(jnp.int32, (chunks, chunks), 0)
    cc = lax.broadcasted_iota(jnp.int32, (chunks, chunks), 1)
    strict_lower = (cc < rr).astype(BF16)
    offset = sum(_dot(strict_lower, p) for p in _split3(totals))
    hi_ref[...], mid_ref[...], lo_ref[...] = _split3(-(within + offset) * LOG2E)


def neg_cumsum(logf_t, batch, seq):
    rows = logf_t.shape[0] * batch
    chunks = seq // LANES
    x = logf_t.reshape(rows, chunks, LANES)
    spec = pl.BlockSpec((None, chunks, LANES), lambda i: (i, 0, 0))
    pieces = pl.pallas_call(
        _neg_cumsum_kernel,
        grid=(rows,),
        in_specs=[spec],
        out_specs=[spec] * 3,
        out_shape=[jax.ShapeDtypeStruct((rows, chunks, LANES), BF16)] * 3,
        compiler_params=_params("arbitrary"),
        name="neg_cumsum",
    )(x)
    rows3 = jnp.stack([p.reshape(logf_t.shape[0], batch, seq) for p in pieces], axis=2)
    return jnp.pad(rows3, ((0, 0), (0, 0), (0, 8 - N_BIAS_PIECES), (0, 0)))


def _query_streams(tq):
    width = min(MXU_WIDTH, tq)
    return [slice(lo, lo + width) for lo in range(0, tq, width)]


ONES_ROWS = 16
ACC_ROWS = HEAD_DIM + ONES_ROWS


def _softmax_stats(s, m_prev):
    m_new = jnp.maximum(m_prev, jnp.max(s, axis=0, keepdims=True))
    return m_new, jnp.exp2(m_prev - m_new), jnp.exp2(s - m_new).astype(BF16)


def _with_ones_rows(vt):
    return jnp.concatenate([vt, jnp.ones((ONES_ROWS, vt.shape[1]), vt.dtype)], axis=0)


def _normalized(acc):
    return acc[0:HEAD_DIM] / acc[HEAD_DIM:HEAD_DIM + 1]


def _causal_mask(key0, query0, n_keys, n_queries):
    key = key0 + lax.broadcasted_iota(jnp.int32, (n_keys, n_queries), 0)
    query = query0 + lax.broadcasted_iota(jnp.int32, (n_keys, n_queries), 1)
    return key <= query


def _sweep_key_tiles(q_block, tq, tk, tile_fn):
    n_full = (q_block * tq) // tk

    def body(kj, carry):
        tile_fn(kj, False)
        return carry

    lax.fori_loop(0, n_full, body, 0)
    tile_fn(n_full, True)


def _head_sweep(nq, tq, tk, streams, raw_scores, logits, values, pipe, steps_per_trip):
    assert tq == tk
    ids = range(len(streams))
    slot_a, slot_b = pipe[:3], pipe[3:]

    for (m_ref, acc_ref), qs in streams:
        m_ref[:, :, qs] = jnp.full((nq, 1, qs.stop - qs.start), -jnp.inf, F32)
        acc_ref[:, :, qs] = jnp.zeros((nq, ACC_ROWS, qs.stop - qs.start), F32)

    def accumulate(qi, pv, rescale):
        for i in ids:
            (_, acc_ref), qs = streams[i]
            acc_ref[qi, :, qs] = rescale[i] * acc_ref[qi, :, qs] + pv[i]

    def products(kj, slot):
        vts = values(kj)
        return [_dot(vts[i], slot[1][i]) for i in ids]

    def step(cur, prev, nxt, cur_slot, nxt_slot, masked):
        raw_next = raw_scores(*nxt)
        if prev is not None:
            pv_prev = products(prev[0], nxt_slot)
            rescale_prev = [nxt_slot[2][i] for i in ids]
        for i in ids:
            nxt_slot[0][i] = raw_next[i]
        kj, qi = cur
        for i in ids:
            (m_ref, _), qs = streams[i]
            m_new, alpha, p = _softmax_stats(logits(kj, qi, i, cur_slot[0][i], masked), m_ref[qi, :, qs])
            m_ref[qi, :, qs] = m_new
            cur_slot[1][i] = p
            cur_slot[2][i] = alpha
        if prev is not None:
            accumulate(prev[1], pv_prev, rescale_prev)

    def run(n_pairs, start, advance, masked):
        if n_pairs == 0:
            return
        slots = (slot_a, slot_b)
        raw0 = raw_scores(*start)
        for i in ids:
            slot_a[0][i] = raw0[i]
        second = advance(*start)
        step(start, None, second, slot_a, slot_b, masked)

        def steps(count, cur, prev):
            for u in range(count):
                nxt = advance(*cur)
                step(cur, prev, nxt, slots[(u + 1) % 2], slots[u % 2], masked)
                prev, cur = cur, nxt
            return cur, prev

        def trip(t, carry):
            cur, prev = steps(steps_per_trip, carry[:2], carry[2:])
            return (*cur, *prev)

        carry = lax.fori_loop(0, (n_pairs - 1) // steps_per_trip, trip, (*second, *start))
        _, prev = steps((n_pairs - 1) % steps_per_trip, carry[:2], carry[2:])
        last_slot = slots[(n_pairs - 1) % 2]
        accumulate(prev[1], products(prev[0], last_slot), [last_slot[2][i] for i in ids])

    last = nq - 1
    zero = jnp.int32(0)
    run(nq, (zero, zero), lambda kj, qi: (jnp.minimum(kj + 1, last), jnp.minimum(qi + 1, last)), True)

    def next_below_diagonal(kj, qi):
        wrap = qi == last
        dist = qi - kj
        return (jnp.where(wrap, 0, kj + 1), jnp.where(wrap, jnp.minimum(dist + 1, last), qi + 1))

    run(nq * (nq - 1) // 2, (zero, jnp.int32(min(1, last))), next_below_diagonal, False)


def _head_softmax_scratch(nq, tq):
    return [pltpu.VMEM((nq, 1, tq), F32), pltpu.VMEM((nq, ACC_ROWS, tq), F32)]


def _pipe_scratch(n_streams, tk, width):
    slot = lambda: [pltpu.VMEM((n_streams, tk, width), F32), pltpu.VMEM((n_streams, tk, width), BF16),
                    pltpu.VMEM((n_streams, 1, width), F32)]
    return slot() + slot()


N_BIAS_PIECES = 3


def _fox_kernel(qt_ref, k_ref, bias_ref, vt_ref, o_ref, m_ref, acc_ref, qa_ref, kb_ref, *pipe, nq, tq, tk):
    slices = _query_streams(tq)
    row = lax.broadcasted_iota(jnp.int32, (HEAD_DIM, tq), 0)
    ones_rows = jnp.where(row < N_BIAS_PIECES, 1.0, 0.0).astype(BF16)
    zeros = jnp.zeros((HEAD_DIM - 8, tk), F32)
    for j in range(nq):
        qa_ref[j] = jnp.concatenate([qt_ref[j], ones_rows], axis=0)
        rows = bias_ref[:, j * tk:(j + 1) * tk].astype(F32)
        kb_ref[j] = jnp.concatenate([rows, zeros], axis=0).T.astype(BF16)

    def raw_scores(kj, qi):
        k = jnp.concatenate([k_ref[pl.ds(pl.multiple_of(kj * tk, tk), tk), :], kb_ref[kj]], axis=1)
        return [_dot(k, qa_ref[qi, :, qs]) for qs in slices]

    def logits(kj, qi, i, raw, masked):
        qs = slices[i]
        if masked:
            raw = jnp.where(_causal_mask(kj * tk, qi * tq + qs.start, tk, qs.stop - qs.start), raw, NEG_INF)
        return raw

    def values(kj):
        return [_with_ones_rows(vt_ref[kj])] * len(slices)

    _head_sweep(nq, tq, tk, [((m_ref, acc_ref), qs) for qs in slices], raw_scores, logits, values, pipe, 4)
    for j in range(nq):
        o_ref[j * tq:(j + 1) * tq, :] = _normalized(acc_ref[j]).T.astype(o_ref.dtype)


def fox_attention(q_fm, q_blk0, k_rows, bias_rows, v_fm, v_blk0, seq):
    b = k_rows.shape[0]
    tq = tk = q_fm.shape[2]
    nq = seq // tq
    slices = _query_streams(tq)
    return pl.pallas_call(
        functools.partial(_fox_kernel, nq=nq, tq=tq, tk=tk),
        grid=(b, H_FOX),
        in_specs=[
            pl.BlockSpec((nq, HEAD_DIM, tq), lambda bi, h: (bi, q_blk0 + h, 0)),
            pl.BlockSpec((None, seq, HEAD_DIM), lambda bi, h: (bi, 0, h)),
            pl.BlockSpec((None, None, 8, seq), lambda bi, h: (h, bi, 0, 0)),
            pl.BlockSpec((nq, HEAD_DIM, tk), lambda bi, h: (bi, v_blk0 + h, 0)),
        ],
        out_specs=pl.BlockSpec((None, seq, HEAD_DIM), lambda bi, h: (bi, 0, h)),
        out_shape=jax.ShapeDtypeStruct((b, seq, W_FOX), BF16),
        scratch_shapes=_head_softmax_scratch(nq, tq)
        + [pltpu.VMEM((nq, 2 * HEAD_DIM, tq), BF16), pltpu.VMEM((nq, tk, HEAD_DIM), BF16)]
        + _pipe_scratch(len(slices), tk, slices[0].stop),
        compiler_params=_params("arbitrary", "arbitrary"),
        name="fox_attention",
    )(q_fm, k_rows, bias_rows, v_fm)


def _diff_kernel(lam_ref, gain_ref, qt_ref, k_ref, vt_ref, o_ref,
                 m1_ref, acc1_ref, m2_ref, acc2_ref, q1_ref, q2_ref, *pipe, nq, tq, tk, lam_init):
    zeros = jnp.zeros((DK_DIFF, tq), BF16)
    for j in range(nq):
        q1_ref[j] = jnp.concatenate([qt_ref[j, 0:DK_DIFF, :], zeros], axis=0)
        q2_ref[j] = jnp.concatenate([zeros, qt_ref[j, DK_DIFF:HEAD_DIM, :]], axis=0)
    streams, q_refs = [], []
    for qs in _query_streams(tq):
        for refs, q_ref in (((m1_ref, acc1_ref), q1_ref), ((m2_ref, acc2_ref), q2_ref)):
            streams.append((refs, qs))
            q_refs.append(q_ref)

    def raw_scores(kj, qi):
        k = k_ref[pl.ds(pl.multiple_of(kj * tk, tk), tk), :]
        return [_dot(k, q_ref[qi, :, qs]) for q_ref, (_, qs) in zip(q_refs, streams)]

    def logits(kj, qi, i, raw, masked):
        qs = streams[i][1]
        if masked:
            raw = jnp.where(_causal_mask(kj * tk, qi * tq + qs.start, tk, qs.stop - qs.start), raw, NEG_INF)
        return raw

    def values(kj):
        return [_with_ones_rows(vt_ref[kj])] * len(streams)

    _head_sweep(nq, tq, tk, streams, raw_scores, logits, values, pipe, 2)

    lam_vecs = lam_ref[...]
    dot1 = jnp.sum(lam_vecs[0:1] * lam_vecs[1:2], axis=1, keepdims=True)
    dot2 = jnp.sum(lam_vecs[2:3] * lam_vecs[3:4], axis=1, keepdims=True)
    lam = jnp.exp(dot1) - jnp.exp(dot2) + lam_init
    for j in range(nq):
        o = (_normalized(acc1_ref[j]) - lam * _normalized(acc2_ref[j])).T
        y = o * lax.rsqrt(jnp.mean(o * o, axis=-1, keepdims=True) + NORM_EPS)
        o_ref[j * tq:(j + 1) * tq, :] = ((y * gain_ref[...]) * (1.0 - lam_init)).astype(o_ref.dtype)


def diff_attention(q_fm, q_blk0, k_rows, v_fm, v_blk0, lam_vecs, subln_gain, lam_init, seq):
    b = k_rows.shape[0]
    tq = tk = q_fm.shape[2]
    nq = seq // tq
    slices = _query_streams(tq)
    return pl.pallas_call(
        functools.partial(_diff_kernel, nq=nq, tq=tq, tk=tk, lam_init=lam_init),
        grid=(b, H_DIFF),
        in_specs=[
            pl.BlockSpec((4, DK_DIFF), lambda bi, h: (0, 0)),
            pl.BlockSpec((1, HEAD_DIM), lambda bi, h: (0, 0)),
            pl.BlockSpec((nq, HEAD_DIM, tq), lambda bi, h: (bi, q_blk0 + h, 0)),
            pl.BlockSpec((None, seq, HEAD_DIM), lambda bi, h: (bi, 0, h)),
            pl.BlockSpec((nq, HEAD_DIM, tk), lambda bi, h: (bi, v_blk0 + h, 0)),
        ],
        out_specs=pl.BlockSpec((None, seq, HEAD_DIM), lambda bi, h: (bi, 0, h)),
        out_shape=jax.ShapeDtypeStruct((b, seq, W_DIFF), BF16),
        scratch_shapes=_head_softmax_scratch(nq, tq) + _head_softmax_scratch(nq, tq)
        + [pltpu.VMEM((nq, HEAD_DIM, tq), BF16), pltpu.VMEM((nq, HEAD_DIM, tq), BF16)]
        + _pipe_scratch(2 * len(slices), tk, slices[0].stop),
        compiler_params=_params("arbitrary", "arbitrary"),
        name="diff_attention",
    )(lam_vecs, subln_gain.reshape(1, HEAD_DIM), q_fm, k_rows, v_fm)


def _dsa_kernel(qt_ref, k_ref, vt_ref, qit_ref, ki_ref, wt_ref, o_ref,
                hi_ref, lo_ref, qm_ref, cut_ref, m_ref, acc_ref, *pipe,
                tq, tk, seq, n_sel):
    qb = pl.program_id(1)
    n_tiles = (qb * tq) // tk + 1
    slices = _query_streams(tq)

    zeros = jnp.zeros((D_IDX, tq), BF16)
    for h in range(H_IDX):
        qm_ref[h] = jnp.concatenate([qit_ref[h * D_IDX:(h + 1) * D_IDX, :], zeros], axis=0)

    def score_tile(kj, masked):
        kk = ki_ref[pl.ds(pl.multiple_of(kj * tk, tk), tk), :]
        for qs in slices:
            width = qs.stop - qs.start
            rel_q = [_dot(kk, qm_ref[h, :, qs]) for h in range(H_IDX)]
            score = jnp.zeros((tk, width), F32)
            for h in range(H_IDX):
                score = score + wt_ref[h:h + 1, qs] * jnp.maximum(rel_q[h], 0.0)
            if masked:
                score = jnp.where(_causal_mask(kj * tk, qb * tq + qs.start, tk, width), score, NEG_INF)
            bits = lax.bitcast_convert_type(score, jnp.int32)
            key = bits ^ ((bits >> 31) & 0x7FFFFFFF)
            hi_ref[kj, :, qs] = (key >> 16).astype(jnp.int16)
            lo_ref[kj, :, qs] = ((key & 0xFFFF) + I16_MIN).astype(jnp.int16)

    _sweep_key_tiles(qb, tq, tk, score_tile)

    n_beyond = (seq - n_tiles * tk).astype(F32)
    neg_hi, neg_lo = KEY_NEG_INF >> 16, (KEY_NEG_INF & 0xFFFF) + I16_MIN
    one, zero = jnp.int16(1), jnp.int16(0)
    as16 = lambda v: v.astype(jnp.int16)

    def count_hits(hit_fn):
        def body(kj, total):
            hit = jnp.where(hit_fn(kj), one, zero)
            part = hit[0:16]
            for r in range(1, tk // 16):
                part = part + hit[r * 16:(r + 1) * 16]
            return total + jnp.sum(part.astype(F32), axis=0, keepdims=True)
        return lax.fori_loop(0, n_tiles, body, jnp.zeros((1, tq), F32))

    def bisect16(enough):
        def step(s, t):
            cand = t + jnp.left_shift(jnp.int32(1), 15 - s)
            return jnp.where(enough(cand), cand, t)
        return lax.fori_loop(0, 16, step, jnp.full((1, tq), I16_MIN, jnp.int32))

    def count_hi_ge(cand):
        n = count_hits(lambda kj, c=as16(cand): hi_ref[kj] >= c)
        return n + jnp.where(cand <= neg_hi, n_beyond, 0.0)

    t_hi = bisect16(lambda cand: count_hi_ge(cand) >= n_sel)
    t_hi16 = as16(t_hi)
    n_above = jnp.where(t_hi < I16_MAX, count_hi_ge(jnp.minimum(t_hi + 1, I16_MAX)), 0.0)
    need_lo = n_sel - n_above

    def mask_lo(kj, carry):
        lo_ref[kj] = jnp.where(hi_ref[kj] == t_hi16, lo_ref[kj], jnp.int16(I16_MIN))
        return carry

    lax.fori_loop(0, n_tiles, mask_lo, 0)
    beyond_in_group = jnp.where(t_hi == neg_hi, n_beyond, 0.0)

    def count_lo_ge(cand):
        n = count_hits(lambda kj, c=as16(cand): lo_ref[kj] >= c)
        return n + jnp.where(cand <= neg_lo, beyond_in_group, 0.0)

    t_lo = bisect16(lambda cand: count_lo_ge(cand) >= need_lo)
    t_lo16 = as16(t_lo)

    def in_group(kj):
        return hi_ref[kj] == t_hi16

    n_ge = n_above + count_hits(lambda kj: in_group(kj) & (lo_ref[kj] >= t_lo16)) \
        + jnp.where(t_lo <= neg_lo, beyond_in_group, 0.0)
    surplus = n_ge - n_sel
    cut_ref[...] = jnp.full((1, tq), I16_MAX, jnp.int32)

    def key_index(kj):
        return as16(kj * tk + lax.broadcasted_iota(jnp.int32, (tk, tq), 0))

    @pl.when(jnp.max(surplus) > 0.0)
    def _():
        n_gt = n_above + count_hits(lambda kj: in_group(kj) & (lo_ref[kj] > t_lo16)) \
            + jnp.where(t_lo < neg_lo, beyond_in_group, 0.0)
        need = jnp.where(surplus > 0.0, n_sel - n_gt, float(seq + 1))

        def count_tied_below(cut):
            c = as16(cut)
            return count_hits(lambda kj: in_group(kj) & (lo_ref[kj] == t_lo16) & (key_index(kj) < c))

        n_bits = max(1, (seq - 1).bit_length())

        def bisect_cut(step, cut):
            cand = cut + jnp.left_shift(jnp.int32(1), n_bits - 1 - step)
            return jnp.where(count_tied_below(cand) < need, cand, cut)

        cut_ref[...] = lax.fori_loop(0, n_bits, bisect_cut, jnp.zeros((1, tq), jnp.int32))

    cut16 = as16(cut_ref[...])
    zero_bias, neg_bias = jnp.zeros((), BF16), jnp.asarray(NEG_INF, BF16)

    def write_bias(kj, carry):
        hi, lo = hi_ref[kj], lo_ref[kj]
        tied_ok = (lo == t_lo16) & (key_index(kj) <= cut16)
        sel = (hi > t_hi16) | ((hi == t_hi16) & ((lo > t_lo16) | tied_ok))
        hi_ref[kj] = lax.bitcast_convert_type(jnp.where(sel, zero_bias, neg_bias), jnp.int16)
        return carry

    lax.fori_loop(0, n_tiles, write_bias, 0)

    n_full = n_tiles - 1
    ids = range(len(slices))
    slots = (pipe[:3], pipe[3:])
    head_rows = lambda h: slice(h * HEAD_DIM, (h + 1) * HEAD_DIM)
    m_ref[...] = jnp.full(m_ref.shape, -jnp.inf, F32)
    acc_ref[...] = jnp.zeros(acc_ref.shape, F32)

    def raw_scores(kj, h):
        k = k_ref[pl.ds(pl.multiple_of(kj * tk, tk), tk), head_rows(h)]
        return [_dot(k, qt_ref[head_rows(h), qs]) for qs in slices]

    def products(kj, h, slot):
        vt = _with_ones_rows(vt_ref[kj, head_rows(h), :])
        return [_dot(vt, slot[1][i]) for i in ids]

    def accumulate(h, pv, rescale):
        for i, qs in enumerate(slices):
            acc_ref[h, :, qs] = rescale[i] * acc_ref[h, :, qs] + pv[i]

    def step(kj, h, first=False, last=False, masked=False):
        cur_slot, nxt_slot = slots[h % 2], slots[(h + 1) % 2]
        nxt = (kj, h + 1) if h + 1 < H_DSA else (kj + 1, 0)
        prev = (kj, h - 1) if h > 0 else (kj - 1, H_DSA - 1)
        raw_next = None if last else raw_scores(*nxt)
        if not first:
            pv_prev = products(*prev, nxt_slot)
            rescale_prev = [nxt_slot[2][i] for i in ids]
        if not last:
            for i in ids:
                nxt_slot[0][i] = raw_next[i]
        for i, qs in enumerate(slices):
            s = cur_slot[0][i] + lax.bitcast_convert_type(hi_ref[kj, :, qs], BF16).astype(F32)
            if masked:
                s = jnp.where(_causal_mask(kj * tk, qb * tq + qs.start, tk, qs.stop - qs.start), s, NEG_INF)
            m_new, alpha, p = _softmax_stats(s, m_ref[h, :, qs])
            m_ref[h, :, qs] = m_new
            cur_slot[1][i] = p
            cur_slot[2][i] = alpha
        if not first:
            accumulate(prev[1], pv_prev, rescale_prev)

    def tile_steps(kj, first=False, diagonal=False):
        for h in range(H_DSA):
            step(kj, h, first=first and h == 0, last=diagonal and h == H_DSA - 1, masked=diagonal)

    raw0 = raw_scores(0, 0)
    for i in ids:
        slots[0][0][i] = raw0[i]

    @pl.when(n_full == 0)
    def _():
        tile_steps(0, first=True, diagonal=True)

    @pl.when(n_full > 0)
    def _():
        tile_steps(0, first=True)

        def body(kj, carry):
            tile_steps(kj)
            return carry

        lax.fori_loop(1, n_full, body, 0)
        tile_steps(n_full, diagonal=True)

    last_slot = slots[(H_DSA - 1) % 2]
    accumulate(H_DSA - 1, products(n_full, H_DSA - 1, last_slot), [last_slot[2][i] for i in ids])
    for h in range(H_DSA):
        o_ref[:, head_rows(h)] = _normalized(acc_ref[h]).T.astype(o_ref.dtype)


def dsa_attention(q_fm, k_rows, v_fm, qi_fm, ki_rows, ki_blk, wi_t, seq):
    b = k_rows.shape[0]
    tq = tk = q_fm.shape[2]
    nq = seq // tq
    n_sel = min(TOPK_MAX, seq // 4)
    once = pl.Buffered(1)
    slices = _query_streams(tq)
    fm_q = lambda rows: pl.BlockSpec((None, rows, tq), lambda bi, i: (bi * nq + i, 0, 0))
    return pl.pallas_call(
        functools.partial(_dsa_kernel, tq=tq, tk=tk, seq=seq, n_sel=n_sel),
        grid=(b, nq),
        in_specs=[
            fm_q(W_DSA),
            pl.BlockSpec((None, seq, W_DSA), lambda bi, i: (bi, 0, 0), pipeline_mode=once),
            pl.BlockSpec((nq, W_DSA, tk), lambda bi, i: (bi, 0, 0), pipeline_mode=once),
            fm_q(W_IDX),
            pl.BlockSpec((None, seq, LANES), lambda bi, i: (bi, 0, ki_blk), pipeline_mode=once),
            pl.BlockSpec((H_IDX, tq), lambda bi, i: (0, bi * nq + i)),
        ],
        out_specs=pl.BlockSpec((None, tq, W_DSA), lambda bi, i: (bi, i, 0)),
        out_shape=jax.ShapeDtypeStruct((b, seq, W_DSA), BF16),
        scratch_shapes=[
            pltpu.VMEM((nq, tk, tq), jnp.int16),
            pltpu.VMEM((nq, tk, tq), jnp.int16),
            pltpu.VMEM((H_IDX, 2 * D_IDX, tq), BF16),
            pltpu.VMEM((1, tq), jnp.int32),
        ] + _head_softmax_scratch(H_DSA, tq) + _pipe_scratch(len(slices), tk, slices[0].stop),
        compiler_params=_params("arbitrary", "arbitrary"),
        name="dsa_attention",
    )(q_fm, k_rows, v_fm, qi_fm, ki_rows, wi_t)


def _merge_kernel(h_ref, oa_ref, ob_ref, oc_ref, mw0_ref, mw1_ref, mw2_ref, mb_ref, wa_ref, wb_ref, wc_ref, o_ref,
                  *bf16_refs):
    f32_refs = (mw0_ref, mw1_ref, mw2_ref, wa_ref, wb_ref, wc_ref)
    _cast_on_first_row_step(tuple(zip(f32_refs, bf16_refs)))
    gate_w, branch_w = bf16_refs[:N_BRANCH], bf16_refs[N_BRANCH:]
    h = h_ref[...]
    merged = None
    for i, o_b_ref in enumerate((oa_ref, ob_ref, oc_ref)):
        gate = jax.nn.sigmoid(_dot(h, gate_w[i][...]) + mb_ref[i])
        term = gate * _dot(o_b_ref[...], branch_w[i][...])
        merged = term if merged is None else merged + term
    o_ref[...] = merged.astype(o_ref.dtype)


def merge_branches(h, oa, ob, oc, merge_w, merge_b, wa, wb, wc, layer, seq):
    n, d = h.shape
    tm = min(1024, seq)
    tn = min(256, d)
    nj = d // tn
    row = lambda width: pl.BlockSpec((tm, width), lambda j, i: (i, 0))
    col = lambda rows: pl.BlockSpec((None, rows, tn), lambda j, i: (layer, 0, j))
    gate_w = lambda g: pl.BlockSpec((None, d, tn), lambda j, i: (layer, 0, g * nj + j))
    widths = (oa.shape[1], ob.shape[1], oc.shape[1])
    return pl.pallas_call(
        _merge_kernel,
        grid=(nj, n // tm),
        in_specs=[
            row(d), row(widths[0]), row(widths[1]), row(widths[2]),
            gate_w(0), gate_w(1), gate_w(2),
            pl.BlockSpec((None, N_BRANCH, 1, tn), lambda j, i: (layer, 0, 0, j)),
            col(widths[0]), col(widths[1]), col(widths[2]),
        ],
        out_specs=pl.BlockSpec((tm, tn), lambda j, i: (i, j)),
        out_shape=jax.ShapeDtypeStruct((n, d), BF16),
        scratch_shapes=[pltpu.VMEM((d, tn), BF16)] * N_BRANCH + [pltpu.VMEM((w, tn), BF16) for w in widths],
        compiler_params=_params("arbitrary", "arbitrary"),
        name="merge_branches",
    )(h, oa, ob, oc, merge_w, merge_w, merge_w, merge_b.reshape(merge_b.shape[0], N_BRANCH, 1, d), wa, wb, wc)


def _rope_tables(seq, dim):
    inv_freq = 1.0 / (ROPE_THETA ** (jnp.arange(0, dim, 2, dtype=F32) / dim))
    ang = jnp.arange(seq, dtype=F32)[:, None] * inv_freq[None, :]
    cos, sin = lax.optimization_barrier((jnp.cos(ang), jnp.sin(ang)))
    cos_g = jnp.concatenate([cos, cos], axis=1)
    sin_g = jnp.concatenate([-sin, sin], axis=1)
    reps = LANES // dim
    return jnp.tile(cos_g, (1, reps)), jnp.tile(sin_g, (1, reps)), cos_g.T, sin_g.T


def _split_w_in(w_in):
    sizes = (W_DIFF, W_DIFF, W_DIFF, W_FOX, W_FOX, W_FOX, H_FOX, W_DSA, W_DSA, W_DSA, W_IDX, D_IDX, H_IDX)
    parts, start = [], 0
    for n in sizes:
        parts.append(w_in[:, start:start + n])
        start += n
    return parts


def kernel(x, c, w_ada, b_ada, norm_ffn1, ffn1_w1, ffn1_w3, ffn1_w2, norm_mix, w_in, b_forget, lam_q1, lam_k1, lam_q2, lam_k2, subln_gain, merge_w, merge_b, w_branch_a, w_branch_b, w_branch_c, w_out, norm_ffn2, ffn2_w1, ffn2_w3, ffn2_w2, norm_final):
    batch, seq, d = x.shape
    depth = w_ada.shape[0]
    n = batch * seq
    tk = min(TOKEN_TILE, seq)
    bf = lambda a: a.astype(BF16)

    cos64, sin64, cos64_fm, sin64_fm = _rope_tables(seq, DK_DIFF)
    cos128, sin128, cos128_fm, sin128_fm = _rope_tables(seq, HEAD_DIM)
    mod = adaln_mod(c, w_ada, b_ada)
    xf = x.reshape(n, d)

    for l in range(depth):
        lam_init = 0.8 - 0.6 * math.exp(-0.3 * l)
        modl = mod[l].reshape(batch * N_MOD, 1, d)

        h = norm_mod(xf, norm_ffn1[l], modl, 0, seq)
        u = ffn_up(h, ffn1_w1, ffn1_w3, l)
        xf = resid_mm(u, ffn1_w2, l, xf, modl, 2, 0.5, seq)

        h = norm_mod(xf, norm_mix[l], modl, 3, seq)
        qa, ka, va, qb, kb, vb, fb, qc, kc, vc, qi, ki, wi = _split_w_in(w_in[l])
        k64 = proj(h, jnp.concatenate([ka, ki, ki], axis=1), cos64, sin64, DK_DIFF, seq).reshape(batch, seq, -1)
        k128 = proj(h, kc, cos128, sin128, HEAD_DIM, seq).reshape(batch, seq, -1)
        k0 = proj(h, kb, cos64, sin64, 0, seq).reshape(batch, seq, -1)
        log2e_over_sqrt = lambda width: (width ** -0.5) * LOG2E
        q64_fm = proj_fm(h, jnp.concatenate([qi, qa], axis=1), cos64_fm, sin64_fm, DK_DIFF, seq,
                         ((W_IDX, 1.0), (W_DIFF, log2e_over_sqrt(DK_DIFF))))
        q128_fm = proj_fm(h, qc, cos128_fm, sin128_fm, HEAD_DIM, seq, ((W_DSA, log2e_over_sqrt(HEAD_DIM)),))
        qb_fm = proj_fm(h, qb, cos64_fm, sin64_fm, 0, seq, ((W_FOX, log2e_over_sqrt(HEAD_DIM)),))
        v_fm = proj_fm(h, jnp.concatenate([vc, va, vb], axis=1), cos64_fm, sin64_fm, 0, seq)
        blk = lambda rows: rows // HEAD_DIM
        qa_blk, va_blk, vb_blk = blk(W_IDX), blk(W_DSA), blk(W_DSA + W_DIFF)

        w_small_t = bf(jnp.concatenate([fb, jnp.zeros((d, 8 - H_FOX), F32), wi], axis=1).T)
        b_forget8 = jnp.concatenate([b_forget[l], jnp.zeros((8 - H_FOX,), F32)]).reshape(8, 1)
        logf_t, wi_t = small_proj(h, w_small_t, b_forget8, seq)
        bias_rows = neg_cumsum(logf_t, batch, seq)

        lam_vecs = jnp.stack([lam_q1[l], lam_k1[l], lam_q2[l], lam_k2[l]])
        oa = diff_attention(q64_fm, qa_blk, k64, v_fm, va_blk, lam_vecs, subln_gain[l], lam_init, seq)
        ob = fox_attention(qb_fm, 0, k0, bias_rows, v_fm, vb_blk, seq)
        oc = dsa_attention(q128_fm, k128, v_fm, q64_fm, k64, blk(W_DIFF), wi_t, seq)

        merged = merge_branches(h, oa.reshape(n, -1), ob.reshape(n, -1), oc.reshape(n, -1), merge_w, merge_b,
                                w_branch_a, w_branch_b, w_branch_c, l, seq)
        xf = resid_mm(merged, w_out, l, xf, modl, 5, 1.0, seq)

        h = norm_mod(xf, norm_ffn2[l], modl, 6, seq)
        u = ffn_up(h, ffn2_w1, ffn2_w3, l)
        xf = resid_mm(u, ffn2_w2, l, xf, modl, 8, 0.5, seq)

    return final_norm(xf, norm_final).reshape(batch, seq, d)
```

```python
import functools
import math

import numpy as np
import jax
import jax.numpy as jnp
from jax import lax
from jax.experimental import pallas as pl
from jax.experimental.pallas import tpu as pltpu

HEAD_DIM = 128
H_DIFF = 6
DK_DIFF = HEAD_DIM // 2
H_FOX = 6
H_DSA = 4
H_IDX = 8
D_IDX = 64
TOPK_MAX = 256
ROPE_THETA = 10000.0
NORM_EPS = 1e-6
N_BRANCH = 3
N_MOD = 9
NEG_INF = -1e30
IDX_W_SCALE = (H_IDX ** -0.5) * (D_IDX ** -0.5)
LOG2E = math.log2(math.e)

W_DIFF = H_DIFF * HEAD_DIM
W_FOX = H_FOX * HEAD_DIM
W_DSA = H_DSA * HEAD_DIM
W_IDX = H_IDX * D_IDX

LANES = 128
MXU_WIDTH = 256
VMEM_LIMIT = 56 * 1024 * 1024
TOKEN_TILE = 512
I16_MIN, I16_MAX = -(2 ** 15), 2 ** 15 - 1

BF16 = jnp.bfloat16
F32 = jnp.float32


def _order_key_of(value):
    bits = int(np.array(value, np.float32).view(np.int32))
    return bits ^ ((bits >> 31) & 0x7FFFFFFF)


KEY_NEG_INF = _order_key_of(NEG_INF)


def _params(*semantics):
    return pltpu.CompilerParams(dimension_semantics=semantics, vmem_limit_bytes=VMEM_LIMIT)


def _nt_dot(a, b):
    return lax.dot_general(a, b, (((1,), (1,)), ((), ())), preferred_element_type=F32)


def _dot(a, b):
    return jnp.dot(a, b, preferred_element_type=F32)


def _adaln_kernel(c_ref, w_ref, b_ref, o_ref):
    c = c_ref[...]
    c_act = (c * jax.nn.sigmoid(c)).astype(BF16)
    o_ref[...] = _dot(c_act, w_ref[...].astype(BF16)) + b_ref[...]


def adaln_mod(c, w_ada, b_ada):
    depth, d, nd = w_ada.shape
    b = c.shape[0]
    tn = min(1024, d)
    return pl.pallas_call(
        _adaln_kernel,
        grid=(depth, nd // tn),
        in_specs=[
            pl.BlockSpec((b, d), lambda l, j: (0, 0)),
            pl.BlockSpec((None, d, tn), lambda l, j: (l, 0, j)),
            pl.BlockSpec((None, 1, tn), lambda l, j: (l, 0, j)),
        ],
        out_specs=pl.BlockSpec((None, b, tn), lambda l, j: (l, 0, j)),
        out_shape=jax.ShapeDtypeStruct((depth, b, nd), F32),
        compiler_params=_params("arbitrary", "arbitrary"),
        name="adaln_mod",
    )(c, w_ada, b_ada.reshape(depth, 1, nd))


def _norm_mod_kernel(x_ref, gain_ref, sc_ref, sh_ref, o_ref):
    x = x_ref[...]
    y = x * lax.rsqrt(jnp.mean(x * x, axis=-1, keepdims=True) + NORM_EPS)
    o_ref[...] = ((y * gain_ref[...]) * (1.0 + sc_ref[...]) + sh_ref[...]).astype(o_ref.dtype)


def norm_mod(x, gain, modl, i_shift, seq):
    n, d = x.shape
    tm = min(512, seq)
    return pl.pallas_call(
        _norm_mod_kernel,
        grid=(n // tm,),
        in_specs=[
            pl.BlockSpec((tm, d), lambda i: (i, 0)),
            pl.BlockSpec((1, d), lambda i: (0, 0)),
            pl.BlockSpec((None, 1, d), lambda i: ((i * tm) // seq * N_MOD + i_shift + 1, 0, 0)),
            pl.BlockSpec((None, 1, d), lambda i: ((i * tm) // seq * N_MOD + i_shift, 0, 0)),
        ],
        out_specs=pl.BlockSpec((tm, d), lambda i: (i, 0)),
        out_shape=jax.ShapeDtypeStruct((n, d), BF16),
        compiler_params=_params("arbitrary"),
        name="norm_mod",
    )(x, gain.reshape(1, d), modl, modl)


def _final_norm_kernel(x_ref, gain_ref, o_ref):
    x = x_ref[...]
    y = x * lax.rsqrt(jnp.mean(x * x, axis=-1, keepdims=True) + NORM_EPS)
    o_ref[...] = y * gain_ref[...]


def final_norm(x, gain):
    n, d = x.shape
    tm = min(512, n)
    return pl.pallas_call(
        _final_norm_kernel,
        grid=(n // tm,),
        in_specs=[pl.BlockSpec((tm, d), lambda i: (i, 0)), pl.BlockSpec((1, d), lambda i: (0, 0))],
        out_specs=pl.BlockSpec((tm, d), lambda i: (i, 0)),
        out_shape=jax.ShapeDtypeStruct((n, d), F32),
        compiler_params=_params("arbitrary"),
        name="final_norm",
    )(x, gain.reshape(1, d))


def _cast_on_first_row_step(pairs):
    @pl.when(pl.program_id(1) == 0)
    def _():
        for src_ref, dst_ref in pairs:
            dst_ref[...] = src_ref[...].astype(dst_ref.dtype)


def _ffn_up_kernel(h_ref, w1_ref, w3_ref, o_ref, w1b_ref, w3b_ref):
    _cast_on_first_row_step(((w1_ref, w1b_ref), (w3_ref, w3b_ref)))
    h = h_ref[...]
    a = _dot(h, w1b_ref[...])
    b = _dot(h, w3b_ref[...])
    o_ref[...] = ((a * jax.nn.sigmoid(a)) * b).astype(o_ref.dtype)


def ffn_up(h, w1, w3, layer):
    n, d = h.shape
    f = w1.shape[2]
    tm = min(1024, n)
    tn = 512 if f % 512 == 0 else f
    w_spec = pl.BlockSpec((None, d, tn), lambda j, i: (layer, 0, j))
    return pl.pallas_call(
        _ffn_up_kernel,
        grid=(f // tn, n // tm),
        in_specs=[pl.BlockSpec((tm, d), lambda j, i: (i, 0)), w_spec, w_spec],
        out_specs=pl.BlockSpec((tm, tn), lambda j, i: (i, j)),
        out_shape=jax.ShapeDtypeStruct((n, f), BF16),
        scratch_shapes=[pltpu.VMEM((d, tn), BF16), pltpu.VMEM((d, tn), BF16)],
        compiler_params=_params("arbitrary", "arbitrary"),
        name="ffn_up",
    )(h, w1, w3)


def _resid_mm_kernel(a_ref, w_ref, x_ref, g_ref, o_ref, wb_ref, *, gscale):
    _cast_on_first_row_step(((w_ref, wb_ref),))
    y = _dot(a_ref[...], wb_ref[...])
    o_ref[...] = x_ref[...] + (gscale * g_ref[...]) * y


def resid_mm(a, w, layer, x, modl, i_gate, gscale, seq):
    n, k = a.shape
    d = w.shape[2]
    tm = min(seq, max(512, 2 ** int(math.log2(12 * 2 ** 20 // (2 * k)))))
    tn = min(512, d)
    return pl.pallas_call(
        functools.partial(_resid_mm_kernel, gscale=gscale),
        grid=(d // tn, n // tm),
        in_specs=[
            pl.BlockSpec((tm, k), lambda j, i: (i, 0)),
            pl.BlockSpec((None, k, tn), lambda j, i: (layer, 0, j), pipeline_mode=pl.Buffered(1)),
            pl.BlockSpec((tm, tn), lambda j, i: (i, j)),
            pl.BlockSpec((None, 1, tn), lambda j, i: ((i * tm) // seq * N_MOD + i_gate, 0, j)),
        ],
        out_specs=pl.BlockSpec((tm, tn), lambda j, i: (i, j)),
        out_shape=jax.ShapeDtypeStruct((n, d), F32),
        scratch_shapes=[pltpu.VMEM((k, tn), BF16)],
        compiler_params=_params("arbitrary", "arbitrary"),
        name="resid_mm",
    )(a, w, x, modl)


def _widest_tile(n_chunks, chunk, bytes_per_unit, limit_bytes):
    return chunk * max(t for t in range(1, n_chunks + 1)
                       if n_chunks % t == 0 and (t == 1 or t * chunk * bytes_per_unit <= limit_bytes))


def _swap_halves(z, group):
    if group == LANES:
        return pltpu.roll(z, LANES // 2, axis=1)
    half = group // 2
    lane = lax.broadcasted_iota(jnp.int32, z.shape, 1)
    from_above = pltpu.roll(z, LANES - half, axis=1)
    from_below = pltpu.roll(z, half, axis=1)
    return jnp.where((lane & (group - 1)) < half, from_above, from_below)


def _proj_kernel(h_ref, w_ref, cos_ref, sin_ref, o_ref, wb_ref, *, group):
    _cast_on_first_row_step(((w_ref, wb_ref),))
    z = _dot(h_ref[...], wb_ref[...])
    if group == 0:
        o_ref[...] = z.astype(o_ref.dtype)
        return
    cos = cos_ref[...]
    sin = sin_ref[...]
    for c in range(z.shape[1] // LANES):
        zc = z[:, c * LANES:(c + 1) * LANES]
        o_ref[:, c * LANES:(c + 1) * LANES] = (zc * cos + _swap_halves(zc, group) * sin).astype(o_ref.dtype)


def proj(h, w, cos, sin, group, seq):
    n, d = h.shape
    cols = w.shape[1]
    tm = min(1024, seq)
    tn = _widest_tile(cols // LANES, LANES, d * 4, 12 * 2 ** 20)
    s_blocks = seq // tm
    return pl.pallas_call(
        functools.partial(_proj_kernel, group=group),
        grid=(cols // tn, n // tm),
        in_specs=[
            pl.BlockSpec((tm, d), lambda j, i: (i, 0)),
            pl.BlockSpec((d, tn), lambda j, i: (0, j)),
            pl.BlockSpec((tm, LANES), lambda j, i: (i % s_blocks, 0)),
            pl.BlockSpec((tm, LANES), lambda j, i: (i % s_blocks, 0)),
        ],
        out_specs=pl.BlockSpec((tm, tn), lambda j, i: (i, j)),
        out_shape=jax.ShapeDtypeStruct((n, cols), BF16),
        scratch_shapes=[pltpu.VMEM((d, tn), BF16)],
        compiler_params=_params("arbitrary", "arbitrary"),
        name=f"proj_rope{group}",
    )(h, w, cos, sin)


def _proj_fm_kernel(w_ref, h_ref, cos_ref, sin_ref, o_ref, wt_ref, *, group, row_scales):
    @pl.when(pl.program_id(1) == 0)
    def _():
        wt_ref[...] = w_ref[...].T.astype(wt_ref.dtype)

    zt = _nt_dot(wt_ref[...], h_ref[...])
    step = group if group else LANES
    half = group // 2
    scale_of_row = [s for n_rows, s in row_scales for _ in range(n_rows // step)]
    if len(row_scales) == 1:
        scale_of_row = scale_of_row[:1] * (zt.shape[0] // step)
    for c in range(zt.shape[0] // step):
        blk = zt[c * step:(c + 1) * step]
        if group:
            swapped = jnp.concatenate([blk[half:], blk[:half]], axis=0)
            blk = blk * cos_ref[...] + swapped * sin_ref[...]
        if scale_of_row[c] != 1.0:
            blk = blk * scale_of_row[c]
        tile = o_ref.shape[2]
        for t in range(o_ref.shape[0]):
            o_ref[t, c * step:(c + 1) * step, :] = blk[:, t * tile:(t + 1) * tile].astype(o_ref.dtype)


def proj_fm(h, w, cos_fm, sin_fm, group, seq, row_scales=None):
    n, d = h.shape
    rows = w.shape[1]
    tile = min(TOKEN_TILE, seq)
    tm = min(2 * tile, seq)
    unit = max(group, LANES)
    row_scales = row_scales or ((rows, 1.0),)
    limit = 12 * 2 ** 20 if len(row_scales) == 1 else rows * d * 4
    tn = _widest_tile(rows // unit, unit, d * 4, limit)
    s_blocks = seq // tm
    g = max(group, 8)
    return pl.pallas_call(
        functools.partial(_proj_fm_kernel, group=group, row_scales=row_scales),
        grid=(rows // tn, n // tm),
        in_specs=[
            pl.BlockSpec((d, tn), lambda j, i: (0, j), pipeline_mode=pl.Buffered(1 if tn == rows else 2)),
            pl.BlockSpec((tm, d), lambda j, i: (i, 0)),
            pl.BlockSpec((g, tm), lambda j, i: (0, i % s_blocks)),
            pl.BlockSpec((g, tm), lambda j, i: (0, i % s_blocks)),
        ],
        out_specs=pl.BlockSpec((tm // tile, tn, tile), lambda j, i: (i, j, 0)),
        out_shape=jax.ShapeDtypeStruct((n // tile, rows, tile), BF16),
        scratch_shapes=[pltpu.VMEM((tn, d), BF16)],
        compiler_params=_params("arbitrary", "arbitrary"),
        name=f"proj_fm_rope{group}",
    )(w, h, cos_fm, sin_fm)


def _small_proj_kernel(h_ref, wt_ref, bf_ref, logf_ref, wi_ref):
    zt = _nt_dot(wt_ref[...], h_ref[...])
    logf_ref[...] = jax.nn.log_sigmoid(zt[0:8, :] + bf_ref[...])
    wi_ref[...] = zt[8:16, :] * IDX_W_SCALE


def small_proj(h, w_small_t, b_forget8, seq):
    n, d = h.shape
    tm = min(512, seq)
    return pl.pallas_call(
        _small_proj_kernel,
        grid=(n // tm,),
        in_specs=[
            pl.BlockSpec((tm, d), lambda i: (i, 0)),
            pl.BlockSpec((16, d), lambda i: (0, 0)),
            pl.BlockSpec((8, 1), lambda i: (0, 0)),
        ],
        out_specs=[pl.BlockSpec((8, tm), lambda i: (0, i)), pl.BlockSpec((8, tm), lambda i: (0, i))],
        out_shape=[jax.ShapeDtypeStruct((8, n), F32), jax.ShapeDtypeStruct((8, n), F32)],
        compiler_params=_params("arbitrary"),
        name="small_proj",
    )(h, w_small_t, b_forget8)


def _split3(x):
    x1 = x.astype(BF16)
    r1 = x - x1.astype(F32)
    x2 = r1.astype(BF16)
    x3 = (r1 - x2.astype(F32)).astype(BF16)
    return x1, x2, x3


def _neg_cumsum_kernel(x_ref, hi_ref, mid_ref, lo_ref):
    x = x_ref[...]
    chunks = x.shape[0]
    r = lax.broadcasted_iota(jnp.int32, (LANES, LANES), 0)
    c = lax.broadcasted_iota(jnp.int32, (LANES, LANES), 1)
    upper = (r <= c).astype(BF16)
    within = sum(_dot(p, upper) for p in _split3(x))
    totals = jnp.broadcast_to(within[:, LANES - 1:LANES], (chunks, LANES))
    rr = lax.broadcasted_iota(jnp.int32, (chunks, chunks), 0)
    cc = lax.broadcasted_iota(jnp.int32, (chunks, chunks), 1)
    strict_lower = (cc < rr).astype(BF16)
    offset = sum(_dot(strict_lower, p) for p in _split3(totals))
    hi_ref[...], mid_ref[...], lo_ref[...] = _split3(-(within + offset) * LOG2E)


def neg_cumsum(logf_t, batch, seq):
    rows = logf_t.shape[0] * batch
    chunks = seq // LANES
    x = logf_t.reshape(rows, chunks, LANES)
    spec = pl.BlockSpec((None, chunks, LANES), lambda i: (i, 0, 0))
    pieces = pl.pallas_call(
        _neg_cumsum_kernel,
        grid=(rows,),
        in_specs=[spec],
        out_specs=[spec] * 3,
        out_shape=[jax.ShapeDtypeStruct((rows, chunks, LANES), BF16)] * 3,
        compiler_params=_params("arbitrary"),
        name="neg_cumsum",
    )(x)
    rows3 = jnp.stack([p.reshape(logf_t.shape[0], batch, seq) for p in pieces], axis=2)
    return jnp.pad(rows3, ((0, 0), (0, 0), (0, 8 - N_BIAS_PIECES), (0, 0)))


def _query_streams(tq):
    width = min(MXU_WIDTH, tq)
    return [slice(lo, lo + width) for lo in range(0, tq, width)]


ONES_ROWS = 16
ACC_ROWS = HEAD_DIM + ONES_ROWS


def _softmax_stats(s, m_prev):
    m_new = jnp.maximum(m_prev, jnp.max(s, axis=0, keepdims=True))
    return m_new, jnp.exp2(m_prev - m_new), jnp.exp2(s - m_new).astype(BF16)


def _with_ones_rows(vt):
    return jnp.concatenate([vt, jnp.ones((ONES_ROWS, vt.shape[1]), vt.dtype)], axis=0)


def _normalized(acc):
    return acc[0:HEAD_DIM] / acc[HEAD_DIM:HEAD_DIM + 1]


def _causal_mask(key0, query0, n_keys, n_queries):
    key = key0 + lax.broadcasted_iota(jnp.int32, (n_keys, n_queries), 0)
    query = query0 + lax.broadcasted_iota(jnp.int32, (n_keys, n_queries), 1)
    return key <= query


def _sweep_key_tiles(q_block, tq, tk, tile_fn):
    n_full = (q_block * tq) // tk

    def body(kj, carry):
        tile_fn(kj, False)
        return carry

    lax.fori_loop(0, n_full, body, 0)
    tile_fn(n_full, True)


def _head_sweep(nq, tq, tk, streams, raw_scores, logits, values, pipe, steps_per_trip):
    assert tq == tk
    ids = range(len(streams))
    slot_a, slot_b = pipe[:3], pipe[3:]

    for (m_ref, acc_ref), qs in streams:
        m_ref[:, :, qs] = jnp.full((nq, 1, qs.stop - qs.start), -jnp.inf, F32)
        acc_ref[:, :, qs] = jnp.zeros((nq, ACC_ROWS, qs.stop - qs.start), F32)

    def accumulate(qi, pv, rescale):
        for i in ids:
            (_, acc_ref), qs = streams[i]
            acc_ref[qi, :, qs] = rescale[i] * acc_ref[qi, :, qs] + pv[i]

    def products(kj, slot):
        vts = values(kj)
        return [_dot(vts[i], slot[1][i]) for i in ids]

    def step(cur, prev, nxt, cur_slot, nxt_slot, masked):
        raw_next = raw_scores(*nxt)
        if prev is not None:
            pv_prev = products(prev[0], nxt_slot)
            rescale_prev = [nxt_slot[2][i] for i in ids]
        for i in ids:
            nxt_slot[0][i] = raw_next[i]
        kj, qi = cur
        for i in ids:
            (m_ref, _), qs = streams[i]
            m_new, alpha, p = _softmax_stats(logits(kj, qi, i, cur_slot[0][i], masked), m_ref[qi, :, qs])
            m_ref[qi, :, qs] = m_new
            cur_slot[1][i] = p
            cur_slot[2][i] = alpha
        if prev is not None:
            accumulate(prev[1], pv_prev, rescale_prev)

    def run(n_pairs, start, advance, masked):
        if n_pairs == 0:
            return
        slots = (slot_a, slot_b)
        raw0 = raw_scores(*start)
        for i in ids:
            slot_a[0][i] = raw0[i]
        second = advance(*start)
        step(start, None, second, slot_a, slot_b, masked)

        def steps(count, cur, prev):
            for u in range(count):
                nxt = advance(*cur)
                step(cur, prev, nxt, slots[(u + 1) % 2], slots[u % 2], masked)
                prev, cur = cur, nxt
            return cur, prev

        def trip(t, carry):
            cur, prev = steps(steps_per_trip, carry[:2], carry[2:])
            return (*cur, *prev)

        carry = lax.fori_loop(0, (n_pairs - 1) // steps_per_trip, trip, (*second, *start))
        _, prev = steps((n_pairs - 1) % steps_per_trip, carry[:2], carry[2:])
        last_slot = slots[(n_pairs - 1) % 2]
        accumulate(prev[1], products(prev[0], last_slot), [last_slot[2][i] for i in ids])

    last = nq - 1
    zero = jnp.int32(0)
    run(nq, (zero, zero), lambda kj, qi: (jnp.minimum(kj + 1, last), jnp.minimum(qi + 1, last)), True)

    def next_below_diagonal(kj, qi):
        wrap = qi == last
        dist = qi - kj
        return (jnp.where(wrap, 0, kj + 1), jnp.where(wrap, jnp.minimum(dist + 1, last), qi + 1))

    run(nq * (nq - 1) // 2, (zero, jnp.int32(min(1, last))), next_below_diagonal, False)


def _head_softmax_scratch(nq, tq):
    return [pltpu.VMEM((nq, 1, tq), F32), pltpu.VMEM((nq, ACC_ROWS, tq), F32)]


def _pipe_scratch(n_streams, tk, width):
    slot = lambda: [pltpu.VMEM((n_streams, tk, width), F32), pltpu.VMEM((n_streams, tk, width), BF16),
                    pltpu.VMEM((n_streams, 1, width), F32)]
    return slot() + slot()


N_BIAS_PIECES = 3


def _fox_kernel(qt_ref, k_ref, bias_ref, vt_ref, o_ref, m_ref, acc_ref, qa_ref, kb_ref, *pipe, nq, tq, tk):
    slices = _query_streams(tq)
    row = lax.broadcasted_iota(jnp.int32, (HEAD_DIM, tq), 0)
    ones_rows = jnp.where(row < N_BIAS_PIECES, 1.0, 0.0).astype(BF16)
    zeros = jnp.zeros((HEAD_DIM - 8, tk), F32)
    for j in range(nq):
        qa_ref[j] = jnp.concatenate([qt_ref[j], ones_rows], axis=0)
        rows = bias_ref[:, j * tk:(j + 1) * tk].astype(F32)
        kb_ref[j] = jnp.concatenate([rows, zeros], axis=0).T.astype(BF16)

    def raw_scores(kj, qi):
        k = jnp.concatenate([k_ref[pl.ds(pl.multiple_of(kj * tk, tk), tk), :], kb_ref[kj]], axis=1)
        return [_dot(k, qa_ref[qi, :, qs]) for qs in slices]

    def logits(kj, qi, i, raw, masked):
        qs = slices[i]
        if masked:
            raw = jnp.where(_causal_mask(kj * tk, qi * tq + qs.start, tk, qs.stop - qs.start), raw, NEG_INF)
        return raw

    def values(kj):
        return [_with_ones_rows(vt_ref[kj])] * len(slices)

    _head_sweep(nq, tq, tk, [((m_ref, acc_ref), qs) for qs in slices], raw_scores, logits, values, pipe, 4)
    for j in range(nq):
        o_ref[j * tq:(j + 1) * tq, :] = _normalized(acc_ref[j]).T.astype(o_ref.dtype)


def fox_attention(q_fm, q_blk0, k_rows, bias_rows, v_fm, v_blk0, seq):
    b = k_rows.shape[0]
    tq = tk = q_fm.shape[2]
    nq = seq // tq
    slices = _query_streams(tq)
    return pl.pallas_call(
        functools.partial(_fox_kernel, nq=nq, tq=tq, tk=tk),
        grid=(b, H_FOX),
        in_specs=[
            pl.BlockSpec((nq, HEAD_DIM, tq), lambda bi, h: (bi, q_blk0 + h, 0)),
            pl.BlockSpec((None, seq, HEAD_DIM), lambda bi, h: (bi, 0, h)),
            pl.BlockSpec((None, None, 8, seq), lambda bi, h: (h, bi, 0, 0)),
            pl.BlockSpec((nq, HEAD_DIM, tk), lambda bi, h: (bi, v_blk0 + h, 0)),
        ],
        out_specs=pl.BlockSpec((None, seq, HEAD_DIM), lambda bi, h: (bi, 0, h)),
        out_shape=jax.ShapeDtypeStruct((b, seq, W_FOX), BF16),
        scratch_shapes=_head_softmax_scratch(nq, tq)
        + [pltpu.VMEM((nq, 2 * HEAD_DIM, tq), BF16), pltpu.VMEM((nq, tk, HEAD_DIM), BF16)]
        + _pipe_scratch(len(slices), tk, slices[0].stop),
        compiler_params=_params("arbitrary", "arbitrary"),
        name="fox_attention",
    )(q_fm, k_rows, bias_rows, v_fm)


def _diff_kernel(lam_ref, gain_ref, qt_ref, k_ref, vt_ref, o_ref,
                 m1_ref, acc1_ref, m2_ref, acc2_ref, q1_ref, q2_ref, *pipe, nq, tq, tk, lam_init):
    zeros = jnp.zeros((DK_DIFF, tq), BF16)
    for j in range(nq):
        q1_ref[j] = jnp.concatenate([qt_ref[j, 0:DK_DIFF, :], zeros], axis=0)
        q2_ref[j] = jnp.concatenate([zeros, qt_ref[j, DK_DIFF:HEAD_DIM, :]], axis=0)
    streams, q_refs = [], []
    for qs in _query_streams(tq):
        for refs, q_ref in (((m1_ref, acc1_ref), q1_ref), ((m2_ref, acc2_ref), q2_ref)):
            streams.append((refs, qs))
            q_refs.append(q_ref)

    def raw_scores(kj, qi):
        k = k_ref[pl.ds(pl.multiple_of(kj * tk, tk), tk), :]
        return [_dot(k, q_ref[qi, :, qs]) for q_ref, (_, qs) in zip(q_refs, streams)]

    def logits(kj, qi, i, raw, masked):
        qs = streams[i][1]
        if masked:
            raw = jnp.where(_causal_mask(kj * tk, qi * tq + qs.start, tk, qs.stop - qs.start), raw, NEG_INF)
        return raw

    def values(kj):
        return [_with_ones_rows(vt_ref[kj])] * len(streams)

    _head_sweep(nq, tq, tk, streams, raw_scores, logits, values, pipe, 2)

    lam_vecs = lam_ref[...]
    dot1 = jnp.sum(lam_vecs[0:1] * lam_vecs[1:2], axis=1, keepdims=True)
    dot2 = jnp.sum(lam_vecs[2:3] * lam_vecs[3:4], axis=1, keepdims=True)
    lam = jnp.exp(dot1) - jnp.exp(dot2) + lam_init
    for j in range(nq):
        o = (_normalized(acc1_ref[j]) - lam * _normalized(acc2_ref[j])).T
        y = o * lax.rsqrt(jnp.mean(o * o, axis=-1, keepdims=True) + NORM_EPS)
        o_ref[j * tq:(j + 1) * tq, :] = ((y * gain_ref[...]) * (1.0 - lam_init)).astype(o_ref.dtype)


def diff_attention(q_fm, q_blk0, k_rows, v_fm, v_blk0, lam_vecs, subln_gain, lam_init, seq):
    b = k_rows.shape[0]
    tq = tk = q_fm.shape[2]
    nq = seq // tq
    slices = _query_streams(tq)
    return pl.pallas_call(
        functools.partial(_diff_kernel, nq=nq, tq=tq, tk=tk, lam_init=lam_init),
        grid=(b, H_DIFF),
        in_specs=[
            pl.BlockSpec((4, DK_DIFF), lambda bi, h: (0, 0)),
            pl.BlockSpec((1, HEAD_DIM), lambda bi, h: (0, 0)),
            pl.BlockSpec((nq, HEAD_DIM, tq), lambda bi, h: (bi, q_blk0 + h, 0)),
            pl.BlockSpec((None, seq, HEAD_DIM), lambda bi, h: (bi, 0, h)),
            pl.BlockSpec((nq, HEAD_DIM, tk), lambda bi, h: (bi, v_blk0 + h, 0)),
        ],
        out_specs=pl.BlockSpec((None, seq, HEAD_DIM), lambda bi, h: (bi, 0, h)),
        out_shape=jax.ShapeDtypeStruct((b, seq, W_DIFF), BF16),
        scratch_shapes=_head_softmax_scratch(nq, tq) + _head_softmax_scratch(nq, tq)
        + [pltpu.VMEM((nq, HEAD_DIM, tq), BF16), pltpu.VMEM((nq, HEAD_DIM, tq), BF16)]
        + _pipe_scratch(2 * len(slices), tk, slices[0].stop),
        compiler_params=_params("arbitrary", "arbitrary"),
        name="diff_attention",
    )(lam_vecs, subln_gain.reshape(1, HEAD_DIM), q_fm, k_rows, v_fm)


def _dsa_kernel(qt_ref, k_ref, vt_ref, qit_ref, ki_ref, wt_ref, o_ref,
                hi_ref, lo_ref, qm_ref, cut_ref, m_ref, acc_ref, *pipe,
                tq, tk, seq, n_sel):
    qb = pl.program_id(1)
    n_tiles = (qb * tq) // tk + 1
    slices = _query_streams(tq)

    zeros = jnp.zeros((D_IDX, tq), BF16)
    for h in range(H_IDX):
        qm_ref[h] = jnp.concatenate([qit_ref[h * D_IDX:(h + 1) * D_IDX, :], zeros], axis=0)

    def score_tile(kj, masked):
        kk = ki_ref[pl.ds(pl.multiple_of(kj * tk, tk), tk), :]
        for qs in slices:
            width = qs.stop - qs.start
            rel_q = [_dot(kk, qm_ref[h, :, qs]) for h in range(H_IDX)]
            score = jnp.zeros((tk, width), F32)
            for h in range(H_IDX):
                score = score + wt_ref[h:h + 1, qs] * jnp.maximum(rel_q[h], 0.0)
            if masked:
                score = jnp.where(_causal_mask(kj * tk, qb * tq + qs.start, tk, width), score, NEG_INF)
            bits = lax.bitcast_convert_type(score, jnp.int32)
            key = bits ^ ((bits >> 31) & 0x7FFFFFFF)
            hi_ref[kj, :, qs] = (key >> 16).astype(jnp.int16)
            lo_ref[kj, :, qs] = ((key & 0xFFFF) + I16_MIN).astype(jnp.int16)

    _sweep_key_tiles(qb, tq, tk, score_tile)

    n_beyond = (seq - n_tiles * tk).astype(F32)
    neg_hi, neg_lo = KEY_NEG_INF >> 16, (KEY_NEG_INF & 0xFFFF) + I16_MIN
    one, zero = jnp.int16(1), jnp.int16(0)
    as16 = lambda v: v.astype(jnp.int16)

    def count_hits(hit_fn):
        def body(kj, total):
            hit = jnp.where(hit_fn(kj), one, zero)
            part = hit[0:16]
            for r in range(1, tk // 16):
                part = part + hit[r * 16:(r + 1) * 16]
            return total + jnp.sum(part.astype(F32), axis=0, keepdims=True)
        return lax.fori_loop(0, n_tiles, body, jnp.zeros((1, tq), F32))

    def bisect16(enough):
        def step(s, t):
            cand = t + jnp.left_shift(jnp.int32(1), 15 - s)
            return jnp.where(enough(cand), cand, t)
        return lax.fori_loop(0, 16, step, jnp.full((1, tq), I16_MIN, jnp.int32))

    def count_hi_ge(cand):
        n = count_hits(lambda kj, c=as16(cand): hi_ref[kj] >= c)
        return n + jnp.where(cand <= neg_hi, n_beyond, 0.0)

    t_hi = bisect16(lambda cand: count_hi_ge(cand) >= n_sel)
    t_hi16 = as16(t_hi)
    n_above = jnp.where(t_hi < I16_MAX, count_hi_ge(jnp.minimum(t_hi + 1, I16_MAX)), 0.0)
    need_lo = n_sel - n_above

    def mask_lo(kj, carry):
        lo_ref[kj] = jnp.where(hi_ref[kj] == t_hi16, lo_ref[kj], jnp.int16(I16_MIN))
        return carry

    lax.fori_loop(0, n_tiles, mask_lo, 0)
    beyond_in_group = jnp.where(t_hi == neg_hi, n_beyond, 0.0)

    def count_lo_ge(cand):
        n = count_hits(lambda kj, c=as16(cand): lo_ref[kj] >= c)
        return n + jnp.where(cand <= neg_lo, beyond_in_group, 0.0)

    t_lo = bisect16(lambda cand: count_lo_ge(cand) >= need_lo)
    t_lo16 = as16(t_lo)

    def in_group(kj):
        return hi_ref[kj] == t_hi16

    n_ge = n_above + count_hits(lambda kj: in_group(kj) & (lo_ref[kj] >= t_lo16)) \
        + jnp.where(t_lo <= neg_lo, beyond_in_group, 0.0)
    surplus = n_ge - n_sel
    cut_ref[...] = jnp.full((1, tq), I16_MAX, jnp.int32)

    def key_index(kj):
        return as16(kj * tk + lax.broadcasted_iota(jnp.int32, (tk, tq), 0))

    @pl.when(jnp.max(surplus) > 0.0)
    def _():
        n_gt = n_above + count_hits(lambda kj: in_group(kj) & (lo_ref[kj] > t_lo16)) \
            + jnp.where(t_lo < neg_lo, beyond_in_group, 0.0)
        need = jnp.where(surplus > 0.0, n_sel - n_gt, float(seq + 1))

        def count_tied_below(cut):
            c = as16(cut)
            return count_hits(lambda kj: in_group(kj) & (lo_ref[kj] == t_lo16) & (key_index(kj) < c))

        n_bits = max(1, (seq - 1).bit_length())

        def bisect_cut(step, cut):
            cand = cut + jnp.left_shift(jnp.int32(1), n_bits - 1 - step)
            return jnp.where(count_tied_below(cand) < need, cand, cut)

        cut_ref[...] = lax.fori_loop(0, n_bits, bisect_cut, jnp.zeros((1, tq), jnp.int32))

    cut16 = as16(cut_ref[...])
    zero_bias, neg_bias = jnp.zeros((), BF16), jnp.asarray(NEG_INF, BF16)

    def write_bias(kj, carry):
        hi, lo = hi_ref[kj], lo_ref[kj]
        tied_ok = (lo == t_lo16) & (key_index(kj) <= cut16)
        sel = (hi > t_hi16) | ((hi == t_hi16) & ((lo > t_lo16) | tied_ok))
        hi_ref[kj] = lax.bitcast_convert_type(jnp.where(sel, zero_bias, neg_bias), jnp.int16)
        return carry

    lax.fori_loop(0, n_tiles, write_bias, 0)

    n_full = n_tiles - 1
    ids = range(len(slices))
    slots = (pipe[:3], pipe[3:])
    head_rows = lambda h: slice(h * HEAD_DIM, (h + 1) * HEAD_DIM)
    m_ref[...] = jnp.full(m_ref.shape, -jnp.inf, F32)
    acc_ref[...] = jnp.zeros(acc_ref.shape, F32)

    def raw_scores(kj, h):
        k = k_ref[pl.ds(pl.multiple_of(kj * tk, tk), tk), head_rows(h)]
        return [_dot(k, qt_ref[head_rows(h), qs]) for qs in slices]

    def products(kj, h, slot):
        vt = _with_ones_rows(vt_ref[kj, head_rows(h), :])
        return [_dot(vt, slot[1][i]) for i in ids]

    def accumulate(h, pv, rescale):
        for i, qs in enumerate(slices):
            acc_ref[h, :, qs] = rescale[i] * acc_ref[h, :, qs] + pv[i]

    def step(kj, h, first=False, last=False, masked=False):
        cur_slot, nxt_slot = slots[h % 2], slots[(h + 1) % 2]
        nxt = (kj, h + 1) if h + 1 < H_DSA else (kj + 1, 0)
        prev = (kj, h - 1) if h > 0 else (kj - 1, H_DSA - 1)
        raw_next = None if last else raw_scores(*nxt)
        if not first:
            pv_prev = products(*prev, nxt_slot)
            rescale_prev = [nxt_slot[2][i] for i in ids]
        if not last:
            for i in ids:
                nxt_slot[0][i] = raw_next[i]
        for i, qs in enumerate(slices):
            s = cur_slot[0][i] + lax.bitcast_convert_type(hi_ref[kj, :, qs], BF16).astype(F32)
            if masked:
                s = jnp.where(_causal_mask(kj * tk, qb * tq + qs.start, tk, qs.stop - qs.start), s, NEG_INF)
            m_new, alpha, p = _softmax_stats(s, m_ref[h, :, qs])
            m_ref[h, :, qs] = m_new
            cur_slot[1][i] = p
            cur_slot[2][i] = alpha
        if not first:
            accumulate(prev[1], pv_prev, rescale_prev)

    def tile_steps(kj, first=False, diagonal=False):
        for h in range(H_DSA):
            step(kj, h, first=first and h == 0, last=diagonal and h == H_DSA - 1, masked=diagonal)

    raw0 = raw_scores(0, 0)
    for i in ids:
        slots[0][0][i] = raw0[i]

    @pl.when(n_full == 0)
    def _():
        tile_steps(0, first=True, diagonal=True)

    @pl.when(n_full > 0)
    def _():
        tile_steps(0, first=True)

        def body(kj, carry):
            tile_steps(kj)
            return carry

        lax.fori_loop(1, n_full, body, 0)
        tile_steps(n_full, diagonal=True)

    last_slot = slots[(H_DSA - 1) % 2]
    accumulate(H_DSA - 1, products(n_full, H_DSA - 1, last_slot), [last_slot[2][i] for i in ids])
    for h in range(H_DSA):
        o_ref[:, head_rows(h)] = _normalized(acc_ref[h]).T.astype(o_ref.dtype)


def dsa_attention(q_fm, k_rows, v_fm, qi_fm, ki_rows, ki_blk, wi_t, seq):
    b = k_rows.shape[0]
    tq = tk = q_fm.shape[2]
    nq = seq // tq
    n_sel = min(TOPK_MAX, seq // 4)
    once = pl.Buffered(1)
    slices = _query_streams(tq)
    fm_q = lambda rows: pl.BlockSpec((None, rows, tq), lambda bi, i: (bi * nq + i, 0, 0))
    return pl.pallas_call(
        functools.partial(_dsa_kernel, tq=tq, tk=tk, seq=seq, n_sel=n_sel),
        grid=(b, nq),
        in_specs=[
            fm_q(W_DSA),
            pl.BlockSpec((None, seq, W_DSA), lambda bi, i: (bi, 0, 0), pipeline_mode=once),
            pl.BlockSpec((nq, W_DSA, tk), lambda bi, i: (bi, 0, 0), pipeline_mode=once),
            fm_q(W_IDX),
            pl.BlockSpec((None, seq, LANES), lambda bi, i: (bi, 0, ki_blk), pipeline_mode=once),
            pl.BlockSpec((H_IDX, tq), lambda bi, i: (0, bi * nq + i)),
        ],
        out_specs=pl.BlockSpec((None, tq, W_DSA), lambda bi, i: (bi, i, 0)),
        out_shape=jax.ShapeDtypeStruct((b, seq, W_DSA), BF16),
        scratch_shapes=[
            pltpu.VMEM((nq, tk, tq), jnp.int16),
            pltpu.VMEM((nq, tk, tq), jnp.int16),
            pltpu.VMEM((H_IDX, 2 * D_IDX, tq), BF16),
            pltpu.VMEM((1, tq), jnp.int32),
        ] + _head_softmax_scratch(H_DSA, tq) + _pipe_scratch(len(slices), tk, slices[0].stop),
        compiler_params=_params("arbitrary", "arbitrary"),
        name="dsa_attention",
    )(q_fm, k_rows, v_fm, qi_fm, ki_rows, wi_t)


def _merge_kernel(h_ref, oa_ref, ob_ref, oc_ref, mw0_ref, mw1_ref, mw2_ref, mb_ref, wa_ref, wb_ref, wc_ref, o_ref,
                  *bf16_refs):
    f32_refs = (mw0_ref, mw1_ref, mw2_ref, wa_ref, wb_ref, wc_ref)
    _cast_on_first_row_step(tuple(zip(f32_refs, bf16_refs)))
    gate_w, branch_w = bf16_refs[:N_BRANCH], bf16_refs[N_BRANCH:]
    h = h_ref[...]
    merged = None
    for i, o_b_ref in enumerate((oa_ref, ob_ref, oc_ref)):
        gate = jax.nn.sigmoid(_dot(h, gate_w[i][...]) + mb_ref[i])
        term = gate * _dot(o_b_ref[...], branch_w[i][...])
        merged = term if merged is None else merged + term
    o_ref[...] = merged.astype(o_ref.dtype)


def merge_branches(h, oa, ob, oc, merge_w, merge_b, wa, wb, wc, layer, seq):
    n, d = h.shape
    tm = min(1024, seq)
    tn = min(256, d)
    nj = d // tn
    row = lambda width: pl.BlockSpec((tm, width), lambda j, i: (i, 0))
    col = lambda rows: pl.BlockSpec((None, rows, tn), lambda j, i: (layer, 0, j))
    gate_w = lambda g: pl.BlockSpec((None, d, tn), lambda j, i: (layer, 0, g * nj + j))
    widths = (oa.shape[1], ob.shape[1], oc.shape[1])
    return pl.pallas_call(
        _merge_kernel,
        grid=(nj, n // tm),
        in_specs=[
            row(d), row(widths[0]), row(widths[1]), row(widths[2]),
            gate_w(0), gate_w(1), gate_w(2),
            pl.BlockSpec((None, N_BRANCH, 1, tn), lambda j, i: (layer, 0, 0, j)),
            col(widths[0]), col(widths[1]), col(widths[2]),
        ],
        out_specs=pl.BlockSpec((tm, tn), lambda j, i: (i, j)),
        out_shape=jax.ShapeDtypeStruct((n, d), BF16),
        scratch_shapes=[pltpu.VMEM((d, tn), BF16)] * N_BRANCH + [pltpu.VMEM((w, tn), BF16) for w in widths],
        compiler_params=_params("arbitrary", "arbitrary"),
        name="merge_branches",
    )(h, oa, ob, oc, merge_w, merge_w, merge_w, merge_b.reshape(merge_b.shape[0], N_BRANCH, 1, d), wa, wb, wc)


def _rope_tables(seq, dim):
    inv_freq = 1.0 / (ROPE_THETA ** (jnp.arange(0, dim, 2, dtype=F32) / dim))
    ang = jnp.arange(seq, dtype=F32)[:, None] * inv_freq[None, :]
    cos, sin = lax.optimization_barrier((jnp.cos(ang), jnp.sin(ang)))
    cos_g = jnp.concatenate([cos, cos], axis=1)
    sin_g = jnp.concatenate([-sin, sin], axis=1)
    reps = LANES // dim
    return jnp.tile(cos_g, (1, reps)), jnp.tile(sin_g, (1, reps)), cos_g.T, sin_g.T


def _split_w_in(w_in):
    sizes = (W_DIFF, W_DIFF, W_DIFF, W_FOX, W_FOX, W_FOX, H_FOX, W_DSA, W_DSA, W_DSA, W_IDX, D_IDX, H_IDX)
    parts, start = [], 0
    for n in sizes:
        parts.append(w_in[:, start:start + n])
        start += n
    return parts


def kernel(x, c, w_ada, b_ada, norm_ffn1, ffn1_w1, ffn1_w3, ffn1_w2, norm_mix, w_in, b_forget, lam_q1, lam_k1, lam_q2, lam_k2, subln_gain, merge_w, merge_b, w_branch_a, w_branch_b, w_branch_c, w_out, norm_ffn2, ffn2_w1, ffn2_w3, ffn2_w2, norm_final):
    batch, seq, d = x.shape
    depth = w_ada.shape[0]
    n = batch * seq
    tk = min(TOKEN_TILE, seq)
    bf = lambda a: a.astype(BF16)

    cos64, sin64, cos64_fm, sin64_fm = _rope_tables(seq, DK_DIFF)
    cos128, sin128, cos128_fm, sin128_fm = _rope_tables(seq, HEAD_DIM)
    mod = adaln_mod(c, w_ada, b_ada)
    xf = x.reshape(n, d)

    for l in range(depth):
        lam_init = 0.8 - 0.6 * math.exp(-0.3 * l)
        modl = mod[l].reshape(batch * N_MOD, 1, d)

        h = norm_mod(xf, norm_ffn1[l], modl, 0, seq)
        u = ffn_up(h, ffn1_w1, ffn1_w3, l)
        xf = resid_mm(u, ffn1_w2, l, xf, modl, 2, 0.5, seq)

        h = norm_mod(xf, norm_mix[l], modl, 3, seq)
        qa, ka, va, qb, kb, vb, fb, qc, kc, vc, qi, ki, wi = _split_w_in(w_in[l])
        k64 = proj(h, jnp.concatenate([ka, ki, ki], axis=1), cos64, sin64, DK_DIFF, seq).reshape(batch, seq, -1)
        k128 = proj(h, kc, cos128, sin128, HEAD_DIM, seq).reshape(batch, seq, -1)
        k0 = proj(h, kb, cos64, sin64, 0, seq).reshape(batch, seq, -1)
        log2e_over_sqrt = lambda width: (width ** -0.5) * LOG2E
        q64_fm = proj_fm(h, jnp.concatenate([qi, qa], axis=1), cos64_fm, sin64_fm, DK_DIFF, seq,
                         ((W_IDX, 1.0), (W_DIFF, log2e_over_sqrt(DK_DIFF))))
        q128_fm = proj_fm(h, qc, cos128_fm, sin128_fm, HEAD_DIM, seq, ((W_DSA, log2e_over_sqrt(HEAD_DIM)),))
        qb_fm = proj_fm(h, qb, cos64_fm, sin64_fm, 0, seq, ((W_FOX, log2e_over_sqrt(HEAD_DIM)),))
        v_fm = proj_fm(h, jnp.concatenate([vc, va, vb], axis=1), cos64_fm, sin64_fm, 0, seq)
        blk = lambda rows: rows // HEAD_DIM
        qa_blk, va_blk, vb_blk = blk(W_IDX), blk(W_DSA), blk(W_DSA + W_DIFF)

        w_small_t = bf(jnp.concatenate([fb, jnp.zeros((d, 8 - H_FOX), F32), wi], axis=1).T)
        b_forget8 = jnp.concatenate([b_forget[l], jnp.zeros((8 - H_FOX,), F32)]).reshape(8, 1)
        logf_t, wi_t = small_proj(h, w_small_t, b_forget8, seq)
        bias_rows = neg_cumsum(logf_t, batch, seq)

        lam_vecs = jnp.stack([lam_q1[l], lam_k1[l], lam_q2[l], lam_k2[l]])
        oa = diff_attention(q64_fm, qa_blk, k64, v_fm, va_blk, lam_vecs, subln_gain[l], lam_init, seq)
        ob = fox_attention(qb_fm, 0, k0, bias_rows, v_fm, vb_blk, seq)
        oc = dsa_attention(q128_fm, k128, v_fm, q64_fm, k64, blk(W_DIFF), wi_t, seq)

        merged = merge_branches(h, oa.reshape(n, -1), ob.reshape(n, -1), oc.reshape(n, -1), merge_w, merge_b,
                                w_branch_a, w_branch_b, w_branch_c, l, seq)
        xf = resid_mm(merged, w_out, l, xf, modl, 5, 1.0, seq)

        h = norm_mod(xf, norm_ffn2[l], modl, 6, seq)
        u = ffn_up(h, ffn2_w1, ffn2_w3, l)
        xf = resid_mm(u, ffn2_w2, l, xf, modl, 8, 0.5, seq)

    return final_norm(xf, norm_final).reshape(batch, seq, d)
```

```python
import functools
import math

import numpy as np
import jax
import jax.numpy as jnp
from jax import lax
from jax.experimental import pallas as pl
from jax.experimental.pallas import tpu as pltpu

HEAD_DIM = 128
H_DIFF = 6
DK_DIFF = HEAD_DIM // 2
H_FOX = 6
H_DSA = 4
H_IDX = 8
D_IDX = 64
TOPK_MAX = 256
ROPE_THETA = 10000.0
NORM_EPS = 1e-6
N_BRANCH = 3
N_MOD = 9
NEG_INF = -1e30
IDX_W_SCALE = (H_IDX ** -0.5) * (D_IDX ** -0.5)
LOG2E = math.log2(math.e)

W_DIFF = H_DIFF * HEAD_DIM
W_FOX = H_FOX * HEAD_DIM
W_DSA = H_DSA * HEAD_DIM
W_IDX = H_IDX * D_IDX

LANES = 128
MXU_WIDTH = 256
VMEM_LIMIT = 56 * 1024 * 1024
TOKEN_TILE = 512
I16_MIN, I16_MAX = -(2 ** 15), 2 ** 15 - 1

BF16 = jnp.bfloat16
F32 = jnp.float32


def _order_key_of(value):
    bits = int(np.array(value, np.float32).view(np.int32))
    return bits ^ ((bits >> 31) & 0x7FFFFFFF)


KEY_NEG_INF = _order_key_of(NEG_INF)


def _params(*semantics):
    return pltpu.CompilerParams(dimension_semantics=semantics, vmem_limit_bytes=VMEM_LIMIT)


def _nt_dot(a, b):
    return lax.dot_general(a, b, (((1,), (1,)), ((), ())), preferred_element_type=F32)


def _dot(a, b):
    return jnp.dot(a, b, preferred_element_type=F32)


def _adaln_kernel(c_ref, w_ref, b_ref, o_ref):
    c = c_ref[...]
    c_act = (c * jax.nn.sigmoid(c)).astype(BF16)
    o_ref[...] = _dot(c_act, w_ref[...].astype(BF16)) + b_ref[...]


def adaln_mod(c, w_ada, b_ada):
    depth, d, nd = w_ada.shape
    b = c.shape[0]
    tn = min(1024, d)
    return pl.pallas_call(
        _adaln_kernel,
        grid=(depth, nd // tn),
        in_specs=[
            pl.BlockSpec((b, d), lambda l, j: (0, 0)),
            pl.BlockSpec((None, d, tn), lambda l, j: (l, 0, j)),
            pl.BlockSpec((None, 1, tn), lambda l, j: (l, 0, j)),
        ],
        out_specs=pl.BlockSpec((None, b, tn), lambda l, j: (l, 0, j)),
        out_shape=jax.ShapeDtypeStruct((depth, b, nd), F32),
        compiler_params=_params("arbitrary", "arbitrary"),
        name="adaln_mod",
    )(c, w_ada, b_ada.reshape(depth, 1, nd))


def _norm_mod_kernel(x_ref, gain_ref, sc_ref, sh_ref, o_ref):
    x = x_ref[...]
    y = x * lax.rsqrt(jnp.mean(x * x, axis=-1, keepdims=True) + NORM_EPS)
    o_ref[...] = ((y * gain_ref[...]) * (1.0 + sc_ref[...]) + sh_ref[...]).astype(o_ref.dtype)


def norm_mod(x, gain, modl, i_shift, seq):
    n, d = x.shape
    tm = min(512, seq)
    return pl.pallas_call(
        _norm_mod_kernel,
        grid=(n // tm,),
        in_specs=[
            pl.BlockSpec((tm, d), lambda i: (i, 0)),
            pl.BlockSpec((1, d), lambda i: (0, 0)),
            pl.BlockSpec((None, 1, d), lambda i: ((i * tm) // seq * N_MOD + i_shift + 1, 0, 0)),
            pl.BlockSpec((None, 1, d), lambda i: ((i * tm) // seq * N_MOD + i_shift, 0, 0)),
        ],
        out_specs=pl.BlockSpec((tm, d), lambda i: (i, 0)),
        out_shape=jax.ShapeDtypeStruct((n, d), BF16),
        compiler_params=_params("arbitrary"),
        name="norm_mod",
    )(x, gain.reshape(1, d), modl, modl)


def _final_norm_kernel(x_ref, gain_ref, o_ref):
    x = x_ref[...]
    y = x * lax.rsqrt(jnp.mean(x * x, axis=-1, keepdims=True) + NORM_EPS)
    o_ref[...] = y * gain_ref[...]


def final_norm(x, gain):
    n, d = x.shape
    tm = min(512, n)
    return pl.pallas_call(
        _final_norm_kernel,
        grid=(n // tm,),
        in_specs=[pl.BlockSpec((tm, d), lambda i: (i, 0)), pl.BlockSpec((1, d), lambda i: (0, 0))],
        out_specs=pl.BlockSpec((tm, d), lambda i: (i, 0)),
        out_shape=jax.ShapeDtypeStruct((n, d), F32),
        compiler_params=_params("arbitrary"),
        name="final_norm",
    )(x, gain.reshape(1, d))


def _cast_on_first_row_step(pairs):
    @pl.when(pl.program_id(1) == 0)
    def _():
        for src_ref, dst_ref in pairs:
            dst_ref[...] = src_ref[...].astype(dst_ref.dtype)


def _ffn_up_kernel(h_ref, w1_ref, w3_ref, o_ref, w1b_ref, w3b_ref):
    _cast_on_first_row_step(((w1_ref, w1b_ref), (w3_ref, w3b_ref)))
    h = h_ref[...]
    a = _dot(h, w1b_ref[...])
    b = _dot(h, w3b_ref[...])
    o_ref[...] = ((a * jax.nn.sigmoid(a)) * b).astype(o_ref.dtype)


def ffn_up(h, w1, w3, layer):
    n, d = h.shape
    f = w1.shape[2]
    tm = min(1024, n)
    tn = 512 if f % 512 == 0 else f
    w_spec = pl.BlockSpec((None, d, tn), lambda j, i: (layer, 0, j))
    return pl.pallas_call(
        _ffn_up_kernel,
        grid=(f // tn, n // tm),
        in_specs=[pl.BlockSpec((tm, d), lambda j, i: (i, 0)), w_spec, w_spec],
        out_specs=pl.BlockSpec((tm, tn), lambda j, i: (i, j)),
        out_shape=jax.ShapeDtypeStruct((n, f), BF16),
        scratch_shapes=[pltpu.VMEM((d, tn), BF16), pltpu.VMEM((d, tn), BF16)],
        compiler_params=_params("arbitrary", "arbitrary"),
        name="ffn_up",
    )(h, w1, w3)


def _resid_mm_kernel(a_ref, w_ref, x_ref, g_ref, o_ref, wb_ref, *, gscale):
    _cast_on_first_row_step(((w_ref, wb_ref),))
    y = _dot(a_ref[...], wb_ref[...])
    o_ref[...] = x_ref[...] + (gscale * g_ref[...]) * y


def resid_mm(a, w, layer, x, modl, i_gate, gscale, seq):
    n, k = a.shape
    d = w.shape[2]
    tm = min(seq, max(512, 2 ** int(math.log2(12 * 2 ** 20 // (2 * k)))))
    tn = min(512, d)
    return pl.pallas_call(
        functools.partial(_resid_mm_kernel, gscale=gscale),
        grid=(d // tn, n // tm),
        in_specs=[
            pl.BlockSpec((tm, k), lambda j, i: (i, 0)),
            pl.BlockSpec((None, k, tn), lambda j, i: (layer, 0, j), pipeline_mode=pl.Buffered(1)),
            pl.BlockSpec((tm, tn), lambda j, i: (i, j)),
            pl.BlockSpec((None, 1, tn), lambda j, i: ((i * tm) // seq * N_MOD + i_gate, 0, j)),
        ],
        out_specs=pl.BlockSpec((tm, tn), lambda j, i: (i, j)),
        out_shape=jax.ShapeDtypeStruct((n, d), F32),
        scratch_shapes=[pltpu.VMEM((k, tn), BF16)],
        compiler_params=_params("arbitrary", "arbitrary"),
        name="resid_mm",
    )(a, w, x, modl)


def _widest_tile(n_chunks, chunk, bytes_per_unit, limit_bytes):
    return chunk * max(t for t in range(1, n_chunks + 1)
                       if n_chunks % t == 0 and (t == 1 or t * chunk * bytes_per_unit <= limit_bytes))


def _swap_halves(z, group):
    if group == LANES:
        return pltpu.roll(z, LANES // 2, axis=1)
    half = group // 2
    lane = lax.broadcasted_iota(jnp.int32, z.shape, 1)
    from_above = pltpu.roll(z, LANES - half, axis=1)
    from_below = pltpu.roll(z, half, axis=1)
    return jnp.where((lane & (group - 1)) < half, from_above, from_below)


def _proj_kernel(h_ref, w_ref, cos_ref, sin_ref, o_ref, wb_ref, *, group):
    _cast_on_first_row_step(((w_ref, wb_ref),))
    z = _nt_dot(h_ref[...], wb_ref[...])
    if group == 0:
        o_ref[...] = z.astype(o_ref.dtype)
        return
    cos = cos_ref[...]
    sin = sin_ref[...]
    for c in range(z.shape[1] // LANES):
        zc = z[:, c * LANES:(c + 1) * LANES]
        o_ref[:, c * LANES:(c + 1) * LANES] = (zc * cos + _swap_halves(zc, group) * sin).astype(o_ref.dtype)


def proj(h, wt, cos, sin, group, seq):
    n, d = h.shape
    cols = wt.shape[0]
    tm = min(1024, seq)
    tn = _widest_tile(cols // LANES, LANES, d * 4, 12 * 2 ** 20)
    s_blocks = seq // tm
    return pl.pallas_call(
        functools.partial(_proj_kernel, group=group),
        grid=(cols // tn, n // tm),
        in_specs=[
            pl.BlockSpec((tm, d), lambda j, i: (i, 0)),
            pl.BlockSpec((tn, d), lambda j, i: (j, 0)),
            pl.BlockSpec((tm, LANES), lambda j, i: (i % s_blocks, 0)),
            pl.BlockSpec((tm, LANES), lambda j, i: (i % s_blocks, 0)),
        ],
        out_specs=pl.BlockSpec((tm, tn), lambda j, i: (i, j)),
        out_shape=jax.ShapeDtypeStruct((n, cols), BF16),
        scratch_shapes=[pltpu.VMEM((tn, d), BF16)],
        compiler_params=_params("arbitrary", "arbitrary"),
        name=f"proj_rope{group}",
    )(h, wt, cos, sin)


def _proj_fm_kernel(w_ref, h_ref, cos_ref, sin_ref, o_ref, wt_ref, *, group, row_scales):
    _cast_on_first_row_step(((w_ref, wt_ref),))
    zt = _nt_dot(wt_ref[...], h_ref[...])
    step = group if group else LANES
    half = group // 2
    scale_of_row = [s for n_rows, s in row_scales for _ in range(n_rows // step)]
    if len(row_scales) == 1:
        scale_of_row = scale_of_row[:1] * (zt.shape[0] // step)
    for c in range(zt.shape[0] // step):
        blk = zt[c * step:(c + 1) * step]
        if group:
            swapped = jnp.concatenate([blk[half:], blk[:half]], axis=0)
            blk = blk * cos_ref[...] + swapped * sin_ref[...]
        if scale_of_row[c] != 1.0:
            blk = blk * scale_of_row[c]
        tile = o_ref.shape[2]
        for t in range(o_ref.shape[0]):
            o_ref[t, c * step:(c + 1) * step, :] = blk[:, t * tile:(t + 1) * tile].astype(o_ref.dtype)


def proj_fm(h, w, cos_fm, sin_fm, group, seq, row_scales=None):
    n, d = h.shape
    rows = w.shape[0]
    tile = min(TOKEN_TILE, seq)
    tm = min(2 * tile, seq)
    unit = max(group, LANES)
    row_scales = row_scales or ((rows, 1.0),)
    limit = 12 * 2 ** 20 if len(row_scales) == 1 else rows * d * 4
    tn = _widest_tile(rows // unit, unit, d * 4, limit)
    s_blocks = seq // tm
    g = max(group, 8)
    return pl.pallas_call(
        functools.partial(_proj_fm_kernel, group=group, row_scales=row_scales),
        grid=(rows // tn, n // tm),
        in_specs=[
            pl.BlockSpec((tn, d), lambda j, i: (j, 0), pipeline_mode=pl.Buffered(1 if tn == rows else 2)),
            pl.BlockSpec((tm, d), lambda j, i: (i, 0)),
            pl.BlockSpec((g, tm), lambda j, i: (0, i % s_blocks)),
            pl.BlockSpec((g, tm), lambda j, i: (0, i % s_blocks)),
        ],
        out_specs=pl.BlockSpec((tm // tile, tn, tile), lambda j, i: (i, j, 0)),
        out_shape=jax.ShapeDtypeStruct((n // tile, rows, tile), BF16),
        scratch_shapes=[pltpu.VMEM((tn, d), BF16)],
        compiler_params=_params("arbitrary", "arbitrary"),
        name=f"proj_fm_rope{group}",
    )(w, h, cos_fm, sin_fm)


def _small_proj_kernel(h_ref, wt_ref, bf_ref, logf_ref, wi_ref):
    zt = _nt_dot(wt_ref[...], h_ref[...])
    logf_ref[...] = jax.nn.log_sigmoid(zt[0:8, :] + bf_ref[...])
    wi_ref[...] = zt[8:16, :] * IDX_W_SCALE


def small_proj(h, w_small_t, b_forget8, seq):
    n, d = h.shape
    tm = min(512, seq)
    return pl.pallas_call(
        _small_proj_kernel,
        grid=(n // tm,),
        in_specs=[
            pl.BlockSpec((tm, d), lambda i: (i, 0)),
            pl.BlockSpec((16, d), lambda i: (0, 0)),
            pl.BlockSpec((8, 1), lambda i: (0, 0)),
        ],
        out_specs=[pl.BlockSpec((8, tm), lambda i: (0, i)), pl.BlockSpec((8, tm), lambda i: (0, i))],
        out_shape=[jax.ShapeDtypeStruct((8, n), F32), jax.ShapeDtypeStruct((8, n), F32)],
        compiler_params=_params("arbitrary"),
        name="small_proj",
    )(h, w_small_t, b_forget8)


def _split3(x):
    x1 = x.astype(BF16)
    r1 = x - x1.astype(F32)
    x2 = r1.astype(BF16)
    x3 = (r1 - x2.astype(F32)).astype(BF16)
    return x1, x2, x3


def _neg_cumsum_kernel(x_ref, hi_ref, mid_ref, lo_ref):
    x = x_ref[...]
    chunks = x.shape[0]
    r = lax.broadcasted_iota(jnp.int32, (LANES, LANES), 0)
    c = lax.broadcasted_iota(jnp.int32, (LANES, LANES), 1)
    upper = (r <= c).astype(BF16)
    within = sum(_dot(p, upper) for p in _split3(x))
    totals = jnp.broadcast_to(within[:, LANES - 1:LANES], (chunks, LANES))
    rr = lax.broadcasted_iota(jnp.int32, (chunks, chunks), 0)
    cc = lax.broadcasted_iota(jnp.int32, (chunks, chunks), 1)
    strict_lower = (cc < rr).astype(BF16)
    offset = sum(_dot(strict_lower, p) for p in _split3(totals))
    hi_ref[...], mid_ref[...], lo_ref[...] = _split3(-(within + offset) * LOG2E)


def neg_cumsum(logf_t, batch, seq):
    rows = logf_t.shape[0] * batch
    chunks = seq // LANES
    x = logf_t.reshape(rows, chunks, LANES)
    spec = pl.BlockSpec((None, chunks, LANES), lambda i: (i, 0, 0))
    pieces = pl.pallas_call(
        _neg_cumsum_kernel,
        grid=(rows,),
        in_specs=[spec],
        out_specs=[spec] * 3,
        out_shape=[jax.ShapeDtypeStruct((rows, chunks, LANES), BF16)] * 3,
        compiler_params=_params("arbitrary"),
        name="neg_cumsum",
    )(x)
    rows3 = jnp.stack([p.reshape(logf_t.shape[0], batch, seq) for p in pieces], axis=2)
    return jnp.pad(rows3, ((0, 0), (0, 0), (0, 8 - N_BIAS_PIECES), (0, 0)))


def _query_streams(tq):
    width = min(MXU_WIDTH, tq)
    return [slice(lo, lo + width) for lo in range(0, tq, width)]


ONES_ROWS = 16
ACC_ROWS = HEAD_DIM + ONES_ROWS


def _softmax_stats(s, m_prev):
    m_new = jnp.maximum(m_prev, jnp.max(s, axis=0, keepdims=True))
    return m_new, jnp.exp2(m_prev - m_new), jnp.exp2(s - m_new).astype(BF16)


def _with_ones_rows(vt):
    return jnp.concatenate([vt, jnp.ones((ONES_ROWS, vt.shape[1]), vt.dtype)], axis=0)


def _normalized(acc):
    return acc[0:HEAD_DIM] / acc[HEAD_DIM:HEAD_DIM + 1]


def _causal_mask(key0, query0, n_keys, n_queries):
    key = key0 + lax.broadcasted_iota(jnp.int32, (n_keys, n_queries), 0)
    query = query0 + lax.broadcasted_iota(jnp.int32, (n_keys, n_queries), 1)
    return key <= query


def _sweep_key_tiles(q_block, tq, tk, tile_fn):
    n_full = (q_block * tq) // tk

    def body(kj, carry):
        tile_fn(kj, False)
        return carry

    lax.fori_loop(0, n_full, body, 0)
    tile_fn(n_full, True)


def _head_sweep(nq, tq, tk, streams, raw_scores, logits, values, pipe, steps_per_trip):
    assert tq == tk
    ids = range(len(streams))
    slot_a, slot_b = pipe[:3], pipe[3:]

    for (m_ref, acc_ref), qs in streams:
        m_ref[:, :, qs] = jnp.full((nq, 1, qs.stop - qs.start), -jnp.inf, F32)
        acc_ref[:, :, qs] = jnp.zeros((nq, ACC_ROWS, qs.stop - qs.start), F32)

    def accumulate(qi, pv, rescale):
        for i in ids:
            (_, acc_ref), qs = streams[i]
            acc_ref[qi, :, qs] = rescale[i] * acc_ref[qi, :, qs] + pv[i]

    def products(kj, slot):
        vts = values(kj)
        return [_dot(vts[i], slot[1][i]) for i in ids]

    def step(cur, prev, nxt, cur_slot, nxt_slot, masked):
        raw_next = raw_scores(*nxt)
        if prev is not None:
            pv_prev = products(prev[0], nxt_slot)
            rescale_prev = [nxt_slot[2][i] for i in ids]
        for i in ids:
            nxt_slot[0][i] = raw_next[i]
        kj, qi = cur
        for i in ids:
            (m_ref, _), qs = streams[i]
            m_new, alpha, p = _softmax_stats(logits(kj, qi, i, cur_slot[0][i], masked), m_ref[qi, :, qs])
            m_ref[qi, :, qs] = m_new
            cur_slot[1][i] = p
            cur_slot[2][i] = alpha
        if prev is not None:
            accumulate(prev[1], pv_prev, rescale_prev)

    def run(n_pairs, start, advance, masked):
        if n_pairs == 0:
            return
        slots = (slot_a, slot_b)
        raw0 = raw_scores(*start)
        for i in ids:
            slot_a[0][i] = raw0[i]
        second = advance(*start)
        step(start, None, second, slot_a, slot_b, masked)

        def steps(count, cur, prev):
            for u in range(count):
                nxt = advance(*cur)
                step(cur, prev, nxt, slots[(u + 1) % 2], slots[u % 2], masked)
                prev, cur = cur, nxt
            return cur, prev

        def trip(t, carry):
            cur, prev = steps(steps_per_trip, carry[:2], carry[2:])
            return (*cur, *prev)

        carry = lax.fori_loop(0, (n_pairs - 1) // steps_per_trip, trip, (*second, *start))
        _, prev = steps((n_pairs - 1) % steps_per_trip, carry[:2], carry[2:])
        last_slot = slots[(n_pairs - 1) % 2]
        accumulate(prev[1], products(prev[0], last_slot), [last_slot[2][i] for i in ids])

    last = nq - 1
    zero = jnp.int32(0)
    run(nq, (zero, zero), lambda kj, qi: (jnp.minimum(kj + 1, last), jnp.minimum(qi + 1, last)), True)

    def next_below_diagonal(kj, qi):
        wrap = qi == last
        dist = qi - kj
        return (jnp.where(wrap, 0, kj + 1), jnp.where(wrap, jnp.minimum(dist + 1, last), qi + 1))

    run(nq * (nq - 1) // 2, (zero, jnp.int32(min(1, last))), next_below_diagonal, False)


def _head_softmax_scratch(nq, tq):
    return [pltpu.VMEM((nq, 1, tq), F32), pltpu.VMEM((nq, ACC_ROWS, tq), F32)]


def _pipe_scratch(n_streams, tk, width):
    slot = lambda: [pltpu.VMEM((n_streams, tk, width), F32), pltpu.VMEM((n_streams, tk, width), BF16),
                    pltpu.VMEM((n_streams, 1, width), F32)]
    return slot() + slot()


N_BIAS_PIECES = 3


def _fox_kernel(qt_ref, k_ref, bias_ref, vt_ref, o_ref, m_ref, acc_ref, qa_ref, kb_ref, *pipe, nq, tq, tk):
    slices = _query_streams(tq)
    row = lax.broadcasted_iota(jnp.int32, (HEAD_DIM, tq), 0)
    ones_rows = jnp.where(row < N_BIAS_PIECES, 1.0, 0.0).astype(BF16)
    zeros = jnp.zeros((HEAD_DIM - 8, tk), F32)
    for j in range(nq):
        qa_ref[j] = jnp.concatenate([qt_ref[j], ones_rows], axis=0)
        rows = bias_ref[:, j * tk:(j + 1) * tk].astype(F32)
        kb_ref[j] = jnp.concatenate([rows, zeros], axis=0).T.astype(BF16)

    def raw_scores(kj, qi):
        k = jnp.concatenate([k_ref[pl.ds(pl.multiple_of(kj * tk, tk), tk), :], kb_ref[kj]], axis=1)
        return [_dot(k, qa_ref[qi, :, qs]) for qs in slices]

    def logits(kj, qi, i, raw, masked):
        qs = slices[i]
        if masked:
            raw = jnp.where(_causal_mask(kj * tk, qi * tq + qs.start, tk, qs.stop - qs.start), raw, NEG_INF)
        return raw

    def values(kj):
        return [_with_ones_rows(vt_ref[kj])] * len(slices)

    _head_sweep(nq, tq, tk, [((m_ref, acc_ref), qs) for qs in slices], raw_scores, logits, values, pipe, 4)
    for j in range(nq):
        o_ref[j * tq:(j + 1) * tq, :] = _normalized(acc_ref[j]).T.astype(o_ref.dtype)


def fox_attention(q_fm, q_blk0, k_rows, bias_rows, v_fm, v_blk0, seq):
    b = k_rows.shape[0]
    tq = tk = q_fm.shape[2]
    nq = seq // tq
    slices = _query_streams(tq)
    return pl.pallas_call(
        functools.partial(_fox_kernel, nq=nq, tq=tq, tk=tk),
        grid=(b, H_FOX),
        in_specs=[
            pl.BlockSpec((nq, HEAD_DIM, tq), lambda bi, h: (bi, q_blk0 + h, 0)),
            pl.BlockSpec((None, seq, HEAD_DIM), lambda bi, h: (bi, 0, h)),
            pl.BlockSpec((None, None, 8, seq), lambda bi, h: (h, bi, 0, 0)),
            pl.BlockSpec((nq, HEAD_DIM, tk), lambda bi, h: (bi, v_blk0 + h, 0)),
        ],
        out_specs=pl.BlockSpec((None, seq, HEAD_DIM), lambda bi, h: (bi, 0, h)),
        out_shape=jax.ShapeDtypeStruct((b, seq, W_FOX), BF16),
        scratch_shapes=_head_softmax_scratch(nq, tq)
        + [pltpu.VMEM((nq, 2 * HEAD_DIM, tq), BF16), pltpu.VMEM((nq, tk, HEAD_DIM), BF16)]
        + _pipe_scratch(len(slices), tk, slices[0].stop),
        compiler_params=_params("arbitrary", "arbitrary"),
        name="fox_attention",
    )(q_fm, k_rows, bias_rows, v_fm)


def _diff_kernel(lam_ref, gain_ref, qt_ref, k_ref, vt_ref, o_ref,
                 m1_ref, acc1_ref, m2_ref, acc2_ref, q1_ref, q2_ref, *pipe, nq, tq, tk, lam_init):
    zeros = jnp.zeros((DK_DIFF, tq), BF16)
    for j in range(nq):
        q1_ref[j] = jnp.concatenate([qt_ref[j, 0:DK_DIFF, :], zeros], axis=0)
        q2_ref[j] = jnp.concatenate([zeros, qt_ref[j, DK_DIFF:HEAD_DIM, :]], axis=0)
    streams, q_refs = [], []
    for qs in _query_streams(tq):
        for refs, q_ref in (((m1_ref, acc1_ref), q1_ref), ((m2_ref, acc2_ref), q2_ref)):
            streams.append((refs, qs))
            q_refs.append(q_ref)

    def raw_scores(kj, qi):
        k = k_ref[pl.ds(pl.multiple_of(kj * tk, tk), tk), :]
        return [_dot(k, q_ref[qi, :, qs]) for q_ref, (_, qs) in zip(q_refs, streams)]

    def logits(kj, qi, i, raw, masked):
        qs = streams[i][1]
        if masked:
            raw = jnp.where(_causal_mask(kj * tk, qi * tq + qs.start, tk, qs.stop - qs.start), raw, NEG_INF)
        return raw

    def values(kj):
        return [_with_ones_rows(vt_ref[kj])] * len(streams)

    _head_sweep(nq, tq, tk, streams, raw_scores, logits, values, pipe, 2)

    lam_vecs = lam_ref[...]
    dot1 = jnp.sum(lam_vecs[0:1] * lam_vecs[1:2], axis=1, keepdims=True)
    dot2 = jnp.sum(lam_vecs[2:3] * lam_vecs[3:4], axis=1, keepdims=True)
    lam = jnp.exp(dot1) - jnp.exp(dot2) + lam_init
    for j in range(nq):
        o = (_normalized(acc1_ref[j]) - lam * _normalized(acc2_ref[j])).T
        y = o * lax.rsqrt(jnp.mean(o * o, axis=-1, keepdims=True) + NORM_EPS)
        o_ref[j * tq:(j + 1) * tq, :] = ((y * gain_ref[...]) * (1.0 - lam_init)).astype(o_ref.dtype)


def diff_attention(q_fm, q_blk0, k_rows, v_fm, v_blk0, lam_vecs, subln_gain, lam_init, seq):
    b = k_rows.shape[0]
    tq = tk = q_fm.shape[2]
    nq = seq // tq
    slices = _query_streams(tq)
    return pl.pallas_call(
        functools.partial(_diff_kernel, nq=nq, tq=tq, tk=tk, lam_init=lam_init),
        grid=(b, H_DIFF),
        in_specs=[
            pl.BlockSpec((4, DK_DIFF), lambda bi, h: (0, 0)),
            pl.BlockSpec((1, HEAD_DIM), lambda bi, h: (0, 0)),
            pl.BlockSpec((nq, HEAD_DIM, tq), lambda bi, h: (bi, q_blk0 + h, 0)),
            pl.BlockSpec((None, seq, HEAD_DIM), lambda bi, h: (bi, 0, h)),
            pl.BlockSpec((nq, HEAD_DIM, tk), lambda bi, h: (bi, v_blk0 + h, 0)),
        ],
        out_specs=pl.BlockSpec((None, seq, HEAD_DIM), lambda bi, h: (bi, 0, h)),
        out_shape=jax.ShapeDtypeStruct((b, seq, W_DIFF), BF16),
        scratch_shapes=_head_softmax_scratch(nq, tq) + _head_softmax_scratch(nq, tq)
        + [pltpu.VMEM((nq, HEAD_DIM, tq), BF16), pltpu.VMEM((nq, HEAD_DIM, tq), BF16)]
        + _pipe_scratch(2 * len(slices), tk, slices[0].stop),
        compiler_params=_params("arbitrary", "arbitrary"),
        name="diff_attention",
    )(lam_vecs, subln_gain.reshape(1, HEAD_DIM), q_fm, k_rows, v_fm)


def _dsa_kernel(qt_ref, k_ref, vt_ref, qit_ref, ki_ref, wt_ref, o_ref,
                hi_ref, lo_ref, qm_ref, cut_ref, m_ref, acc_ref, *pipe,
                tq, tk, seq, n_sel):
    qb = pl.program_id(1)
    n_tiles = (qb * tq) // tk + 1
    slices = _query_streams(tq)

    zeros = jnp.zeros((D_IDX, tq), BF16)
    for h in range(H_IDX):
        qm_ref[h] = jnp.concatenate([qit_ref[h * D_IDX:(h + 1) * D_IDX, :], zeros], axis=0)

    def score_tile(kj, masked):
        kk = ki_ref[pl.ds(pl.multiple_of(kj * tk, tk), tk), :]
        for qs in slices:
            width = qs.stop - qs.start
            rel_q = [_dot(kk, qm_ref[h, :, qs]) for h in range(H_IDX)]
            score = jnp.zeros((tk, width), F32)
            for h in range(H_IDX):
                score = score + wt_ref[h:h + 1, qs] * jnp.maximum(rel_q[h], 0.0)
            if masked:
                score = jnp.where(_causal_mask(kj * tk, qb * tq + qs.start, tk, width), score, NEG_INF)
            bits = lax.bitcast_convert_type(score, jnp.int32)
            key = bits ^ ((bits >> 31) & 0x7FFFFFFF)
            hi_ref[kj, :, qs] = (key >> 16).astype(jnp.int16)
            lo_ref[kj, :, qs] = ((key & 0xFFFF) + I16_MIN).astype(jnp.int16)

    _sweep_key_tiles(qb, tq, tk, score_tile)

    n_beyond = (seq - n_tiles * tk).astype(F32)
    neg_hi, neg_lo = KEY_NEG_INF >> 16, (KEY_NEG_INF & 0xFFFF) + I16_MIN
    one, zero = jnp.int16(1), jnp.int16(0)
    as16 = lambda v: v.astype(jnp.int16)

    def count_hits(hit_fn):
        def body(kj, total):
            hit = jnp.where(hit_fn(kj), one, zero)
            part = hit[0:16]
            for r in range(1, tk // 16):
                part = part + hit[r * 16:(r + 1) * 16]
            return total + jnp.sum(part.astype(F32), axis=0, keepdims=True)
        return lax.fori_loop(0, n_tiles, body, jnp.zeros((1, tq), F32))

    def bisect16(enough):
        def step(s, t):
            cand = t + jnp.left_shift(jnp.int32(1), 15 - s)
            return jnp.where(enough(cand), cand, t)
        return lax.fori_loop(0, 16, step, jnp.full((1, tq), I16_MIN, jnp.int32))

    def count_hi_ge(cand):
        n = count_hits(lambda kj, c=as16(cand): hi_ref[kj] >= c)
        return n + jnp.where(cand <= neg_hi, n_beyond, 0.0)

    t_hi = bisect16(lambda cand: count_hi_ge(cand) >= n_sel)
    t_hi16 = as16(t_hi)
    n_above = jnp.where(t_hi < I16_MAX, count_hi_ge(jnp.minimum(t_hi + 1, I16_MAX)), 0.0)
    need_lo = n_sel - n_above

    def mask_lo(kj, carry):
        lo_ref[kj] = jnp.where(hi_ref[kj] == t_hi16, lo_ref[kj], jnp.int16(I16_MIN))
        return carry

    lax.fori_loop(0, n_tiles, mask_lo, 0)
    beyond_in_group = jnp.where(t_hi == neg_hi, n_beyond, 0.0)

    def count_lo_ge(cand):
        n = count_hits(lambda kj, c=as16(cand): lo_ref[kj] >= c)
        return n + jnp.where(cand <= neg_lo, beyond_in_group, 0.0)

    t_lo = bisect16(lambda cand: count_lo_ge(cand) >= need_lo)
    t_lo16 = as16(t_lo)

    def in_group(kj):
        return hi_ref[kj] == t_hi16

    n_ge = n_above + count_hits(lambda kj: in_group(kj) & (lo_ref[kj] >= t_lo16)) \
        + jnp.where(t_lo <= neg_lo, beyond_in_group, 0.0)
    surplus = n_ge - n_sel
    cut_ref[...] = jnp.full((1, tq), I16_MAX, jnp.int32)

    def key_index(kj):
        return as16(kj * tk + lax.broadcasted_iota(jnp.int32, (tk, tq), 0))

    @pl.when(jnp.max(surplus) > 0.0)
    def _():
        n_gt = n_above + count_hits(lambda kj: in_group(kj) & (lo_ref[kj] > t_lo16)) \
            + jnp.where(t_lo < neg_lo, beyond_in_group, 0.0)
        need = jnp.where(surplus > 0.0, n_sel - n_gt, float(seq + 1))

        def count_tied_below(cut):
            c = as16(cut)
            return count_hits(lambda kj: in_group(kj) & (lo_ref[kj] == t_lo16) & (key_index(kj) < c))

        n_bits = max(1, (seq - 1).bit_length())

        def bisect_cut(step, cut):
            cand = cut + jnp.left_shift(jnp.int32(1), n_bits - 1 - step)
            return jnp.where(count_tied_below(cand) < need, cand, cut)

        cut_ref[...] = lax.fori_loop(0, n_bits, bisect_cut, jnp.zeros((1, tq), jnp.int32))

    cut16 = as16(cut_ref[...])
    zero_bias, neg_bias = jnp.zeros((), BF16), jnp.asarray(NEG_INF, BF16)

    def write_bias(kj, carry):
        hi, lo = hi_ref[kj], lo_ref[kj]
        tied_ok = (lo == t_lo16) & (key_index(kj) <= cut16)
        sel = (hi > t_hi16) | ((hi == t_hi16) & ((lo > t_lo16) | tied_ok))
        hi_ref[kj] = lax.bitcast_convert_type(jnp.where(sel, zero_bias, neg_bias), jnp.int16)
        return carry

    lax.fori_loop(0, n_tiles, write_bias, 0)

    n_full = n_tiles - 1
    ids = range(len(slices))
    slots = (pipe[:3], pipe[3:])
    head_rows = lambda h: slice(h * HEAD_DIM, (h + 1) * HEAD_DIM)
    m_ref[...] = jnp.full(m_ref.shape, -jnp.inf, F32)
    acc_ref[...] = jnp.zeros(acc_ref.shape, F32)

    def raw_scores(kj, h):
        k = k_ref[pl.ds(pl.multiple_of(kj * tk, tk), tk), head_rows(h)]
        return [_dot(k, qt_ref[head_rows(h), qs]) for qs in slices]

    def products(kj, h, slot):
        vt = _with_ones_rows(vt_ref[kj, head_rows(h), :])
        return [_dot(vt, slot[1][i]) for i in ids]

    def accumulate(h, pv, rescale):
        for i, qs in enumerate(slices):
            acc_ref[h, :, qs] = rescale[i] * acc_ref[h, :, qs] + pv[i]

    def step(kj, h, first=False, last=False, masked=False):
        cur_slot, nxt_slot = slots[h % 2], slots[(h + 1) % 2]
        nxt = (kj, h + 1) if h + 1 < H_DSA else (kj + 1, 0)
        prev = (kj, h - 1) if h > 0 else (kj - 1, H_DSA - 1)
        raw_next = None if last else raw_scores(*nxt)
        if not first:
            pv_prev = products(*prev, nxt_slot)
            rescale_prev = [nxt_slot[2][i] for i in ids]
        if not last:
            for i in ids:
                nxt_slot[0][i] = raw_next[i]
        for i, qs in enumerate(slices):
            s = cur_slot[0][i] + lax.bitcast_convert_type(hi_ref[kj, :, qs], BF16).astype(F32)
            if masked:
                s = jnp.where(_causal_mask(kj * tk, qb * tq + qs.start, tk, qs.stop - qs.start), s, NEG_INF)
            m_new, alpha, p = _softmax_stats(s, m_ref[h, :, qs])
            m_ref[h, :, qs] = m_new
            cur_slot[1][i] = p
            cur_slot[2][i] = alpha
        if not first:
            accumulate(prev[1], pv_prev, rescale_prev)

    def tile_steps(kj, first=False, diagonal=False):
        for h in range(H_DSA):
            step(kj, h, first=first and h == 0, last=diagonal and h == H_DSA - 1, masked=diagonal)

    raw0 = raw_scores(0, 0)
    for i in ids:
        slots[0][0][i] = raw0[i]

    @pl.when(n_full == 0)
    def _():
        tile_steps(0, first=True, diagonal=True)

    @pl.when(n_full > 0)
    def _():
        tile_steps(0, first=True)

        def body(kj, carry):
            tile_steps(kj)
            return carry

        lax.fori_loop(1, n_full, body, 0)
        tile_steps(n_full, diagonal=True)

    last_slot = slots[(H_DSA - 1) % 2]
    accumulate(H_DSA - 1, products(n_full, H_DSA - 1, last_slot), [last_slot[2][i] for i in ids])
    for h in range(H_DSA):
        o_ref[:, head_rows(h)] = _normalized(acc_ref[h]).T.astype(o_ref.dtype)


def dsa_attention(q_fm, k_rows, v_fm, qi_fm, ki_rows, ki_blk, wi_t, seq):
    b = k_rows.shape[0]
    tq = tk = q_fm.shape[2]
    nq = seq // tq
    n_sel = min(TOPK_MAX, seq // 4)
    once = pl.Buffered(1)
    slices = _query_streams(tq)
    fm_q = lambda rows: pl.BlockSpec((None, rows, tq), lambda bi, i: (bi * nq + i, 0, 0))
    return pl.pallas_call(
        functools.partial(_dsa_kernel, tq=tq, tk=tk, seq=seq, n_sel=n_sel),
        grid=(b, nq),
        in_specs=[
            fm_q(W_DSA),
            pl.BlockSpec((None, seq, W_DSA), lambda bi, i: (bi, 0, 0), pipeline_mode=once),
            pl.BlockSpec((nq, W_DSA, tk), lambda bi, i: (bi, 0, 0), pipeline_mode=once),
            fm_q(W_IDX),
            pl.BlockSpec((None, seq, LANES), lambda bi, i: (bi, 0, ki_blk), pipeline_mode=once),
            pl.BlockSpec((H_IDX, tq), lambda bi, i: (0, bi * nq + i)),
        ],
        out_specs=pl.BlockSpec((None, tq, W_DSA), lambda bi, i: (bi, i, 0)),
        out_shape=jax.ShapeDtypeStruct((b, seq, W_DSA), BF16),
        scratch_shapes=[
            pltpu.VMEM((nq, tk, tq), jnp.int16),
            pltpu.VMEM((nq, tk, tq), jnp.int16),
            pltpu.VMEM((H_IDX, 2 * D_IDX, tq), BF16),
            pltpu.VMEM((1, tq), jnp.int32),
        ] + _head_softmax_scratch(H_DSA, tq) + _pipe_scratch(len(slices), tk, slices[0].stop),
        compiler_params=_params("arbitrary", "arbitrary"),
        name="dsa_attention",
    )(q_fm, k_rows, v_fm, qi_fm, ki_rows, wi_t)


def _merge_kernel(h_ref, oa_ref, ob_ref, oc_ref, mw0_ref, mw1_ref, mw2_ref, mb_ref, wa_ref, wb_ref, wc_ref, o_ref,
                  *bf16_refs):
    f32_refs = (mw0_ref, mw1_ref, mw2_ref, wa_ref, wb_ref, wc_ref)
    _cast_on_first_row_step(tuple(zip(f32_refs, bf16_refs)))
    gate_w, branch_w = bf16_refs[:N_BRANCH], bf16_refs[N_BRANCH:]
    h = h_ref[...]
    merged = None
    for i, o_b_ref in enumerate((oa_ref, ob_ref, oc_ref)):
        gate = jax.nn.sigmoid(_dot(h, gate_w[i][...]) + mb_ref[i])
        term = gate * _dot(o_b_ref[...], branch_w[i][...])
        merged = term if merged is None else merged + term
    o_ref[...] = merged.astype(o_ref.dtype)


def merge_branches(h, oa, ob, oc, merge_w, merge_b, wa, wb, wc, layer, seq):
    n, d = h.shape
    tm = min(1024, seq)
    tn = min(256, d)
    nj = d // tn
    row = lambda width: pl.BlockSpec((tm, width), lambda j, i: (i, 0))
    col = lambda rows: pl.BlockSpec((None, rows, tn), lambda j, i: (layer, 0, j))
    gate_w = lambda g: pl.BlockSpec((None, d, tn), lambda j, i: (layer, 0, g * nj + j))
    widths = (oa.shape[1], ob.shape[1], oc.shape[1])
    return pl.pallas_call(
        _merge_kernel,
        grid=(nj, n // tm),
        in_specs=[
            row(d), row(widths[0]), row(widths[1]), row(widths[2]),
            gate_w(0), gate_w(1), gate_w(2),
            pl.BlockSpec((None, N_BRANCH, 1, tn), lambda j, i: (layer, 0, 0, j)),
            col(widths[0]), col(widths[1]), col(widths[2]),
        ],
        out_specs=pl.BlockSpec((tm, tn), lambda j, i: (i, j)),
        out_shape=jax.ShapeDtypeStruct((n, d), BF16),
        scratch_shapes=[pltpu.VMEM((d, tn), BF16)] * N_BRANCH + [pltpu.VMEM((w, tn), BF16) for w in widths],
        compiler_params=_params("arbitrary", "arbitrary"),
        name="merge_branches",
    )(h, oa, ob, oc, merge_w, merge_w, merge_w, merge_b.reshape(merge_b.shape[0], N_BRANCH, 1, d), wa, wb, wc)


def _rope_tables(seq, dim):
    inv_freq = 1.0 / (ROPE_THETA ** (jnp.arange(0, dim, 2, dtype=F32) / dim))
    ang = jnp.arange(seq, dtype=F32)[:, None] * inv_freq[None, :]
    cos, sin = lax.optimization_barrier((jnp.cos(ang), jnp.sin(ang)))
    cos_g = jnp.concatenate([cos, cos], axis=1)
    sin_g = jnp.concatenate([-sin, sin], axis=1)
    reps = LANES // dim
    return jnp.tile(cos_g, (1, reps)), jnp.tile(sin_g, (1, reps)), cos_g.T, sin_g.T


def _split_w_in(w_in_t):
    sizes = (W_DIFF, W_DIFF, W_DIFF, W_FOX, W_FOX, W_FOX, H_FOX, W_DSA, W_DSA, W_DSA, W_IDX, D_IDX, H_IDX)
    parts, start = [], 0
    for n in sizes:
        parts.append(w_in_t[start:start + n])
        start += n
    return parts


def kernel(x, c, w_ada, b_ada, norm_ffn1, ffn1_w1, ffn1_w3, ffn1_w2, norm_mix, w_in, b_forget, lam_q1, lam_k1, lam_q2, lam_k2, subln_gain, merge_w, merge_b, w_branch_a, w_branch_b, w_branch_c, w_out, norm_ffn2, ffn2_w1, ffn2_w3, ffn2_w2, norm_final):
    batch, seq, d = x.shape
    depth = w_ada.shape[0]
    n = batch * seq
    tk = min(TOKEN_TILE, seq)
    bf = lambda a: a.astype(BF16)

    cos64, sin64, cos64_fm, sin64_fm = _rope_tables(seq, DK_DIFF)
    cos128, sin128, cos128_fm, sin128_fm = _rope_tables(seq, HEAD_DIM)
    mod = adaln_mod(c, w_ada, b_ada)
    xf = x.reshape(n, d)

    for l in range(depth):
        lam_init = 0.8 - 0.6 * math.exp(-0.3 * l)
        modl = mod[l].reshape(batch * N_MOD, 1, d)

        h = norm_mod(xf, norm_ffn1[l], modl, 0, seq)
        u = ffn_up(h, ffn1_w1, ffn1_w3, l)
        xf = resid_mm(u, ffn1_w2, l, xf, modl, 2, 0.5, seq)

        h = norm_mod(xf, norm_mix[l], modl, 3, seq)
        qa, ka, va, qb, kb, vb, fb, qc, kc, vc, qi, ki, wi = _split_w_in(jnp.transpose(w_in[l]))
        k64 = proj(h, jnp.concatenate([ka, ki, ki], axis=0), cos64, sin64, DK_DIFF, seq).reshape(batch, seq, -1)
        k128 = proj(h, kc, cos128, sin128, HEAD_DIM, seq).reshape(batch, seq, -1)
        k0 = proj(h, kb, cos64, sin64, 0, seq).reshape(batch, seq, -1)
        log2e_over_sqrt = lambda width: (width ** -0.5) * LOG2E
        q64_fm = proj_fm(h, jnp.concatenate([qi, qa], axis=0), cos64_fm, sin64_fm, DK_DIFF, seq,
                         ((W_IDX, 1.0), (W_DIFF, log2e_over_sqrt(DK_DIFF))))
        q128_fm = proj_fm(h, qc, cos128_fm, sin128_fm, HEAD_DIM, seq, ((W_DSA, log2e_over_sqrt(HEAD_DIM)),))
        qb_fm = proj_fm(h, qb, cos64_fm, sin64_fm, 0, seq, ((W_FOX, log2e_over_sqrt(HEAD_DIM)),))
        v_fm = proj_fm(h, jnp.concatenate([vc, va, vb], axis=0), cos64_fm, sin64_fm, 0, seq)
        blk = lambda rows: rows // HEAD_DIM
        qa_blk, va_blk, vb_blk = blk(W_IDX), blk(W_DSA), blk(W_DSA + W_DIFF)

        w_small_t = bf(jnp.concatenate([fb, jnp.zeros((8 - H_FOX, d), F32), wi], axis=0))
        b_forget8 = jnp.concatenate([b_forget[l], jnp.zeros((8 - H_FOX,), F32)]).reshape(8, 1)
        logf_t, wi_t = small_proj(h, w_small_t, b_forget8, seq)
        bias_rows = neg_cumsum(logf_t, batch, seq)

        lam_vecs = jnp.stack([lam_q1[l], lam_k1[l], lam_q2[l], lam_k2[l]])
        oa = diff_attention(q64_fm, qa_blk, k64, v_fm, va_blk, lam_vecs, subln_gain[l], lam_init, seq)
        ob = fox_attention(qb_fm, 0, k0, bias_rows, v_fm, vb_blk, seq)
        oc = dsa_attention(q128_fm, k128, v_fm, q64_fm, k64, blk(W_DIFF), wi_t, seq)

        merged = merge_branches(h, oa.reshape(n, -1), ob.reshape(n, -1), oc.reshape(n, -1), merge_w, merge_b,
                                w_branch_a, w_branch_b, w_branch_c, l, seq)
        xf = resid_mm(merged, w_out, l, xf, modl, 5, 1.0, seq)

        h = norm_mod(xf, norm_ffn2[l], modl, 6, seq)
        u = ffn_up(h, ffn2_w1, ffn2_w3, l)
        xf = resid_mm(u, ffn2_w2, l, xf, modl, 8, 0.5, seq)

    return final_norm(xf, norm_final).reshape(batch, seq, d)
```

```python
import functools
import math

import numpy as np
import jax
import jax.numpy as jnp
from jax import lax
from jax.experimental import pallas as pl
from jax.experimental.pallas import tpu as pltpu

HEAD_DIM = 128
H_DIFF = 6
DK_DIFF = HEAD_DIM // 2
H_FOX = 6
H_DSA = 4
H_IDX = 8
D_IDX = 64
TOPK_MAX = 256
ROPE_THETA = 10000.0
NORM_EPS = 1e-6
N_BRANCH = 3
N_MOD = 9
NEG_INF = -1e30
IDX_W_SCALE = (H_IDX ** -0.5) * (D_IDX ** -0.5)
LOG2E = math.log2(math.e)

W_DIFF = H_DIFF * HEAD_DIM
W_FOX = H_FOX * HEAD_DIM
W_DSA = H_DSA * HEAD_DIM
W_IDX = H_IDX * D_IDX

LANES = 128
MXU_WIDTH = 256
VMEM_LIMIT = 56 * 1024 * 1024
TOKEN_TILE = 512
I16_MIN, I16_MAX = -(2 ** 15), 2 ** 15 - 1

BF16 = jnp.bfloat16
F32 = jnp.float32


def _order_key_of(value):
    bits = int(np.array(value, np.float32).view(np.int32))
    return bits ^ ((bits >> 31) & 0x7FFFFFFF)


KEY_NEG_INF = _order_key_of(NEG_INF)


def _params(*semantics):
    return pltpu.CompilerParams(dimension_semantics=semantics, vmem_limit_bytes=VMEM_LIMIT)


def _nt_dot(a, b):
    return lax.dot_general(a, b, (((1,), (1,)), ((), ())), preferred_element_type=F32)


def _dot(a, b):
    return jnp.dot(a, b, preferred_element_type=F32)


def _adaln_kernel(c_ref, w_ref, b_ref, o_ref):
    c = c_ref[...]
    c_act = (c * jax.nn.sigmoid(c)).astype(BF16)
    o_ref[...] = _dot(c_act, w_ref[...].astype(BF16)) + b_ref[...]


def adaln_mod(c, w_ada, b_ada):
    depth, d, nd = w_ada.shape
    b = c.shape[0]
    tn = min(1024, d)
    return pl.pallas_call(
        _adaln_kernel,
        grid=(depth, nd // tn),
        in_specs=[
            pl.BlockSpec((b, d), lambda l, j: (0, 0)),
            pl.BlockSpec((None, d, tn), lambda l, j: (l, 0, j)),
            pl.BlockSpec((None, 1, tn), lambda l, j: (l, 0, j)),
        ],
        out_specs=pl.BlockSpec((None, b, tn), lambda l, j: (l, 0, j)),
        out_shape=jax.ShapeDtypeStruct((depth, b, nd), F32),
        compiler_params=_params("arbitrary", "arbitrary"),
        name="adaln_mod",
    )(c, w_ada, b_ada.reshape(depth, 1, nd))


def _norm_mod_kernel(x_ref, gain_ref, sc_ref, sh_ref, o_ref):
    x = x_ref[...]
    y = x * lax.rsqrt(jnp.mean(x * x, axis=-1, keepdims=True) + NORM_EPS)
    o_ref[...] = ((y * gain_ref[...]) * (1.0 + sc_ref[...]) + sh_ref[...]).astype(o_ref.dtype)


def norm_mod(x, gain, modl, i_shift, seq):
    n, d = x.shape
    tm = min(512, seq)
    return pl.pallas_call(
        _norm_mod_kernel,
        grid=(n // tm,),
        in_specs=[
            pl.BlockSpec((tm, d), lambda i: (i, 0)),
            pl.BlockSpec((1, d), lambda i: (0, 0)),
            pl.BlockSpec((None, 1, d), lambda i: ((i * tm) // seq * N_MOD + i_shift + 1, 0, 0)),
            pl.BlockSpec((None, 1, d), lambda i: ((i * tm) // seq * N_MOD + i_shift, 0, 0)),
        ],
        out_specs=pl.BlockSpec((tm, d), lambda i: (i, 0)),
        out_shape=jax.ShapeDtypeStruct((n, d), BF16),
        compiler_params=_params("arbitrary"),
        name="norm_mod",
    )(x, gain.reshape(1, d), modl, modl)


def _final_norm_kernel(x_ref, gain_ref, o_ref):
    x = x_ref[...]
    y = x * lax.rsqrt(jnp.mean(x * x, axis=-1, keepdims=True) + NORM_EPS)
    o_ref[...] = y * gain_ref[...]


def final_norm(x, gain):
    n, d = x.shape
    tm = min(512, n)
    return pl.pallas_call(
        _final_norm_kernel,
        grid=(n // tm,),
        in_specs=[pl.BlockSpec((tm, d), lambda i: (i, 0)), pl.BlockSpec((1, d), lambda i: (0, 0))],
        out_specs=pl.BlockSpec((tm, d), lambda i: (i, 0)),
        out_shape=jax.ShapeDtypeStruct((n, d), F32),
        compiler_params=_params("arbitrary"),
        name="final_norm",
    )(x, gain.reshape(1, d))


def _cast_on_first_row_step(pairs):
    @pl.when(pl.program_id(1) == 0)
    def _():
        for src_ref, dst_ref in pairs:
            dst_ref[...] = src_ref[...].astype(dst_ref.dtype)


def _ffn_up_kernel(h_ref, w1_ref, w3_ref, o_ref, w1b_ref, w3b_ref):
    _cast_on_first_row_step(((w1_ref, w1b_ref), (w3_ref, w3b_ref)))
    h = h_ref[...]
    a = _dot(h, w1b_ref[...])
    b = _dot(h, w3b_ref[...])
    o_ref[...] = ((a * jax.nn.sigmoid(a)) * b).astype(o_ref.dtype)


def ffn_up(h, w1, w3, layer):
    n, d = h.shape
    f = w1.shape[2]
    tm = min(1024, n)
    tn = 512 if f % 512 == 0 else f
    w_spec = pl.BlockSpec((None, d, tn), lambda j, i: (layer, 0, j))
    return pl.pallas_call(
        _ffn_up_kernel,
        grid=(f // tn, n // tm),
        in_specs=[pl.BlockSpec((tm, d), lambda j, i: (i, 0)), w_spec, w_spec],
        out_specs=pl.BlockSpec((tm, tn), lambda j, i: (i, j)),
        out_shape=jax.ShapeDtypeStruct((n, f), BF16),
        scratch_shapes=[pltpu.VMEM((d, tn), BF16), pltpu.VMEM((d, tn), BF16)],
        compiler_params=_params("arbitrary", "arbitrary"),
        name="ffn_up",
    )(h, w1, w3)


def _resid_mm_kernel(a_ref, w_ref, x_ref, g_ref, o_ref, wb_ref, *, gscale):
    _cast_on_first_row_step(((w_ref, wb_ref),))
    y = _dot(a_ref[...], wb_ref[...])
    o_ref[...] = x_ref[...] + (gscale * g_ref[...]) * y


def resid_mm(a, w, layer, x, modl, i_gate, gscale, seq):
    n, k = a.shape
    d = w.shape[2]
    tm = min(seq, max(512, 2 ** int(math.log2(12 * 2 ** 20 // (2 * k)))))
    tn = min(512, d)
    return pl.pallas_call(
        functools.partial(_resid_mm_kernel, gscale=gscale),
        grid=(d // tn, n // tm),
        in_specs=[
            pl.BlockSpec((tm, k), lambda j, i: (i, 0)),
            pl.BlockSpec((None, k, tn), lambda j, i: (layer, 0, j), pipeline_mode=pl.Buffered(1)),
            pl.BlockSpec((tm, tn), lambda j, i: (i, j)),
            pl.BlockSpec((None, 1, tn), lambda j, i: ((i * tm) // seq * N_MOD + i_gate, 0, j)),
        ],
        out_specs=pl.BlockSpec((tm, tn), lambda j, i: (i, j)),
        out_shape=jax.ShapeDtypeStruct((n, d), F32),
        scratch_shapes=[pltpu.VMEM((k, tn), BF16)],
        compiler_params=_params("arbitrary", "arbitrary"),
        name="resid_mm",
    )(a, w, x, modl)


def _widest_tile(n_chunks, chunk, bytes_per_unit, limit_bytes):
    return chunk * max(t for t in range(1, n_chunks + 1)
                       if n_chunks % t == 0 and (t == 1 or t * chunk * bytes_per_unit <= limit_bytes))


def _swap_halves(z, group):
    if group == LANES:
        return pltpu.roll(z, LANES // 2, axis=1)
    half = group // 2
    lane = lax.broadcasted_iota(jnp.int32, z.shape, 1)
    from_above = pltpu.roll(z, LANES - half, axis=1)
    from_below = pltpu.roll(z, half, axis=1)
    return jnp.where((lane & (group - 1)) < half, from_above, from_below)


def _proj_kernel(h_ref, w_ref, cos_ref, sin_ref, o_ref, wb_ref, *, group):
    _cast_on_first_row_step(((w_ref, wb_ref),))
    z = _nt_dot(h_ref[...], wb_ref[...])
    if group == 0:
        o_ref[...] = z.astype(o_ref.dtype)
        return
    cos = cos_ref[...]
    sin = sin_ref[...]
    for c in range(z.shape[1] // LANES):
        zc = z[:, c * LANES:(c + 1) * LANES]
        o_ref[:, c * LANES:(c + 1) * LANES] = (zc * cos + _swap_halves(zc, group) * sin).astype(o_ref.dtype)


def proj(h, wt, cos, sin, group, seq):
    n, d = h.shape
    cols = wt.shape[0]
    tm = min(1024, seq)
    tn = _widest_tile(cols // LANES, LANES, d * 4, 12 * 2 ** 20)
    s_blocks = seq // tm
    return pl.pallas_call(
        functools.partial(_proj_kernel, group=group),
        grid=(cols // tn, n // tm),
        in_specs=[
            pl.BlockSpec((tm, d), lambda j, i: (i, 0)),
            pl.BlockSpec((tn, d), lambda j, i: (j, 0)),
            pl.BlockSpec((tm, LANES), lambda j, i: (i % s_blocks, 0)),
            pl.BlockSpec((tm, LANES), lambda j, i: (i % s_blocks, 0)),
        ],
        out_specs=pl.BlockSpec((tm, tn), lambda j, i: (i, j)),
        out_shape=jax.ShapeDtypeStruct((n, cols), BF16),
        scratch_shapes=[pltpu.VMEM((tn, d), BF16)],
        compiler_params=_params("arbitrary", "arbitrary"),
        name=f"proj_rope{group}",
    )(h, wt, cos, sin)


def _proj_fm_kernel(w_ref, h_ref, cos_ref, sin_ref, o_ref, wt_ref, *, group, row_scales):
    _cast_on_first_row_step(((w_ref, wt_ref),))
    zt = _nt_dot(wt_ref[...], h_ref[...])
    step = group if group else LANES
    half = group // 2
    scale_of_row = [s for n_rows, s in row_scales for _ in range(n_rows // step)]
    if len(row_scales) == 1:
        scale_of_row = scale_of_row[:1] * (zt.shape[0] // step)
    for c in range(zt.shape[0] // step):
        blk = zt[c * step:(c + 1) * step]
        if group:
            swapped = jnp.concatenate([blk[half:], blk[:half]], axis=0)
            blk = blk * cos_ref[...] + swapped * sin_ref[...]
        if scale_of_row[c] != 1.0:
            blk = blk * scale_of_row[c]
        tile = o_ref.shape[2]
        for t in range(o_ref.shape[0]):
            o_ref[t, c * step:(c + 1) * step, :] = blk[:, t * tile:(t + 1) * tile].astype(o_ref.dtype)


def proj_fm(h, w, cos_fm, sin_fm, group, seq, row_scales=None):
    n, d = h.shape
    rows = w.shape[0]
    tile = min(TOKEN_TILE, seq)
    tm = min(2 * tile, seq)
    unit = max(group, LANES)
    row_scales = row_scales or ((rows, 1.0),)
    limit = 12 * 2 ** 20 if len(row_scales) == 1 else rows * d * 4
    tn = _widest_tile(rows // unit, unit, d * 4, limit)
    s_blocks = seq // tm
    g = max(group, 8)
    return pl.pallas_call(
        functools.partial(_proj_fm_kernel, group=group, row_scales=row_scales),
        grid=(rows // tn, n // tm),
        in_specs=[
            pl.BlockSpec((tn, d), lambda j, i: (j, 0), pipeline_mode=pl.Buffered(1 if tn == rows else 2)),
            pl.BlockSpec((tm, d), lambda j, i: (i, 0)),
            pl.BlockSpec((g, tm), lambda j, i: (0, i % s_blocks)),
            pl.BlockSpec((g, tm), lambda j, i: (0, i % s_blocks)),
        ],
        out_specs=pl.BlockSpec((tm // tile, tn, tile), lambda j, i: (i, j, 0)),
        out_shape=jax.ShapeDtypeStruct((n // tile, rows, tile), BF16),
        scratch_shapes=[pltpu.VMEM((tn, d), BF16)],
        compiler_params=_params("arbitrary", "arbitrary"),
        name=f"proj_fm_rope{group}",
    )(w, h, cos_fm, sin_fm)


def _small_proj_kernel(h_ref, wt_ref, bf_ref, logf_ref, wi_ref):
    zt = _nt_dot(wt_ref[...], h_ref[...])
    logf_ref[...] = jax.nn.log_sigmoid(zt[0:8, :] + bf_ref[...])
    wi_ref[...] = zt[8:16, :] * IDX_W_SCALE


def small_proj(h, w_small_t, b_forget8, seq):
    n, d = h.shape
    tm = min(512, seq)
    return pl.pallas_call(
        _small_proj_kernel,
        grid=(n // tm,),
        in_specs=[
            pl.BlockSpec((tm, d), lambda i: (i, 0)),
            pl.BlockSpec((16, d), lambda i: (0, 0)),
            pl.BlockSpec((8, 1), lambda i: (0, 0)),
        ],
        out_specs=[pl.BlockSpec((8, tm), lambda i: (0, i)), pl.BlockSpec((8, tm), lambda i: (0, i))],
        out_shape=[jax.ShapeDtypeStruct((8, n), F32), jax.ShapeDtypeStruct((8, n), F32)],
        compiler_params=_params("arbitrary"),
        name="small_proj",
    )(h, w_small_t, b_forget8)


def _split3(x):
    x1 = x.astype(BF16)
    r1 = x - x1.astype(F32)
    x2 = r1.astype(BF16)
    x3 = (r1 - x2.astype(F32)).astype(BF16)
    return x1, x2, x3


def _neg_cumsum_kernel(x_ref, hi_ref, mid_ref, lo_ref):
    x = x_ref[...]
    chunks = x.shape[0]
    r = lax.broadcasted_iota(jnp.int32, (LANES, LANES), 0)
    c = lax.broadcasted_iota(jnp.int32, (LANES, LANES), 1)
    upper = (r <= c).astype(BF16)
    within = sum(_dot(p, upper) for p in _split3(x))
    totals = jnp.broadcast_to(within[:, LANES - 1:LANES], (chunks, LANES))
    rr = lax.broadcasted_iota(jnp.int32, (chunks, chunks), 0)
    cc = lax.broadcasted_iota(jnp.int32, (chunks, chunks), 1)
    strict_lower = (cc < rr).astype(BF16)
    offset = sum(_dot(strict_lower, p) for p in _split3(totals))
    hi_ref[...], mid_ref[...], lo_ref[...] = _split3(-(within + offset) * LOG2E)


def neg_cumsum(logf_t, batch, seq):
    rows = logf_t.shape[0] * batch
    chunks = seq // LANES
    x = logf_t.reshape(rows, chunks, LANES)
    spec = pl.BlockSpec((None, chunks, LANES), lambda i: (i, 0, 0))
    pieces = pl.pallas_call(
        _neg_cumsum_kernel,
        grid=(rows,),
        in_specs=[spec],
        out_specs=[spec] * 3,
        out_shape=[jax.ShapeDtypeStruct((rows, chunks, LANES), BF16)] * 3,
        compiler_params=_params("arbitrary"),
        name="neg_cumsum",
    )(x)
    rows3 = jnp.stack([p.reshape(logf_t.shape[0], batch, seq) for p in pieces], axis=2)
    return jnp.pad(rows3, ((0, 0), (0, 0), (0, 8 - N_BIAS_PIECES), (0, 0)))


def _query_streams(tq):
    width = min(MXU_WIDTH, tq)
    return [slice(lo, lo + width) for lo in range(0, tq, width)]


ONES_ROWS = 16
ACC_ROWS = HEAD_DIM + ONES_ROWS


def _softmax_stats(s, m_prev):
    m_new = jnp.maximum(m_prev, jnp.max(s, axis=0, keepdims=True))
    return m_new, jnp.exp2(m_prev - m_new), jnp.exp2(s - m_new).astype(BF16)


def _with_ones_rows(vt):
    return jnp.concatenate([vt, jnp.ones((ONES_ROWS, vt.shape[1]), vt.dtype)], axis=0)


def _normalized(acc):
    return acc[0:HEAD_DIM] / acc[HEAD_DIM:HEAD_DIM + 1]


def _causal_mask(key0, query0, n_keys, n_queries):
    key = key0 + lax.broadcasted_iota(jnp.int32, (n_keys, n_queries), 0)
    query = query0 + lax.broadcasted_iota(jnp.int32, (n_keys, n_queries), 1)
    return key <= query


def _sweep_key_tiles(q_block, tq, tk, tile_fn):
    n_full = (q_block * tq) // tk

    def body(kj, carry):
        tile_fn(kj, False)
        return carry

    lax.fori_loop(0, n_full, body, 0)
    tile_fn(n_full, True)


def _head_sweep(nq, tq, tk, streams, raw_scores, logits, values, pipe, steps_per_trip):
    assert tq == tk
    ids = range(len(streams))
    slot_a, slot_b = pipe[:3], pipe[3:]

    for (m_ref, acc_ref), qs in streams:
        m_ref[:, :, qs] = jnp.full((nq, 1, qs.stop - qs.start), -jnp.inf, F32)
        acc_ref[:, :, qs] = jnp.zeros((nq, ACC_ROWS, qs.stop - qs.start), F32)

    def accumulate(qi, pv, rescale):
        for i in ids:
            (_, acc_ref), qs = streams[i]
            acc_ref[qi, :, qs] = rescale[i] * acc_ref[qi, :, qs] + pv[i]

    def products(kj, slot):
        vts = values(kj)
        return [_dot(vts[i], slot[1][i]) for i in ids]

    def step(cur, prev, nxt, cur_slot, nxt_slot, masked):
        raw_next = raw_scores(*nxt)
        if prev is not None:
            pv_prev = products(prev[0], nxt_slot)
            rescale_prev = [nxt_slot[2][i] for i in ids]
        for i in ids:
            nxt_slot[0][i] = raw_next[i]
        kj, qi = cur
        for i in ids:
            (m_ref, _), qs = streams[i]
            m_new, alpha, p = _softmax_stats(logits(kj, qi, i, cur_slot[0][i], masked), m_ref[qi, :, qs])
            m_ref[qi, :, qs] = m_new
            cur_slot[1][i] = p
            cur_slot[2][i] = alpha
        if prev is not None:
            accumulate(prev[1], pv_prev, rescale_prev)

    def run(n_pairs, start, advance, masked):
        if n_pairs == 0:
            return
        slots = (slot_a, slot_b)
        raw0 = raw_scores(*start)
        for i in ids:
            slot_a[0][i] = raw0[i]
        second = advance(*start)
        step(start, None, second, slot_a, slot_b, masked)

        def steps(count, cur, prev):
            for u in range(count):
                nxt = advance(*cur)
                step(cur, prev, nxt, slots[(u + 1) % 2], slots[u % 2], masked)
                prev, cur = cur, nxt
            return cur, prev

        def trip(t, carry):
            cur, prev = steps(steps_per_trip, carry[:2], carry[2:])
            return (*cur, *prev)

        carry = lax.fori_loop(0, (n_pairs - 1) // steps_per_trip, trip, (*second, *start))
        _, prev = steps((n_pairs - 1) % steps_per_trip, carry[:2], carry[2:])
        last_slot = slots[(n_pairs - 1) % 2]
        accumulate(prev[1], products(prev[0], last_slot), [last_slot[2][i] for i in ids])

    last = nq - 1
    zero = jnp.int32(0)
    run(nq, (zero, zero), lambda kj, qi: (jnp.minimum(kj + 1, last), jnp.minimum(qi + 1, last)), True)

    def next_below_diagonal(kj, qi):
        wrap = qi == last
        dist = qi - kj
        return (jnp.where(wrap, 0, kj + 1), jnp.where(wrap, jnp.minimum(dist + 1, last), qi + 1))

    run(nq * (nq - 1) // 2, (zero, jnp.int32(min(1, last))), next_below_diagonal, False)


def _head_softmax_scratch(nq, tq):
    return [pltpu.VMEM((nq, 1, tq), F32), pltpu.VMEM((nq, ACC_ROWS, tq), F32)]


def _pipe_scratch(n_streams, tk, width):
    slot = lambda: [pltpu.VMEM((n_streams, tk, width), F32), pltpu.VMEM((n_streams, tk, width), BF16),
                    pltpu.VMEM((n_streams, 1, width), F32)]
    return slot() + slot()


N_BIAS_PIECES = 3


def _fox_kernel(qt_ref, k_ref, bias_ref, vt_ref, o_ref, m_ref, acc_ref, qa_ref, kb_ref, *pipe, nq, tq, tk):
    slices = _query_streams(tq)
    row = lax.broadcasted_iota(jnp.int32, (HEAD_DIM, tq), 0)
    ones_rows = jnp.where(row < N_BIAS_PIECES, 1.0, 0.0).astype(BF16)
    zeros = jnp.zeros((HEAD_DIM - 8, tk), F32)
    for j in range(nq):
        qa_ref[j] = jnp.concatenate([qt_ref[j], ones_rows], axis=0)
        rows = bias_ref[:, j * tk:(j + 1) * tk].astype(F32)
        kb_ref[j] = jnp.concatenate([rows, zeros], axis=0).T.astype(BF16)

    def raw_scores(kj, qi):
        k = jnp.concatenate([k_ref[pl.ds(pl.multiple_of(kj * tk, tk), tk), :], kb_ref[kj]], axis=1)
        return [_dot(k, qa_ref[qi, :, qs]) for qs in slices]

    def logits(kj, qi, i, raw, masked):
        qs = slices[i]
        if masked:
            raw = jnp.where(_causal_mask(kj * tk, qi * tq + qs.start, tk, qs.stop - qs.start), raw, NEG_INF)
        return raw

    def values(kj):
        return [_with_ones_rows(vt_ref[kj])] * len(slices)

    _head_sweep(nq, tq, tk, [((m_ref, acc_ref), qs) for qs in slices], raw_scores, logits, values, pipe, 4)
    for j in range(nq):
        o_ref[j * tq:(j + 1) * tq, :] = _normalized(acc_ref[j]).T.astype(o_ref.dtype)


def fox_attention(q_fm, q_blk0, k_rows, bias_rows, v_fm, v_blk0, seq):
    b = k_rows.shape[0]
    tq = tk = q_fm.shape[2]
    nq = seq // tq
    slices = _query_streams(tq)
    return pl.pallas_call(
        functools.partial(_fox_kernel, nq=nq, tq=tq, tk=tk),
        grid=(b, H_FOX),
        in_specs=[
            pl.BlockSpec((nq, HEAD_DIM, tq), lambda bi, h: (bi, q_blk0 + h, 0)),
            pl.BlockSpec((None, seq, HEAD_DIM), lambda bi, h: (bi, 0, h)),
            pl.BlockSpec((None, None, 8, seq), lambda bi, h: (h, bi, 0, 0)),
            pl.BlockSpec((nq, HEAD_DIM, tk), lambda bi, h: (bi, v_blk0 + h, 0)),
        ],
        out_specs=pl.BlockSpec((None, seq, HEAD_DIM), lambda bi, h: (bi, 0, h)),
        out_shape=jax.ShapeDtypeStruct((b, seq, W_FOX), BF16),
        scratch_shapes=_head_softmax_scratch(nq, tq)
        + [pltpu.VMEM((nq, 2 * HEAD_DIM, tq), BF16), pltpu.VMEM((nq, tk, HEAD_DIM), BF16)]
        + _pipe_scratch(len(slices), tk, slices[0].stop),
        compiler_params=_params("arbitrary", "arbitrary"),
        name="fox_attention",
    )(q_fm, k_rows, bias_rows, v_fm)


def _diff_kernel(lam_ref, gain_ref, qt_ref, k_ref, vt_ref, o_ref,
                 m1_ref, acc1_ref, m2_ref, acc2_ref, q1_ref, q2_ref, *pipe, nq, tq, tk, lam_init):
    zeros = jnp.zeros((DK_DIFF, tq), BF16)
    for j in range(nq):
        q1_ref[j] = jnp.concatenate([qt_ref[j, 0:DK_DIFF, :], zeros], axis=0)
        q2_ref[j] = jnp.concatenate([zeros, qt_ref[j, DK_DIFF:HEAD_DIM, :]], axis=0)
    streams, q_refs = [], []
    for qs in _query_streams(tq):
        for refs, q_ref in (((m1_ref, acc1_ref), q1_ref), ((m2_ref, acc2_ref), q2_ref)):
            streams.append((refs, qs))
            q_refs.append(q_ref)

    def raw_scores(kj, qi):
        k = k_ref[pl.ds(pl.multiple_of(kj * tk, tk), tk), :]
        return [_dot(k, q_ref[qi, :, qs]) for q_ref, (_, qs) in zip(q_refs, streams)]

    def logits(kj, qi, i, raw, masked):
        qs = streams[i][1]
        if masked:
            raw = jnp.where(_causal_mask(kj * tk, qi * tq + qs.start, tk, qs.stop - qs.start), raw, NEG_INF)
        return raw

    def values(kj):
        return [_with_ones_rows(vt_ref[kj])] * len(streams)

    _head_sweep(nq, tq, tk, streams, raw_scores, logits, values, pipe, 2)

    lam_vecs = lam_ref[...]
    dot1 = jnp.sum(lam_vecs[0:1] * lam_vecs[1:2], axis=1, keepdims=True)
    dot2 = jnp.sum(lam_vecs[2:3] * lam_vecs[3:4], axis=1, keepdims=True)
    lam = jnp.exp(dot1) - jnp.exp(dot2) + lam_init
    for j in range(nq):
        o = (_normalized(acc1_ref[j]) - lam * _normalized(acc2_ref[j])).T
        y = o * lax.rsqrt(jnp.mean(o * o, axis=-1, keepdims=True) + NORM_EPS)
        o_ref[j * tq:(j + 1) * tq, :] = ((y * gain_ref[...]) * (1.0 - lam_init)).astype(o_ref.dtype)


def diff_attention(q_fm, q_blk0, k_rows, v_fm, v_blk0, lam_vecs, subln_gain, lam_init, seq):
    b = k_rows.shape[0]
    tq = tk = q_fm.shape[2]
    nq = seq // tq
    slices = _query_streams(tq)
    return pl.pallas_call(
        functools.partial(_diff_kernel, nq=nq, tq=tq, tk=tk, lam_init=lam_init),
        grid=(b, H_DIFF),
        in_specs=[
            pl.BlockSpec((4, DK_DIFF), lambda bi, h: (0, 0)),
            pl.BlockSpec((1, HEAD_DIM), lambda bi, h: (0, 0)),
            pl.BlockSpec((nq, HEAD_DIM, tq), lambda bi, h: (bi, q_blk0 + h, 0)),
            pl.BlockSpec((None, seq, HEAD_DIM), lambda bi, h: (bi, 0, h)),
            pl.BlockSpec((nq, HEAD_DIM, tk), lambda bi, h: (bi, v_blk0 + h, 0)),
        ],
        out_specs=pl.BlockSpec((None, seq, HEAD_DIM), lambda bi, h: (bi, 0, h)),
        out_shape=jax.ShapeDtypeStruct((b, seq, W_DIFF), BF16),
        scratch_shapes=_head_softmax_scratch(nq, tq) + _head_softmax_scratch(nq, tq)
        + [pltpu.VMEM((nq, HEAD_DIM, tq), BF16), pltpu.VMEM((nq, HEAD_DIM, tq), BF16)]
        + _pipe_scratch(2 * len(slices), tk, slices[0].stop),
        compiler_params=_params("arbitrary", "arbitrary"),
        name="diff_attention",
    )(lam_vecs, subln_gain.reshape(1, HEAD_DIM), q_fm, k_rows, v_fm)


def _dsa_kernel(qt_ref, k_ref, vt_ref, qit_ref, ki_ref, wt_ref, o_ref,
                hi_ref, lo_ref, qm_ref, cut_ref, m_ref, acc_ref, *pipe,
                tq, tk, seq, n_sel):
    qb = pl.program_id(1)
    n_tiles = (qb * tq) // tk + 1
    slices = _query_streams(tq)

    zeros = jnp.zeros((D_IDX, tq), BF16)
    for h in range(H_IDX):
        qm_ref[h] = jnp.concatenate([qit_ref[h * D_IDX:(h + 1) * D_IDX, :], zeros], axis=0)

    def score_tile(kj, masked):
        kk = ki_ref[pl.ds(pl.multiple_of(kj * tk, tk), tk), :]
        for qs in slices:
            width = qs.stop - qs.start
            rel_q = [_dot(kk, qm_ref[h, :, qs]) for h in range(H_IDX)]
            score = jnp.zeros((tk, width), F32)
            for h in range(H_IDX):
                score = score + wt_ref[h:h + 1, qs] * jnp.maximum(rel_q[h], 0.0)
            if masked:
                score = jnp.where(_causal_mask(kj * tk, qb * tq + qs.start, tk, width), score, NEG_INF)
            bits = lax.bitcast_convert_type(score, jnp.int32)
            key = bits ^ ((bits >> 31) & 0x7FFFFFFF)
            hi_ref[kj, :, qs] = (key >> 16).astype(jnp.int16)
            lo_ref[kj, :, qs] = ((key & 0xFFFF) + I16_MIN).astype(jnp.int16)

    _sweep_key_tiles(qb, tq, tk, score_tile)

    n_beyond = (seq - n_tiles * tk).astype(F32)
    neg_hi, neg_lo = KEY_NEG_INF >> 16, (KEY_NEG_INF & 0xFFFF) + I16_MIN
    one, zero = jnp.int16(1), jnp.int16(0)
    as16 = lambda v: v.astype(jnp.int16)

    def count_hits(hit_fn):
        def body(kj, total):
            hit = jnp.where(hit_fn(kj), one, zero)
            part = hit[0:16]
            for r in range(1, tk // 16):
                part = part + hit[r * 16:(r + 1) * 16]
            return total + jnp.sum(part.astype(F32), axis=0, keepdims=True)
        return lax.fori_loop(0, n_tiles, body, jnp.zeros((1, tq), F32))

    def bisect16(enough):
        def step(s, t):
            cand = t + jnp.left_shift(jnp.int32(1), 15 - s)
            return jnp.where(enough(cand), cand, t)
        return lax.fori_loop(0, 16, step, jnp.full((1, tq), I16_MIN, jnp.int32))

    def count_hi_ge(cand):
        n = count_hits(lambda kj, c=as16(cand): hi_ref[kj] >= c)
        return n + jnp.where(cand <= neg_hi, n_beyond, 0.0)

    t_hi = bisect16(lambda cand: count_hi_ge(cand) >= n_sel)
    t_hi16 = as16(t_hi)
    n_above = jnp.where(t_hi < I16_MAX, count_hi_ge(jnp.minimum(t_hi + 1, I16_MAX)), 0.0)
    need_lo = n_sel - n_above

    def mask_lo(kj, carry):
        lo_ref[kj] = jnp.where(hi_ref[kj] == t_hi16, lo_ref[kj], jnp.int16(I16_MIN))
        return carry

    lax.fori_loop(0, n_tiles, mask_lo, 0)
    beyond_in_group = jnp.where(t_hi == neg_hi, n_beyond, 0.0)

    def count_lo_ge(cand):
        n = count_hits(lambda kj, c=as16(cand): lo_ref[kj] >= c)
        return n + jnp.where(cand <= neg_lo, beyond_in_group, 0.0)

    t_lo = bisect16(lambda cand: count_lo_ge(cand) >= need_lo)
    t_lo16 = as16(t_lo)

    def in_group(kj):
        return hi_ref[kj] == t_hi16

    n_ge = n_above + count_hits(lambda kj: in_group(kj) & (lo_ref[kj] >= t_lo16)) \
        + jnp.where(t_lo <= neg_lo, beyond_in_group, 0.0)
    surplus = n_ge - n_sel
    cut_ref[...] = jnp.full((1, tq), I16_MAX, jnp.int32)

    def key_index(kj):
        return as16(kj * tk + lax.broadcasted_iota(jnp.int32, (tk, tq), 0))

    @pl.when(jnp.max(surplus) > 0.0)
    def _():
        n_gt = n_above + count_hits(lambda kj: in_group(kj) & (lo_ref[kj] > t_lo16)) \
            + jnp.where(t_lo < neg_lo, beyond_in_group, 0.0)
        need = jnp.where(surplus > 0.0, n_sel - n_gt, float(seq + 1))

        def count_tied_below(cut):
            c = as16(cut)
            return count_hits(lambda kj: in_group(kj) & (lo_ref[kj] == t_lo16) & (key_index(kj) < c))

        n_bits = max(1, (seq - 1).bit_length())

        def bisect_cut(step, cut):
            cand = cut + jnp.left_shift(jnp.int32(1), n_bits - 1 - step)
            return jnp.where(count_tied_below(cand) < need, cand, cut)

        cut_ref[...] = lax.fori_loop(0, n_bits, bisect_cut, jnp.zeros((1, tq), jnp.int32))

    cut16 = as16(cut_ref[...])
    zero_bias, neg_bias = jnp.zeros((), BF16), jnp.asarray(NEG_INF, BF16)

    def write_bias(kj, carry):
        hi, lo = hi_ref[kj], lo_ref[kj]
        tied_ok = (lo == t_lo16) & (key_index(kj) <= cut16)
        sel = (hi > t_hi16) | ((hi == t_hi16) & ((lo > t_lo16) | tied_ok))
        hi_ref[kj] = lax.bitcast_convert_type(jnp.where(sel, zero_bias, neg_bias), jnp.int16)
        return carry

    lax.fori_loop(0, n_tiles, write_bias, 0)

    n_full = n_tiles - 1
    ids = range(len(slices))
    slots = (pipe[:3], pipe[3:])
    head_rows = lambda h: slice(h * HEAD_DIM, (h + 1) * HEAD_DIM)
    m_ref[...] = jnp.full(m_ref.shape, -jnp.inf, F32)
    acc_ref[...] = jnp.zeros(acc_ref.shape, F32)

    def raw_scores(kj, h):
        k = k_ref[pl.ds(pl.multiple_of(kj * tk, tk), tk), head_rows(h)]
        return [_dot(k, qt_ref[head_rows(h), qs]) for qs in slices]

    def products(kj, h, slot):
        vt = _with_ones_rows(vt_ref[kj, head_rows(h), :])
        return [_dot(vt, slot[1][i]) for i in ids]

    def accumulate(h, pv, rescale):
        for i, qs in enumerate(slices):
            acc_ref[h, :, qs] = rescale[i] * acc_ref[h, :, qs] + pv[i]

    def step(kj, h, first=False, last=False, masked=False):
        cur_slot, nxt_slot = slots[h % 2], slots[(h + 1) % 2]
        nxt = (kj, h + 1) if h + 1 < H_DSA else (kj + 1, 0)
        prev = (kj, h - 1) if h > 0 else (kj - 1, H_DSA - 1)
        raw_next = None if last else raw_scores(*nxt)
        if not first:
            pv_prev = products(*prev, nxt_slot)
            rescale_prev = [nxt_slot[2][i] for i in ids]
        if not last:
            for i in ids:
                nxt_slot[0][i] = raw_next[i]
        for i, qs in enumerate(slices):
            s = cur_slot[0][i] + lax.bitcast_convert_type(hi_ref[kj, :, qs], BF16).astype(F32)
            if masked:
                s = jnp.where(_causal_mask(kj * tk, qb * tq + qs.start, tk, qs.stop - qs.start), s, NEG_INF)
            m_new, alpha, p = _softmax_stats(s, m_ref[h, :, qs])
            m_ref[h, :, qs] = m_new
            cur_slot[1][i] = p
            cur_slot[2][i] = alpha
        if not first:
            accumulate(prev[1], pv_prev, rescale_prev)

    def tile_steps(kj, first=False, diagonal=False):
        for h in range(H_DSA):
            step(kj, h, first=first and h == 0, last=diagonal and h == H_DSA - 1, masked=diagonal)

    raw0 = raw_scores(0, 0)
    for i in ids:
        slots[0][0][i] = raw0[i]

    @pl.when(n_full == 0)
    def _():
        tile_steps(0, first=True, diagonal=True)

    @pl.when(n_full > 0)
    def _():
        tile_steps(0, first=True)

        def body(kj, carry):
            tile_steps(kj)
            return carry

        lax.fori_loop(1, n_full, body, 0)
        tile_steps(n_full, diagonal=True)

    last_slot = slots[(H_DSA - 1) % 2]
    accumulate(H_DSA - 1, products(n_full, H_DSA - 1, last_slot), [last_slot[2][i] for i in ids])
    for h in range(H_DSA):
        o_ref[:, head_rows(h)] = _normalized(acc_ref[h]).T.astype(o_ref.dtype)


def dsa_attention(q_fm, k_rows, v_fm, qi_fm, ki_rows, ki_blk, wi_t, seq):
    b = k_rows.shape[0]
    tq = tk = q_fm.shape[2]
    nq = seq // tq
    n_sel = min(TOPK_MAX, seq // 4)
    once = pl.Buffered(1)
    slices = _query_streams(tq)
    fm_q = lambda rows: pl.BlockSpec((None, rows, tq), lambda bi, i: (bi * nq + i, 0, 0))
    return pl.pallas_call(
        functools.partial(_dsa_kernel, tq=tq, tk=tk, seq=seq, n_sel=n_sel),
        grid=(b, nq),
        in_specs=[
            fm_q(W_DSA),
            pl.BlockSpec((None, seq, W_DSA), lambda bi, i: (bi, 0, 0), pipeline_mode=once),
            pl.BlockSpec((nq, W_DSA, tk), lambda bi, i: (bi, 0, 0), pipeline_mode=once),
            fm_q(W_IDX),
            pl.BlockSpec((None, seq, LANES), lambda bi, i: (bi, 0, ki_blk), pipeline_mode=once),
            pl.BlockSpec((H_IDX, tq), lambda bi, i: (0, bi * nq + i)),
        ],
        out_specs=pl.BlockSpec((None, tq, W_DSA), lambda bi, i: (bi, i, 0)),
        out_shape=jax.ShapeDtypeStruct((b, seq, W_DSA), BF16),
        scratch_shapes=[
            pltpu.VMEM((nq, tk, tq), jnp.int16),
            pltpu.VMEM((nq, tk, tq), jnp.int16),
            pltpu.VMEM((H_IDX, 2 * D_IDX, tq), BF16),
            pltpu.VMEM((1, tq), jnp.int32),
        ] + _head_softmax_scratch(H_DSA, tq) + _pipe_scratch(len(slices), tk, slices[0].stop),
        compiler_params=_params("arbitrary", "arbitrary"),
        name="dsa_attention",
    )(q_fm, k_rows, v_fm, qi_fm, ki_rows, wi_t)


def _merge_kernel(h_ref, oa_ref, ob_ref, oc_ref, mw0_ref, mw1_ref, mw2_ref, mb_ref, wa_ref, wb_ref, wc_ref, o_ref,
                  *bf16_refs):
    f32_refs = (mw0_ref, mw1_ref, mw2_ref, wa_ref, wb_ref, wc_ref)
    _cast_on_first_row_step(tuple(zip(f32_refs, bf16_refs)))
    gate_w, branch_w = bf16_refs[:N_BRANCH], bf16_refs[N_BRANCH:]
    h = h_ref[...]
    merged = None
    for i, o_b_ref in enumerate((oa_ref, ob_ref, oc_ref)):
        gate = jax.nn.sigmoid(_dot(h, gate_w[i][...]) + mb_ref[i])
        term = gate * _dot(o_b_ref[...], branch_w[i][...])
        merged = term if merged is None else merged + term
    o_ref[...] = merged.astype(o_ref.dtype)


def merge_branches(h, oa, ob, oc, merge_w, merge_b, wa, wb, wc, layer, seq):
    n, d = h.shape
    tm = min(1024, seq)
    tn = min(256, d)
    nj = d // tn
    row = lambda width: pl.BlockSpec((tm, width), lambda j, i: (i, 0))
    col = lambda rows: pl.BlockSpec((None, rows, tn), lambda j, i: (layer, 0, j))
    gate_w = lambda g: pl.BlockSpec((None, d, tn), lambda j, i: (layer, 0, g * nj + j))
    widths = (oa.shape[1], ob.shape[1], oc.shape[1])
    return pl.pallas_call(
        _merge_kernel,
        grid=(nj, n // tm),
        in_specs=[
            row(d), row(widths[0]), row(widths[1]), row(widths[2]),
            gate_w(0), gate_w(1), gate_w(2),
            pl.BlockSpec((None, N_BRANCH, 1, tn), lambda j, i: (layer, 0, 0, j)),
            col(widths[0]), col(widths[1]), col(widths[2]),
        ],
        out_specs=pl.BlockSpec((tm, tn), lambda j, i: (i, j)),
        out_shape=jax.ShapeDtypeStruct((n, d), BF16),
        scratch_shapes=[pltpu.VMEM((d, tn), BF16)] * N_BRANCH + [pltpu.VMEM((w, tn), BF16) for w in widths],
        compiler_params=_params("arbitrary", "arbitrary"),
        name="merge_branches",
    )(h, oa, ob, oc, merge_w, merge_w, merge_w, merge_b.reshape(merge_b.shape[0], N_BRANCH, 1, d), wa, wb, wc)


def _rope_tables(seq, dim):
    inv_freq = 1.0 / (ROPE_THETA ** (jnp.arange(0, dim, 2, dtype=F32) / dim))
    ang = jnp.arange(seq, dtype=F32)[:, None] * inv_freq[None, :]
    cos, sin = lax.optimization_barrier((jnp.cos(ang), jnp.sin(ang)))
    cos_g = jnp.concatenate([cos, cos], axis=1)
    sin_g = jnp.concatenate([-sin, sin], axis=1)
    reps = LANES // dim
    return jnp.tile(cos_g, (1, reps)), jnp.tile(sin_g, (1, reps)), cos_g.T, sin_g.T


def _split_w_in(w_in_t):
    sizes = (W_DIFF, W_DIFF, W_DIFF, W_FOX, W_FOX, W_FOX, H_FOX, W_DSA, W_DSA, W_DSA, W_IDX, D_IDX, H_IDX)
    parts, start = [], 0
    for n in sizes:
        parts.append(w_in_t[start:start + n])
        start += n
    return parts


def kernel(x, c, w_ada, b_ada, norm_ffn1, ffn1_w1, ffn1_w3, ffn1_w2, norm_mix, w_in, b_forget, lam_q1, lam_k1, lam_q2, lam_k2, subln_gain, merge_w, merge_b, w_branch_a, w_branch_b, w_branch_c, w_out, norm_ffn2, ffn2_w1, ffn2_w3, ffn2_w2, norm_final):
    batch, seq, d = x.shape
    depth = w_ada.shape[0]
    n = batch * seq
    assert seq % min(TOKEN_TILE, seq) == 0 and seq % LANES == 0 and d % LANES == 0
    assert seq <= I16_MAX + 1, "the top-k tie rule compares key positions as int16"
    bf = lambda a: a.astype(BF16)

    cos64, sin64, cos64_fm, sin64_fm = _rope_tables(seq, DK_DIFF)
    cos128, sin128, cos128_fm, sin128_fm = _rope_tables(seq, HEAD_DIM)
    mod = adaln_mod(c, w_ada, b_ada)
    xf = x.reshape(n, d)

    for l in range(depth):
        lam_init = 0.8 - 0.6 * math.exp(-0.3 * l)
        modl = mod[l].reshape(batch * N_MOD, 1, d)

        h = norm_mod(xf, norm_ffn1[l], modl, 0, seq)
        u = ffn_up(h, ffn1_w1, ffn1_w3, l)
        xf = resid_mm(u, ffn1_w2, l, xf, modl, 2, 0.5, seq)

        h = norm_mod(xf, norm_mix[l], modl, 3, seq)
        qa, ka, va, qb, kb, vb, fb, qc, kc, vc, qi, ki, wi = _split_w_in(jnp.transpose(w_in[l]))
        k64 = proj(h, jnp.concatenate([ka, ki, ki], axis=0), cos64, sin64, DK_DIFF, seq).reshape(batch, seq, -1)
        k128 = proj(h, kc, cos128, sin128, HEAD_DIM, seq).reshape(batch, seq, -1)
        k0 = proj(h, kb, cos64, sin64, 0, seq).reshape(batch, seq, -1)
        log2e_over_sqrt = lambda width: (width ** -0.5) * LOG2E
        q64_fm = proj_fm(h, jnp.concatenate([qi, qa], axis=0), cos64_fm, sin64_fm, DK_DIFF, seq,
                         ((W_IDX, 1.0), (W_DIFF, log2e_over_sqrt(DK_DIFF))))
        q128_fm = proj_fm(h, qc, cos128_fm, sin128_fm, HEAD_DIM, seq, ((W_DSA, log2e_over_sqrt(HEAD_DIM)),))
        qb_fm = proj_fm(h, qb, cos64_fm, sin64_fm, 0, seq, ((W_FOX, log2e_over_sqrt(HEAD_DIM)),))
        v_fm = proj_fm(h, jnp.concatenate([vc, va, vb], axis=0), cos64_fm, sin64_fm, 0, seq)
        blk = lambda rows: rows // HEAD_DIM
        qa_blk, va_blk, vb_blk = blk(W_IDX), blk(W_DSA), blk(W_DSA + W_DIFF)

        w_small_t = bf(jnp.concatenate([fb, jnp.zeros((8 - H_FOX, d), F32), wi], axis=0))
        b_forget8 = jnp.concatenate([b_forget[l], jnp.zeros((8 - H_FOX,), F32)]).reshape(8, 1)
        logf_t, wi_t = small_proj(h, w_small_t, b_forget8, seq)
        bias_rows = neg_cumsum(logf_t, batch, seq)

        lam_vecs = jnp.stack([lam_q1[l], lam_k1[l], lam_q2[l], lam_k2[l]])
        oa = diff_attention(q64_fm, qa_blk, k64, v_fm, va_blk, lam_vecs, subln_gain[l], lam_init, seq)
        ob = fox_attention(qb_fm, 0, k0, bias_rows, v_fm, vb_blk, seq)
        oc = dsa_attention(q128_fm, k128, v_fm, q64_fm, k64, blk(W_DIFF), wi_t, seq)

        merged = merge_branches(h, oa.reshape(n, -1), ob.reshape(n, -1), oc.reshape(n, -1), merge_w, merge_b,
                                w_branch_a, w_branch_b, w_branch_c, l, seq)
        xf = resid_mm(merged, w_out, l, xf, modl, 5, 1.0, seq)

        h = norm_mod(xf, norm_ffn2[l], modl, 6, seq)
        u = ffn_up(h, ffn2_w1, ffn2_w3, l)
        xf = resid_mm(u, ffn2_w2, l, xf, modl, 8, 0.5, seq)

    return final_norm(xf, norm_final).reshape(batch, seq, d)
```

```python
import functools
import math

import numpy as np
import jax
import jax.numpy as jnp
from jax import lax
from jax.experimental import pallas as pl
from jax.experimental.pallas import tpu as pltpu

HEAD_DIM = 128
H_DIFF = 6
DK_DIFF = HEAD_DIM // 2
H_FOX = 6
H_DSA = 4
H_IDX = 8
D_IDX = 64
TOPK_MAX = 256
ROPE_THETA = 10000.0
NORM_EPS = 1e-6
N_BRANCH = 3
N_MOD = 9
NEG_INF = -1e30
IDX_W_SCALE = (H_IDX ** -0.5) * (D_IDX ** -0.5)
LOG2E = math.log2(math.e)

W_DIFF = H_DIFF * HEAD_DIM
W_FOX = H_FOX * HEAD_DIM
W_DSA = H_DSA * HEAD_DIM
W_IDX = H_IDX * D_IDX

LANES = 128
MXU_WIDTH = 256
VMEM_LIMIT = 56 * 1024 * 1024
TOKEN_TILE = 512
I16_MIN, I16_MAX = -(2 ** 15), 2 ** 15 - 1

BF16 = jnp.bfloat16
F32 = jnp.float32


def _order_key_of(value):
    bits = int(np.array(value, np.float32).view(np.int32))
    return bits ^ ((bits >> 31) & 0x7FFFFFFF)


KEY_NEG_INF = _order_key_of(NEG_INF)


def _params(*semantics):
    return pltpu.CompilerParams(dimension_semantics=semantics, vmem_limit_bytes=VMEM_LIMIT)


def _nt_dot(a, b):
    return lax.dot_general(a, b, (((1,), (1,)), ((), ())), preferred_element_type=F32)


def _dot(a, b):
    return jnp.dot(a, b, preferred_element_type=F32)


def _adaln_kernel(c_ref, w_ref, b_ref, o_ref):
    c = c_ref[...]
    c_act = (c * jax.nn.sigmoid(c)).astype(BF16)
    o_ref[...] = _dot(c_act, w_ref[...].astype(BF16)) + b_ref[...]


def adaln_mod(c, w_ada, b_ada):
    depth, d, nd = w_ada.shape
    b = c.shape[0]
    tn = min(1024, d)
    return pl.pallas_call(
        _adaln_kernel,
        grid=(depth, nd // tn),
        in_specs=[
            pl.BlockSpec((b, d), lambda l, j: (0, 0)),
            pl.BlockSpec((None, d, tn), lambda l, j: (l, 0, j)),
            pl.BlockSpec((None, 1, tn), lambda l, j: (l, 0, j)),
        ],
        out_specs=pl.BlockSpec((None, b, tn), lambda l, j: (l, 0, j)),
        out_shape=jax.ShapeDtypeStruct((depth, b, nd), F32),
        compiler_params=_params("arbitrary", "arbitrary"),
        name="adaln_mod",
    )(c, w_ada, b_ada.reshape(depth, 1, nd))


def _norm_mod_kernel(x_ref, gain_ref, sc_ref, sh_ref, o_ref):
    x = x_ref[...]
    y = x * lax.rsqrt(jnp.mean(x * x, axis=-1, keepdims=True) + NORM_EPS)
    o_ref[...] = ((y * gain_ref[...]) * (1.0 + sc_ref[...]) + sh_ref[...]).astype(o_ref.dtype)


def norm_mod(x, gain, modl, i_shift, seq):
    n, d = x.shape
    tm = min(512, seq)
    return pl.pallas_call(
        _norm_mod_kernel,
        grid=(n // tm,),
        in_specs=[
            pl.BlockSpec((tm, d), lambda i: (i, 0)),
            pl.BlockSpec((1, d), lambda i: (0, 0)),
            pl.BlockSpec((None, 1, d), lambda i: ((i * tm) // seq * N_MOD + i_shift + 1, 0, 0)),
            pl.BlockSpec((None, 1, d), lambda i: ((i * tm) // seq * N_MOD + i_shift, 0, 0)),
        ],
        out_specs=pl.BlockSpec((tm, d), lambda i: (i, 0)),
        out_shape=jax.ShapeDtypeStruct((n, d), BF16),
        compiler_params=_params("arbitrary"),
        name="norm_mod",
    )(x, gain.reshape(1, d), modl, modl)


def _final_norm_kernel(x_ref, gain_ref, o_ref):
    x = x_ref[...]
    y = x * lax.rsqrt(jnp.mean(x * x, axis=-1, keepdims=True) + NORM_EPS)
    o_ref[...] = y * gain_ref[...]


def final_norm(x, gain):
    n, d = x.shape
    tm = min(512, n)
    return pl.pallas_call(
        _final_norm_kernel,
        grid=(n // tm,),
        in_specs=[pl.BlockSpec((tm, d), lambda i: (i, 0)), pl.BlockSpec((1, d), lambda i: (0, 0))],
        out_specs=pl.BlockSpec((tm, d), lambda i: (i, 0)),
        out_shape=jax.ShapeDtypeStruct((n, d), F32),
        compiler_params=_params("arbitrary"),
        name="final_norm",
    )(x, gain.reshape(1, d))


def _cast_on_first_row_step(pairs):
    @pl.when(pl.program_id(1) == 0)
    def _():
        for src_ref, dst_ref in pairs:
            dst_ref[...] = src_ref[...].astype(dst_ref.dtype)


def _ffn_up_kernel(h_ref, w1_ref, w3_ref, o_ref, w1b_ref, w3b_ref):
    _cast_on_first_row_step(((w1_ref, w1b_ref), (w3_ref, w3b_ref)))
    h = h_ref[...]
    a = _dot(h, w1b_ref[...])
    b = _dot(h, w3b_ref[...])
    o_ref[...] = ((a * jax.nn.sigmoid(a)) * b).astype(o_ref.dtype)


def ffn_up(h, w1, w3, layer):
    n, d = h.shape
    f = w1.shape[2]
    tm = min(1024, n)
    tn = 512 if f % 512 == 0 else f
    w_spec = pl.BlockSpec((None, d, tn), lambda j, i: (layer, 0, j))
    return pl.pallas_call(
        _ffn_up_kernel,
        grid=(f // tn, n // tm),
        in_specs=[pl.BlockSpec((tm, d), lambda j, i: (i, 0)), w_spec, w_spec],
        out_specs=pl.BlockSpec((tm, tn), lambda j, i: (i, j)),
        out_shape=jax.ShapeDtypeStruct((n, f), BF16),
        scratch_shapes=[pltpu.VMEM((d, tn), BF16), pltpu.VMEM((d, tn), BF16)],
        compiler_params=_params("arbitrary", "arbitrary"),
        name="ffn_up",
    )(h, w1, w3)


def _resid_mm_kernel(a_ref, w_ref, x_ref, g_ref, o_ref, wb_ref, *, gscale):
    _cast_on_first_row_step(((w_ref, wb_ref),))
    y = _dot(a_ref[...], wb_ref[...])
    o_ref[...] = x_ref[...] + (gscale * g_ref[...]) * y


def resid_mm(a, w, layer, x, modl, i_gate, gscale, seq):
    n, k = a.shape
    d = w.shape[2]
    tm = min(seq, max(512, 2 ** int(math.log2(12 * 2 ** 20 // (2 * k)))))
    tn = min(512, d)
    return pl.pallas_call(
        functools.partial(_resid_mm_kernel, gscale=gscale),
        grid=(d // tn, n // tm),
        in_specs=[
            pl.BlockSpec((tm, k), lambda j, i: (i, 0)),
            pl.BlockSpec((None, k, tn), lambda j, i: (layer, 0, j), pipeline_mode=pl.Buffered(1)),
            pl.BlockSpec((tm, tn), lambda j, i: (i, j)),
            pl.BlockSpec((None, 1, tn), lambda j, i: ((i * tm) // seq * N_MOD + i_gate, 0, j)),
        ],
        out_specs=pl.BlockSpec((tm, tn), lambda j, i: (i, j)),
        out_shape=jax.ShapeDtypeStruct((n, d), F32),
        scratch_shapes=[pltpu.VMEM((k, tn), BF16)],
        compiler_params=_params("arbitrary", "arbitrary"),
        name="resid_mm",
    )(a, w, x, modl)


def _widest_tile(n_chunks, chunk, bytes_per_unit, limit_bytes):
    return chunk * max(t for t in range(1, n_chunks + 1)
                       if n_chunks % t == 0 and (t == 1 or t * chunk * bytes_per_unit <= limit_bytes))


def _swap_halves(z, group):
    if group == LANES:
        return pltpu.roll(z, LANES // 2, axis=1)
    half = group // 2
    lane = lax.broadcasted_iota(jnp.int32, z.shape, 1)
    from_above = pltpu.roll(z, LANES - half, axis=1)
    from_below = pltpu.roll(z, half, axis=1)
    return jnp.where((lane & (group - 1)) < half, from_above, from_below)


def _proj_kernel(h_ref, w_ref, cos_ref, sin_ref, o_ref, wb_ref, *, group):
    _cast_on_first_row_step(((w_ref, wb_ref),))
    z = _nt_dot(h_ref[...], wb_ref[...])
    if group == 0:
        o_ref[...] = z.astype(o_ref.dtype)
        return
    cos = cos_ref[...]
    sin = sin_ref[...]
    for c in range(z.shape[1] // LANES):
        zc = z[:, c * LANES:(c + 1) * LANES]
        o_ref[:, c * LANES:(c + 1) * LANES] = (zc * cos + _swap_halves(zc, group) * sin).astype(o_ref.dtype)


def proj(h, wt, cos, sin, group, seq):
    n, d = h.shape
    cols = wt.shape[0]
    tm = min(1024, seq)
    tn = _widest_tile(cols // LANES, LANES, d * 4, 12 * 2 ** 20)
    s_blocks = seq // tm
    return pl.pallas_call(
        functools.partial(_proj_kernel, group=group),
        grid=(cols // tn, n // tm),
        in_specs=[
            pl.BlockSpec((tm, d), lambda j, i: (i, 0)),
            pl.BlockSpec((tn, d), lambda j, i: (j, 0)),
            pl.BlockSpec((tm, LANES), lambda j, i: (i % s_blocks, 0)),
            pl.BlockSpec((tm, LANES), lambda j, i: (i % s_blocks, 0)),
        ],
        out_specs=pl.BlockSpec((tm, tn), lambda j, i: (i, j)),
        out_shape=jax.ShapeDtypeStruct((n, cols), BF16),
        scratch_shapes=[pltpu.VMEM((tn, d), BF16)],
        compiler_params=_params("arbitrary", "arbitrary"),
        name=f"proj_rope{group}",
    )(h, wt, cos, sin)


def _proj_fm_kernel(w_ref, h_ref, cos_ref, sin_ref, o_ref, wt_ref, *, group, row_scales):
    _cast_on_first_row_step(((w_ref, wt_ref),))
    zt = _nt_dot(wt_ref[...], h_ref[...])
    step = group if group else LANES
    half = group // 2
    scale_of_row = [s for n_rows, s in row_scales for _ in range(n_rows // step)]
    if len(row_scales) == 1:
        scale_of_row = scale_of_row[:1] * (zt.shape[0] // step)
    for c in range(zt.shape[0] // step):
        blk = zt[c * step:(c + 1) * step]
        if group:
            swapped = jnp.concatenate([blk[half:], blk[:half]], axis=0)
            blk = blk * cos_ref[...] + swapped * sin_ref[...]
        if scale_of_row[c] != 1.0:
            blk = blk * scale_of_row[c]
        tile = o_ref.shape[2]
        for t in range(o_ref.shape[0]):
            o_ref[t, c * step:(c + 1) * step, :] = blk[:, t * tile:(t + 1) * tile].astype(o_ref.dtype)


def proj_fm(h, w, cos_fm, sin_fm, group, seq, row_scales=None):
    n, d = h.shape
    rows = w.shape[0]
    tile = min(TOKEN_TILE, seq)
    tm = min(2 * tile, seq)
    unit = max(group, LANES)
    row_scales = row_scales or ((rows, 1.0),)
    limit = 12 * 2 ** 20 if len(row_scales) == 1 else rows * d * 4
    tn = _widest_tile(rows // unit, unit, d * 4, limit)
    s_blocks = seq // tm
    g = max(group, 8)
    return pl.pallas_call(
        functools.partial(_proj_fm_kernel, group=group, row_scales=row_scales),
        grid=(rows // tn, n // tm),
        in_specs=[
            pl.BlockSpec((tn, d), lambda j, i: (j, 0), pipeline_mode=pl.Buffered(1 if tn == rows else 2)),
            pl.BlockSpec((tm, d), lambda j, i: (i, 0)),
            pl.BlockSpec((g, tm), lambda j, i: (0, i % s_blocks)),
            pl.BlockSpec((g, tm), lambda j, i: (0, i % s_blocks)),
        ],
        out_specs=pl.BlockSpec((tm // tile, tn, tile), lambda j, i: (i, j, 0)),
        out_shape=jax.ShapeDtypeStruct((n // tile, rows, tile), BF16),
        scratch_shapes=[pltpu.VMEM((tn, d), BF16)],
        compiler_params=_params("arbitrary", "arbitrary"),
        name=f"proj_fm_rope{group}",
    )(w, h, cos_fm, sin_fm)


def _small_proj_kernel(h_ref, wt_ref, bf_ref, logf_ref, wi_ref):
    zt = _nt_dot(wt_ref[...], h_ref[...])
    logf_ref[...] = jax.nn.log_sigmoid(zt[0:8, :] + bf_ref[...])
    wi_ref[...] = zt[8:16, :] * IDX_W_SCALE


def small_proj(h, w_small_t, b_forget8, seq):
    n, d = h.shape
    tm = min(512, seq)
    return pl.pallas_call(
        _small_proj_kernel,
        grid=(n // tm,),
        in_specs=[
            pl.BlockSpec((tm, d), lambda i: (i, 0)),
            pl.BlockSpec((16, d), lambda i: (0, 0)),
            pl.BlockSpec((8, 1), lambda i: (0, 0)),
        ],
        out_specs=[pl.BlockSpec((8, tm), lambda i: (0, i)), pl.BlockSpec((8, tm), lambda i: (0, i))],
        out_shape=[jax.ShapeDtypeStruct((8, n), F32), jax.ShapeDtypeStruct((8, n), F32)],
        compiler_params=_params("arbitrary"),
        name="small_proj",
    )(h, w_small_t, b_forget8)


def _split3(x):
    x1 = x.astype(BF16)
    r1 = x - x1.astype(F32)
    x2 = r1.astype(BF16)
    x3 = (r1 - x2.astype(F32)).astype(BF16)
    return x1, x2, x3


def _neg_cumsum_kernel(x_ref, hi_ref, mid_ref, lo_ref):
    x = x_ref[...]
    chunks = x.shape[0]
    r = lax.broadcasted_iota(jnp.int32, (LANES, LANES), 0)
    c = lax.broadcasted_iota(jnp.int32, (LANES, LANES), 1)
    upper = (r <= c).astype(BF16)
    within = sum(_dot(p, upper) for p in _split3(x))
    totals = jnp.broadcast_to(within[:, LANES - 1:LANES], (chunks, LANES))
    rr = lax.broadcasted_iota(jnp.int32, (chunks, chunks), 0)
    cc = lax.broadcasted_iota(jnp.int32, (chunks, chunks), 1)
    strict_lower = (cc < rr).astype(BF16)
    offset = sum(_dot(strict_lower, p) for p in _split3(totals))
    hi_ref[...], mid_ref[...], lo_ref[...] = _split3(-(within + offset) * LOG2E)


def neg_cumsum(logf_t, batch, seq):
    rows = logf_t.shape[0] * batch
    chunks = seq // LANES
    x = logf_t.reshape(rows, chunks, LANES)
    spec = pl.BlockSpec((None, chunks, LANES), lambda i: (i, 0, 0))
    pieces = pl.pallas_call(
        _neg_cumsum_kernel,
        grid=(rows,),
        in_specs=[spec],
        out_specs=[spec] * 3,
        out_shape=[jax.ShapeDtypeStruct((rows, chunks, LANES), BF16)] * 3,
        compiler_params=_params("arbitrary"),
        name="neg_cumsum",
    )(x)
    rows3 = jnp.stack([p.reshape(logf_t.shape[0], batch, seq) for p in pieces], axis=2)
    return jnp.pad(rows3, ((0, 0), (0, 0), (0, 8 - N_BIAS_PIECES), (0, 0)))


def _query_streams(tq):
    width = min(MXU_WIDTH, tq)
    return [slice(lo, lo + width) for lo in range(0, tq, width)]


ONES_ROWS = 16
ACC_ROWS = HEAD_DIM + ONES_ROWS


def _softmax_stats(s, m_prev):
    m_new = jnp.maximum(m_prev, jnp.max(s, axis=0, keepdims=True))
    return m_new, jnp.exp2(m_prev - m_new), jnp.exp2(s - m_new).astype(BF16)


def _with_ones_rows(vt):
    return jnp.concatenate([vt, jnp.ones((ONES_ROWS, vt.shape[1]), vt.dtype)], axis=0)


def _normalized(acc):
    return acc[0:HEAD_DIM] / acc[HEAD_DIM:HEAD_DIM + 1]


def _causal_mask(key0, query0, n_keys, n_queries):
    key = key0 + lax.broadcasted_iota(jnp.int32, (n_keys, n_queries), 0)
    query = query0 + lax.broadcasted_iota(jnp.int32, (n_keys, n_queries), 1)
    return key <= query


def _sweep_key_tiles(q_block, tq, tk, tile_fn):
    n_full = (q_block * tq) // tk

    def body(kj, carry):
        tile_fn(kj, False)
        return carry

    lax.fori_loop(0, n_full, body, 0)
    tile_fn(n_full, True)


def _head_sweep(nq, tq, tk, streams, raw_scores, logits, values, pipe, steps_per_trip):
    assert tq == tk
    ids = range(len(streams))
    slot_a, slot_b = pipe[:3], pipe[3:]

    for (m_ref, acc_ref), qs in streams:
        m_ref[:, :, qs] = jnp.full((nq, 1, qs.stop - qs.start), -jnp.inf, F32)
        acc_ref[:, :, qs] = jnp.zeros((nq, ACC_ROWS, qs.stop - qs.start), F32)

    def accumulate(qi, pv, rescale):
        for i in ids:
            (_, acc_ref), qs = streams[i]
            acc_ref[qi, :, qs] = rescale[i] * acc_ref[qi, :, qs] + pv[i]

    def products(kj, slot):
        vts = values(kj)
        return [_dot(vts[i], slot[1][i]) for i in ids]

    def step(cur, prev, nxt, cur_slot, nxt_slot, masked):
        raw_next = raw_scores(*nxt)
        if prev is not None:
            pv_prev = products(prev[0], nxt_slot)
            rescale_prev = [nxt_slot[2][i] for i in ids]
        for i in ids:
            nxt_slot[0][i] = raw_next[i]
        kj, qi = cur
        for i in ids:
            (m_ref, _), qs = streams[i]
            m_new, alpha, p = _softmax_stats(logits(kj, qi, i, cur_slot[0][i], masked), m_ref[qi, :, qs])
            m_ref[qi, :, qs] = m_new
            cur_slot[1][i] = p
            cur_slot[2][i] = alpha
        if prev is not None:
            accumulate(prev[1], pv_prev, rescale_prev)

    def run(n_pairs, start, advance, masked):
        if n_pairs == 0:
            return
        slots = (slot_a, slot_b)
        raw0 = raw_scores(*start)
        for i in ids:
            slot_a[0][i] = raw0[i]
        second = advance(*start)
        step(start, None, second, slot_a, slot_b, masked)

        def steps(count, cur, prev):
            for u in range(count):
                nxt = advance(*cur)
                step(cur, prev, nxt, slots[(u + 1) % 2], slots[u % 2], masked)
                prev, cur = cur, nxt
            return cur, prev

        def trip(t, carry):
            cur, prev = steps(steps_per_trip, carry[:2], carry[2:])
            return (*cur, *prev)

        carry = lax.fori_loop(0, (n_pairs - 1) // steps_per_trip, trip, (*second, *start))
        _, prev = steps((n_pairs - 1) % steps_per_trip, carry[:2], carry[2:])
        last_slot = slots[(n_pairs - 1) % 2]
        accumulate(prev[1], products(prev[0], last_slot), [last_slot[2][i] for i in ids])

    last = nq - 1
    zero = jnp.int32(0)
    run(nq, (zero, zero), lambda kj, qi: (jnp.minimum(kj + 1, last), jnp.minimum(qi + 1, last)), True)

    def next_below_diagonal(kj, qi):
        wrap = qi == last
        dist = qi - kj
        return (jnp.where(wrap, 0, kj + 1), jnp.where(wrap, jnp.minimum(dist + 1, last), qi + 1))

    run(nq * (nq - 1) // 2, (zero, jnp.int32(min(1, last))), next_below_diagonal, False)


def _head_softmax_scratch(nq, tq):
    return [pltpu.VMEM((nq, 1, tq), F32), pltpu.VMEM((nq, ACC_ROWS, tq), F32)]


def _pipe_scratch(n_streams, tk, width):
    slot = lambda: [pltpu.VMEM((n_streams, tk, width), F32), pltpu.VMEM((n_streams, tk, width), BF16),
                    pltpu.VMEM((n_streams, 1, width), F32)]
    return slot() + slot()


N_BIAS_PIECES = 3


def _fox_kernel(qt_ref, k_ref, bias_ref, vt_ref, o_ref, m_ref, acc_ref, qa_ref, kb_ref, *pipe, nq, tq, tk):
    slices = _query_streams(tq)
    row = lax.broadcasted_iota(jnp.int32, (HEAD_DIM, tq), 0)
    ones_rows = jnp.where(row < N_BIAS_PIECES, 1.0, 0.0).astype(BF16)
    zeros = jnp.zeros((HEAD_DIM - 8, tk), F32)
    for j in range(nq):
        qa_ref[j] = jnp.concatenate([qt_ref[j], ones_rows], axis=0)
        rows = bias_ref[:, j * tk:(j + 1) * tk].astype(F32)
        kb_ref[j] = jnp.concatenate([rows, zeros], axis=0).T.astype(BF16)

    def raw_scores(kj, qi):
        k = jnp.concatenate([k_ref[pl.ds(pl.multiple_of(kj * tk, tk), tk), :], kb_ref[kj]], axis=1)
        return [_dot(k, qa_ref[qi, :, qs]) for qs in slices]

    def logits(kj, qi, i, raw, masked):
        qs = slices[i]
        if masked:
            raw = jnp.where(_causal_mask(kj * tk, qi * tq + qs.start, tk, qs.stop - qs.start), raw, NEG_INF)
        return raw

    def values(kj):
        return [_with_ones_rows(vt_ref[kj])] * len(slices)

    _head_sweep(nq, tq, tk, [((m_ref, acc_ref), qs) for qs in slices], raw_scores, logits, values, pipe, 4)
    for j in range(nq):
        o_ref[j * tq:(j + 1) * tq, :] = _normalized(acc_ref[j]).T.astype(o_ref.dtype)


def fox_attention(q_fm, q_blk0, k_rows, bias_rows, v_fm, v_blk0, seq):
    b = k_rows.shape[0]
    tq = tk = q_fm.shape[2]
    nq = seq // tq
    slices = _query_streams(tq)
    return pl.pallas_call(
        functools.partial(_fox_kernel, nq=nq, tq=tq, tk=tk),
        grid=(b, H_FOX),
        in_specs=[
            pl.BlockSpec((nq, HEAD_DIM, tq), lambda bi, h: (bi, q_blk0 + h, 0)),
            pl.BlockSpec((None, seq, HEAD_DIM), lambda bi, h: (bi, 0, h)),
            pl.BlockSpec((None, None, 8, seq), lambda bi, h: (h, bi, 0, 0)),
            pl.BlockSpec((nq, HEAD_DIM, tk), lambda bi, h: (bi, v_blk0 + h, 0)),
        ],
        out_specs=pl.BlockSpec((None, seq, HEAD_DIM), lambda bi, h: (bi, 0, h)),
        out_shape=jax.ShapeDtypeStruct((b, seq, W_FOX), BF16),
        scratch_shapes=_head_softmax_scratch(nq, tq)
        + [pltpu.VMEM((nq, 2 * HEAD_DIM, tq), BF16), pltpu.VMEM((nq, tk, HEAD_DIM), BF16)]
        + _pipe_scratch(len(slices), tk, slices[0].stop),
        compiler_params=_params("arbitrary", "arbitrary"),
        name="fox_attention",
    )(q_fm, k_rows, bias_rows, v_fm)


def _diff_kernel(lam_ref, gain_ref, qt_ref, k_ref, vt_ref, o_ref,
                 m1_ref, acc1_ref, m2_ref, acc2_ref, q1_ref, q2_ref, *pipe, nq, tq, tk, lam_init):
    zeros = jnp.zeros((DK_DIFF, tq), BF16)
    for j in range(nq):
        q1_ref[j] = jnp.concatenate([qt_ref[j, 0:DK_DIFF, :], zeros], axis=0)
        q2_ref[j] = jnp.concatenate([zeros, qt_ref[j, DK_DIFF:HEAD_DIM, :]], axis=0)
    streams, q_refs = [], []
    for qs in _query_streams(tq):
        for refs, q_ref in (((m1_ref, acc1_ref), q1_ref), ((m2_ref, acc2_ref), q2_ref)):
            streams.append((refs, qs))
            q_refs.append(q_ref)

    def raw_scores(kj, qi):
        k = k_ref[pl.ds(pl.multiple_of(kj * tk, tk), tk), :]
        return [_dot(k, q_ref[qi, :, qs]) for q_ref, (_, qs) in zip(q_refs, streams)]

    def logits(kj, qi, i, raw, masked):
        qs = streams[i][1]
        if masked:
            raw = jnp.where(_causal_mask(kj * tk, qi * tq + qs.start, tk, qs.stop - qs.start), raw, NEG_INF)
        return raw

    def values(kj):
        return [_with_ones_rows(vt_ref[kj])] * len(streams)

    _head_sweep(nq, tq, tk, streams, raw_scores, logits, values, pipe, 2)

    lam_vecs = lam_ref[...]
    dot1 = jnp.sum(lam_vecs[0:1] * lam_vecs[1:2], axis=1, keepdims=True)
    dot2 = jnp.sum(lam_vecs[2:3] * lam_vecs[3:4], axis=1, keepdims=True)
    lam = jnp.exp(dot1) - jnp.exp(dot2) + lam_init
    for j in range(nq):
        o = (_normalized(acc1_ref[j]) - lam * _normalized(acc2_ref[j])).T
        y = o * lax.rsqrt(jnp.mean(o * o, axis=-1, keepdims=True) + NORM_EPS)
        o_ref[j * tq:(j + 1) * tq, :] = ((y * gain_ref[...]) * (1.0 - lam_init)).astype(o_ref.dtype)


def diff_attention(q_fm, q_blk0, k_rows, v_fm, v_blk0, lam_vecs, subln_gain, lam_init, seq):
    b = k_rows.shape[0]
    tq = tk = q_fm.shape[2]
    nq = seq // tq
    slices = _query_streams(tq)
    return pl.pallas_call(
        functools.partial(_diff_kernel, nq=nq, tq=tq, tk=tk, lam_init=lam_init),
        grid=(b, H_DIFF),
        in_specs=[
            pl.BlockSpec((4, DK_DIFF), lambda bi, h: (0, 0)),
            pl.BlockSpec((1, HEAD_DIM), lambda bi, h: (0, 0)),
            pl.BlockSpec((nq, HEAD_DIM, tq), lambda bi, h: (bi, q_blk0 + h, 0)),
            pl.BlockSpec((None, seq, HEAD_DIM), lambda bi, h: (bi, 0, h)),
            pl.BlockSpec((nq, HEAD_DIM, tk), lambda bi, h: (bi, v_blk0 + h, 0)),
        ],
        out_specs=pl.BlockSpec((None, seq, HEAD_DIM), lambda bi, h: (bi, 0, h)),
        out_shape=jax.ShapeDtypeStruct((b, seq, W_DIFF), BF16),
        scratch_shapes=_head_softmax_scratch(nq, tq) + _head_softmax_scratch(nq, tq)
        + [pltpu.VMEM((nq, HEAD_DIM, tq), BF16), pltpu.VMEM((nq, HEAD_DIM, tq), BF16)]
        + _pipe_scratch(2 * len(slices), tk, slices[0].stop),
        compiler_params=_params("arbitrary", "arbitrary"),
        name="diff_attention",
    )(lam_vecs, subln_gain.reshape(1, HEAD_DIM), q_fm, k_rows, v_fm)


def _dsa_kernel(qt_ref, k_ref, vt_ref, qit_ref, ki_ref, wt_ref, o_ref,
                hi_ref, lo_ref, qm_ref, cut_ref, m_ref, acc_ref, *pipe,
                tq, tk, seq, n_sel):
    qb = pl.program_id(1)
    n_tiles = (qb * tq) // tk + 1
    slices = _query_streams(tq)

    zeros = jnp.zeros((D_IDX, tq), BF16)
    for h in range(H_IDX):
        qm_ref[h] = jnp.concatenate([qit_ref[h * D_IDX:(h + 1) * D_IDX, :], zeros], axis=0)

    def score_tile(kj, masked):
        kk = ki_ref[pl.ds(pl.multiple_of(kj * tk, tk), tk), :]
        for qs in slices:
            width = qs.stop - qs.start
            rel_q = [_dot(kk, qm_ref[h, :, qs]) for h in range(H_IDX)]
            score = jnp.zeros((tk, width), F32)
            for h in range(H_IDX):
                score = score + wt_ref[h:h + 1, qs] * jnp.maximum(rel_q[h], 0.0)
            if masked:
                score = jnp.where(_causal_mask(kj * tk, qb * tq + qs.start, tk, width), score, NEG_INF)
            bits = lax.bitcast_convert_type(score, jnp.int32)
            key = bits ^ ((bits >> 31) & 0x7FFFFFFF)
            hi_ref[kj, :, qs] = (key >> 16).astype(jnp.int16)
            lo_ref[kj, :, qs] = ((key & 0xFFFF) + I16_MIN).astype(jnp.int16)

    _sweep_key_tiles(qb, tq, tk, score_tile)

    n_beyond = (seq - n_tiles * tk).astype(F32)
    neg_hi, neg_lo = KEY_NEG_INF >> 16, (KEY_NEG_INF & 0xFFFF) + I16_MIN
    one, zero = jnp.int16(1), jnp.int16(0)
    as16 = lambda v: v.astype(jnp.int16)

    def count_hits(hit_fn):
        def body(kj, part):
            hit = jnp.where(hit_fn(kj), one, zero)
            for r in range(tk // 16):
                part = part + hit[r * 16:(r + 1) * 16]
            return part
        part = lax.fori_loop(0, n_tiles, body, jnp.zeros((16, tq), jnp.int16))
        return jnp.sum(part.astype(F32), axis=0, keepdims=True)

    def bisect16(enough):
        def step(s, t):
            cand = t + jnp.left_shift(jnp.int32(1), 15 - s)
            return jnp.where(enough(cand), cand, t)
        return lax.fori_loop(0, 16, step, jnp.full((1, tq), I16_MIN, jnp.int32))

    def count_hi_ge(cand):
        n = count_hits(lambda kj, c=as16(cand): hi_ref[kj] >= c)
        return n + jnp.where(cand <= neg_hi, n_beyond, 0.0)

    t_hi = bisect16(lambda cand: count_hi_ge(cand) >= n_sel)
    t_hi16 = as16(t_hi)
    n_above = jnp.where(t_hi < I16_MAX, count_hi_ge(jnp.minimum(t_hi + 1, I16_MAX)), 0.0)
    need_lo = n_sel - n_above

    def mask_lo(kj, carry):
        lo_ref[kj] = jnp.where(hi_ref[kj] == t_hi16, lo_ref[kj], jnp.int16(I16_MIN))
        return carry

    lax.fori_loop(0, n_tiles, mask_lo, 0)
    beyond_in_group = jnp.where(t_hi == neg_hi, n_beyond, 0.0)

    def count_lo_ge(cand):
        n = count_hits(lambda kj, c=as16(cand): lo_ref[kj] >= c)
        return n + jnp.where(cand <= neg_lo, beyond_in_group, 0.0)

    t_lo = bisect16(lambda cand: count_lo_ge(cand) >= need_lo)
    t_lo16 = as16(t_lo)

    def in_group(kj):
        return hi_ref[kj] == t_hi16

    n_ge = n_above + count_hits(lambda kj: in_group(kj) & (lo_ref[kj] >= t_lo16)) \
        + jnp.where(t_lo <= neg_lo, beyond_in_group, 0.0)
    surplus = n_ge - n_sel
    cut_ref[...] = jnp.full((1, tq), I16_MAX, jnp.int32)

    row_in_tile = as16(lax.broadcasted_iota(jnp.int32, (tk, tq), 0))

    def position_before(kj, bound):
        return row_in_tile < as16(jnp.minimum(bound - kj * tk, I16_MAX))

    @pl.when(jnp.max(surplus) > 0.0)
    def _():
        n_gt = n_above + count_hits(lambda kj: in_group(kj) & (lo_ref[kj] > t_lo16)) \
            + jnp.where(t_lo < neg_lo, beyond_in_group, 0.0)
        need = jnp.where(surplus > 0.0, n_sel - n_gt, float(seq + 1))

        def count_tied_below(cut):
            return count_hits(lambda kj: in_group(kj) & (lo_ref[kj] == t_lo16) & position_before(kj, cut))

        n_bits = max(1, (seq - 1).bit_length())

        def bisect_cut(step, cut):
            cand = cut + jnp.left_shift(jnp.int32(1), n_bits - 1 - step)
            return jnp.where(count_tied_below(cand) < need, cand, cut)

        cut_ref[...] = lax.fori_loop(0, n_bits, bisect_cut, jnp.zeros((1, tq), jnp.int32))

    cut_end = cut_ref[...] + 1
    zero_bias, neg_bias = jnp.zeros((), BF16), jnp.asarray(NEG_INF, BF16)

    def write_bias(kj, carry):
        hi, lo = hi_ref[kj], lo_ref[kj]
        tied_ok = (lo == t_lo16) & position_before(kj, cut_end)
        sel = (hi > t_hi16) | ((hi == t_hi16) & ((lo > t_lo16) | tied_ok))
        hi_ref[kj] = lax.bitcast_convert_type(jnp.where(sel, zero_bias, neg_bias), jnp.int16)
        return carry

    lax.fori_loop(0, n_tiles, write_bias, 0)

    n_full = n_tiles - 1
    ids = range(len(slices))
    slots = (pipe[:3], pipe[3:])
    head_rows = lambda h: slice(h * HEAD_DIM, (h + 1) * HEAD_DIM)
    m_ref[...] = jnp.full(m_ref.shape, -jnp.inf, F32)
    acc_ref[...] = jnp.zeros(acc_ref.shape, F32)

    def raw_scores(kj, h):
        k = k_ref[pl.ds(pl.multiple_of(kj * tk, tk), tk), head_rows(h)]
        return [_dot(k, qt_ref[head_rows(h), qs]) for qs in slices]

    def products(kj, h, slot):
        vt = _with_ones_rows(vt_ref[kj, head_rows(h), :])
        return [_dot(vt, slot[1][i]) for i in ids]

    def accumulate(h, pv, rescale):
        for i, qs in enumerate(slices):
            acc_ref[h, :, qs] = rescale[i] * acc_ref[h, :, qs] + pv[i]

    def step(kj, h, first=False, last=False, masked=False):
        cur_slot, nxt_slot = slots[h % 2], slots[(h + 1) % 2]
        nxt = (kj, h + 1) if h + 1 < H_DSA else (kj + 1, 0)
        prev = (kj, h - 1) if h > 0 else (kj - 1, H_DSA - 1)
        raw_next = None if last else raw_scores(*nxt)
        if not first:
            pv_prev = products(*prev, nxt_slot)
            rescale_prev = [nxt_slot[2][i] for i in ids]
        if not last:
            for i in ids:
                nxt_slot[0][i] = raw_next[i]
        for i, qs in enumerate(slices):
            s = cur_slot[0][i] + lax.bitcast_convert_type(hi_ref[kj, :, qs], BF16).astype(F32)
            if masked:
                s = jnp.where(_causal_mask(kj * tk, qb * tq + qs.start, tk, qs.stop - qs.start), s, NEG_INF)
            m_new, alpha, p = _softmax_stats(s, m_ref[h, :, qs])
            m_ref[h, :, qs] = m_new
            cur_slot[1][i] = p
            cur_slot[2][i] = alpha
        if not first:
            accumulate(prev[1], pv_prev, rescale_prev)

    def tile_steps(kj, first=False, diagonal=False):
        for h in range(H_DSA):
            step(kj, h, first=first and h == 0, last=diagonal and h == H_DSA - 1, masked=diagonal)

    raw0 = raw_scores(0, 0)
    for i in ids:
        slots[0][0][i] = raw0[i]

    @pl.when(n_full == 0)
    def _():
        tile_steps(0, first=True, diagonal=True)

    @pl.when(n_full > 0)
    def _():
        tile_steps(0, first=True)

        def body(kj, carry):
            tile_steps(kj)
            return carry

        lax.fori_loop(1, n_full, body, 0)
        tile_steps(n_full, diagonal=True)

    last_slot = slots[(H_DSA - 1) % 2]
    accumulate(H_DSA - 1, products(n_full, H_DSA - 1, last_slot), [last_slot[2][i] for i in ids])
    for h in range(H_DSA):
        o_ref[:, head_rows(h)] = _normalized(acc_ref[h]).T.astype(o_ref.dtype)


def dsa_attention(q_fm, k_rows, v_fm, qi_fm, ki_rows, ki_blk, wi_t, seq):
    b = k_rows.shape[0]
    tq = tk = q_fm.shape[2]
    nq = seq // tq
    n_sel = min(TOPK_MAX, seq // 4)
    once = pl.Buffered(1)
    slices = _query_streams(tq)
    fm_q = lambda rows: pl.BlockSpec((None, rows, tq), lambda bi, i: (bi * nq + i, 0, 0))
    return pl.pallas_call(
        functools.partial(_dsa_kernel, tq=tq, tk=tk, seq=seq, n_sel=n_sel),
        grid=(b, nq),
        in_specs=[
            fm_q(W_DSA),
            pl.BlockSpec((None, seq, W_DSA), lambda bi, i: (bi, 0, 0), pipeline_mode=once),
            pl.BlockSpec((nq, W_DSA, tk), lambda bi, i: (bi, 0, 0), pipeline_mode=once),
            fm_q(W_IDX),
            pl.BlockSpec((None, seq, LANES), lambda bi, i: (bi, 0, ki_blk), pipeline_mode=once),
            pl.BlockSpec((H_IDX, tq), lambda bi, i: (0, bi * nq + i)),
        ],
        out_specs=pl.BlockSpec((None, tq, W_DSA), lambda bi, i: (bi, i, 0)),
        out_shape=jax.ShapeDtypeStruct((b, seq, W_DSA), BF16),
        scratch_shapes=[
            pltpu.VMEM((nq, tk, tq), jnp.int16),
            pltpu.VMEM((nq, tk, tq), jnp.int16),
            pltpu.VMEM((H_IDX, 2 * D_IDX, tq), BF16),
            pltpu.VMEM((1, tq), jnp.int32),
        ] + _head_softmax_scratch(H_DSA, tq) + _pipe_scratch(len(slices), tk, slices[0].stop),
        compiler_params=_params("arbitrary", "arbitrary"),
        name="dsa_attention",
    )(q_fm, k_rows, v_fm, qi_fm, ki_rows, wi_t)


def _merge_kernel(h_ref, oa_ref, ob_ref, oc_ref, mw0_ref, mw1_ref, mw2_ref, mb_ref, wa_ref, wb_ref, wc_ref, o_ref,
                  *bf16_refs):
    f32_refs = (mw0_ref, mw1_ref, mw2_ref, wa_ref, wb_ref, wc_ref)
    _cast_on_first_row_step(tuple(zip(f32_refs, bf16_refs)))
    gate_w, branch_w = bf16_refs[:N_BRANCH], bf16_refs[N_BRANCH:]
    h = h_ref[...]
    merged = None
    for i, o_b_ref in enumerate((oa_ref, ob_ref, oc_ref)):
        gate = jax.nn.sigmoid(_dot(h, gate_w[i][...]) + mb_ref[i])
        term = gate * _dot(o_b_ref[...], branch_w[i][...])
        merged = term if merged is None else merged + term
    o_ref[...] = merged.astype(o_ref.dtype)


def merge_branches(h, oa, ob, oc, merge_w, merge_b, wa, wb, wc, layer, seq):
    n, d = h.shape
    tm = min(1024, seq)
    tn = min(256, d)
    nj = d // tn
    row = lambda width: pl.BlockSpec((tm, width), lambda j, i: (i, 0))
    col = lambda rows: pl.BlockSpec((None, rows, tn), lambda j, i: (layer, 0, j))
    gate_w = lambda g: pl.BlockSpec((None, d, tn), lambda j, i: (layer, 0, g * nj + j))
    widths = (oa.shape[1], ob.shape[1], oc.shape[1])
    return pl.pallas_call(
        _merge_kernel,
        grid=(nj, n // tm),
        in_specs=[
            row(d), row(widths[0]), row(widths[1]), row(widths[2]),
            gate_w(0), gate_w(1), gate_w(2),
            pl.BlockSpec((None, N_BRANCH, 1, tn), lambda j, i: (layer, 0, 0, j)),
            col(widths[0]), col(widths[1]), col(widths[2]),
        ],
        out_specs=pl.BlockSpec((tm, tn), lambda j, i: (i, j)),
        out_shape=jax.ShapeDtypeStruct((n, d), BF16),
        scratch_shapes=[pltpu.VMEM((d, tn), BF16)] * N_BRANCH + [pltpu.VMEM((w, tn), BF16) for w in widths],
        compiler_params=_params("arbitrary", "arbitrary"),
        name="merge_branches",
    )(h, oa, ob, oc, merge_w, merge_w, merge_w, merge_b.reshape(merge_b.shape[0], N_BRANCH, 1, d), wa, wb, wc)


def _rope_tables(seq, dim):
    inv_freq = 1.0 / (ROPE_THETA ** (jnp.arange(0, dim, 2, dtype=F32) / dim))
    ang = jnp.arange(seq, dtype=F32)[:, None] * inv_freq[None, :]
    cos, sin = lax.optimization_barrier((jnp.cos(ang), jnp.sin(ang)))
    cos_g = jnp.concatenate([cos, cos], axis=1)
    sin_g = jnp.concatenate([-sin, sin], axis=1)
    reps = LANES // dim
    return jnp.tile(cos_g, (1, reps)), jnp.tile(sin_g, (1, reps)), cos_g.T, sin_g.T


def _split_w_in(w_in_t):
    sizes = (W_DIFF, W_DIFF, W_DIFF, W_FOX, W_FOX, W_FOX, H_FOX, W_DSA, W_DSA, W_DSA, W_IDX, D_IDX, H_IDX)
    parts, start = [], 0
    for n in sizes:
        parts.append(w_in_t[start:start + n])
        start += n
    return parts


def kernel(x, c, w_ada, b_ada, norm_ffn1, ffn1_w1, ffn1_w3, ffn1_w2, norm_mix, w_in, b_forget, lam_q1, lam_k1, lam_q2, lam_k2, subln_gain, merge_w, merge_b, w_branch_a, w_branch_b, w_branch_c, w_out, norm_ffn2, ffn2_w1, ffn2_w3, ffn2_w2, norm_final):
    batch, seq, d = x.shape
    depth = w_ada.shape[0]
    n = batch * seq
    assert seq % min(TOKEN_TILE, seq) == 0 and seq % LANES == 0 and d % LANES == 0
    assert seq <= I16_MAX + 1, "the top-k tie rule compares key positions as int16"
    bf = lambda a: a.astype(BF16)

    cos64, sin64, cos64_fm, sin64_fm = _rope_tables(seq, DK_DIFF)
    cos128, sin128, cos128_fm, sin128_fm = _rope_tables(seq, HEAD_DIM)
    mod = adaln_mod(c, w_ada, b_ada)
    xf = x.reshape(n, d)

    for l in range(depth):
        lam_init = 0.8 - 0.6 * math.exp(-0.3 * l)
        modl = mod[l].reshape(batch * N_MOD, 1, d)

        h = norm_mod(xf, norm_ffn1[l], modl, 0, seq)
        u = ffn_up(h, ffn1_w1, ffn1_w3, l)
        xf = resid_mm(u, ffn1_w2, l, xf, modl, 2, 0.5, seq)

        h = norm_mod(xf, norm_mix[l], modl, 3, seq)
        qa, ka, va, qb, kb, vb, fb, qc, kc, vc, qi, ki, wi = _split_w_in(jnp.transpose(w_in[l]))
        k64 = proj(h, jnp.concatenate([ka, ki, ki], axis=0), cos64, sin64, DK_DIFF, seq).reshape(batch, seq, -1)
        k128 = proj(h, kc, cos128, sin128, HEAD_DIM, seq).reshape(batch, seq, -1)
        k0 = proj(h, kb, cos64, sin64, 0, seq).reshape(batch, seq, -1)
        log2e_over_sqrt = lambda width: (width ** -0.5) * LOG2E
        q64_fm = proj_fm(h, jnp.concatenate([qi, qa], axis=0), cos64_fm, sin64_fm, DK_DIFF, seq,
                         ((W_IDX, 1.0), (W_DIFF, log2e_over_sqrt(DK_DIFF))))
        q128_fm = proj_fm(h, qc, cos128_fm, sin128_fm, HEAD_DIM, seq, ((W_DSA, log2e_over_sqrt(HEAD_DIM)),))
        qb_fm = proj_fm(h, qb, cos64_fm, sin64_fm, 0, seq, ((W_FOX, log2e_over_sqrt(HEAD_DIM)),))
        v_fm = proj_fm(h, jnp.concatenate([vc, va, vb], axis=0), cos64_fm, sin64_fm, 0, seq)
        blk = lambda rows: rows // HEAD_DIM
        qa_blk, va_blk, vb_blk = blk(W_IDX), blk(W_DSA), blk(W_DSA + W_DIFF)

        w_small_t = bf(jnp.concatenate([fb, jnp.zeros((8 - H_FOX, d), F32), wi], axis=0))
        b_forget8 = jnp.concatenate([b_forget[l], jnp.zeros((8 - H_FOX,), F32)]).reshape(8, 1)
        logf_t, wi_t = small_proj(h, w_small_t, b_forget8, seq)
        bias_rows = neg_cumsum(logf_t, batch, seq)

        lam_vecs = jnp.stack([lam_q1[l], lam_k1[l], lam_q2[l], lam_k2[l]])
        oa = diff_attention(q64_fm, qa_blk, k64, v_fm, va_blk, lam_vecs, subln_gain[l], lam_init, seq)
        ob = fox_attention(qb_fm, 0, k0, bias_rows, v_fm, vb_blk, seq)
        oc = dsa_attention(q128_fm, k128, v_fm, q64_fm, k64, blk(W_DIFF), wi_t, seq)

        merged = merge_branches(h, oa.reshape(n, -1), ob.reshape(n, -1), oc.reshape(n, -1), merge_w, merge_b,
                                w_branch_a, w_branch_b, w_branch_c, l, seq)
        xf = resid_mm(merged, w_out, l, xf, modl, 5, 1.0, seq)

        h = norm_mod(xf, norm_ffn2[l], modl, 6, seq)
        u = ffn_up(h, ffn2_w1, ffn2_w3, l)
        xf = resid_mm(u, ffn2_w2, l, xf, modl, 8, 0.5, seq)

    return final_norm(xf, norm_final).reshape(batch, seq, d)
```

```python
import functools
import math

import numpy as np
import jax
import jax.numpy as jnp
from jax import lax
from jax.experimental import pallas as pl
from jax.experimental.pallas import tpu as pltpu

HEAD_DIM = 128
H_DIFF = 6
DK_DIFF = HEAD_DIM // 2
H_FOX = 6
H_DSA = 4
H_IDX = 8
D_IDX = 64
TOPK_MAX = 256
ROPE_THETA = 10000.0
NORM_EPS = 1e-6
N_BRANCH = 3
N_MOD = 9
NEG_INF = -1e30
IDX_W_SCALE = (H_IDX ** -0.5) * (D_IDX ** -0.5)
LOG2E = math.log2(math.e)

W_DIFF = H_DIFF * HEAD_DIM
W_FOX = H_FOX * HEAD_DIM
W_DSA = H_DSA * HEAD_DIM
W_IDX = H_IDX * D_IDX

LANES = 128
MXU_WIDTH = 256
VMEM_LIMIT = 56 * 1024 * 1024
TOKEN_TILE = 512
I16_MIN, I16_MAX = -(2 ** 15), 2 ** 15 - 1

BF16 = jnp.bfloat16
F32 = jnp.float32


def _order_key_of(value):
    bits = int(np.array(value, np.float32).view(np.int32))
    return bits ^ ((bits >> 31) & 0x7FFFFFFF)


KEY_NEG_INF = _order_key_of(NEG_INF)


def _params(*semantics):
    return pltpu.CompilerParams(dimension_semantics=semantics, vmem_limit_bytes=VMEM_LIMIT)


def _nt_dot(a, b):
    return lax.dot_general(a, b, (((1,), (1,)), ((), ())), preferred_element_type=F32)


def _dot(a, b):
    return jnp.dot(a, b, preferred_element_type=F32)


def _adaln_kernel(c_ref, w_ref, b_ref, o_ref):
    c = c_ref[...]
    c_act = (c * jax.nn.sigmoid(c)).astype(BF16)
    o_ref[...] = _dot(c_act, w_ref[...].astype(BF16)) + b_ref[...]


def adaln_mod(c, w_ada, b_ada):
    depth, d, nd = w_ada.shape
    b = c.shape[0]
    tn = min(1024, d)
    return pl.pallas_call(
        _adaln_kernel,
        grid=(depth, nd // tn),
        in_specs=[
            pl.BlockSpec((b, d), lambda l, j: (0, 0)),
            pl.BlockSpec((None, d, tn), lambda l, j: (l, 0, j)),
            pl.BlockSpec((None, 1, tn), lambda l, j: (l, 0, j)),
        ],
        out_specs=pl.BlockSpec((None, b, tn), lambda l, j: (l, 0, j)),
        out_shape=jax.ShapeDtypeStruct((depth, b, nd), F32),
        compiler_params=_params("arbitrary", "arbitrary"),
        name="adaln_mod",
    )(c, w_ada, b_ada.reshape(depth, 1, nd))


def _norm_mod_kernel(x_ref, gain_ref, sc_ref, sh_ref, o_ref):
    x = x_ref[...]
    y = x * lax.rsqrt(jnp.mean(x * x, axis=-1, keepdims=True) + NORM_EPS)
    o_ref[...] = ((y * gain_ref[...]) * (1.0 + sc_ref[...]) + sh_ref[...]).astype(o_ref.dtype)


def norm_mod(x, gain, modl, i_shift, seq):
    n, d = x.shape
    tm = min(512, seq)
    return pl.pallas_call(
        _norm_mod_kernel,
        grid=(n // tm,),
        in_specs=[
            pl.BlockSpec((tm, d), lambda i: (i, 0)),
            pl.BlockSpec((1, d), lambda i: (0, 0)),
            pl.BlockSpec((None, 1, d), lambda i: ((i * tm) // seq * N_MOD + i_shift + 1, 0, 0)),
            pl.BlockSpec((None, 1, d), lambda i: ((i * tm) // seq * N_MOD + i_shift, 0, 0)),
        ],
        out_specs=pl.BlockSpec((tm, d), lambda i: (i, 0)),
        out_shape=jax.ShapeDtypeStruct((n, d), BF16),
        compiler_params=_params("arbitrary"),
        name="norm_mod",
    )(x, gain.reshape(1, d), modl, modl)


def _final_norm_kernel(x_ref, gain_ref, o_ref):
    x = x_ref[...]
    y = x * lax.rsqrt(jnp.mean(x * x, axis=-1, keepdims=True) + NORM_EPS)
    o_ref[...] = y * gain_ref[...]


def final_norm(x, gain):
    n, d = x.shape
    tm = min(512, n)
    return pl.pallas_call(
        _final_norm_kernel,
        grid=(n // tm,),
        in_specs=[pl.BlockSpec((tm, d), lambda i: (i, 0)), pl.BlockSpec((1, d), lambda i: (0, 0))],
        out_specs=pl.BlockSpec((tm, d), lambda i: (i, 0)),
        out_shape=jax.ShapeDtypeStruct((n, d), F32),
        compiler_params=_params("arbitrary"),
        name="final_norm",
    )(x, gain.reshape(1, d))


def _cast_on_first_row_step(pairs):
    @pl.when(pl.program_id(1) == 0)
    def _():
        for src_ref, dst_ref in pairs:
            dst_ref[...] = src_ref[...].astype(dst_ref.dtype)


def _ffn_up_kernel(h_ref, w1_ref, w3_ref, o_ref, w1b_ref, w3b_ref):
    _cast_on_first_row_step(((w1_ref, w1b_ref), (w3_ref, w3b_ref)))
    h = h_ref[...]
    a = _dot(h, w1b_ref[...])
    b = _dot(h, w3b_ref[...])
    o_ref[...] = ((a * jax.nn.sigmoid(a)) * b).astype(o_ref.dtype)


def ffn_up(h, w1, w3, layer):
    n, d = h.shape
    f = w1.shape[2]
    tm = min(1024, n)
    tn = 512 if f % 512 == 0 else f
    w_spec = pl.BlockSpec((None, d, tn), lambda j, i: (layer, 0, j))
    return pl.pallas_call(
        _ffn_up_kernel,
        grid=(f // tn, n // tm),
        in_specs=[pl.BlockSpec((tm, d), lambda j, i: (i, 0)), w_spec, w_spec],
        out_specs=pl.BlockSpec((tm, tn), lambda j, i: (i, j)),
        out_shape=jax.ShapeDtypeStruct((n, f), BF16),
        scratch_shapes=[pltpu.VMEM((d, tn), BF16), pltpu.VMEM((d, tn), BF16)],
        compiler_params=_params("arbitrary", "arbitrary"),
        name="ffn_up",
    )(h, w1, w3)


def _resid_mm_kernel(a_ref, w_ref, x_ref, g_ref, o_ref, wb_ref, *, gscale):
    _cast_on_first_row_step(((w_ref, wb_ref),))
    y = _dot(a_ref[...], wb_ref[...])
    o_ref[...] = x_ref[...] + (gscale * g_ref[...]) * y


def resid_mm(a, w, layer, x, modl, i_gate, gscale, seq):
    n, k = a.shape
    d = w.shape[2]
    tm = min(seq, max(512, 2 ** int(math.log2(12 * 2 ** 20 // (2 * k)))))
    tn = min(512, d)
    return pl.pallas_call(
        functools.partial(_resid_mm_kernel, gscale=gscale),
        grid=(d // tn, n // tm),
        in_specs=[
            pl.BlockSpec((tm, k), lambda j, i: (i, 0)),
            pl.BlockSpec((None, k, tn), lambda j, i: (layer, 0, j), pipeline_mode=pl.Buffered(1)),
            pl.BlockSpec((tm, tn), lambda j, i: (i, j)),
            pl.BlockSpec((None, 1, tn), lambda j, i: ((i * tm) // seq * N_MOD + i_gate, 0, j)),
        ],
        out_specs=pl.BlockSpec((tm, tn), lambda j, i: (i, j)),
        out_shape=jax.ShapeDtypeStruct((n, d), F32),
        scratch_shapes=[pltpu.VMEM((k, tn), BF16)],
        compiler_params=_params("arbitrary", "arbitrary"),
        name="resid_mm",
    )(a, w, x, modl)


def _widest_tile(n_chunks, chunk, bytes_per_unit, limit_bytes):
    return chunk * max(t for t in range(1, n_chunks + 1)
                       if n_chunks % t == 0 and (t == 1 or t * chunk * bytes_per_unit <= limit_bytes))


def _swap_halves(z, group):
    if group == LANES:
        return pltpu.roll(z, LANES // 2, axis=1)
    half = group // 2
    lane = lax.broadcasted_iota(jnp.int32, z.shape, 1)
    from_above = pltpu.roll(z, LANES - half, axis=1)
    from_below = pltpu.roll(z, half, axis=1)
    return jnp.where((lane & (group - 1)) < half, from_above, from_below)


def _proj_kernel(h_ref, w_ref, cos_ref, sin_ref, o_ref, wb_ref, *, group):
    _cast_on_first_row_step(((w_ref, wb_ref),))
    z = _nt_dot(h_ref[...], wb_ref[...])
    if group == 0:
        o_ref[...] = z.astype(o_ref.dtype)
        return
    cos = cos_ref[...]
    sin = sin_ref[...]
    for c in range(z.shape[1] // LANES):
        zc = z[:, c * LANES:(c + 1) * LANES]
        o_ref[:, c * LANES:(c + 1) * LANES] = (zc * cos + _swap_halves(zc, group) * sin).astype(o_ref.dtype)


def proj(h, wt, cos, sin, group, seq):
    n, d = h.shape
    cols = wt.shape[0]
    tm = min(1024, seq)
    tn = _widest_tile(cols // LANES, LANES, d * 4, 12 * 2 ** 20)
    s_blocks = seq // tm
    return pl.pallas_call(
        functools.partial(_proj_kernel, group=group),
        grid=(cols // tn, n // tm),
        in_specs=[
            pl.BlockSpec((tm, d), lambda j, i: (i, 0)),
            pl.BlockSpec((tn, d), lambda j, i: (j, 0)),
            pl.BlockSpec((tm, LANES), lambda j, i: (i % s_blocks, 0)),
            pl.BlockSpec((tm, LANES), lambda j, i: (i % s_blocks, 0)),
        ],
        out_specs=pl.BlockSpec((tm, tn), lambda j, i: (i, j)),
        out_shape=jax.ShapeDtypeStruct((n, cols), BF16),
        scratch_shapes=[pltpu.VMEM((tn, d), BF16)],
        compiler_params=_params("arbitrary", "arbitrary"),
        name=f"proj_rope{group}",
    )(h, wt, cos, sin)


def _proj_fm_kernel(w_ref, h_ref, cos_ref, sin_ref, o_ref, wt_ref, *, group, row_scales):
    _cast_on_first_row_step(((w_ref, wt_ref),))
    zt = _nt_dot(wt_ref[...], h_ref[...])
    step = group if group else LANES
    half = group // 2
    scale_of_row = [s for n_rows, s in row_scales for _ in range(n_rows // step)]
    if len(row_scales) == 1:
        scale_of_row = scale_of_row[:1] * (zt.shape[0] // step)
    for c in range(zt.shape[0] // step):
        blk = zt[c * step:(c + 1) * step]
        if group:
            swapped = jnp.concatenate([blk[half:], blk[:half]], axis=0)
            blk = blk * cos_ref[...] + swapped * sin_ref[...]
        if scale_of_row[c] != 1.0:
            blk = blk * scale_of_row[c]
        tile = o_ref.shape[2]
        for t in range(o_ref.shape[0]):
            o_ref[t, c * step:(c + 1) * step, :] = blk[:, t * tile:(t + 1) * tile].astype(o_ref.dtype)


def proj_fm(h, w, cos_fm, sin_fm, group, seq, row_scales=None):
    n, d = h.shape
    rows = w.shape[0]
    tile = min(TOKEN_TILE, seq)
    tm = min(2 * tile, seq)
    unit = max(group, LANES)
    row_scales = row_scales or ((rows, 1.0),)
    limit = 12 * 2 ** 20 if len(row_scales) == 1 else rows * d * 4
    tn = _widest_tile(rows // unit, unit, d * 4, limit)
    s_blocks = seq // tm
    g = max(group, 8)
    return pl.pallas_call(
        functools.partial(_proj_fm_kernel, group=group, row_scales=row_scales),
        grid=(rows // tn, n // tm),
        in_specs=[
            pl.BlockSpec((tn, d), lambda j, i: (j, 0), pipeline_mode=pl.Buffered(1 if tn == rows else 2)),
            pl.BlockSpec((tm, d), lambda j, i: (i, 0)),
            pl.BlockSpec((g, tm), lambda j, i: (0, i % s_blocks)),
            pl.BlockSpec((g, tm), lambda j, i: (0, i % s_blocks)),
        ],
        out_specs=pl.BlockSpec((tm // tile, tn, tile), lambda j, i: (i, j, 0)),
        out_shape=jax.ShapeDtypeStruct((n // tile, rows, tile), BF16),
        scratch_shapes=[pltpu.VMEM((tn, d), BF16)],
        compiler_params=_params("arbitrary", "arbitrary"),
        name=f"proj_fm_rope{group}",
    )(w, h, cos_fm, sin_fm)


def _small_proj_kernel(h_ref, wt_ref, bf_ref, logf_ref, wi_ref):
    zt = _nt_dot(wt_ref[...], h_ref[...])
    logf_ref[...] = jax.nn.log_sigmoid(zt[0:8, :] + bf_ref[...])
    wi_ref[...] = zt[8:16, :] * IDX_W_SCALE


def small_proj(h, w_small_t, b_forget8, seq):
    n, d = h.shape
    tm = min(512, seq)
    return pl.pallas_call(
        _small_proj_kernel,
        grid=(n // tm,),
        in_specs=[
            pl.BlockSpec((tm, d), lambda i: (i, 0)),
            pl.BlockSpec((16, d), lambda i: (0, 0)),
            pl.BlockSpec((8, 1), lambda i: (0, 0)),
        ],
        out_specs=[pl.BlockSpec((8, tm), lambda i: (0, i)), pl.BlockSpec((8, tm), lambda i: (0, i))],
        out_shape=[jax.ShapeDtypeStruct((8, n), F32), jax.ShapeDtypeStruct((8, n), F32)],
        compiler_params=_params("arbitrary"),
        name="small_proj",
    )(h, w_small_t, b_forget8)


def _split3(x):
    x1 = x.astype(BF16)
    r1 = x - x1.astype(F32)
    x2 = r1.astype(BF16)
    x3 = (r1 - x2.astype(F32)).astype(BF16)
    return x1, x2, x3


def _neg_cumsum_kernel(x_ref, hi_ref, mid_ref, lo_ref):
    x = x_ref[...]
    chunks = x.shape[0]
    r = lax.broadcasted_iota(jnp.int32, (LANES, LANES), 0)
    c = lax.broadcasted_iota(jnp.int32, (LANES, LANES), 1)
    upper = (r <= c).astype(BF16)
    within = sum(_dot(p, upper) for p in _split3(x))
    totals = jnp.broadcast_to(within[:, LANES - 1:LANES], (chunks, LANES))
    rr = lax.broadcasted_iota(jnp.int32, (chunks, chunks), 0)
    cc = lax.broadcasted_iota(jnp.int32, (chunks, chunks), 1)
    strict_lower = (cc < rr).astype(BF16)
    offset = sum(_dot(strict_lower, p) for p in _split3(totals))
    hi_ref[...], mid_ref[...], lo_ref[...] = _split3(-(within + offset) * LOG2E)


def neg_cumsum(logf_t, batch, seq):
    rows = logf_t.shape[0] * batch
    chunks = seq // LANES
    x = logf_t.reshape(rows, chunks, LANES)
    spec = pl.BlockSpec((None, chunks, LANES), lambda i: (i, 0, 0))
    pieces = pl.pallas_call(
        _neg_cumsum_kernel,
        grid=(rows,),
        in_specs=[spec],
        out_specs=[spec] * 3,
        out_shape=[jax.ShapeDtypeStruct((rows, chunks, LANES), BF16)] * 3,
        compiler_params=_params("arbitrary"),
        name="neg_cumsum",
    )(x)
    rows3 = jnp.stack([p.reshape(logf_t.shape[0], batch, seq) for p in pieces], axis=2)
    return jnp.pad(rows3, ((0, 0), (0, 0), (0, 8 - N_BIAS_PIECES), (0, 0)))


def _query_streams(tq):
    width = min(MXU_WIDTH, tq)
    return [slice(lo, lo + width) for lo in range(0, tq, width)]


ONES_ROWS = 16
ACC_ROWS = HEAD_DIM + ONES_ROWS


def _softmax_stats(s, m_prev):
    m_new = jnp.maximum(m_prev, jnp.max(s, axis=0, keepdims=True))
    return m_new, jnp.exp2(m_prev - m_new), jnp.exp2(s - m_new).astype(BF16)


def _with_ones_rows(vt):
    return jnp.concatenate([vt, jnp.ones((ONES_ROWS, vt.shape[1]), vt.dtype)], axis=0)


def _normalized(acc):
    return acc[0:HEAD_DIM] / acc[HEAD_DIM:HEAD_DIM + 1]


def _causal_mask(key0, query0, n_keys, n_queries):
    key = key0 + lax.broadcasted_iota(jnp.int32, (n_keys, n_queries), 0)
    query = query0 + lax.broadcasted_iota(jnp.int32, (n_keys, n_queries), 1)
    return key <= query


def _sweep_key_tiles(q_block, tq, tk, tile_fn):
    n_full = (q_block * tq) // tk

    def body(kj, carry):
        tile_fn(kj, False)
        return carry

    lax.fori_loop(0, n_full, body, 0)
    tile_fn(n_full, True)


def _head_sweep(nq, tq, tk, streams, raw_scores, logits, values, pipe, steps_per_trip):
    assert tq == tk
    ids = range(len(streams))
    slot_a, slot_b = pipe[:3], pipe[3:]

    for (m_ref, acc_ref), qs in streams:
        m_ref[:, :, qs] = jnp.full((nq, 1, qs.stop - qs.start), -jnp.inf, F32)
        acc_ref[:, :, qs] = jnp.zeros((nq, ACC_ROWS, qs.stop - qs.start), F32)

    def accumulate(qi, pv, rescale):
        for i in ids:
            (_, acc_ref), qs = streams[i]
            acc_ref[qi, :, qs] = rescale[i] * acc_ref[qi, :, qs] + pv[i]

    def products(kj, slot):
        vts = values(kj)
        return [_dot(vts[i], slot[1][i]) for i in ids]

    def step(cur, prev, nxt, cur_slot, nxt_slot, masked):
        raw_next = raw_scores(*nxt)
        if prev is not None:
            pv_prev = products(prev[0], nxt_slot)
            rescale_prev = [nxt_slot[2][i] for i in ids]
        for i in ids:
            nxt_slot[0][i] = raw_next[i]
        kj, qi = cur
        for i in ids:
            (m_ref, _), qs = streams[i]
            m_new, alpha, p = _softmax_stats(logits(kj, qi, i, cur_slot[0][i], masked), m_ref[qi, :, qs])
            m_ref[qi, :, qs] = m_new
            cur_slot[1][i] = p
            cur_slot[2][i] = alpha
        if prev is not None:
            accumulate(prev[1], pv_prev, rescale_prev)

    def run(n_pairs, start, advance, masked):
        if n_pairs == 0:
            return
        slots = (slot_a, slot_b)
        raw0 = raw_scores(*start)
        for i in ids:
            slot_a[0][i] = raw0[i]
        second = advance(*start)
        step(start, None, second, slot_a, slot_b, masked)

        def steps(count, cur, prev):
            for u in range(count):
                nxt = advance(*cur)
                step(cur, prev, nxt, slots[(u + 1) % 2], slots[u % 2], masked)
                prev, cur = cur, nxt
            return cur, prev

        def trip(t, carry):
            cur, prev = steps(steps_per_trip, carry[:2], carry[2:])
            return (*cur, *prev)

        carry = lax.fori_loop(0, (n_pairs - 1) // steps_per_trip, trip, (*second, *start))
        _, prev = steps((n_pairs - 1) % steps_per_trip, carry[:2], carry[2:])
        last_slot = slots[(n_pairs - 1) % 2]
        accumulate(prev[1], products(prev[0], last_slot), [last_slot[2][i] for i in ids])

    last = nq - 1
    zero = jnp.int32(0)
    run(nq, (zero, zero), lambda kj, qi: (jnp.minimum(kj + 1, last), jnp.minimum(qi + 1, last)), True)

    def next_below_diagonal(kj, qi):
        wrap = qi == last
        dist = qi - kj
        return (jnp.where(wrap, 0, kj + 1), jnp.where(wrap, jnp.minimum(dist + 1, last), qi + 1))

    run(nq * (nq - 1) // 2, (zero, jnp.int32(min(1, last))), next_below_diagonal, False)


def _head_softmax_scratch(nq, tq):
    return [pltpu.VMEM((nq, 1, tq), F32), pltpu.VMEM((nq, ACC_ROWS, tq), F32)]


def _pipe_scratch(n_streams, tk, width):
    slot = lambda: [pltpu.VMEM((n_streams, tk, width), F32), pltpu.VMEM((n_streams, tk, width), BF16),
                    pltpu.VMEM((n_streams, 1, width), F32)]
    return slot() + slot()


N_BIAS_PIECES = 3


def _fox_kernel(qt_ref, k_ref, bias_ref, vt_ref, o_ref, m_ref, acc_ref, qa_ref, kb_ref, *pipe, nq, tq, tk):
    slices = _query_streams(tq)
    row = lax.broadcasted_iota(jnp.int32, (HEAD_DIM, tq), 0)
    ones_rows = jnp.where(row < N_BIAS_PIECES, 1.0, 0.0).astype(BF16)
    zeros = jnp.zeros((HEAD_DIM - 8, tk), F32)
    for j in range(nq):
        qa_ref[j] = jnp.concatenate([qt_ref[j], ones_rows], axis=0)
        rows = bias_ref[:, j * tk:(j + 1) * tk].astype(F32)
        kb_ref[j] = jnp.concatenate([rows, zeros], axis=0).T.astype(BF16)

    def raw_scores(kj, qi):
        k = jnp.concatenate([k_ref[pl.ds(pl.multiple_of(kj * tk, tk), tk), :], kb_ref[kj]], axis=1)
        return [_dot(k, qa_ref[qi, :, qs]) for qs in slices]

    def logits(kj, qi, i, raw, masked):
        qs = slices[i]
        if masked:
            raw = jnp.where(_causal_mask(kj * tk, qi * tq + qs.start, tk, qs.stop - qs.start), raw, NEG_INF)
        return raw

    def values(kj):
        return [_with_ones_rows(vt_ref[kj])] * len(slices)

    _head_sweep(nq, tq, tk, [((m_ref, acc_ref), qs) for qs in slices], raw_scores, logits, values, pipe, 4)
    for j in range(nq):
        o_ref[j * tq:(j + 1) * tq, :] = _normalized(acc_ref[j]).T.astype(o_ref.dtype)


def fox_attention(q_fm, q_blk0, k_rows, bias_rows, v_fm, v_blk0, seq):
    b = k_rows.shape[0]
    tq = tk = q_fm.shape[2]
    nq = seq // tq
    slices = _query_streams(tq)
    return pl.pallas_call(
        functools.partial(_fox_kernel, nq=nq, tq=tq, tk=tk),
        grid=(b, H_FOX),
        in_specs=[
            pl.BlockSpec((nq, HEAD_DIM, tq), lambda bi, h: (bi, q_blk0 + h, 0)),
            pl.BlockSpec((None, seq, HEAD_DIM), lambda bi, h: (bi, 0, h)),
            pl.BlockSpec((None, None, 8, seq), lambda bi, h: (h, bi, 0, 0)),
            pl.BlockSpec((nq, HEAD_DIM, tk), lambda bi, h: (bi, v_blk0 + h, 0)),
        ],
        out_specs=pl.BlockSpec((None, seq, HEAD_DIM), lambda bi, h: (bi, 0, h)),
        out_shape=jax.ShapeDtypeStruct((b, seq, W_FOX), BF16),
        scratch_shapes=_head_softmax_scratch(nq, tq)
        + [pltpu.VMEM((nq, 2 * HEAD_DIM, tq), BF16), pltpu.VMEM((nq, tk, HEAD_DIM), BF16)]
        + _pipe_scratch(len(slices), tk, slices[0].stop),
        compiler_params=_params("arbitrary", "arbitrary"),
        name="fox_attention",
    )(q_fm, k_rows, bias_rows, v_fm)


def _diff_kernel(lam_ref, gain_ref, qt_ref, k_ref, vt_ref, o_ref,
                 m1_ref, acc1_ref, m2_ref, acc2_ref, q1_ref, q2_ref, *pipe, nq, tq, tk, lam_init):
    zeros = jnp.zeros((DK_DIFF, tq), BF16)
    for j in range(nq):
        q1_ref[j] = jnp.concatenate([qt_ref[j, 0:DK_DIFF, :], zeros], axis=0)
        q2_ref[j] = jnp.concatenate([zeros, qt_ref[j, DK_DIFF:HEAD_DIM, :]], axis=0)
    streams, q_refs = [], []
    for qs in _query_streams(tq):
        for refs, q_ref in (((m1_ref, acc1_ref), q1_ref), ((m2_ref, acc2_ref), q2_ref)):
            streams.append((refs, qs))
            q_refs.append(q_ref)

    def raw_scores(kj, qi):
        k = k_ref[pl.ds(pl.multiple_of(kj * tk, tk), tk), :]
        return [_dot(k, q_ref[qi, :, qs]) for q_ref, (_, qs) in zip(q_refs, streams)]

    def logits(kj, qi, i, raw, masked):
        qs = streams[i][1]
        if masked:
            raw = jnp.where(_causal_mask(kj * tk, qi * tq + qs.start, tk, qs.stop - qs.start), raw, NEG_INF)
        return raw

    def values(kj):
        return [_with_ones_rows(vt_ref[kj])] * len(streams)

    _head_sweep(nq, tq, tk, streams, raw_scores, logits, values, pipe, 2)

    lam_vecs = lam_ref[...]
    dot1 = jnp.sum(lam_vecs[0:1] * lam_vecs[1:2], axis=1, keepdims=True)
    dot2 = jnp.sum(lam_vecs[2:3] * lam_vecs[3:4], axis=1, keepdims=True)
    lam = jnp.exp(dot1) - jnp.exp(dot2) + lam_init
    for j in range(nq):
        o = (_normalized(acc1_ref[j]) - lam * _normalized(acc2_ref[j])).T
        y = o * lax.rsqrt(jnp.mean(o * o, axis=-1, keepdims=True) + NORM_EPS)
        o_ref[j * tq:(j + 1) * tq, :] = ((y * gain_ref[...]) * (1.0 - lam_init)).astype(o_ref.dtype)


def diff_attention(q_fm, q_blk0, k_rows, v_fm, v_blk0, lam_vecs, subln_gain, lam_init, seq):
    b = k_rows.shape[0]
    tq = tk = q_fm.shape[2]
    nq = seq // tq
    slices = _query_streams(tq)
    return pl.pallas_call(
        functools.partial(_diff_kernel, nq=nq, tq=tq, tk=tk, lam_init=lam_init),
        grid=(b, H_DIFF),
        in_specs=[
            pl.BlockSpec((4, DK_DIFF), lambda bi, h: (0, 0)),
            pl.BlockSpec((1, HEAD_DIM), lambda bi, h: (0, 0)),
            pl.BlockSpec((nq, HEAD_DIM, tq), lambda bi, h: (bi, q_blk0 + h, 0)),
            pl.BlockSpec((None, seq, HEAD_DIM), lambda bi, h: (bi, 0, h)),
            pl.BlockSpec((nq, HEAD_DIM, tk), lambda bi, h: (bi, v_blk0 + h, 0)),
        ],
        out_specs=pl.BlockSpec((None, seq, HEAD_DIM), lambda bi, h: (bi, 0, h)),
        out_shape=jax.ShapeDtypeStruct((b, seq, W_DIFF), BF16),
        scratch_shapes=_head_softmax_scratch(nq, tq) + _head_softmax_scratch(nq, tq)
        + [pltpu.VMEM((nq, HEAD_DIM, tq), BF16), pltpu.VMEM((nq, HEAD_DIM, tq), BF16)]
        + _pipe_scratch(2 * len(slices), tk, slices[0].stop),
        compiler_params=_params("arbitrary", "arbitrary"),
        name="diff_attention",
    )(lam_vecs, subln_gain.reshape(1, HEAD_DIM), q_fm, k_rows, v_fm)


def _dsa_kernel(qt_ref, k_ref, vt_ref, qit_ref, ki_ref, wt_ref, o_ref,
                hi_ref, lo_ref, qm_ref, cut_ref, m_ref, acc_ref, *pipe,
                tq, tk, seq, n_sel):
    qb = pl.program_id(1)
    n_tiles = (qb * tq) // tk + 1
    slices = _query_streams(tq)

    zeros = jnp.zeros((D_IDX, tq), BF16)
    for h in range(H_IDX):
        qm_ref[h] = jnp.concatenate([qit_ref[h * D_IDX:(h + 1) * D_IDX, :], zeros], axis=0)

    def score_tile(kj, masked):
        kk = ki_ref[pl.ds(pl.multiple_of(kj * tk, tk), tk), :]
        for qs in slices:
            width = qs.stop - qs.start
            rel_q = [_dot(kk, qm_ref[h, :, qs]) for h in range(H_IDX)]
            score = jnp.zeros((tk, width), F32)
            for h in range(H_IDX):
                score = score + wt_ref[h:h + 1, qs] * jnp.maximum(rel_q[h], 0.0)
            if masked:
                score = jnp.where(_causal_mask(kj * tk, qb * tq + qs.start, tk, width), score, NEG_INF)
            bits = lax.bitcast_convert_type(score, jnp.int32)
            key = bits ^ ((bits >> 31) & 0x7FFFFFFF)
            hi_ref[kj, :, qs] = (key >> 16).astype(jnp.int16)
            lo_ref[kj, :, qs] = ((key & 0xFFFF) + I16_MIN).astype(jnp.int16)

    _sweep_key_tiles(qb, tq, tk, score_tile)

    n_beyond = (seq - n_tiles * tk).astype(F32)
    neg_hi, neg_lo = KEY_NEG_INF >> 16, (KEY_NEG_INF & 0xFFFF) + I16_MIN
    one, zero = jnp.int16(1), jnp.int16(0)
    as16 = lambda v: v.astype(jnp.int16)

    def count_hits(hit_fn):
        def body(kj, part):
            hit = jnp.where(hit_fn(kj), one, zero)
            for r in range(tk // 16):
                part = part + hit[r * 16:(r + 1) * 16]
            return part
        part = lax.fori_loop(0, n_tiles, body, jnp.zeros((16, tq), jnp.int16))
        return jnp.sum(part.astype(F32), axis=0, keepdims=True)

    def bisect16(enough):
        def step(s, t):
            cand = t + jnp.left_shift(jnp.int32(1), 15 - s)
            return jnp.where(enough(cand), cand, t)
        return lax.fori_loop(0, 16, step, jnp.full((1, tq), I16_MIN, jnp.int32))

    def count_hi_ge(cand):
        n = count_hits(lambda kj, c=as16(cand): hi_ref[kj] >= c)
        return n + jnp.where(cand <= neg_hi, n_beyond, 0.0)

    t_hi = bisect16(lambda cand: count_hi_ge(cand) >= n_sel)
    t_hi16 = as16(t_hi)
    n_above = jnp.where(t_hi < I16_MAX, count_hi_ge(jnp.minimum(t_hi + 1, I16_MAX)), 0.0)
    need_lo = n_sel - n_above

    def mask_lo(kj, carry):
        lo_ref[kj] = jnp.where(hi_ref[kj] == t_hi16, lo_ref[kj], jnp.int16(I16_MIN))
        return carry

    lax.fori_loop(0, n_tiles, mask_lo, 0)
    beyond_in_group = jnp.where(t_hi == neg_hi, n_beyond, 0.0)

    def count_lo_ge(cand):
        n = count_hits(lambda kj, c=as16(cand): lo_ref[kj] >= c)
        return n + jnp.where(cand <= neg_lo, beyond_in_group, 0.0)

    t_lo = bisect16(lambda cand: count_lo_ge(cand) >= need_lo)
    t_lo16 = as16(t_lo)

    def in_group(kj):
        return hi_ref[kj] == t_hi16

    n_ge = n_above + count_hits(lambda kj: in_group(kj) & (lo_ref[kj] >= t_lo16)) \
        + jnp.where(t_lo <= neg_lo, beyond_in_group, 0.0)
    surplus = n_ge - n_sel
    cut_ref[...] = jnp.full((1, tq), I16_MAX, jnp.int32)

    row_in_tile = as16(lax.broadcasted_iota(jnp.int32, (tk, tq), 0))

    def position_before(kj, bound):
        return row_in_tile < as16(jnp.minimum(bound - kj * tk, I16_MAX))

    any_surplus = jnp.max(surplus) > 0.0

    @pl.when(any_surplus)
    def _():
        n_gt = n_above + count_hits(lambda kj: in_group(kj) & (lo_ref[kj] > t_lo16)) \
            + jnp.where(t_lo < neg_lo, beyond_in_group, 0.0)
        need = jnp.where(surplus > 0.0, n_sel - n_gt, float(seq + 1))

        def count_tied_below(cut):
            return count_hits(lambda kj: in_group(kj) & (lo_ref[kj] == t_lo16) & position_before(kj, cut))

        n_bits = max(1, (seq - 1).bit_length())

        def bisect_cut(step, cut):
            cand = cut + jnp.left_shift(jnp.int32(1), n_bits - 1 - step)
            return jnp.where(count_tied_below(cand) < need, cand, cut)

        cut_ref[...] = lax.fori_loop(0, n_bits, bisect_cut, jnp.zeros((1, tq), jnp.int32))

    cut_end = cut_ref[...] + 1
    zero_bias, neg_bias = jnp.zeros((), BF16), jnp.asarray(NEG_INF, BF16)

    def write_bias(with_ties):
        def body(kj, carry):
            hi, lo = hi_ref[kj], lo_ref[kj]
            if with_ties:
                low_ok = (lo > t_lo16) | ((lo == t_lo16) & position_before(kj, cut_end))
            else:
                low_ok = lo >= t_lo16
            sel = (hi > t_hi16) | ((hi == t_hi16) & low_ok)
            hi_ref[kj] = lax.bitcast_convert_type(jnp.where(sel, zero_bias, neg_bias), jnp.int16)
            return carry
        return body

    @pl.when(any_surplus)
    def _():
        lax.fori_loop(0, n_tiles, write_bias(True), 0)

    @pl.when(jnp.logical_not(any_surplus))
    def _():
        lax.fori_loop(0, n_tiles, write_bias(False), 0)

    n_full = n_tiles - 1
    ids = range(len(slices))
    slots = (pipe[:3], pipe[3:])
    head_rows = lambda h: slice(h * HEAD_DIM, (h + 1) * HEAD_DIM)
    m_ref[...] = jnp.full(m_ref.shape, -jnp.inf, F32)
    acc_ref[...] = jnp.zeros(acc_ref.shape, F32)

    def raw_scores(kj, h):
        k = k_ref[pl.ds(pl.multiple_of(kj * tk, tk), tk), head_rows(h)]
        return [_dot(k, qt_ref[head_rows(h), qs]) for qs in slices]

    def products(kj, h, slot):
        vt = _with_ones_rows(vt_ref[kj, head_rows(h), :])
        return [_dot(vt, slot[1][i]) for i in ids]

    def accumulate(h, pv, rescale):
        for i, qs in enumerate(slices):
            acc_ref[h, :, qs] = rescale[i] * acc_ref[h, :, qs] + pv[i]

    def step(kj, h, first=False, last=False, masked=False):
        cur_slot, nxt_slot = slots[h % 2], slots[(h + 1) % 2]
        nxt = (kj, h + 1) if h + 1 < H_DSA else (kj + 1, 0)
        prev = (kj, h - 1) if h > 0 else (kj - 1, H_DSA - 1)
        raw_next = None if last else raw_scores(*nxt)
        if not first:
            pv_prev = products(*prev, nxt_slot)
            rescale_prev = [nxt_slot[2][i] for i in ids]
        if not last:
            for i in ids:
                nxt_slot[0][i] = raw_next[i]
        for i, qs in enumerate(slices):
            s = cur_slot[0][i] + lax.bitcast_convert_type(hi_ref[kj, :, qs], BF16).astype(F32)
            if masked:
                s = jnp.where(_causal_mask(kj * tk, qb * tq + qs.start, tk, qs.stop - qs.start), s, NEG_INF)
            m_new, alpha, p = _softmax_stats(s, m_ref[h, :, qs])
            m_ref[h, :, qs] = m_new
            cur_slot[1][i] = p
            cur_slot[2][i] = alpha
        if not first:
            accumulate(prev[1], pv_prev, rescale_prev)

    def tile_steps(kj, first=False, diagonal=False):
        for h in range(H_DSA):
            step(kj, h, first=first and h == 0, last=diagonal and h == H_DSA - 1, masked=diagonal)

    raw0 = raw_scores(0, 0)
    for i in ids:
        slots[0][0][i] = raw0[i]

    @pl.when(n_full == 0)
    def _():
        tile_steps(0, first=True, diagonal=True)

    @pl.when(n_full > 0)
    def _():
        tile_steps(0, first=True)

        def body(kj, carry):
            tile_steps(kj)
            return carry

        lax.fori_loop(1, n_full, body, 0)
        tile_steps(n_full, diagonal=True)

    last_slot = slots[(H_DSA - 1) % 2]
    accumulate(H_DSA - 1, products(n_full, H_DSA - 1, last_slot), [last_slot[2][i] for i in ids])
    for h in range(H_DSA):
        o_ref[:, head_rows(h)] = _normalized(acc_ref[h]).T.astype(o_ref.dtype)


def dsa_attention(q_fm, k_rows, v_fm, qi_fm, ki_rows, ki_blk, wi_t, seq):
    b = k_rows.shape[0]
    tq = tk = q_fm.shape[2]
    nq = seq // tq
    n_sel = min(TOPK_MAX, seq // 4)
    once = pl.Buffered(1)
    slices = _query_streams(tq)
    fm_q = lambda rows: pl.BlockSpec((None, rows, tq), lambda bi, i: (bi * nq + i, 0, 0))
    return pl.pallas_call(
        functools.partial(_dsa_kernel, tq=tq, tk=tk, seq=seq, n_sel=n_sel),
        grid=(b, nq),
        in_specs=[
            fm_q(W_DSA),
            pl.BlockSpec((None, seq, W_DSA), lambda bi, i: (bi, 0, 0), pipeline_mode=once),
            pl.BlockSpec((nq, W_DSA, tk), lambda bi, i: (bi, 0, 0), pipeline_mode=once),
            fm_q(W_IDX),
            pl.BlockSpec((None, seq, LANES), lambda bi, i: (bi, 0, ki_blk), pipeline_mode=once),
            pl.BlockSpec((H_IDX, tq), lambda bi, i: (0, bi * nq + i)),
        ],
        out_specs=pl.BlockSpec((None, tq, W_DSA), lambda bi, i: (bi, i, 0)),
        out_shape=jax.ShapeDtypeStruct((b, seq, W_DSA), BF16),
        scratch_shapes=[
            pltpu.VMEM((nq, tk, tq), jnp.int16),
            pltpu.VMEM((nq, tk, tq), jnp.int16),
            pltpu.VMEM((H_IDX, 2 * D_IDX, tq), BF16),
            pltpu.VMEM((1, tq), jnp.int32),
        ] + _head_softmax_scratch(H_DSA, tq) + _pipe_scratch(len(slices), tk, slices[0].stop),
        compiler_params=_params("arbitrary", "arbitrary"),
        name="dsa_attention",
    )(q_fm, k_rows, v_fm, qi_fm, ki_rows, wi_t)


def _merge_kernel(h_ref, oa_ref, ob_ref, oc_ref, mw0_ref, mw1_ref, mw2_ref, mb_ref, wa_ref, wb_ref, wc_ref, o_ref,
                  *bf16_refs):
    f32_refs = (mw0_ref, mw1_ref, mw2_ref, wa_ref, wb_ref, wc_ref)
    _cast_on_first_row_step(tuple(zip(f32_refs, bf16_refs)))
    gate_w, branch_w = bf16_refs[:N_BRANCH], bf16_refs[N_BRANCH:]
    h = h_ref[...]
    merged = None
    for i, o_b_ref in enumerate((oa_ref, ob_ref, oc_ref)):
        gate = jax.nn.sigmoid(_dot(h, gate_w[i][...]) + mb_ref[i])
        term = gate * _dot(o_b_ref[...], branch_w[i][...])
        merged = term if merged is None else merged + term
    o_ref[...] = merged.astype(o_ref.dtype)


def merge_branches(h, oa, ob, oc, merge_w, merge_b, wa, wb, wc, layer, seq):
    n, d = h.shape
    tm = min(1024, seq)
    tn = min(256, d)
    nj = d // tn
    row = lambda width: pl.BlockSpec((tm, width), lambda j, i: (i, 0))
    col = lambda rows: pl.BlockSpec((None, rows, tn), lambda j, i: (layer, 0, j))
    gate_w = lambda g: pl.BlockSpec((None, d, tn), lambda j, i: (layer, 0, g * nj + j))
    widths = (oa.shape[1], ob.shape[1], oc.shape[1])
    return pl.pallas_call(
        _merge_kernel,
        grid=(nj, n // tm),
        in_specs=[
            row(d), row(widths[0]), row(widths[1]), row(widths[2]),
            gate_w(0), gate_w(1), gate_w(2),
            pl.BlockSpec((None, N_BRANCH, 1, tn), lambda j, i: (layer, 0, 0, j)),
            col(widths[0]), col(widths[1]), col(widths[2]),
        ],
        out_specs=pl.BlockSpec((tm, tn), lambda j, i: (i, j)),
        out_shape=jax.ShapeDtypeStruct((n, d), BF16),
        scratch_shapes=[pltpu.VMEM((d, tn), BF16)] * N_BRANCH + [pltpu.VMEM((w, tn), BF16) for w in widths],
        compiler_params=_params("arbitrary", "arbitrary"),
        name="merge_branches",
    )(h, oa, ob, oc, merge_w, merge_w, merge_w, merge_b.reshape(merge_b.shape[0], N_BRANCH, 1, d), wa, wb, wc)


def _rope_tables(seq, dim):
    inv_freq = 1.0 / (ROPE_THETA ** (jnp.arange(0, dim, 2, dtype=F32) / dim))
    ang = jnp.arange(seq, dtype=F32)[:, None] * inv_freq[None, :]
    cos, sin = lax.optimization_barrier((jnp.cos(ang), jnp.sin(ang)))
    cos_g = jnp.concatenate([cos, cos], axis=1)
    sin_g = jnp.concatenate([-sin, sin], axis=1)
    reps = LANES // dim
    return jnp.tile(cos_g, (1, reps)), jnp.tile(sin_g, (1, reps)), cos_g.T, sin_g.T


def _split_w_in(w_in_t):
    sizes = (W_DIFF, W_DIFF, W_DIFF, W_FOX, W_FOX, W_FOX, H_FOX, W_DSA, W_DSA, W_DSA, W_IDX, D_IDX, H_IDX)
    parts, start = [], 0
    for n in sizes:
        parts.append(w_in_t[start:start + n])
        start += n
    return parts


def kernel(x, c, w_ada, b_ada, norm_ffn1, ffn1_w1, ffn1_w3, ffn1_w2, norm_mix, w_in, b_forget, lam_q1, lam_k1, lam_q2, lam_k2, subln_gain, merge_w, merge_b, w_branch_a, w_branch_b, w_branch_c, w_out, norm_ffn2, ffn2_w1, ffn2_w3, ffn2_w2, norm_final):
    batch, seq, d = x.shape
    depth = w_ada.shape[0]
    n = batch * seq
    assert seq % min(TOKEN_TILE, seq) == 0 and seq % LANES == 0 and d % LANES == 0
    assert seq <= I16_MAX + 1, "the top-k tie rule compares key positions as int16"
    bf = lambda a: a.astype(BF16)

    cos64, sin64, cos64_fm, sin64_fm = _rope_tables(seq, DK_DIFF)
    cos128, sin128, cos128_fm, sin128_fm = _rope_tables(seq, HEAD_DIM)
    mod = adaln_mod(c, w_ada, b_ada)
    xf = x.reshape(n, d)

    for l in range(depth):
        lam_init = 0.8 - 0.6 * math.exp(-0.3 * l)
        modl = mod[l].reshape(batch * N_MOD, 1, d)

        h = norm_mod(xf, norm_ffn1[l], modl, 0, seq)
        u = ffn_up(h, ffn1_w1, ffn1_w3, l)
        xf = resid_mm(u, ffn1_w2, l, xf, modl, 2, 0.5, seq)

        h = norm_mod(xf, norm_mix[l], modl, 3, seq)
        qa, ka, va, qb, kb, vb, fb, qc, kc, vc, qi, ki, wi = _split_w_in(jnp.transpose(w_in[l]))
        k64 = proj(h, jnp.concatenate([ka, ki, ki], axis=0), cos64, sin64, DK_DIFF, seq).reshape(batch, seq, -1)
        k128 = proj(h, kc, cos128, sin128, HEAD_DIM, seq).reshape(batch, seq, -1)
        k0 = proj(h, kb, cos64, sin64, 0, seq).reshape(batch, seq, -1)
        log2e_over_sqrt = lambda width: (width ** -0.5) * LOG2E
        q64_fm = proj_fm(h, jnp.concatenate([qi, qa], axis=0), cos64_fm, sin64_fm, DK_DIFF, seq,
                         ((W_IDX, 1.0), (W_DIFF, log2e_over_sqrt(DK_DIFF))))
        q128_fm = proj_fm(h, qc, cos128_fm, sin128_fm, HEAD_DIM, seq, ((W_DSA, log2e_over_sqrt(HEAD_DIM)),))
        qb_fm = proj_fm(h, qb, cos64_fm, sin64_fm, 0, seq, ((W_FOX, log2e_over_sqrt(HEAD_DIM)),))
        v_fm = proj_fm(h, jnp.concatenate([vc, va, vb], axis=0), cos64_fm, sin64_fm, 0, seq)
        blk = lambda rows: rows // HEAD_DIM
        qa_blk, va_blk, vb_blk = blk(W_IDX), blk(W_DSA), blk(W_DSA + W_DIFF)

        w_small_t = bf(jnp.concatenate([fb, jnp.zeros((8 - H_FOX, d), F32), wi], axis=0))
        b_forget8 = jnp.concatenate([b_forget[l], jnp.zeros((8 - H_FOX,), F32)]).reshape(8, 1)
        logf_t, wi_t = small_proj(h, w_small_t, b_forget8, seq)
        bias_rows = neg_cumsum(logf_t, batch, seq)

        lam_vecs = jnp.stack([lam_q1[l], lam_k1[l], lam_q2[l], lam_k2[l]])
        oa = diff_attention(q64_fm, qa_blk, k64, v_fm, va_blk, lam_vecs, subln_gain[l], lam_init, seq)
        ob = fox_attention(qb_fm, 0, k0, bias_rows, v_fm, vb_blk, seq)
        oc = dsa_attention(q128_fm, k128, v_fm, q64_fm, k64, blk(W_DIFF), wi_t, seq)

        merged = merge_branches(h, oa.reshape(n, -1), ob.reshape(n, -1), oc.reshape(n, -1), merge_w, merge_b,
                                w_branch_a, w_branch_b, w_branch_c, l, seq)
        xf = resid_mm(merged, w_out, l, xf, modl, 5, 1.0, seq)

        h = norm_mod(xf, norm_ffn2[l], modl, 6, seq)
        u = ffn_up(h, ffn2_w1, ffn2_w3, l)
        xf = resid_mm(u, ffn2_w2, l, xf, modl, 8, 0.5, seq)

    return final_norm(xf, norm_final).reshape(batch, seq, d)
```

```python
import functools
import math

import numpy as np
import jax
import jax.numpy as jnp
from jax import lax
from jax.experimental import pallas as pl
from jax.experimental.pallas import tpu as pltpu

HEAD_DIM = 128
H_DIFF = 6
DK_DIFF = HEAD_DIM // 2
H_FOX = 6
H_DSA = 4
H_IDX = 8
D_IDX = 64
TOPK_MAX = 256
ROPE_THETA = 10000.0
NORM_EPS = 1e-6
N_BRANCH = 3
N_MOD = 9
NEG_INF = -1e30
IDX_W_SCALE = (H_IDX ** -0.5) * (D_IDX ** -0.5)
LOG2E = math.log2(math.e)

W_DIFF = H_DIFF * HEAD_DIM
W_FOX = H_FOX * HEAD_DIM
W_DSA = H_DSA * HEAD_DIM
W_IDX = H_IDX * D_IDX

LANES = 128
MXU_WIDTH = 256
VMEM_LIMIT = 56 * 1024 * 1024
TOKEN_TILE = 512
I16_MIN, I16_MAX = -(2 ** 15), 2 ** 15 - 1

BF16 = jnp.bfloat16
F32 = jnp.float32


def _order_key_of(value):
    bits = int(np.array(value, np.float32).view(np.int32))
    return bits ^ ((bits >> 31) & 0x7FFFFFFF)


KEY_NEG_INF = _order_key_of(NEG_INF)


def _params(*semantics):
    return pltpu.CompilerParams(dimension_semantics=semantics, vmem_limit_bytes=VMEM_LIMIT)


def _nt_dot(a, b):
    return lax.dot_general(a, b, (((1,), (1,)), ((), ())), preferred_element_type=F32)


def _dot(a, b):
    return jnp.dot(a, b, preferred_element_type=F32)


def _adaln_kernel(c_ref, w_ref, b_ref, o_ref):
    c = c_ref[...]
    o_ref[...] = _dot(c * jax.nn.sigmoid(c), w_ref[...]) + b_ref[...]


def adaln_mod(c, w_ada, b_ada):
    depth, d, nd = w_ada.shape
    b = c.shape[0]
    tn = min(1024, d)
    return pl.pallas_call(
        _adaln_kernel,
        grid=(depth, nd // tn),
        in_specs=[
            pl.BlockSpec((b, d), lambda l, j: (0, 0)),
            pl.BlockSpec((None, d, tn), lambda l, j: (l, 0, j)),
            pl.BlockSpec((None, 1, tn), lambda l, j: (l, 0, j)),
        ],
        out_specs=pl.BlockSpec((None, b, tn), lambda l, j: (l, 0, j)),
        out_shape=jax.ShapeDtypeStruct((depth, b, nd), F32),
        compiler_params=_params("arbitrary", "arbitrary"),
        name="adaln_mod",
    )(c, w_ada, b_ada.reshape(depth, 1, nd))


def _norm_mod_kernel(x_ref, gain_ref, sc_ref, sh_ref, o_ref):
    x = x_ref[...]
    y = x * lax.rsqrt(jnp.mean(x * x, axis=-1, keepdims=True) + NORM_EPS)
    o_ref[...] = ((y * gain_ref[...]) * (1.0 + sc_ref[...]) + sh_ref[...]).astype(o_ref.dtype)


def norm_mod(x, gain, modl, i_shift, seq):
    n, d = x.shape
    tm = min(512, seq)
    return pl.pallas_call(
        _norm_mod_kernel,
        grid=(n // tm,),
        in_specs=[
            pl.BlockSpec((tm, d), lambda i: (i, 0)),
            pl.BlockSpec((1, d), lambda i: (0, 0)),
            pl.BlockSpec((None, 1, d), lambda i: ((i * tm) // seq * N_MOD + i_shift + 1, 0, 0)),
            pl.BlockSpec((None, 1, d), lambda i: ((i * tm) // seq * N_MOD + i_shift, 0, 0)),
        ],
        out_specs=pl.BlockSpec((tm, d), lambda i: (i, 0)),
        out_shape=jax.ShapeDtypeStruct((n, d), BF16),
        compiler_params=_params("arbitrary"),
        name="norm_mod",
    )(x, gain.reshape(1, d), modl, modl)


def _final_norm_kernel(x_ref, gain_ref, o_ref):
    x = x_ref[...]
    y = x * lax.rsqrt(jnp.mean(x * x, axis=-1, keepdims=True) + NORM_EPS)
    o_ref[...] = y * gain_ref[...]


def final_norm(x, gain):
    n, d = x.shape
    tm = min(512, n)
    return pl.pallas_call(
        _final_norm_kernel,
        grid=(n // tm,),
        in_specs=[pl.BlockSpec((tm, d), lambda i: (i, 0)), pl.BlockSpec((1, d), lambda i: (0, 0))],
        out_specs=pl.BlockSpec((tm, d), lambda i: (i, 0)),
        out_shape=jax.ShapeDtypeStruct((n, d), F32),
        compiler_params=_params("arbitrary"),
        name="final_norm",
    )(x, gain.reshape(1, d))


def _cast_on_first_row_step(pairs):
    @pl.when(pl.program_id(1) == 0)
    def _():
        for src_ref, dst_ref in pairs:
            dst_ref[...] = src_ref[...].astype(dst_ref.dtype)


def _ffn_up_kernel(h_ref, w1_ref, w3_ref, o_ref, w1b_ref, w3b_ref):
    _cast_on_first_row_step(((w1_ref, w1b_ref), (w3_ref, w3b_ref)))
    h = h_ref[...]
    a = _dot(h, w1b_ref[...])
    b = _dot(h, w3b_ref[...])
    o_ref[...] = ((a * jax.nn.sigmoid(a)) * b).astype(o_ref.dtype)


def ffn_up(h, w1, w3, layer):
    n, d = h.shape
    f = w1.shape[2]
    tm = min(1024, n)
    tn = 512 if f % 512 == 0 else f
    w_spec = pl.BlockSpec((None, d, tn), lambda j, i: (layer, 0, j))
    return pl.pallas_call(
        _ffn_up_kernel,
        grid=(f // tn, n // tm),
        in_specs=[pl.BlockSpec((tm, d), lambda j, i: (i, 0)), w_spec, w_spec],
        out_specs=pl.BlockSpec((tm, tn), lambda j, i: (i, j)),
        out_shape=jax.ShapeDtypeStruct((n, f), BF16),
        scratch_shapes=[pltpu.VMEM((d, tn), BF16), pltpu.VMEM((d, tn), BF16)],
        compiler_params=_params("arbitrary", "arbitrary"),
        name="ffn_up",
    )(h, w1, w3)


def _resid_mm_kernel(a_ref, w_ref, x_ref, g_ref, o_ref, wb_ref, *, gscale):
    _cast_on_first_row_step(((w_ref, wb_ref),))
    y = _dot(a_ref[...], wb_ref[...])
    o_ref[...] = x_ref[...] + (gscale * g_ref[...]) * y


def resid_mm(a, w, layer, x, modl, i_gate, gscale, seq):
    n, k = a.shape
    d = w.shape[2]
    tm = min(seq, max(512, 2 ** int(math.log2(12 * 2 ** 20 // (2 * k)))))
    tn = min(512, d)
    return pl.pallas_call(
        functools.partial(_resid_mm_kernel, gscale=gscale),
        grid=(d // tn, n // tm),
        in_specs=[
            pl.BlockSpec((tm, k), lambda j, i: (i, 0)),
            pl.BlockSpec((None, k, tn), lambda j, i: (layer, 0, j), pipeline_mode=pl.Buffered(1)),
            pl.BlockSpec((tm, tn), lambda j, i: (i, j)),
            pl.BlockSpec((None, 1, tn), lambda j, i: ((i * tm) // seq * N_MOD + i_gate, 0, j)),
        ],
        out_specs=pl.BlockSpec((tm, tn), lambda j, i: (i, j)),
        out_shape=jax.ShapeDtypeStruct((n, d), F32),
        scratch_shapes=[pltpu.VMEM((k, tn), BF16)],
        compiler_params=_params("arbitrary", "arbitrary"),
        name="resid_mm",
    )(a, w, x, modl)


def _widest_tile(n_chunks, chunk, bytes_per_unit, limit_bytes):
    return chunk * max(t for t in range(1, n_chunks + 1)
                       if n_chunks % t == 0 and (t == 1 or t * chunk * bytes_per_unit <= limit_bytes))


def _swap_halves(z, group):
    if group == LANES:
        return pltpu.roll(z, LANES // 2, axis=1)
    half = group // 2
    lane = lax.broadcasted_iota(jnp.int32, z.shape, 1)
    from_above = pltpu.roll(z, LANES - half, axis=1)
    from_below = pltpu.roll(z, half, axis=1)
    return jnp.where((lane & (group - 1)) < half, from_above, from_below)


def _proj_kernel(h_ref, w_ref, cos_ref, sin_ref, o_ref, wb_ref, *, group):
    _cast_on_first_row_step(((w_ref, wb_ref),))
    z = _nt_dot(h_ref[...], wb_ref[...])
    if group == 0:
        o_ref[...] = z.astype(o_ref.dtype)
        return
    cos = cos_ref[...]
    sin = sin_ref[...]
    for c in range(z.shape[1] // LANES):
        zc = z[:, c * LANES:(c + 1) * LANES]
        o_ref[:, c * LANES:(c + 1) * LANES] = (zc * cos + _swap_halves(zc, group) * sin).astype(o_ref.dtype)


def proj(h, wt, cos, sin, group, seq):
    n, d = h.shape
    cols = wt.shape[0]
    tm = min(1024, seq)
    tn = _widest_tile(cols // LANES, LANES, d * 4, 12 * 2 ** 20)
    s_blocks = seq // tm
    return pl.pallas_call(
        functools.partial(_proj_kernel, group=group),
        grid=(cols // tn, n // tm),
        in_specs=[
            pl.BlockSpec((tm, d), lambda j, i: (i, 0)),
            pl.BlockSpec((tn, d), lambda j, i: (j, 0)),
            pl.BlockSpec((tm, LANES), lambda j, i: (i % s_blocks, 0)),
            pl.BlockSpec((tm, LANES), lambda j, i: (i % s_blocks, 0)),
        ],
        out_specs=pl.BlockSpec((tm, tn), lambda j, i: (i, j)),
        out_shape=jax.ShapeDtypeStruct((n, cols), BF16),
        scratch_shapes=[pltpu.VMEM((tn, d), BF16)],
        compiler_params=_params("arbitrary", "arbitrary"),
        name=f"proj_rope{group}",
    )(h, wt, cos, sin)


def _proj_fm_kernel(w_ref, h_ref, cos_ref, sin_ref, o_ref, wt_ref, *, group, row_scales):
    _cast_on_first_row_step(((w_ref, wt_ref),))
    zt = _nt_dot(wt_ref[...], h_ref[...])
    step = group if group else LANES
    half = group // 2
    scale_of_row = [s for n_rows, s in row_scales for _ in range(n_rows // step)]
    if len(row_scales) == 1:
        scale_of_row = scale_of_row[:1] * (zt.shape[0] // step)
    for c in range(zt.shape[0] // step):
        blk = zt[c * step:(c + 1) * step]
        if group:
            swapped = jnp.concatenate([blk[half:], blk[:half]], axis=0)
            blk = blk * cos_ref[...] + swapped * sin_ref[...]
        if scale_of_row[c] != 1.0:
            blk = blk * scale_of_row[c]
        tile = o_ref.shape[2]
        for t in range(o_ref.shape[0]):
            o_ref[t, c * step:(c + 1) * step, :] = blk[:, t * tile:(t + 1) * tile].astype(o_ref.dtype)


def proj_fm(h, w, cos_fm, sin_fm, group, seq, row_scales=None):
    n, d = h.shape
    rows = w.shape[0]
    tile = min(TOKEN_TILE, seq)
    tm = min(2 * tile, seq)
    unit = max(group, LANES)
    row_scales = row_scales or ((rows, 1.0),)
    limit = 12 * 2 ** 20 if len(row_scales) == 1 else rows * d * 4
    tn = _widest_tile(rows // unit, unit, d * 4, limit)
    s_blocks = seq // tm
    g = max(group, 8)
    return pl.pallas_call(
        functools.partial(_proj_fm_kernel, group=group, row_scales=row_scales),
        grid=(rows // tn, n // tm),
        in_specs=[
            pl.BlockSpec((tn, d), lambda j, i: (j, 0), pipeline_mode=pl.Buffered(1 if tn == rows else 2)),
            pl.BlockSpec((tm, d), lambda j, i: (i, 0)),
            pl.BlockSpec((g, tm), lambda j, i: (0, i % s_blocks)),
            pl.BlockSpec((g, tm), lambda j, i: (0, i % s_blocks)),
        ],
        out_specs=pl.BlockSpec((tm // tile, tn, tile), lambda j, i: (i, j, 0)),
        out_shape=jax.ShapeDtypeStruct((n // tile, rows, tile), BF16),
        scratch_shapes=[pltpu.VMEM((tn, d), BF16)],
        compiler_params=_params("arbitrary", "arbitrary"),
        name=f"proj_fm_rope{group}",
    )(w, h, cos_fm, sin_fm)


def _small_proj_kernel(h_ref, wt_ref, bf_ref, logf_ref, wi_ref):
    zt = _nt_dot(wt_ref[...], h_ref[...])
    logf_ref[...] = jax.nn.log_sigmoid(zt[0:8, :] + bf_ref[...])
    wi_ref[...] = zt[8:16, :] * IDX_W_SCALE


def small_proj(h, w_small_t, b_forget8, seq):
    n, d = h.shape
    tm = min(512, seq)
    return pl.pallas_call(
        _small_proj_kernel,
        grid=(n // tm,),
        in_specs=[
            pl.BlockSpec((tm, d), lambda i: (i, 0)),
            pl.BlockSpec((16, d), lambda i: (0, 0)),
            pl.BlockSpec((8, 1), lambda i: (0, 0)),
        ],
        out_specs=[pl.BlockSpec((8, tm), lambda i: (0, i)), pl.BlockSpec((8, tm), lambda i: (0, i))],
        out_shape=[jax.ShapeDtypeStruct((8, n), F32), jax.ShapeDtypeStruct((8, n), F32)],
        compiler_params=_params("arbitrary"),
        name="small_proj",
    )(h, w_small_t, b_forget8)


def _split3(x):
    x1 = x.astype(BF16)
    r1 = x - x1.astype(F32)
    x2 = r1.astype(BF16)
    x3 = (r1 - x2.astype(F32)).astype(BF16)
    return x1, x2, x3


def _neg_cumsum_kernel(x_ref, hi_ref, mid_ref, lo_ref):
    x = x_ref[...]
    chunks = x.shape[0]
    r = lax.broadcasted_iota(jnp.int32, (LANES, LANES), 0)
    c = lax.broadcasted_iota(jnp.int32, (LANES, LANES), 1)
    upper = (r <= c).astype(BF16)
    within = sum(_dot(p, upper) for p in _split3(x))
    totals = jnp.broadcast_to(within[:, LANES - 1:LANES], (chunks, LANES))
    rr = lax.broadcasted_iota(jnp.int32, (chunks, chunks), 0)
    cc = lax.broadcasted_iota(jnp.int32, (chunks, chunks), 1)
    strict_lower = (cc < rr).astype(BF16)
    offset = sum(_dot(strict_lower, p) for p in _split3(totals))
    hi_ref[...], mid_ref[...], lo_ref[...] = _split3(-(within + offset) * LOG2E)


def neg_cumsum(logf_t, batch, seq):
    rows = logf_t.shape[0] * batch
    chunks = seq // LANES
    x = logf_t.reshape(rows, chunks, LANES)
    spec = pl.BlockSpec((None, chunks, LANES), lambda i: (i, 0, 0))
    pieces = pl.pallas_call(
        _neg_cumsum_kernel,
        grid=(rows,),
        in_specs=[spec],
        out_specs=[spec] * 3,
        out_shape=[jax.ShapeDtypeStruct((rows, chunks, LANES), BF16)] * 3,
        compiler_params=_params("arbitrary"),
        name="neg_cumsum",
    )(x)
    rows3 = jnp.stack([p.reshape(logf_t.shape[0], batch, seq) for p in pieces], axis=2)
    return jnp.pad(rows3, ((0, 0), (0, 0), (0, 8 - N_BIAS_PIECES), (0, 0)))


def _query_streams(tq):
    width = min(MXU_WIDTH, tq)
    return [slice(lo, lo + width) for lo in range(0, tq, width)]


ONES_ROWS = 16
ACC_ROWS = HEAD_DIM + ONES_ROWS


def _softmax_stats(s, m_prev):
    m_new = jnp.maximum(m_prev, jnp.max(s, axis=0, keepdims=True))
    return m_new, jnp.exp2(m_prev - m_new), jnp.exp2(s - m_new).astype(BF16)


def _with_ones_rows(vt):
    return jnp.concatenate([vt, jnp.ones((ONES_ROWS, vt.shape[1]), vt.dtype)], axis=0)


def _normalized(acc):
    return acc[0:HEAD_DIM] / acc[HEAD_DIM:HEAD_DIM + 1]


def _causal_mask(key0, query0, n_keys, n_queries):
    key = key0 + lax.broadcasted_iota(jnp.int32, (n_keys, n_queries), 0)
    query = query0 + lax.broadcasted_iota(jnp.int32, (n_keys, n_queries), 1)
    return key <= query


def _sweep_key_tiles(q_block, tq, tk, tile_fn):
    n_full = (q_block * tq) // tk

    def body(kj, carry):
        tile_fn(kj, False)
        return carry

    lax.fori_loop(0, n_full, body, 0)
    tile_fn(n_full, True)


def _head_sweep(nq, tq, tk, streams, raw_scores, logits, values, pipe, steps_per_trip):
    assert tq == tk
    ids = range(len(streams))
    slot_a, slot_b = pipe[:3], pipe[3:]

    for (m_ref, acc_ref), qs in streams:
        m_ref[:, :, qs] = jnp.full((nq, 1, qs.stop - qs.start), -jnp.inf, F32)
        acc_ref[:, :, qs] = jnp.zeros((nq, ACC_ROWS, qs.stop - qs.start), F32)

    def accumulate(qi, pv, rescale):
        for i in ids:
            (_, acc_ref), qs = streams[i]
            acc_ref[qi, :, qs] = rescale[i] * acc_ref[qi, :, qs] + pv[i]

    def products(kj, slot):
        vts = values(kj)
        return [_dot(vts[i], slot[1][i]) for i in ids]

    def step(cur, prev, nxt, cur_slot, nxt_slot, masked):
        raw_next = raw_scores(*nxt)
        if prev is not None:
            pv_prev = products(prev[0], nxt_slot)
            rescale_prev = [nxt_slot[2][i] for i in ids]
        for i in ids:
            nxt_slot[0][i] = raw_next[i]
        kj, qi = cur
        for i in ids:
            (m_ref, _), qs = streams[i]
            m_new, alpha, p = _softmax_stats(logits(kj, qi, i, cur_slot[0][i], masked), m_ref[qi, :, qs])
            m_ref[qi, :, qs] = m_new
            cur_slot[1][i] = p
            cur_slot[2][i] = alpha
        if prev is not None:
            accumulate(prev[1], pv_prev, rescale_prev)

    def run(n_pairs, start, advance, masked):
        if n_pairs == 0:
            return
        slots = (slot_a, slot_b)
        raw0 = raw_scores(*start)
        for i in ids:
            slot_a[0][i] = raw0[i]
        second = advance(*start)
        step(start, None, second, slot_a, slot_b, masked)

        def steps(count, cur, prev):
            for u in range(count):
                nxt = advance(*cur)
                step(cur, prev, nxt, slots[(u + 1) % 2], slots[u % 2], masked)
                prev, cur = cur, nxt
            return cur, prev

        def trip(t, carry):
            cur, prev = steps(steps_per_trip, carry[:2], carry[2:])
            return (*cur, *prev)

        carry = lax.fori_loop(0, (n_pairs - 1) // steps_per_trip, trip, (*second, *start))
        _, prev = steps((n_pairs - 1) % steps_per_trip, carry[:2], carry[2:])
        last_slot = slots[(n_pairs - 1) % 2]
        accumulate(prev[1], products(prev[0], last_slot), [last_slot[2][i] for i in ids])

    last = nq - 1
    zero = jnp.int32(0)
    run(nq, (zero, zero), lambda kj, qi: (jnp.minimum(kj + 1, last), jnp.minimum(qi + 1, last)), True)

    def next_below_diagonal(kj, qi):
        wrap = qi == last
        dist = qi - kj
        return (jnp.where(wrap, 0, kj + 1), jnp.where(wrap, jnp.minimum(dist + 1, last), qi + 1))

    run(nq * (nq - 1) // 2, (zero, jnp.int32(min(1, last))), next_below_diagonal, False)


def _head_softmax_scratch(nq, tq):
    return [pltpu.VMEM((nq, 1, tq), F32), pltpu.VMEM((nq, ACC_ROWS, tq), F32)]


def _pipe_scratch(n_streams, tk, width):
    slot = lambda: [pltpu.VMEM((n_streams, tk, width), F32), pltpu.VMEM((n_streams, tk, width), BF16),
                    pltpu.VMEM((n_streams, 1, width), F32)]
    return slot() + slot()


N_BIAS_PIECES = 3


def _fox_kernel(qt_ref, k_ref, bias_ref, vt_ref, o_ref, m_ref, acc_ref, qa_ref, kb_ref, *pipe, nq, tq, tk):
    slices = _query_streams(tq)
    row = lax.broadcasted_iota(jnp.int32, (HEAD_DIM, tq), 0)
    ones_rows = jnp.where(row < N_BIAS_PIECES, 1.0, 0.0).astype(BF16)
    zeros = jnp.zeros((HEAD_DIM - 8, tk), F32)
    for j in range(nq):
        qa_ref[j] = jnp.concatenate([qt_ref[j], ones_rows], axis=0)
        rows = bias_ref[:, j * tk:(j + 1) * tk].astype(F32)
        kb_ref[j] = jnp.concatenate([rows, zeros], axis=0).T.astype(BF16)

    def raw_scores(kj, qi):
        k = jnp.concatenate([k_ref[pl.ds(pl.multiple_of(kj * tk, tk), tk), :], kb_ref[kj]], axis=1)
        return [_dot(k, qa_ref[qi, :, qs]) for qs in slices]

    def logits(kj, qi, i, raw, masked):
        qs = slices[i]
        if masked:
            raw = jnp.where(_causal_mask(kj * tk, qi * tq + qs.start, tk, qs.stop - qs.start), raw, NEG_INF)
        return raw

    def values(kj):
        return [_with_ones_rows(vt_ref[kj])] * len(slices)

    _head_sweep(nq, tq, tk, [((m_ref, acc_ref), qs) for qs in slices], raw_scores, logits, values, pipe, 4)
    for j in range(nq):
        o_ref[j * tq:(j + 1) * tq, :] = _normalized(acc_ref[j]).T.astype(o_ref.dtype)


def fox_attention(q_fm, q_blk0, k_rows, bias_rows, v_fm, v_blk0, seq):
    b = k_rows.shape[0]
    tq = tk = q_fm.shape[2]
    nq = seq // tq
    slices = _query_streams(tq)
    return pl.pallas_call(
        functools.partial(_fox_kernel, nq=nq, tq=tq, tk=tk),
        grid=(b, H_FOX),
        in_specs=[
            pl.BlockSpec((nq, HEAD_DIM, tq), lambda bi, h: (bi, q_blk0 + h, 0)),
            pl.BlockSpec((None, seq, HEAD_DIM), lambda bi, h: (bi, 0, h)),
            pl.BlockSpec((None, None, 8, seq), lambda bi, h: (h, bi, 0, 0)),
            pl.BlockSpec((nq, HEAD_DIM, tk), lambda bi, h: (bi, v_blk0 + h, 0)),
        ],
        out_specs=pl.BlockSpec((None, seq, HEAD_DIM), lambda bi, h: (bi, 0, h)),
        out_shape=jax.ShapeDtypeStruct((b, seq, W_FOX), BF16),
        scratch_shapes=_head_softmax_scratch(nq, tq)
        + [pltpu.VMEM((nq, 2 * HEAD_DIM, tq), BF16), pltpu.VMEM((nq, tk, HEAD_DIM), BF16)]
        + _pipe_scratch(len(slices), tk, slices[0].stop),
        compiler_params=_params("arbitrary", "arbitrary"),
        name="fox_attention",
    )(q_fm, k_rows, bias_rows, v_fm)


def _diff_kernel(lam_ref, gain_ref, qt_ref, k_ref, vt_ref, o_ref,
                 m1_ref, acc1_ref, m2_ref, acc2_ref, q1_ref, q2_ref, *pipe, nq, tq, tk, lam_init):
    zeros = jnp.zeros((DK_DIFF, tq), BF16)
    for j in range(nq):
        q1_ref[j] = jnp.concatenate([qt_ref[j, 0:DK_DIFF, :], zeros], axis=0)
        q2_ref[j] = jnp.concatenate([zeros, qt_ref[j, DK_DIFF:HEAD_DIM, :]], axis=0)
    streams, q_refs = [], []
    for qs in _query_streams(tq):
        for refs, q_ref in (((m1_ref, acc1_ref), q1_ref), ((m2_ref, acc2_ref), q2_ref)):
            streams.append((refs, qs))
            q_refs.append(q_ref)

    def raw_scores(kj, qi):
        k = k_ref[pl.ds(pl.multiple_of(kj * tk, tk), tk), :]
        return [_dot(k, q_ref[qi, :, qs]) for q_ref, (_, qs) in zip(q_refs, streams)]

    def logits(kj, qi, i, raw, masked):
        qs = streams[i][1]
        if masked:
            raw = jnp.where(_causal_mask(kj * tk, qi * tq + qs.start, tk, qs.stop - qs.start), raw, NEG_INF)
        return raw

    def values(kj):
        return [_with_ones_rows(vt_ref[kj])] * len(streams)

    _head_sweep(nq, tq, tk, streams, raw_scores, logits, values, pipe, 2)

    lam_vecs = lam_ref[...]
    dot1 = jnp.sum(lam_vecs[0:1] * lam_vecs[1:2], axis=1, keepdims=True)
    dot2 = jnp.sum(lam_vecs[2:3] * lam_vecs[3:4], axis=1, keepdims=True)
    lam = jnp.exp(dot1) - jnp.exp(dot2) + lam_init
    for j in range(nq):
        o = (_normalized(acc1_ref[j]) - lam * _normalized(acc2_ref[j])).T
        y = o * lax.rsqrt(jnp.mean(o * o, axis=-1, keepdims=True) + NORM_EPS)
        o_ref[j * tq:(j + 1) * tq, :] = ((y * gain_ref[...]) * (1.0 - lam_init)).astype(o_ref.dtype)


def diff_attention(q_fm, q_blk0, k_rows, v_fm, v_blk0, lam_vecs, subln_gain, lam_init, seq):
    b = k_rows.shape[0]
    tq = tk = q_fm.shape[2]
    nq = seq // tq
    slices = _query_streams(tq)
    return pl.pallas_call(
        functools.partial(_diff_kernel, nq=nq, tq=tq, tk=tk, lam_init=lam_init),
        grid=(b, H_DIFF),
        in_specs=[
            pl.BlockSpec((4, DK_DIFF), lambda bi, h: (0, 0)),
            pl.BlockSpec((1, HEAD_DIM), lambda bi, h: (0, 0)),
            pl.BlockSpec((nq, HEAD_DIM, tq), lambda bi, h: (bi, q_blk0 + h, 0)),
            pl.BlockSpec((None, seq, HEAD_DIM), lambda bi, h: (bi, 0, h)),
            pl.BlockSpec((nq, HEAD_DIM, tk), lambda bi, h: (bi, v_blk0 + h, 0)),
        ],
        out_specs=pl.BlockSpec((None, seq, HEAD_DIM), lambda bi, h: (bi, 0, h)),
        out_shape=jax.ShapeDtypeStruct((b, seq, W_DIFF), BF16),
        scratch_shapes=_head_softmax_scratch(nq, tq) + _head_softmax_scratch(nq, tq)
        + [pltpu.VMEM((nq, HEAD_DIM, tq), BF16), pltpu.VMEM((nq, HEAD_DIM, tq), BF16)]
        + _pipe_scratch(2 * len(slices), tk, slices[0].stop),
        compiler_params=_params("arbitrary", "arbitrary"),
        name="diff_attention",
    )(lam_vecs, subln_gain.reshape(1, HEAD_DIM), q_fm, k_rows, v_fm)


def _dsa_kernel(qt_ref, k_ref, vt_ref, qit_ref, ki_ref, wt_ref, o_ref,
                hi_ref, lo_ref, qm_ref, cut_ref, m_ref, acc_ref, *pipe,
                tq, tk, seq, n_sel):
    qb = pl.program_id(1)
    n_tiles = (qb * tq) // tk + 1
    slices = _query_streams(tq)

    zeros = jnp.zeros((D_IDX, tq), BF16)
    for h in range(H_IDX):
        qm_ref[h] = jnp.concatenate([qit_ref[h * D_IDX:(h + 1) * D_IDX, :], zeros], axis=0)

    def score_tile(kj, masked):
        kk = ki_ref[pl.ds(pl.multiple_of(kj * tk, tk), tk), :]
        for qs in slices:
            width = qs.stop - qs.start
            rel_q = [_dot(kk, qm_ref[h, :, qs]) for h in range(H_IDX)]
            score = jnp.zeros((tk, width), F32)
            for h in range(H_IDX):
                score = score + wt_ref[h:h + 1, qs] * jnp.maximum(rel_q[h], 0.0)
            if masked:
                score = jnp.where(_causal_mask(kj * tk, qb * tq + qs.start, tk, width), score, NEG_INF)
            bits = lax.bitcast_convert_type(score, jnp.int32)
            key = bits ^ ((bits >> 31) & 0x7FFFFFFF)
            hi_ref[kj, :, qs] = (key >> 16).astype(jnp.int16)
            lo_ref[kj, :, qs] = ((key & 0xFFFF) + I16_MIN).astype(jnp.int16)

    _sweep_key_tiles(qb, tq, tk, score_tile)

    n_beyond = (seq - n_tiles * tk).astype(F32)
    neg_hi, neg_lo = KEY_NEG_INF >> 16, (KEY_NEG_INF & 0xFFFF) + I16_MIN
    one, zero = jnp.int16(1), jnp.int16(0)
    as16 = lambda v: v.astype(jnp.int16)

    def count_hits(hit_fn):
        def body(kj, part):
            hit = jnp.where(hit_fn(kj), one, zero)
            for r in range(tk // 16):
                part = part + hit[r * 16:(r + 1) * 16]
            return part
        part = lax.fori_loop(0, n_tiles, body, jnp.zeros((16, tq), jnp.int16))
        return jnp.sum(part.astype(F32), axis=0, keepdims=True)

    def bisect16(enough):
        def step(s, t):
            cand = t + jnp.left_shift(jnp.int32(1), 15 - s)
            return jnp.where(enough(cand), cand, t)
        return lax.fori_loop(0, 16, step, jnp.full((1, tq), I16_MIN, jnp.int32))

    def count_hi_ge(cand):
        n = count_hits(lambda kj, c=as16(cand): hi_ref[kj] >= c)
        return n + jnp.where(cand <= neg_hi, n_beyond, 0.0)

    t_hi = bisect16(lambda cand: count_hi_ge(cand) >= n_sel)
    t_hi16 = as16(t_hi)
    n_above = jnp.where(t_hi < I16_MAX, count_hi_ge(jnp.minimum(t_hi + 1, I16_MAX)), 0.0)
    need_lo = n_sel - n_above

    def mask_lo(kj, carry):
        lo_ref[kj] = jnp.where(hi_ref[kj] == t_hi16, lo_ref[kj], jnp.int16(I16_MIN))
        return carry

    lax.fori_loop(0, n_tiles, mask_lo, 0)
    beyond_in_group = jnp.where(t_hi == neg_hi, n_beyond, 0.0)

    def count_lo_ge(cand):
        n = count_hits(lambda kj, c=as16(cand): lo_ref[kj] >= c)
        return n + jnp.where(cand <= neg_lo, beyond_in_group, 0.0)

    t_lo = bisect16(lambda cand: count_lo_ge(cand) >= need_lo)
    t_lo16 = as16(t_lo)

    def in_group(kj):
        return hi_ref[kj] == t_hi16

    n_ge = n_above + count_hits(lambda kj: in_group(kj) & (lo_ref[kj] >= t_lo16)) \
        + jnp.where(t_lo <= neg_lo, beyond_in_group, 0.0)
    surplus = n_ge - n_sel
    cut_ref[...] = jnp.full((1, tq), I16_MAX, jnp.int32)

    row_in_tile = as16(lax.broadcasted_iota(jnp.int32, (tk, tq), 0))

    def position_before(kj, bound):
        return row_in_tile < as16(jnp.minimum(bound - kj * tk, I16_MAX))

    @pl.when(jnp.max(surplus) > 0.0)
    def _():
        n_gt = n_above + count_hits(lambda kj: in_group(kj) & (lo_ref[kj] > t_lo16)) \
            + jnp.where(t_lo < neg_lo, beyond_in_group, 0.0)
        need = jnp.where(surplus > 0.0, n_sel - n_gt, float(seq + 1))

        def count_tied_below(cut):
            return count_hits(lambda kj: in_group(kj) & (lo_ref[kj] == t_lo16) & position_before(kj, cut))

        n_bits = max(1, (seq - 1).bit_length())

        def bisect_cut(step, cut):
            cand = cut + jnp.left_shift(jnp.int32(1), n_bits - 1 - step)
            return jnp.where(count_tied_below(cand) < need, cand, cut)

        cut_ref[...] = lax.fori_loop(0, n_bits, bisect_cut, jnp.zeros((1, tq), jnp.int32))

    cut_end = cut_ref[...] + 1
    zero_bias, neg_bias = jnp.zeros((), BF16), jnp.asarray(NEG_INF, BF16)

    def write_bias(kj, carry):
        hi, lo = hi_ref[kj], lo_ref[kj]
        tied_ok = (lo == t_lo16) & position_before(kj, cut_end)
        sel = (hi > t_hi16) | ((hi == t_hi16) & ((lo > t_lo16) | tied_ok))
        hi_ref[kj] = lax.bitcast_convert_type(jnp.where(sel, zero_bias, neg_bias), jnp.int16)
        return carry

    lax.fori_loop(0, n_tiles, write_bias, 0)

    n_full = n_tiles - 1
    ids = range(len(slices))
    slots = (pipe[:3], pipe[3:])
    head_rows = lambda h: slice(h * HEAD_DIM, (h + 1) * HEAD_DIM)
    m_ref[...] = jnp.full(m_ref.shape, -jnp.inf, F32)
    acc_ref[...] = jnp.zeros(acc_ref.shape, F32)

    def raw_scores(kj, h):
        k = k_ref[pl.ds(pl.multiple_of(kj * tk, tk), tk), head_rows(h)]
        return [_dot(k, qt_ref[head_rows(h), qs]) for qs in slices]

    def products(kj, h, slot):
        vt = _with_ones_rows(vt_ref[kj, head_rows(h), :])
        return [_dot(vt, slot[1][i]) for i in ids]

    def accumulate(h, pv, rescale):
        for i, qs in enumerate(slices):
            acc_ref[h, :, qs] = rescale[i] * acc_ref[h, :, qs] + pv[i]

    def step(kj, h, first=False, last=False, masked=False):
        cur_slot, nxt_slot = slots[h % 2], slots[(h + 1) % 2]
        nxt = (kj, h + 1) if h + 1 < H_DSA else (kj + 1, 0)
        prev = (kj, h - 1) if h > 0 else (kj - 1, H_DSA - 1)
        raw_next = None if last else raw_scores(*nxt)
        if not first:
            pv_prev = products(*prev, nxt_slot)
            rescale_prev = [nxt_slot[2][i] for i in ids]
        if not last:
            for i in ids:
                nxt_slot[0][i] = raw_next[i]
        for i, qs in enumerate(slices):
            s = cur_slot[0][i] + lax.bitcast_convert_type(hi_ref[kj, :, qs], BF16).astype(F32)
            if masked:
                s = jnp.where(_causal_mask(kj * tk, qb * tq + qs.start, tk, qs.stop - qs.start), s, NEG_INF)
            m_new, alpha, p = _softmax_stats(s, m_ref[h, :, qs])
            m_ref[h, :, qs] = m_new
            cur_slot[1][i] = p
            cur_slot[2][i] = alpha
        if not first:
            accumulate(prev[1], pv_prev, rescale_prev)

    def tile_steps(kj, first=False, diagonal=False):
        for h in range(H_DSA):
            step(kj, h, first=first and h == 0, last=diagonal and h == H_DSA - 1, masked=diagonal)

    raw0 = raw_scores(0, 0)
    for i in ids:
        slots[0][0][i] = raw0[i]

    @pl.when(n_full == 0)
    def _():
        tile_steps(0, first=True, diagonal=True)

    @pl.when(n_full > 0)
    def _():
        tile_steps(0, first=True)

        def body(kj, carry):
            tile_steps(kj)
            return carry

        lax.fori_loop(1, n_full, body, 0)
        tile_steps(n_full, diagonal=True)

    last_slot = slots[(H_DSA - 1) % 2]
    accumulate(H_DSA - 1, products(n_full, H_DSA - 1, last_slot), [last_slot[2][i] for i in ids])
    for h in range(H_DSA):
        o_ref[:, head_rows(h)] = _normalized(acc_ref[h]).T.astype(o_ref.dtype)


def dsa_attention(q_fm, k_rows, v_fm, qi_fm, ki_rows, ki_blk, wi_t, seq):
    b = k_rows.shape[0]
    tq = tk = q_fm.shape[2]
    nq = seq // tq
    n_sel = min(TOPK_MAX, seq // 4)
    once = pl.Buffered(1)
    slices = _query_streams(tq)
    fm_q = lambda rows: pl.BlockSpec((None, rows, tq), lambda bi, i: (bi * nq + i, 0, 0))
    return pl.pallas_call(
        functools.partial(_dsa_kernel, tq=tq, tk=tk, seq=seq, n_sel=n_sel),
        grid=(b, nq),
        in_specs=[
            fm_q(W_DSA),
            pl.BlockSpec((None, seq, W_DSA), lambda bi, i: (bi, 0, 0), pipeline_mode=once),
            pl.BlockSpec((nq, W_DSA, tk), lambda bi, i: (bi, 0, 0), pipeline_mode=once),
            fm_q(W_IDX),
            pl.BlockSpec((None, seq, LANES), lambda bi, i: (bi, 0, ki_blk), pipeline_mode=once),
            pl.BlockSpec((H_IDX, tq), lambda bi, i: (0, bi * nq + i)),
        ],
        out_specs=pl.BlockSpec((None, tq, W_DSA), lambda bi, i: (bi, i, 0)),
        out_shape=jax.ShapeDtypeStruct((b, seq, W_DSA), BF16),
        scratch_shapes=[
            pltpu.VMEM((nq, tk, tq), jnp.int16),
            pltpu.VMEM((nq, tk, tq), jnp.int16),
            pltpu.VMEM((H_IDX, 2 * D_IDX, tq), BF16),
            pltpu.VMEM((1, tq), jnp.int32),
        ] + _head_softmax_scratch(H_DSA, tq) + _pipe_scratch(len(slices), tk, slices[0].stop),
        compiler_params=_params("arbitrary", "arbitrary"),
        name="dsa_attention",
    )(q_fm, k_rows, v_fm, qi_fm, ki_rows, wi_t)


def _merge_kernel(h_ref, oa_ref, ob_ref, oc_ref, mw0_ref, mw1_ref, mw2_ref, mb_ref, wa_ref, wb_ref, wc_ref, o_ref,
                  *bf16_refs):
    f32_refs = (mw0_ref, mw1_ref, mw2_ref, wa_ref, wb_ref, wc_ref)
    _cast_on_first_row_step(tuple(zip(f32_refs, bf16_refs)))
    gate_w, branch_w = bf16_refs[:N_BRANCH], bf16_refs[N_BRANCH:]
    h = h_ref[...]
    merged = None
    for i, o_b_ref in enumerate((oa_ref, ob_ref, oc_ref)):
        gate = jax.nn.sigmoid(_dot(h, gate_w[i][...]) + mb_ref[i])
        term = gate * _dot(o_b_ref[...], branch_w[i][...])
        merged = term if merged is None else merged + term
    o_ref[...] = merged.astype(o_ref.dtype)


def merge_branches(h, oa, ob, oc, merge_w, merge_b, wa, wb, wc, layer, seq):
    n, d = h.shape
    tm = min(1024, seq)
    tn = min(256, d)
    nj = d // tn
    row = lambda width: pl.BlockSpec((tm, width), lambda j, i: (i, 0))
    col = lambda rows: pl.BlockSpec((None, rows, tn), lambda j, i: (layer, 0, j))
    gate_w = lambda g: pl.BlockSpec((None, d, tn), lambda j, i: (layer, 0, g * nj + j))
    widths = (oa.shape[1], ob.shape[1], oc.shape[1])
    return pl.pallas_call(
        _merge_kernel,
        grid=(nj, n // tm),
        in_specs=[
            row(d), row(widths[0]), row(widths[1]), row(widths[2]),
            gate_w(0), gate_w(1), gate_w(2),
            pl.BlockSpec((None, N_BRANCH, 1, tn), lambda j, i: (layer, 0, 0, j)),
            col(widths[0]), col(widths[1]), col(widths[2]),
        ],
        out_specs=pl.BlockSpec((tm, tn), lambda j, i: (i, j)),
        out_shape=jax.ShapeDtypeStruct((n, d), BF16),
        scratch_shapes=[pltpu.VMEM((d, tn), BF16)] * N_BRANCH + [pltpu.VMEM((w, tn), BF16) for w in widths],
        compiler_params=_params("arbitrary", "arbitrary"),
        name="merge_branches",
    )(h, oa, ob, oc, merge_w, merge_w, merge_w, merge_b.reshape(merge_b.shape[0], N_BRANCH, 1, d), wa, wb, wc)


def _rope_tables(seq, dim):
    inv_freq = 1.0 / (ROPE_THETA ** (jnp.arange(0, dim, 2, dtype=F32) / dim))
    ang = jnp.arange(seq, dtype=F32)[:, None] * inv_freq[None, :]
    cos, sin = lax.optimization_barrier((jnp.cos(ang), jnp.sin(ang)))
    cos_g = jnp.concatenate([cos, cos], axis=1)
    sin_g = jnp.concatenate([-sin, sin], axis=1)
    reps = LANES // dim
    return jnp.tile(cos_g, (1, reps)), jnp.tile(sin_g, (1, reps)), cos_g.T, sin_g.T


def _split_w_in(w_in_t):
    sizes = (W_DIFF, W_DIFF, W_DIFF, W_FOX, W_FOX, W_FOX, H_FOX, W_DSA, W_DSA, W_DSA, W_IDX, D_IDX, H_IDX)
    parts, start = [], 0
    for n in sizes:
        parts.append(w_in_t[start:start + n])
        start += n
    return parts


def kernel(x, c, w_ada, b_ada, norm_ffn1, ffn1_w1, ffn1_w3, ffn1_w2, norm_mix, w_in, b_forget, lam_q1, lam_k1, lam_q2, lam_k2, subln_gain, merge_w, merge_b, w_branch_a, w_branch_b, w_branch_c, w_out, norm_ffn2, ffn2_w1, ffn2_w3, ffn2_w2, norm_final):
    batch, seq, d = x.shape
    depth = w_ada.shape[0]
    n = batch * seq
    assert seq % min(TOKEN_TILE, seq) == 0 and seq % LANES == 0 and d % LANES == 0
    assert seq <= I16_MAX + 1, "the top-k tie rule compares key positions as int16"
    bf = lambda a: a.astype(BF16)

    cos64, sin64, cos64_fm, sin64_fm = _rope_tables(seq, DK_DIFF)
    cos128, sin128, cos128_fm, sin128_fm = _rope_tables(seq, HEAD_DIM)
    mod = adaln_mod(c, w_ada, b_ada)
    xf = x.reshape(n, d)

    for l in range(depth):
        lam_init = 0.8 - 0.6 * math.exp(-0.3 * l)
        modl = mod[l].reshape(batch * N_MOD, 1, d)

        h = norm_mod(xf, norm_ffn1[l], modl, 0, seq)
        u = ffn_up(h, ffn1_w1, ffn1_w3, l)
        xf = resid_mm(u, ffn1_w2, l, xf, modl, 2, 0.5, seq)

        h = norm_mod(xf, norm_mix[l], modl, 3, seq)
        qa, ka, va, qb, kb, vb, fb, qc, kc, vc, qi, ki, wi = _split_w_in(jnp.transpose(w_in[l]))
        k64 = proj(h, jnp.concatenate([ka, ki, ki], axis=0), cos64, sin64, DK_DIFF, seq).reshape(batch, seq, -1)
        k128 = proj(h, kc, cos128, sin128, HEAD_DIM, seq).reshape(batch, seq, -1)
        k0 = proj(h, kb, cos64, sin64, 0, seq).reshape(batch, seq, -1)
        log2e_over_sqrt = lambda width: (width ** -0.5) * LOG2E
        q64_fm = proj_fm(h, jnp.concatenate([qi, qa], axis=0), cos64_fm, sin64_fm, DK_DIFF, seq,
                         ((W_IDX, 1.0), (W_DIFF, log2e_over_sqrt(DK_DIFF))))
        q128_fm = proj_fm(h, qc, cos128_fm, sin128_fm, HEAD_DIM, seq, ((W_DSA, log2e_over_sqrt(HEAD_DIM)),))
        qb_fm = proj_fm(h, qb, cos64_fm, sin64_fm, 0, seq, ((W_FOX, log2e_over_sqrt(HEAD_DIM)),))
        v_fm = proj_fm(h, jnp.concatenate([vc, va, vb], axis=0), cos64_fm, sin64_fm, 0, seq)
        blk = lambda rows: rows // HEAD_DIM
        qa_blk, va_blk, vb_blk = blk(W_IDX), blk(W_DSA), blk(W_DSA + W_DIFF)

        w_small_t = bf(jnp.concatenate([fb, jnp.zeros((8 - H_FOX, d), F32), wi], axis=0))
        b_forget8 = jnp.concatenate([b_forget[l], jnp.zeros((8 - H_FOX,), F32)]).reshape(8, 1)
        logf_t, wi_t = small_proj(h, w_small_t, b_forget8, seq)
        bias_rows = neg_cumsum(logf_t, batch, seq)

        lam_vecs = jnp.stack([lam_q1[l], lam_k1[l], lam_q2[l], lam_k2[l]])
        oa = diff_attention(q64_fm, qa_blk, k64, v_fm, va_blk, lam_vecs, subln_gain[l], lam_init, seq)
        ob = fox_attention(qb_fm, 0, k0, bias_rows, v_fm, vb_blk, seq)
        oc = dsa_attention(q128_fm, k128, v_fm, q64_fm, k64, blk(W_DIFF), wi_t, seq)

        merged = merge_branches(h, oa.reshape(n, -1), ob.reshape(n, -1), oc.reshape(n, -1), merge_w, merge_b,
                                w_branch_a, w_branch_b, w_branch_c, l, seq)
        xf = resid_mm(merged, w_out, l, xf, modl, 5, 1.0, seq)

        h = norm_mod(xf, norm_ffn2[l], modl, 6, seq)
        u = ffn_up(h, ffn2_w1, ffn2_w3, l)
        xf = resid_mm(u, ffn2_w2, l, xf, modl, 8, 0.5, seq)

    return final_norm(xf, norm_final).reshape(batch, seq, d)
```
